```python
import jax
import jax.numpy as jnp
from jax import lax
import numpy as np

D_MODEL = 2048
BATCH = 8
SEQ = 8192
DEPTH = 2

CHUNK = 64
LN_EPS = 1e-5
ALPHA = (2 * DEPTH) ** 0.25
BETA = (8 * DEPTH) ** -0.25
N_EVEN = (DEPTH + 1) // 2
N_ODD = DEPTH // 2

CONV_CH = D_MODEL // 2
CONV_WIDTH = 31
HGRN_HEADS = 8
HGRN_DK = 128
HGRN_DV = (D_MODEL // 2) // HGRN_HEADS
HGRN_KDIM = HGRN_HEADS * HGRN_DK
HGRN_VDIM = HGRN_HEADS * HGRN_DV
FOX_HEADS = 8
FOX_DH = 128
FOX_WIDTH = FOX_HEADS * FOX_DH
FOX_QBLOCK = 128
CA_HEADS = 8
CA_DH = 128
CA_WIDTH = CA_HEADS * CA_DH
CA_LEFT_CHUNKS = 8
CA_BAND = (CA_LEFT_CHUNKS + 1) * CHUNK
REL_CLIP = 256
REL_TABLE = (CHUNK - 1) + REL_CLIP + 1
D_FF = 4 * D_MODEL

EVEN_IN = 2 * CONV_CH + 2 * HGRN_KDIM + 2 * HGRN_VDIM
ODD_IN = 3 * FOX_WIDTH + FOX_HEADS + 3 * CA_WIDTH

kernel_name = 'hybrid_conv_hgrn2_fox_chunkattn_trunk'


def layer_norm(x, g, b):
    xf = x.astype(jnp.float32)
    mu = jnp.mean(xf, axis=-1, keepdims=True)
    var = jnp.mean(jnp.square(xf - mu), axis=-1, keepdims=True)
    return ((xf - mu) * lax.rsqrt(var + LN_EPS) * g + b).astype(x.dtype)


def conformer_conv(u, conv_w, conv_b, ln_g, ln_b):
    a, gate = jnp.split(u, 2, axis=-1)
    h = a * jax.nn.sigmoid(gate)
    h = lax.conv_general_dilated(
        h, conv_w[:, None, :].astype(h.dtype), window_strides=(1,),
        padding=[(CONV_WIDTH - 1, 0)],
        dimension_numbers=('NWC', 'WIO', 'NWC'),
        feature_group_count=CONV_CH) + conv_b
    return jax.nn.silu(layer_norm(h, ln_g, ln_b))


def hgrn2(q, f_logit, i, g, lb, gnorm_g):
    B, S, _ = q.shape
    nc = S // CHUNK
    f32 = jnp.float32
    f = lb + (1.0 - lb) * jax.nn.sigmoid(f_logit.astype(f32))
    log_f = jnp.log(f)
    k = 1.0 - f

    def to_chunks(t, d):
        return t.astype(f32).reshape(B, nc, CHUNK, HGRN_HEADS, d).transpose(1, 0, 3, 2, 4)

    qc = to_chunks(jax.nn.silu(q), HGRN_DK)
    kc = to_chunks(k, HGRN_DK)
    vc = to_chunks(i, HGRN_DV)
    lc = to_chunks(log_f, HGRN_DK)
    causal = jnp.tril(jnp.ones((CHUNK, CHUNK), dtype=bool))[:, :, None]

    def step(state, inp):
        qb, kb, vb, lfb = inp
        L = jnp.cumsum(lfb, axis=2)
        diff = L[:, :, :, None, :] - L[:, :, None, :, :]
        decay = jnp.exp(jnp.where(causal, diff, -jnp.inf))
        scores = jnp.einsum('bhtk,bhsk,bhtsk->bhts', qb, kb, decay)
        o = (jnp.einsum('bhts,bhsv->bhtv', scores, vb)
             + jnp.einsum('bhtk,bhkv->bhtv', qb * jnp.exp(L), state))
        L_end = L[:, :, -1, :]
        state = (jnp.exp(L_end)[..., None] * state
                 + jnp.einsum('bhsk,bhsv->bhkv', kb * jnp.exp(L_end[:, :, None, :] - L), vb))
        return state, o

    s0 = jnp.zeros((B, HGRN_HEADS, HGRN_DK, HGRN_DV), f32)
    _, o = lax.scan(step, s0, (qc, kc, vc, lc))
    o = o.transpose(1, 0, 3, 2, 4).reshape(B, S, HGRN_HEADS, HGRN_DV)
    o = o * lax.rsqrt(jnp.mean(jnp.square(o), axis=-1, keepdims=True) + LN_EPS)
    o = o.reshape(B, S, HGRN_VDIM) * gnorm_g * jax.nn.silu(g.astype(f32))
    return o.astype(q.dtype)


def forgetting_attention(q, k, v, f_logit):
    B, S, _ = q.shape
    nb = S // FOX_QBLOCK
    heads = lambda t: t.reshape(B, S, FOX_HEADS, FOX_DH).transpose(0, 2, 1, 3)
    qh, kh, vh = heads(q), heads(k), heads(v)
    F = jnp.cumsum(jax.nn.log_sigmoid(f_logit.astype(jnp.float32)), axis=1).transpose(0, 2, 1)
    q_blocks = qh.reshape(B, FOX_HEADS, nb, FOX_QBLOCK, FOX_DH).transpose(2, 0, 1, 3, 4)
    F_blocks = F.reshape(B, FOX_HEADS, nb, FOX_QBLOCK).transpose(2, 0, 1, 3)
    kpos = jnp.arange(S)
    scale = FOX_DH ** -0.5

    def block(args):
        qb, Fb, bidx = args
        s = jnp.einsum('bhqd,bhkd->bhqk', qb, kh).astype(jnp.float32) * scale
        s = s + Fb[..., None] - F[:, :, None, :]
        qpos = bidx * FOX_QBLOCK + jnp.arange(FOX_QBLOCK)
        s = jnp.where(kpos[None, :] <= qpos[:, None], s, -jnp.inf)
        p = jax.nn.softmax(s, axis=-1).astype(vh.dtype)
        return jnp.einsum('bhqk,bhkd->bhqd', p, vh)

    out = lax.map(block, (q_blocks, F_blocks, jnp.arange(nb)))
    return out.transpose(1, 0, 3, 2, 4).reshape(B, S, FOX_WIDTH)


def chunked_relpos_attention(q, k, v, rel_bias):
    B, S, _ = q.shape
    nc = S // CHUNK
    left = CA_LEFT_CHUNKS * CHUNK
    qc = q.reshape(B, nc, CHUNK, CA_HEADS, CA_DH).transpose(1, 0, 3, 2, 4)
    pad = lambda t: jnp.pad(t.reshape(B, S, CA_HEADS, CA_DH),
                            ((0, 0), (left, 0), (0, 0), (0, 0))).transpose(0, 2, 1, 3)
    kp, vp = pad(k), pad(v)
    qi = jnp.arange(CHUNK)
    km = jnp.arange(CA_BAND)
    rel = (left + qi[:, None]) - km[None, :]
    bias = rel_bias[:, jnp.minimum(rel, REL_CLIP) + (CHUNK - 1)].astype(jnp.float32)
    scale = CA_DH ** -0.5

    def chunk(args):
        qb, c = args
        start = c * CHUNK
        kb = lax.dynamic_slice_in_dim(kp, start, CA_BAND, axis=2)
        vb = lax.dynamic_slice_in_dim(vp, start, CA_BAND, axis=2)
        s = jnp.einsum('bhqd,bhkd->bhqk', qb, kb).astype(jnp.float32) * scale + bias
        s = jnp.where(start + km >= left, s, -jnp.inf)
        p = jax.nn.softmax(s, axis=-1).astype(vb.dtype)
        return jnp.einsum('bhqk,bhkd->bhqd', p, vb)

    out = lax.map(chunk, (qc, jnp.arange(nc)))
    return out.transpose(1, 0, 3, 2, 4).reshape(B, S, CA_WIDTH)


def _fwd_setup_inputs(seed: int = 0) -> dict:
    key = jax.random.key(seed)
    ks = jax.random.split(key, 19)
    f32 = jnp.float32

    def nrm(k, shape, scale):
        return scale * jax.random.normal(k, shape, f32)

    return {
        'x': nrm(ks[0], (BATCH, SEQ, D_MODEL), 1.0),
        'ev_w_in': nrm(ks[1], (N_EVEN, D_MODEL, EVEN_IN), D_MODEL ** -0.5),
        'ev_conv_w': nrm(ks[2], (N_EVEN, CONV_WIDTH, CONV_CH), CONV_WIDTH ** -0.5),
        'ev_conv_b': nrm(ks[3], (N_EVEN, CONV_CH), 0.02),
        'ev_conv_ln_g': 1.0 + nrm(ks[4], (N_EVEN, CONV_CH), 0.1),
        'ev_conv_ln_b': nrm(ks[5], (N_EVEN, CONV_CH), 0.02),
        'hgrn_lb_logits': nrm(ks[6], (N_EVEN + 1, HGRN_KDIM), 0.1),
        'ev_gnorm_g': 1.0 + nrm(ks[7], (N_EVEN, HGRN_VDIM), 0.1),
        'ev_w_out': nrm(ks[8], (N_EVEN, CONV_CH + HGRN_VDIM, D_MODEL), (CONV_CH + HGRN_VDIM) ** -0.5 * BETA),
        'od_w_in': nrm(ks[9], (N_ODD, D_MODEL, ODD_IN), D_MODEL ** -0.5),
        'fox_b_f': 3.0 + nrm(ks[10], (N_ODD, FOX_HEADS), 0.5),
        'rel_bias': nrm(ks[11], (N_ODD, CA_HEADS, REL_TABLE), 0.5),
        'od_w_out': nrm(ks[12], (N_ODD, FOX_WIDTH + CA_WIDTH, D_MODEL), (FOX_WIDTH + CA_WIDTH) ** -0.5 * BETA),
        'ln_mix_g': 1.0 + nrm(ks[13], (DEPTH, D_MODEL), 0.1),
        'ln_mix_b': nrm(ks[14], (DEPTH, D_MODEL), 0.02),
        'mlp_w1': nrm(ks[15], (DEPTH, D_MODEL, D_FF), D_MODEL ** -0.5),
        'mlp_w2': nrm(ks[16], (DEPTH, D_FF, D_MODEL), D_FF ** -0.5 * BETA),
        'ln_mlp_g': 1.0 + nrm(ks[17], (DEPTH, D_MODEL), 0.1),
        'ln_mlp_b': nrm(ks[18], (DEPTH, D_MODEL), 0.02),
    }


def _fwd_reference(x, ev_w_in, ev_conv_w, ev_conv_b, ev_conv_ln_g, ev_conv_ln_b, hgrn_lb_logits,
              ev_gnorm_g, ev_w_out, od_w_in, fox_b_f, rel_bias, od_w_out, ln_mix_g, ln_mix_b,
              mlp_w1, mlp_w2, ln_mlp_g, ln_mlp_b):
    lower_bounds = jnp.cumsum(jax.nn.softmax(hgrn_lb_logits.astype(jnp.float32), axis=0), axis=0)
    for l in range(DEPTH):
        j = l // 2
        if l % 2 == 0:
            u = x @ ev_w_in[j]
            conv_in, hq, hf, hi, hg = jnp.split(
                u, [2 * CONV_CH, 2 * CONV_CH + HGRN_KDIM, 2 * CONV_CH + 2 * HGRN_KDIM,
                    2 * CONV_CH + 2 * HGRN_KDIM + HGRN_VDIM], axis=-1)
            a_out = conformer_conv(conv_in, ev_conv_w[j], ev_conv_b[j], ev_conv_ln_g[j], ev_conv_ln_b[j])
            b_out = hgrn2(hq, hf, hi, hg, lower_bounds[j], ev_gnorm_g[j])
            mix = jnp.concatenate([a_out, b_out], axis=-1) @ ev_w_out[j]
        else:
            u = x @ od_w_in[j]
            c_q, c_k, c_v, c_f, d_q, d_k, d_v = jnp.split(
                u, [FOX_WIDTH, 2 * FOX_WIDTH, 3 * FOX_WIDTH, 3 * FOX_WIDTH + FOX_HEADS,
                    3 * FOX_WIDTH + FOX_HEADS + CA_WIDTH, 3 * FOX_WIDTH + FOX_HEADS + 2 * CA_WIDTH], axis=-1)
            c_out = forgetting_attention(c_q, c_k, c_v, c_f + fox_b_f[j])
            d_out = chunked_relpos_attention(d_q, d_k, d_v, rel_bias[j])
            mix = jnp.concatenate([c_out, d_out], axis=-1) @ od_w_out[j]
        x = layer_norm(ALPHA * x + mix, ln_mix_g[l], ln_mix_b[l])
        h = jnp.square(jax.nn.relu(x @ mlp_w1[l])) @ mlp_w2[l]
        x = layer_norm(ALPHA * x + h, ln_mlp_g[l], ln_mlp_b[l])
    return x


import jax as _jax
import jax.numpy as _jnp

TWIN_FORMAT = 'train_step'
FWD_PARAMS = ['x', 'ev_w_in', 'ev_conv_w', 'ev_conv_b', 'ev_conv_ln_g', 'ev_conv_ln_b', 'hgrn_lb_logits', 'ev_gnorm_g', 'ev_w_out', 'od_w_in', 'fox_b_f', 'rel_bias', 'od_w_out', 'ln_mix_g', 'ln_mix_b', 'mlp_w1', 'mlp_w2', 'ln_mlp_g', 'ln_mlp_b']
TWIN_WEIGHTS = ['ev_w_in', 'ev_conv_w', 'ev_conv_b', 'ev_conv_ln_g', 'ev_conv_ln_b', 'hgrn_lb_logits', 'ev_gnorm_g', 'ev_w_out', 'od_w_in', 'fox_b_f', 'rel_bias', 'od_w_out', 'ln_mix_g', 'ln_mix_b', 'mlp_w1', 'mlp_w2', 'ln_mlp_g', 'ln_mlp_b']
TWIN_DIFF_INPUT = 'x'
TWIN_INPUTS = ['x', 'ev_w_in', 'ev_conv_w', 'ev_conv_b', 'ev_conv_ln_g', 'ev_conv_ln_b', 'hgrn_lb_logits', 'ev_gnorm_g', 'ev_w_out', 'od_w_in', 'fox_b_f', 'rel_bias', 'od_w_out', 'ln_mix_g', 'ln_mix_b', 'mlp_w1', 'mlp_w2', 'ln_mlp_g', 'ln_mlp_b', 'loss_target', 'm_ev_w_in', 'm_ev_conv_w', 'm_ev_conv_b', 'm_ev_conv_ln_g', 'm_ev_conv_ln_b', 'm_hgrn_lb_logits', 'm_ev_gnorm_g', 'm_ev_w_out', 'm_od_w_in', 'm_fox_b_f', 'm_rel_bias', 'm_od_w_out', 'm_ln_mix_g', 'm_ln_mix_b', 'm_mlp_w1', 'm_mlp_w2', 'm_ln_mlp_g', 'm_ln_mlp_b', 'v_ev_w_in', 'v_ev_conv_w', 'v_ev_conv_b', 'v_ev_conv_ln_g', 'v_ev_conv_ln_b', 'v_hgrn_lb_logits', 'v_ev_gnorm_g', 'v_ev_w_out', 'v_od_w_in', 'v_fox_b_f', 'v_rel_bias', 'v_od_w_out', 'v_ln_mix_g', 'v_ln_mix_b', 'v_mlp_w1', 'v_mlp_w2', 'v_ln_mlp_g', 'v_ln_mlp_b']
TWIN_OUTPUTS = ['loss', 'grad_x', 'grad_ev_w_in', 'grad_ev_conv_w', 'grad_ev_conv_b', 'grad_ev_conv_ln_g', 'grad_ev_conv_ln_b', 'grad_hgrn_lb_logits', 'grad_ev_gnorm_g', 'grad_ev_w_out', 'grad_od_w_in', 'grad_fox_b_f', 'grad_rel_bias', 'grad_od_w_out', 'grad_ln_mix_g', 'grad_ln_mix_b', 'grad_mlp_w1', 'grad_mlp_w2', 'grad_ln_mlp_g', 'grad_ln_mlp_b', 'delta_ev_w_in', 'delta_ev_conv_w', 'delta_ev_conv_b', 'delta_ev_conv_ln_g', 'delta_ev_conv_ln_b', 'delta_hgrn_lb_logits', 'delta_ev_gnorm_g', 'delta_ev_w_out', 'delta_od_w_in', 'delta_fox_b_f', 'delta_rel_bias', 'delta_od_w_out', 'delta_ln_mix_g', 'delta_ln_mix_b', 'delta_mlp_w1', 'delta_mlp_w2', 'delta_ln_mlp_g', 'delta_ln_mlp_b', 'new_m_ev_w_in', 'new_m_ev_conv_w', 'new_m_ev_conv_b', 'new_m_ev_conv_ln_g', 'new_m_ev_conv_ln_b', 'new_m_hgrn_lb_logits', 'new_m_ev_gnorm_g', 'new_m_ev_w_out', 'new_m_od_w_in', 'new_m_fox_b_f', 'new_m_rel_bias', 'new_m_od_w_out', 'new_m_ln_mix_g', 'new_m_ln_mix_b', 'new_m_mlp_w1', 'new_m_mlp_w2', 'new_m_ln_mlp_g', 'new_m_ln_mlp_b', 'new_v_ev_w_in', 'new_v_ev_conv_w', 'new_v_ev_conv_b', 'new_v_ev_conv_ln_g', 'new_v_ev_conv_ln_b', 'new_v_hgrn_lb_logits', 'new_v_ev_gnorm_g', 'new_v_ev_w_out', 'new_v_od_w_in', 'new_v_fox_b_f', 'new_v_rel_bias', 'new_v_od_w_out', 'new_v_ln_mix_g', 'new_v_ln_mix_b', 'new_v_mlp_w1', 'new_v_mlp_w2', 'new_v_ln_mlp_g', 'new_v_ln_mlp_b']
TWIN_LEAF_KINDS = {'loss': 'loss', 'grad_x': 'grad_x', 'grad_ev_w_in': 'grad_w', 'grad_ev_conv_w': 'grad_w', 'grad_ev_conv_b': 'grad_w', 'grad_ev_conv_ln_g': 'grad_w', 'grad_ev_conv_ln_b': 'grad_w', 'grad_hgrn_lb_logits': 'grad_w', 'grad_ev_gnorm_g': 'grad_w', 'grad_ev_w_out': 'grad_w', 'grad_od_w_in': 'grad_w', 'grad_fox_b_f': 'grad_w', 'grad_rel_bias': 'grad_w', 'grad_od_w_out': 'grad_w', 'grad_ln_mix_g': 'grad_w', 'grad_ln_mix_b': 'grad_w', 'grad_mlp_w1': 'grad_w', 'grad_mlp_w2': 'grad_w', 'grad_ln_mlp_g': 'grad_w', 'grad_ln_mlp_b': 'grad_w', 'delta_ev_w_in': 'delta_w', 'delta_ev_conv_w': 'delta_w', 'delta_ev_conv_b': 'delta_w', 'delta_ev_conv_ln_g': 'delta_w', 'delta_ev_conv_ln_b': 'delta_w', 'delta_hgrn_lb_logits': 'delta_w', 'delta_ev_gnorm_g': 'delta_w', 'delta_ev_w_out': 'delta_w', 'delta_od_w_in': 'delta_w', 'delta_fox_b_f': 'delta_w', 'delta_rel_bias': 'delta_w', 'delta_od_w_out': 'delta_w', 'delta_ln_mix_g': 'delta_w', 'delta_ln_mix_b': 'delta_w', 'delta_mlp_w1': 'delta_w', 'delta_mlp_w2': 'delta_w', 'delta_ln_mlp_g': 'delta_w', 'delta_ln_mlp_b': 'delta_w', 'new_m_ev_w_in': 'new_m', 'new_m_ev_conv_w': 'new_m', 'new_m_ev_conv_b': 'new_m', 'new_m_ev_conv_ln_g': 'new_m', 'new_m_ev_conv_ln_b': 'new_m', 'new_m_hgrn_lb_logits': 'new_m', 'new_m_ev_gnorm_g': 'new_m', 'new_m_ev_w_out': 'new_m', 'new_m_od_w_in': 'new_m', 'new_m_fox_b_f': 'new_m', 'new_m_rel_bias': 'new_m', 'new_m_od_w_out': 'new_m', 'new_m_ln_mix_g': 'new_m', 'new_m_ln_mix_b': 'new_m', 'new_m_mlp_w1': 'new_m', 'new_m_mlp_w2': 'new_m', 'new_m_ln_mlp_g': 'new_m', 'new_m_ln_mlp_b': 'new_m', 'new_v_ev_w_in': 'new_v', 'new_v_ev_conv_w': 'new_v', 'new_v_ev_conv_b': 'new_v', 'new_v_ev_conv_ln_g': 'new_v', 'new_v_ev_conv_ln_b': 'new_v', 'new_v_hgrn_lb_logits': 'new_v', 'new_v_ev_gnorm_g': 'new_v', 'new_v_ev_w_out': 'new_v', 'new_v_od_w_in': 'new_v', 'new_v_fox_b_f': 'new_v', 'new_v_rel_bias': 'new_v', 'new_v_od_w_out': 'new_v', 'new_v_ln_mix_g': 'new_v', 'new_v_ln_mix_b': 'new_v', 'new_v_mlp_w1': 'new_v', 'new_v_mlp_w2': 'new_v', 'new_v_ln_mlp_g': 'new_v', 'new_v_ln_mlp_b': 'new_v'}


def _forward(args):
    return _fwd_reference(*[args[k] for k in FWD_PARAMS])


def _output_shape():
    def fwd():
        inp = _fwd_setup_inputs(0)
        return _fwd_reference(*[inp[k] for k in FWD_PARAMS])
    out = _jax.eval_shape(fwd)
    return out.shape, out.dtype

N_MICROBATCH = 1
ADAM_LR = 0.001
ADAM_B1 = 0.9
ADAM_B2 = 0.999
ADAM_EPS = 1e-08
ADAM_WD = 0.01
ADAM_STEP = 10
PER_EXAMPLE_BATCH_AXIS = {'x': 0, 'loss_target': 0}
SHARED_INPUTS = []
_WEIGHT_DTYPES = {'ev_w_in': _jnp.float32, 'ev_conv_w': _jnp.float32, 'ev_conv_b': _jnp.float32, 'ev_conv_ln_g': _jnp.float32, 'ev_conv_ln_b': _jnp.float32, 'hgrn_lb_logits': _jnp.float32, 'ev_gnorm_g': _jnp.float32, 'ev_w_out': _jnp.float32, 'od_w_in': _jnp.float32, 'fox_b_f': _jnp.float32, 'rel_bias': _jnp.float32, 'od_w_out': _jnp.float32, 'ln_mix_g': _jnp.float32, 'ln_mix_b': _jnp.float32, 'mlp_w1': _jnp.float32, 'mlp_w2': _jnp.float32, 'ln_mlp_g': _jnp.float32, 'ln_mlp_b': _jnp.float32}
MOMENT_SCALE = {'ev_w_in': 2.098378e-02, 'ev_conv_w': 3.343800e-02, 'ev_conv_b': 4.007690e-01, 'ev_conv_ln_g': 1.473304e-01, 'ev_conv_ln_b': 2.401752e-01, 'hgrn_lb_logits': 2.745563e-03, 'ev_gnorm_g': 2.953193e-02, 'ev_w_out': 1.144622e-01, 'od_w_in': 3.783262e-02, 'fox_b_f': 1.081166e-01, 'rel_bias': 4.473155e-03, 'od_w_out': 1.476012e-01, 'ln_mix_g': 5.009182e+00, 'ln_mix_b': 1.402074e+00, 'mlp_w1': 3.541949e-02, 'mlp_w2': 4.287465e-01, 'ln_mlp_g': 2.382711e+01, 'ln_mlp_b': 5.462546e+00}


def _to_microbatches(a, axis):
    t = _jnp.moveaxis(a, axis, 0)
    t = t.reshape((N_MICROBATCH, t.shape[0] // N_MICROBATCH) + t.shape[1:])
    return _jnp.moveaxis(t, 1, axis + 1)


def setup_inputs(seed: int = 0) -> dict:
    inp = _fwd_setup_inputs(seed)
    key = _jax.random.fold_in(_jax.random.key(seed), 7919)
    shape, _ = _output_shape()
    out = dict(inp)
    out["loss_target"] = _jax.random.normal(_jax.random.fold_in(key, 0), shape, _jnp.float32)
    for i, name in enumerate(TWIN_WEIGHTS):
        w = inp[name].astype(_jnp.float32)
        if MOMENT_SCALE is None:
            s = _jnp.sqrt(_jnp.mean(_jnp.square(w)) + 1e-30)
        else:
            s = MOMENT_SCALE[name]
        km, kv = _jax.random.split(_jax.random.fold_in(key, i + 1))
        out[name] = w
        out["m_" + name] = s * _jax.random.normal(km, w.shape, _jnp.float32)
        out["v_" + name] = (s * s) * _jax.random.uniform(kv, w.shape, _jnp.float32, 0.5, 1.5)
    if N_MICROBATCH > 1:
        for name, axis in PER_EXAMPLE_BATCH_AXIS.items():
            out[name] = _to_microbatches(out[name], axis)
    return {'x': out['x'], 'ev_w_in': out['ev_w_in'], 'ev_conv_w': out['ev_conv_w'], 'ev_conv_b': out['ev_conv_b'], 'ev_conv_ln_g': out['ev_conv_ln_g'], 'ev_conv_ln_b': out['ev_conv_ln_b'], 'hgrn_lb_logits': out['hgrn_lb_logits'], 'ev_gnorm_g': out['ev_gnorm_g'], 'ev_w_out': out['ev_w_out'], 'od_w_in': out['od_w_in'], 'fox_b_f': out['fox_b_f'], 'rel_bias': out['rel_bias'], 'od_w_out': out['od_w_out'], 'ln_mix_g': out['ln_mix_g'], 'ln_mix_b': out['ln_mix_b'], 'mlp_w1': out['mlp_w1'], 'mlp_w2': out['mlp_w2'], 'ln_mlp_g': out['ln_mlp_g'], 'ln_mlp_b': out['ln_mlp_b'], 'loss_target': out['loss_target'], 'm_ev_w_in': out['m_ev_w_in'], 'm_ev_conv_w': out['m_ev_conv_w'], 'm_ev_conv_b': out['m_ev_conv_b'], 'm_ev_conv_ln_g': out['m_ev_conv_ln_g'], 'm_ev_conv_ln_b': out['m_ev_conv_ln_b'], 'm_hgrn_lb_logits': out['m_hgrn_lb_logits'], 'm_ev_gnorm_g': out['m_ev_gnorm_g'], 'm_ev_w_out': out['m_ev_w_out'], 'm_od_w_in': out['m_od_w_in'], 'm_fox_b_f': out['m_fox_b_f'], 'm_rel_bias': out['m_rel_bias'], 'm_od_w_out': out['m_od_w_out'], 'm_ln_mix_g': out['m_ln_mix_g'], 'm_ln_mix_b': out['m_ln_mix_b'], 'm_mlp_w1': out['m_mlp_w1'], 'm_mlp_w2': out['m_mlp_w2'], 'm_ln_mlp_g': out['m_ln_mlp_g'], 'm_ln_mlp_b': out['m_ln_mlp_b'], 'v_ev_w_in': out['v_ev_w_in'], 'v_ev_conv_w': out['v_ev_conv_w'], 'v_ev_conv_b': out['v_ev_conv_b'], 'v_ev_conv_ln_g': out['v_ev_conv_ln_g'], 'v_ev_conv_ln_b': out['v_ev_conv_ln_b'], 'v_hgrn_lb_logits': out['v_hgrn_lb_logits'], 'v_ev_gnorm_g': out['v_ev_gnorm_g'], 'v_ev_w_out': out['v_ev_w_out'], 'v_od_w_in': out['v_od_w_in'], 'v_fox_b_f': out['v_fox_b_f'], 'v_rel_bias': out['v_rel_bias'], 'v_od_w_out': out['v_od_w_out'], 'v_ln_mix_g': out['v_ln_mix_g'], 'v_ln_mix_b': out['v_ln_mix_b'], 'v_mlp_w1': out['v_mlp_w1'], 'v_mlp_w2': out['v_mlp_w2'], 'v_ln_mlp_g': out['v_ln_mlp_g'], 'v_ln_mlp_b': out['v_ln_mlp_b']}


def _loss(weights, diff, rest, loss_target):
    with _jax.named_scope("forward"):
        args = {**rest, TWIN_DIFF_INPUT: diff, **{k: w.astype(_WEIGHT_DTYPES[k]) for k, w in weights.items()}}
        y = _forward(args)
    with _jax.named_scope("loss_head"):
        err = _jnp.square(y.astype(_jnp.float32) - loss_target)
        return 0.5 * _jnp.sum(_jnp.mean(err, axis=-1)) if err.ndim else 0.5 * err


def _adamw(w, g, m, v):
    m = ADAM_B1 * m + (1.0 - ADAM_B1) * g
    v = ADAM_B2 * v + (1.0 - ADAM_B2) * _jnp.square(g)
    m_hat = m / (1.0 - ADAM_B1 ** ADAM_STEP)
    v_hat = v / (1.0 - ADAM_B2 ** ADAM_STEP)
    delta = -ADAM_LR * (m_hat / (_jnp.sqrt(v_hat) + ADAM_EPS) + ADAM_WD * w)
    return delta, m, v


def reference(x, ev_w_in, ev_conv_w, ev_conv_b, ev_conv_ln_g, ev_conv_ln_b, hgrn_lb_logits, ev_gnorm_g, ev_w_out, od_w_in, fox_b_f, rel_bias, od_w_out, ln_mix_g, ln_mix_b, mlp_w1, mlp_w2, ln_mlp_g, ln_mlp_b, loss_target, m_ev_w_in, m_ev_conv_w, m_ev_conv_b, m_ev_conv_ln_g, m_ev_conv_ln_b, m_hgrn_lb_logits, m_ev_gnorm_g, m_ev_w_out, m_od_w_in, m_fox_b_f, m_rel_bias, m_od_w_out, m_ln_mix_g, m_ln_mix_b, m_mlp_w1, m_mlp_w2, m_ln_mlp_g, m_ln_mlp_b, v_ev_w_in, v_ev_conv_w, v_ev_conv_b, v_ev_conv_ln_g, v_ev_conv_ln_b, v_hgrn_lb_logits, v_ev_gnorm_g, v_ev_w_out, v_od_w_in, v_fox_b_f, v_rel_bias, v_od_w_out, v_ln_mix_g, v_ln_mix_b, v_mlp_w1, v_mlp_w2, v_ln_mlp_g, v_ln_mlp_b):
    given = dict(x=x, ev_w_in=ev_w_in, ev_conv_w=ev_conv_w, ev_conv_b=ev_conv_b, ev_conv_ln_g=ev_conv_ln_g, ev_conv_ln_b=ev_conv_ln_b, hgrn_lb_logits=hgrn_lb_logits, ev_gnorm_g=ev_gnorm_g, ev_w_out=ev_w_out, od_w_in=od_w_in, fox_b_f=fox_b_f, rel_bias=rel_bias, od_w_out=od_w_out, ln_mix_g=ln_mix_g, ln_mix_b=ln_mix_b, mlp_w1=mlp_w1, mlp_w2=mlp_w2, ln_mlp_g=ln_mlp_g, ln_mlp_b=ln_mlp_b, loss_target=loss_target, m_ev_w_in=m_ev_w_in, m_ev_conv_w=m_ev_conv_w, m_ev_conv_b=m_ev_conv_b, m_ev_conv_ln_g=m_ev_conv_ln_g, m_ev_conv_ln_b=m_ev_conv_ln_b, m_hgrn_lb_logits=m_hgrn_lb_logits, m_ev_gnorm_g=m_ev_gnorm_g, m_ev_w_out=m_ev_w_out, m_od_w_in=m_od_w_in, m_fox_b_f=m_fox_b_f, m_rel_bias=m_rel_bias, m_od_w_out=m_od_w_out, m_ln_mix_g=m_ln_mix_g, m_ln_mix_b=m_ln_mix_b, m_mlp_w1=m_mlp_w1, m_mlp_w2=m_mlp_w2, m_ln_mlp_g=m_ln_mlp_g, m_ln_mlp_b=m_ln_mlp_b, v_ev_w_in=v_ev_w_in, v_ev_conv_w=v_ev_conv_w, v_ev_conv_b=v_ev_conv_b, v_ev_conv_ln_g=v_ev_conv_ln_g, v_ev_conv_ln_b=v_ev_conv_ln_b, v_hgrn_lb_logits=v_hgrn_lb_logits, v_ev_gnorm_g=v_ev_gnorm_g, v_ev_w_out=v_ev_w_out, v_od_w_in=v_od_w_in, v_fox_b_f=v_fox_b_f, v_rel_bias=v_rel_bias, v_od_w_out=v_od_w_out, v_ln_mix_g=v_ln_mix_g, v_ln_mix_b=v_ln_mix_b, v_mlp_w1=v_mlp_w1, v_mlp_w2=v_mlp_w2, v_ln_mlp_g=v_ln_mlp_g, v_ln_mlp_b=v_ln_mlp_b)
    weights = {n: given[n] for n in TWIN_WEIGHTS}
    shared = {n: given[n] for n in SHARED_INPUTS}
    per_example = {n: given[n] for n in ['x']}
    grad_fn = _jax.value_and_grad(_loss, argnums=(0, 1))

    def one_microbatch(ex, loss_target):
        ex = dict(ex)
        diff = ex.pop(TWIN_DIFF_INPUT)
        return grad_fn(weights, diff, {**shared, **ex}, loss_target)

    if N_MICROBATCH == 1:
        loss, (grad_w, grad_x) = one_microbatch(per_example, given["loss_target"])
    else:
        def body(carry, xs):
            loss_sum, grad_sum = carry
            l_k, (gw_k, gx_k) = one_microbatch(xs[0], xs[1])
            with _jax.named_scope("update"):
                return (loss_sum + l_k, _jax.tree.map(_jnp.add, grad_sum, gw_k)), gx_k

        init = (_jnp.zeros((), _jnp.float32), _jax.tree.map(_jnp.zeros_like, weights))
        (loss, grad_w), grad_x = _jax.lax.scan(body, init, (per_example, given["loss_target"]))
    with _jax.named_scope("update"):
        delta_w, new_m, new_v = {}, {}, {}
        for n in TWIN_WEIGHTS:
            delta_w[n], new_m[n], new_v[n] = _adamw(weights[n], grad_w[n], given["m_" + n], given["v_" + n])
    return (loss, grad_x, *[grad_w[n] for n in TWIN_WEIGHTS], *[delta_w[n] for n in TWIN_WEIGHTS],
            *[new_m[n] for n in TWIN_WEIGHTS], *[new_v[n] for n in TWIN_WEIGHTS])
```

```python
import functools

import jax
import jax.numpy as jnp
from jax import lax
from jax.experimental import pallas as pl
from jax.experimental.pallas import tpu as pltpu

F32 = jnp.float32
BF16 = jnp.bfloat16
MESH = pl.DeviceIdType.MESH

DEPTH = 2
ALPHA = (2 * DEPTH) ** 0.25
LN_EPS = 1e-5
HEADS = 8
DH = 128
CHUNK = 64
CONV_WIDTH = 31
HALO = 32
CA_LEFT = 8
CA_BLK = CA_LEFT * CHUNK
CA_WIN = (CA_LEFT + 2) * CHUNK
REL_CLIP = 256
REL_TABLE = (CHUNK - 1) + REL_CLIP + 1
NEG = -1e30

ADAM_LR = 0.001
ADAM_B1 = 0.9
ADAM_B2 = 0.999
ADAM_EPS = 1e-08
ADAM_WD = 0.01
ADAM_STEP = 10

VMEM_LIMIT = 48 * 1024 * 1024


def _cparams(sem):
    return pltpu.CompilerParams(dimension_semantics=sem, vmem_limit_bytes=VMEM_LIMIT)


def _pick(n, cands):
    for c in cands:
        if n % c == 0:
            return c
    return n


def _sigmoid(x):
    return 1.0 / (1.0 + jnp.exp(-x))


def _dot(a, b, dims):
    return lax.dot_general(a.astype(BF16), b.astype(BF16), (dims, ((), ())), preferred_element_type=F32)


def _nn(a, b):
    return _dot(a, b, ((1,), (0,)))


def _nt(a, b):
    return _dot(a, b, ((1,), (1,)))


def _tn(a, b):
    return _dot(a, b, ((0,), (0,)))


def _dot3(a, b, dims):
    a_hi = a.astype(BF16)
    b_hi = b.astype(BF16)
    a_lo = (a - a_hi.astype(F32)).astype(BF16)
    b_lo = (b - b_hi.astype(F32)).astype(BF16)
    dn = (dims, ((), ()))
    return (lax.dot_general(a_hi, b_hi, dn, preferred_element_type=F32)
            + (lax.dot_general(a_hi, b_lo, dn, preferred_element_type=F32)
               + lax.dot_general(a_lo, b_hi, dn, preferred_element_type=F32)))


def _nn3(a, b):
    return _dot3(a, b, ((1,), (0,)))


def _nt3(a, b):
    return _dot3(a, b, ((1,), (1,)))


def _tn3(a, b):
    return _dot3(a, b, ((0,), (0,)))


def _split3(x):
    hi = x.astype(BF16)
    r1 = x - hi.astype(F32)
    mid = r1.astype(BF16)
    lo = (r1 - mid.astype(F32)).astype(BF16)
    return hi, mid, lo


def _tri_matmul(tri, x, terms):
    parts = _split3(x)[:terms]
    acc = None
    for p in parts:
        t = lax.dot_general(tri, p, (((1,), (0,)), ((), ())), preferred_element_type=F32)
        acc = t if acc is None else acc + t
    return acc


def _tril(n, upper=False):
    r = lax.broadcasted_iota(jnp.int32, (n, n), 0)
    c = lax.broadcasted_iota(jnp.int32, (n, n), 1)
    m = (c >= r) if upper else (c <= r)
    return jnp.where(m, 1.0, 0.0).astype(BF16)


def _matmul(a, b, *, ta=False, tb=False, out_dtype=F32, epi=None, extra=None, scale=ALPHA, name):
    m = a.shape[1] if ta else a.shape[0]
    kd = a.shape[0] if ta else a.shape[1]
    n = b.shape[0] if tb else b.shape[1]
    bm = _pick(m, (1024, 512, 256, 128))
    bn = _pick(n, (1024, 768, 512, 256, 128))
    bk = _pick(kd, (512, 256, 128))
    nk = kd // bk
    n_out = 2 if epi == "relu2" else 1

    def body(*refs):
        a_ref, b_ref = refs[0], refs[1]
        e_ref = refs[2] if extra is not None else None
        outs = refs[2 + (extra is not None):2 + (extra is not None) + n_out]
        acc_ref = refs[-1]
        k = pl.program_id(2)

        @pl.when(k == 0)
        def _():
            acc_ref[...] = jnp.zeros_like(acc_ref)

        dims = ((0 if ta else 1,), (1 if tb else 0,))
        acc_ref[...] += _dot(a_ref[...], b_ref[...], dims)

        @pl.when(k == nk - 1)
        def _():
            r = acc_ref[...]
            if epi == "relu2":
                outs[0][...] = r.astype(out_dtype)
                outs[1][...] = jnp.square(jnp.maximum(r, 0.0)).astype(out_dtype)
            elif epi == "drelu2":
                outs[0][...] = (r * (2.0 * jnp.maximum(e_ref[...].astype(F32), 0.0))).astype(out_dtype)
            elif epi == "add":
                outs[0][...] = (r + scale * e_ref[...].astype(F32)).astype(out_dtype)
            else:
                outs[0][...] = r.astype(out_dtype)

    a_spec = pl.BlockSpec((bk, bm), lambda i, j, k: (k, i)) if ta else pl.BlockSpec((bm, bk), lambda i, j, k: (i, k))
    b_spec = pl.BlockSpec((bn, bk), lambda i, j, k: (j, k)) if tb else pl.BlockSpec((bk, bn), lambda i, j, k: (k, j))
    o_spec = pl.BlockSpec((bm, bn), lambda i, j, k: (i, j))
    in_specs = [a_spec, b_spec] + ([o_spec] if extra is not None else [])
    args = (a, b) + ((extra,) if extra is not None else ())
    o_shape = jax.ShapeDtypeStruct((m, n), out_dtype)
    res = pl.pallas_call(
        body, grid=(m // bm, n // bn, nk), in_specs=in_specs,
        out_specs=[o_spec] * n_out, out_shape=[o_shape] * n_out,
        scratch_shapes=[pltpu.VMEM((bm, bn), F32)],
        compiler_params=_cparams(("parallel", "parallel", "arbitrary")), name=name)(*args)
    return res if n_out == 2 else res[0]


def _ln_fwd(x, mix, g, b, name):
    s, d = x.shape
    br = _pick(s, (256, 128, 64, 8))

    def body(x_ref, m_ref, g_ref, b_ref, r_ref, y_ref, yb_ref):
        r = ALPHA * x_ref[...] + m_ref[...]
        mu = jnp.mean(r, axis=-1, keepdims=True)
        dlt = r - mu
        var = jnp.mean(dlt * dlt, axis=-1, keepdims=True)
        y = dlt * lax.rsqrt(var + LN_EPS) * g_ref[...] + b_ref[...]
        r_ref[...] = r
        y_ref[...] = y
        yb_ref[...] = y.astype(BF16)

    row = pl.BlockSpec((br, d), lambda i: (i, 0))
    vec = pl.BlockSpec((1, d), lambda i: (0, 0))
    return pl.pallas_call(
        body, grid=(s // br,), in_specs=[row, row, vec, vec], out_specs=[row, row, row],
        out_shape=[jax.ShapeDtypeStruct((s, d), F32), jax.ShapeDtypeStruct((s, d), F32),
                   jax.ShapeDtypeStruct((s, d), BF16)],
        compiler_params=_cparams(("parallel",)), name=name)(x, mix, g.reshape(1, d), b.reshape(1, d))


def _ln_bwd(dy, r, g, name):
    s, d = r.shape
    br = _pick(s, (256, 128, 64, 8))

    def body(dy_ref, r_ref, g_ref, dr_ref, drb_ref, dg_ref, db_ref):
        @pl.when(pl.program_id(0) == 0)
        def _():
            dg_ref[...] = jnp.zeros_like(dg_ref)
            db_ref[...] = jnp.zeros_like(db_ref)

        rv = r_ref[...]
        dyv = dy_ref[...]
        mu = jnp.mean(rv, axis=-1, keepdims=True)
        dlt = rv - mu
        var = jnp.mean(dlt * dlt, axis=-1, keepdims=True)
        rstd = lax.rsqrt(var + LN_EPS)
        xhat = dlt * rstd
        dxh = dyv * g_ref[...]
        m1 = jnp.mean(dxh, axis=-1, keepdims=True)
        m2 = jnp.mean(dxh * xhat, axis=-1, keepdims=True)
        dr = rstd * (dxh - m1 - xhat * m2)
        dr_ref[...] = dr
        drb_ref[...] = dr.astype(BF16)
        dg_ref[...] += jnp.sum(dyv * xhat, axis=0, keepdims=True)
        db_ref[...] += jnp.sum(dyv, axis=0, keepdims=True)

    row = pl.BlockSpec((br, d), lambda i: (i, 0))
    vec = pl.BlockSpec((1, d), lambda i: (0, 0))
    dr, drb, dg, db = pl.pallas_call(
        body, grid=(s // br,), in_specs=[row, row, vec], out_specs=[row, row, vec, vec],
        out_shape=[jax.ShapeDtypeStruct((s, d), F32), jax.ShapeDtypeStruct((s, d), BF16),
                   jax.ShapeDtypeStruct((1, d), F32), jax.ShapeDtypeStruct((1, d), F32)],
        compiler_params=_cparams(("arbitrary",)), name=name)(dy, r, g.reshape(1, d))
    return dr, drb, dg[0], db[0]


def _loss_head(y, tgt, name):
    s, d = y.shape
    br = _pick(s, (256, 128, 64, 8))

    def body(y_ref, t_ref, dy_ref, l_ref):
        @pl.when(pl.program_id(0) == 0)
        def _():
            l_ref[...] = jnp.zeros_like(l_ref)

        e = y_ref[...] - t_ref[...]
        dy_ref[...] = e * (1.0 / d)
        rows = jnp.sum(e * e, axis=-1, keepdims=True) * (0.5 / d)
        l_ref[...] += jnp.sum(rows, axis=0, keepdims=True)

    row = pl.BlockSpec((br, d), lambda i: (i, 0))
    dy, l = pl.pallas_call(
        body, grid=(s // br,), in_specs=[row, row],
        out_specs=[row, pl.BlockSpec((1, 1), lambda i: (0, 0))],
        out_shape=[jax.ShapeDtypeStruct((s, d), F32), jax.ShapeDtypeStruct((1, 1), F32)],
        compiler_params=_cparams(("arbitrary",)), name=name)(y, tgt)
    return dy, l[0, 0]


def _conv_recompute(i, a_ref, gt_ref, ah_ref, gh_ref, w_ref, cb_ref, hext_ref, tt):
    h = a_ref[...] * _sigmoid(gt_ref[...])
    hh = ah_ref[...] * _sigmoid(gh_ref[...])
    hh = jnp.where(i > 0, hh, 0.0)
    hext_ref[0:HALO, :] = hh
    hext_ref[HALO:HALO + tt, :] = h
    acc = jnp.zeros_like(h) + cb_ref[...]
    off = HALO - (CONV_WIDTH - 1)
    for j in range(CONV_WIDTH):
        acc = acc + w_ref[j:j + 1, :] * hext_ref[pl.ds(off + j, tt), :]
    mu = jnp.mean(acc, axis=-1, keepdims=True)
    dlt = acc - mu
    var = jnp.mean(dlt * dlt, axis=-1, keepdims=True)
    rstd = lax.rsqrt(var + LN_EPS)
    return dlt * rstd, rstd


def _conv_specs(tt, cc, s):
    nh = tt // HALO
    cur = lambda cb: pl.BlockSpec((tt, cc), lambda i: (i, cb))
    prev = lambda cb: pl.BlockSpec((HALO, cc), lambda i: (jnp.maximum(i * nh - 1, 0), cb))
    vec = pl.BlockSpec((1, cc), lambda i: (0, 0))
    wsp = pl.BlockSpec((HALO, cc), lambda i: (0, 0))
    return cur, prev, vec, wsp


def _pad_conv_w(w):
    return jnp.concatenate([w, jnp.zeros((HALO - CONV_WIDTH, w.shape[1]), F32)], axis=0)


def _conv_fwd(u, w, cb, lg, lb, name):
    s = u.shape[0]
    cc = w.shape[1]
    tt = _pick(s, (256, 128, 64))
    cur, prev, vec, wsp = _conv_specs(tt, cc, s)

    def body(a_ref, gt_ref, ah_ref, gh_ref, w_ref, cb_ref, lg_ref, lb_ref, o_ref, hext_ref):
        xhat, _ = _conv_recompute(pl.program_id(0), a_ref, gt_ref, ah_ref, gh_ref, w_ref, cb_ref, hext_ref, tt)
        nrm = xhat * lg_ref[...] + lb_ref[...]
        o_ref[...] = nrm * _sigmoid(nrm)

    return pl.pallas_call(
        body, grid=(s // tt,), in_specs=[cur(0), cur(1), prev(0), prev(1), wsp, vec, vec, vec],
        out_specs=pl.BlockSpec((tt, cc), lambda i: (i, 0)), out_shape=jax.ShapeDtypeStruct((s, cc), F32),
        scratch_shapes=[pltpu.VMEM((tt + HALO, cc), F32)],
        compiler_params=_cparams(("parallel",)), name=name)(
            u, u, u, u, _pad_conv_w(w), cb.reshape(1, cc), lg.reshape(1, cc), lb.reshape(1, cc))


def _conv_bwd_params(u, dout, w, cb, lg, lb, name):
    s = u.shape[0]
    cc = w.shape[1]
    tt = _pick(s, (256, 128, 64))
    cur, prev, vec, wsp = _conv_specs(tt, cc, s)

    def body(a_ref, gt_ref, ah_ref, gh_ref, w_ref, cb_ref, lg_ref, lb_ref, do_ref,
             dc_ref, dw_ref, dcb_ref, dlg_ref, dlb_ref, hext_ref):
        i = pl.program_id(0)

        @pl.when(i == 0)
        def _():
            dw_ref[...] = jnp.zeros_like(dw_ref)
            dcb_ref[...] = jnp.zeros_like(dcb_ref)
            dlg_ref[...] = jnp.zeros_like(dlg_ref)
            dlb_ref[...] = jnp.zeros_like(dlb_ref)

        xhat, rstd = _conv_recompute(i, a_ref, gt_ref, ah_ref, gh_ref, w_ref, cb_ref, hext_ref, tt)
        nrm = xhat * lg_ref[...] + lb_ref[...]
        sg = _sigmoid(nrm)
        dn = do_ref[...] * (sg * (1.0 + nrm * (1.0 - sg)))
        dxh = dn * lg_ref[...]
        m1 = jnp.mean(dxh, axis=-1, keepdims=True)
        m2 = jnp.mean(dxh * xhat, axis=-1, keepdims=True)
        dc = rstd * (dxh - m1 - xhat * m2)
        dc_ref[...] = dc
        dlg_ref[...] += jnp.sum(dn * xhat, axis=0, keepdims=True)
        dlb_ref[...] += jnp.sum(dn, axis=0, keepdims=True)
        dcb_ref[...] += jnp.sum(dc, axis=0, keepdims=True)
        off = HALO - (CONV_WIDTH - 1)
        for j in range(CONV_WIDTH):
            dw_ref[j:j + 1, :] += jnp.sum(dc * hext_ref[pl.ds(off + j, tt), :], axis=0, keepdims=True)

    dcol = pl.BlockSpec((tt, cc), lambda i: (i, 0))
    dc, dw, dcb, dlg, dlb = pl.pallas_call(
        body, grid=(s // tt,), in_specs=[cur(0), cur(1), prev(0), prev(1), wsp, vec, vec, vec, dcol],
        out_specs=[dcol, wsp, vec, vec, vec],
        out_shape=[jax.ShapeDtypeStruct((s, cc), F32), jax.ShapeDtypeStruct((HALO, cc), F32)]
        + [jax.ShapeDtypeStruct((1, cc), F32)] * 3,
        scratch_shapes=[pltpu.VMEM((tt + HALO, cc), F32)],
        compiler_params=_cparams(("arbitrary",)), name=name)(
            u, u, u, u, _pad_conv_w(w), cb.reshape(1, cc), lg.reshape(1, cc), lb.reshape(1, cc), dout)
    return dc, dw[:CONV_WIDTH], dcb[0], dlg[0], dlb[0]


def _conv_bwd_input(u, dc, w, name):
    s = u.shape[0]
    cc = w.shape[1]
    tt = _pick(s, (256, 128, 64))
    nh = tt // HALO
    nlast = s // HALO - 1
    cur = lambda cb: pl.BlockSpec((tt, cc), lambda i: (i, cb))
    nxt = pl.BlockSpec((HALO, cc), lambda i: (jnp.minimum((i + 1) * nh, nlast), 0))
    wsp = pl.BlockSpec((HALO, cc), lambda i: (0, 0))
    nblk = s // tt

    def body(a_ref, gt_ref, dc_ref, dn_ref, w_ref, da_ref, dg_ref, ext_ref):
        i = pl.program_id(0)
        ext_ref[0:tt, :] = dc_ref[...]
        ext_ref[tt:tt + HALO, :] = jnp.where(i < nblk - 1, dn_ref[...], 0.0)
        dh = jnp.zeros((tt, cc), F32)
        for j in range(CONV_WIDTH):
            dh = dh + w_ref[j:j + 1, :] * ext_ref[pl.ds(CONV_WIDTH - 1 - j, tt), :]
        a = a_ref[...]
        sg = _sigmoid(gt_ref[...])
        da_ref[...] = (dh * sg).astype(BF16)
        dg_ref[...] = (dh * a * sg * (1.0 - sg)).astype(BF16)

    ocol = pl.BlockSpec((tt, cc), lambda i: (i, 0))
    return pl.pallas_call(
        body, grid=(nblk,), in_specs=[cur(0), cur(1), ocol, nxt, wsp], out_specs=[ocol, ocol],
        out_shape=[jax.ShapeDtypeStruct((s, cc), BF16)] * 2,
        scratch_shapes=[pltpu.VMEM((tt + HALO, cc), F32)],
        compiler_params=_cparams(("parallel",)), name=name)(u, u, dc, dc, _pad_conv_w(w))


def _hgrn_gates(hq, hf, lb):
    sg = _sigmoid(hf)
    f = lb + (1.0 - lb) * sg
    lf = jnp.log(f)
    big_l = _tri_matmul(_tril(CHUNK), lf, 3)
    l_end = jnp.sum(lf, axis=0, keepdims=True)
    l_mid = jnp.sum(lf[0:CHUNK // 2, :], axis=0, keepdims=True)
    sq = _sigmoid(hq)
    q = hq * sq
    return sg, f, 1.0 - f, big_l, l_end, l_mid, sq, q


def _causal_mask(n):
    r = lax.broadcasted_iota(jnp.int32, (n, n), 0)
    c = lax.broadcasted_iota(jnp.int32, (n, n), 1)
    return c <= r


def _hgrn_fwd(u, lb, gg, name):
    s = u.shape[0]
    w = HEADS * DH
    nch = s // CHUNK

    def body(q_ref, f_ref, i_ref, g_ref, lb_ref, gg_ref, o_ref, out_ref, st_ref, state):
        @pl.when(pl.program_id(0) == 0)
        def _():
            state[...] = jnp.zeros_like(state)

        mask = _causal_mask(CHUNK)
        for hd in range(HEADS):
            sl = slice(hd * DH, (hd + 1) * DH)
            _, _, kk, big_l, l_end, l_mid, _, q = _hgrn_gates(q_ref[:, sl], f_ref[:, sl], lb_ref[:, sl])
            v = i_ref[:, sl]
            qs = q * jnp.exp(big_l - l_mid)
            ks = kk * jnp.exp(l_mid - big_l)
            att = jnp.where(mask, _nt3(qs, ks), 0.0)
            st0 = state[hd]
            st_ref[0, hd] = st0
            o = _nn3(att, v) + _nt3(q * jnp.exp(big_l), st0)
            state[hd] = st0 * jnp.exp(l_end) + _tn3(v, kk * jnp.exp(l_end - big_l))
            o_ref[:, sl] = o
            on = o * lax.rsqrt(jnp.mean(o * o, axis=-1, keepdims=True) + LN_EPS)
            gv = g_ref[:, sl]
            out_ref[:, sl] = on * gg_ref[:, sl] * (gv * _sigmoid(gv))

    col = lambda cb: pl.BlockSpec((CHUNK, w), lambda i: (i, cb))
    vec = pl.BlockSpec((1, w), lambda i: (0, 0))
    ocol = pl.BlockSpec((CHUNK, w), lambda i: (i, 0))
    return pl.pallas_call(
        body, grid=(nch,), in_specs=[col(2), col(3), col(4), col(5), vec, vec],
        out_specs=[ocol, ocol, pl.BlockSpec((1, HEADS, DH, DH), lambda i: (i, 0, 0, 0))],
        out_shape=[jax.ShapeDtypeStruct((s, w), F32), jax.ShapeDtypeStruct((s, w), F32),
                   jax.ShapeDtypeStruct((nch, HEADS, DH, DH), F32)],
        scratch_shapes=[pltpu.VMEM((HEADS, DH, DH), F32)],
        compiler_params=_cparams(("arbitrary",)), name=name)(u, u, u, u, lb.reshape(1, w), gg.reshape(1, w))


def _hgrn_bwd(u, o_raw, states, dout, lb, gg, name):
    s = u.shape[0]
    w = HEADS * DH
    nch = s // CHUNK

    def body(q_ref, f_ref, i_ref, g_ref, o_ref, st_ref, do_ref, lb_ref, gg_ref,
             dq_ref, df_ref, di_ref, dg_ref, dlb_ref, dgg_ref, dstate):
        @pl.when(pl.program_id(0) == 0)
        def _():
            dstate[...] = jnp.zeros_like(dstate)
            dlb_ref[...] = jnp.zeros_like(dlb_ref)
            dgg_ref[...] = jnp.zeros_like(dgg_ref)

        mask = _causal_mask(CHUNK)
        last_row = lax.broadcasted_iota(jnp.int32, (CHUNK, DH), 0) == CHUNK - 1
        tri_up = _tril(CHUNK, upper=True)
        for hd in range(HEADS):
            sl = slice(hd * DH, (hd + 1) * DH)
            hq = q_ref[:, sl]
            lbv = lb_ref[:, sl]
            sg, f, kk, big_l, l_end, l_mid, sq, q = _hgrn_gates(hq, f_ref[:, sl], lbv)
            v = i_ref[:, sl]
            e_l = jnp.exp(big_l)
            e_qm = jnp.exp(big_l - l_mid)
            e_km = jnp.exp(l_mid - big_l)
            e_ke = jnp.exp(l_end - big_l)
            e_end = jnp.exp(l_end)
            qs = q * e_qm
            ks = kk * e_km
            qe = q * e_l
            ke = kk * e_ke
            att = jnp.where(mask, _nt3(qs, ks), 0.0)
            st0 = st_ref[0, hd]
            dst1 = dstate[hd]
            o = o_ref[:, sl]
            rinv = lax.rsqrt(jnp.mean(o * o, axis=-1, keepdims=True) + LN_EPS)
            on = o * rinv
            gv = g_ref[:, sl]
            sgg = _sigmoid(gv)
            gsil = gv * sgg
            ggv = gg_ref[:, sl]
            dov = do_ref[:, sl]
            don = dov * ggv * gsil
            dg_ref[:, sl] = (dov * on * ggv * (sgg * (1.0 + gv * (1.0 - sgg)))).astype(BF16)
            dgg_ref[:, sl] += jnp.sum(dov * on * gsil, axis=0, keepdims=True)
            do = rinv * (don - on * jnp.mean(don * on, axis=-1, keepdims=True))
            datt = jnp.where(mask, _nt3(do, v), 0.0)
            dv = _tn3(att, do) + _nt3(ke, dst1)
            dqs = _nn3(datt, ks)
            dks = _tn3(datt, qs)
            dqe = _nn3(do, st0)
            dke = _nn3(v, dst1)
            dq = dqs * e_qm + dqe * e_l
            dk = dks * e_km + dke * e_ke
            dke_ke = dke * ke
            dl = dqs * qs - dks * ks + dqe * qe - dke_ke
            dl_end = jnp.sum(dke_ke, axis=0, keepdims=True) + jnp.sum(dst1 * st0, axis=0, keepdims=True) * e_end
            dl = dl + jnp.where(last_row, dl_end, 0.0)
            dlf = _tri_matmul(tri_up, dl, 2)
            dfv = dlf / f - dk
            df_ref[:, sl] = (dfv * (1.0 - lbv) * sg * (1.0 - sg)).astype(BF16)
            dlb_ref[:, sl] += jnp.sum(dfv * (1.0 - sg), axis=0, keepdims=True)
            dq_ref[:, sl] = (dq * (sq * (1.0 + hq * (1.0 - sq)))).astype(BF16)
            di_ref[:, sl] = dv.astype(BF16)
            dstate[hd] = dst1 * e_end + _tn3(do, qe)

    rev = lambda i: nch - 1 - i
    col = lambda cb: pl.BlockSpec((CHUNK, w), lambda i: (rev(i), cb))
    vec = pl.BlockSpec((1, w), lambda i: (0, 0))
    ocol = pl.BlockSpec((CHUNK, w), lambda i: (rev(i), 0))
    res = pl.pallas_call(
        body, grid=(nch,),
        in_specs=[col(2), col(3), col(4), col(5), ocol,
                  pl.BlockSpec((1, HEADS, DH, DH), lambda i: (rev(i), 0, 0, 0)), col(1), vec, vec],
        out_specs=[ocol, ocol, ocol, ocol, vec, vec],
        out_shape=[jax.ShapeDtypeStruct((s, w), BF16)] * 4 + [jax.ShapeDtypeStruct((1, w), F32)] * 2,
        scratch_shapes=[pltpu.VMEM((HEADS, DH, DH), F32)],
        compiler_params=_cparams(("arbitrary",)), name=name)(
            u, u, u, u, o_raw, states, dout, lb.reshape(1, w), gg.reshape(1, w))
    return res[0], res[1], res[2], res[3], res[4][0], res[5][0]


def _log_sigmoid(x):
    return jnp.minimum(x, 0.0) - jnp.log(1.0 + jnp.exp(-jnp.abs(x)))


def _fgate_fwd(cf, bf, name):
    s, wl = cf.shape
    tb = _pick(s, (512, 256, 128, 64))

    def body(c_ref, b_ref, f_ref, carry):
        @pl.when(pl.program_id(0) == 0)
        def _():
            carry[...] = jnp.zeros_like(carry)

        ls = _log_sigmoid(c_ref[...] + b_ref[...])
        f_ref[...] = _tri_matmul(_tril(tb), ls, 3) + carry[...]
        carry[...] += jnp.sum(ls, axis=0, keepdims=True)

    return pl.pallas_call(
        body, grid=(s // tb,), in_specs=[pl.BlockSpec((tb, wl), lambda i: (i, 0)), pl.BlockSpec((1, wl), lambda i: (0, 0))],
        out_specs=pl.BlockSpec((tb, wl), lambda i: (i, 0)), out_shape=jax.ShapeDtypeStruct((s, wl), F32),
        scratch_shapes=[pltpu.VMEM((1, wl), F32)],
        compiler_params=_cparams(("arbitrary",)), name=name)(cf, bf)


def _fgate_bwd(dF, cf, bf, name):
    s, wl = cf.shape
    tb = _pick(s, (512, 256, 128, 64))
    nb = s // tb

    def body(d_ref, c_ref, b_ref, dc_ref, db_ref, carry):
        @pl.when(pl.program_id(0) == 0)
        def _():
            carry[...] = jnp.zeros_like(carry)
            db_ref[...] = jnp.zeros_like(db_ref)

        dv = d_ref[...]
        dls = _tri_matmul(_tril(tb, upper=True), dv, 3) + carry[...]
        carry[...] += jnp.sum(dv, axis=0, keepdims=True)
        dc = dls * (1.0 - _sigmoid(c_ref[...] + b_ref[...]))
        dc_ref[...] = dc.astype(BF16)
        db_ref[...] += jnp.sum(dc, axis=0, keepdims=True)

    blk = pl.BlockSpec((tb, wl), lambda i: (nb - 1 - i, 0))
    vec = pl.BlockSpec((1, wl), lambda i: (0, 0))
    dc, db = pl.pallas_call(
        body, grid=(nb,), in_specs=[blk, blk, vec], out_specs=[blk, vec],
        out_shape=[jax.ShapeDtypeStruct((s, wl), BF16), jax.ShapeDtypeStruct((1, wl), F32)],
        scratch_shapes=[pltpu.VMEM((1, wl), F32)],
        compiler_params=_cparams(("arbitrary",)), name=name)(dF, cf, bf)
    return dc, db[0]


def _fox_scores(q, k, fq, fk, diag):
    sc = _nt(q, k) * (DH ** -0.5) + fq - fk
    t = sc.shape[0]
    r = lax.broadcasted_iota(jnp.int32, sc.shape, 0)
    c = lax.broadcasted_iota(jnp.int32, sc.shape, 1)
    return jnp.where(jnp.logical_or(jnp.logical_not(diag), c <= r), sc, NEG)


def _fox_fwd(u, f_col, f_row, name):
    s = u.shape[0]
    t = _pick(s, (512, 256, 128))
    nb = s // t

    def body(q_ref, k_ref, v_ref, fq_ref, fk_ref, o_ref, lse_ref, m_sc, l_sc, acc):
        i, j = pl.program_id(1), pl.program_id(2)

        @pl.when(j == 0)
        def _():
            m_sc[...] = jnp.full_like(m_sc, NEG)
            l_sc[...] = jnp.zeros_like(l_sc)
            acc[...] = jnp.zeros_like(acc)

        @pl.when(j <= i)
        def _():
            sc = _fox_scores(q_ref[...], k_ref[...], fq_ref[0], fk_ref[0], j == i)
            m_new = jnp.maximum(m_sc[...], jnp.max(sc, axis=-1, keepdims=True))
            a = jnp.exp(m_sc[...] - m_new)
            p = jnp.exp(sc - m_new)
            l_sc[...] = a * l_sc[...] + jnp.sum(p, axis=-1, keepdims=True)
            acc[...] = a * acc[...] + _nn(p, v_ref[...])
            m_sc[...] = m_new

        @pl.when(j == i)
        def _():
            o_ref[...] = acc[...] / l_sc[...]
            lse_ref[0] = m_sc[...] + jnp.log(l_sc[...])

    qs = pl.BlockSpec((t, DH), lambda h, i, j: (i, h))
    ks = lambda base: pl.BlockSpec((t, DH), lambda h, i, j: (jnp.minimum(j, i), base + h))
    return pl.pallas_call(
        body, grid=(HEADS, nb, nb),
        in_specs=[qs, ks(HEADS), ks(2 * HEADS),
                  pl.BlockSpec((1, t, 1), lambda h, i, j: (h, i, 0)),
                  pl.BlockSpec((1, 1, t), lambda h, i, j: (h, 0, jnp.minimum(j, i)))],
        out_specs=[pl.BlockSpec((t, DH), lambda h, i, j: (i, h)), pl.BlockSpec((1, t, 1), lambda h, i, j: (h, i, 0))],
        out_shape=[jax.ShapeDtypeStruct((s, HEADS * DH), F32), jax.ShapeDtypeStruct((HEADS, s, 1), F32)],
        scratch_shapes=[pltpu.VMEM((t, 1), F32), pltpu.VMEM((t, 1), F32), pltpu.VMEM((t, DH), F32)],
        compiler_params=_cparams(("parallel", "parallel", "arbitrary")), name=name)(u, u, u, f_col, f_row)


def _fox_bwd_kv(u, o, lse, dout, f_col, f_row, name):
    s = u.shape[0]
    t = _pick(s, (512, 256, 128))
    nb = s // t

    def body(q_ref, k_ref, v_ref, o_ref, do_ref, lse_ref, fq_ref, fk_ref, dk_ref, dv_ref, dfk_ref, dk_acc, dv_acc, df_acc):
        j, i = pl.program_id(1), pl.program_id(2)

        @pl.when(i == 0)
        def _():
            dk_acc[...] = jnp.zeros_like(dk_acc)
            dv_acc[...] = jnp.zeros_like(dv_acc)
            df_acc[...] = jnp.zeros_like(df_acc)

        @pl.when(i >= j)
        def _():
            q = q_ref[...]
            do = do_ref[...]
            sc = _fox_scores(q, k_ref[...], fq_ref[0], fk_ref[0], j == i)
            p = jnp.exp(sc - lse_ref[0])
            delta = jnp.sum(do * o_ref[...], axis=-1, keepdims=True)
            dv_acc[...] += _tn(p, do)
            ds = p * (_nt(do, v_ref[...]) - delta)
            dk_acc[...] += _tn(ds, q) * (DH ** -0.5)
            df_acc[...] -= jnp.sum(ds, axis=0, keepdims=True)

        @pl.when(i == nb - 1)
        def _():
            dk_ref[...] = dk_acc[...].astype(BF16)
            dv_ref[...] = dv_acc[...].astype(BF16)
            dfk_ref[0] = df_acc[...]

    qi = lambda j, i: jnp.maximum(i, j)
    qs = lambda base: pl.BlockSpec((t, DH), lambda h, j, i: (qi(j, i), base + h))
    ks = lambda base: pl.BlockSpec((t, DH), lambda h, j, i: (j, base + h))
    return pl.pallas_call(
        body, grid=(HEADS, nb, nb),
        in_specs=[qs(0), ks(HEADS), ks(2 * HEADS), qs(0), qs(0),
                  pl.BlockSpec((1, t, 1), lambda h, j, i: (h, qi(j, i), 0)),
                  pl.BlockSpec((1, t, 1), lambda h, j, i: (h, qi(j, i), 0)),
                  pl.BlockSpec((1, 1, t), lambda h, j, i: (h, 0, j))],
        out_specs=[pl.BlockSpec((t, DH), lambda h, j, i: (j, h)), pl.BlockSpec((t, DH), lambda h, j, i: (j, h)),
                   pl.BlockSpec((1, 1, t), lambda h, j, i: (h, 0, j))],
        out_shape=[jax.ShapeDtypeStruct((s, HEADS * DH), BF16)] * 2 + [jax.ShapeDtypeStruct((HEADS, 1, s), F32)],
        scratch_shapes=[pltpu.VMEM((t, DH), F32), pltpu.VMEM((t, DH), F32), pltpu.VMEM((1, t), F32)],
        compiler_params=_cparams(("parallel", "parallel", "arbitrary")), name=name)(
            u, u, u, o, dout, lse, f_col, f_row)


def _fox_bwd_q(u, o, lse, dout, f_col, f_row, name):
    s = u.shape[0]
    t = _pick(s, (512, 256, 128))
    nb = s // t

    def body(q_ref, k_ref, v_ref, o_ref, do_ref, lse_ref, fq_ref, fk_ref, dq_ref, dfq_ref, dq_acc, df_acc):
        i, j = pl.program_id(1), pl.program_id(2)

        @pl.when(j == 0)
        def _():
            dq_acc[...] = jnp.zeros_like(dq_acc)
            df_acc[...] = jnp.zeros_like(df_acc)

        @pl.when(j <= i)
        def _():
            do = do_ref[...]
            k = k_ref[...]
            sc = _fox_scores(q_ref[...], k, fq_ref[0], fk_ref[0], j == i)
            p = jnp.exp(sc - lse_ref[0])
            delta = jnp.sum(do * o_ref[...], axis=-1, keepdims=True)
            ds = p * (_nt(do, v_ref[...]) - delta)
            dq_acc[...] += _nn(ds, k) * (DH ** -0.5)
            df_acc[...] += jnp.sum(ds, axis=-1, keepdims=True)

        @pl.when(j == i)
        def _():
            dq_ref[...] = dq_acc[...].astype(BF16)
            dfq_ref[0] = df_acc[...]

    qs = lambda base: pl.BlockSpec((t, DH), lambda h, i, j: (i, base + h))
    ks = lambda base: pl.BlockSpec((t, DH), lambda h, i, j: (jnp.minimum(j, i), base + h))
    return pl.pallas_call(
        body, grid=(HEADS, nb, nb),
        in_specs=[qs(0), ks(HEADS), ks(2 * HEADS), qs(0), qs(0),
                  pl.BlockSpec((1, t, 1), lambda h, i, j: (h, i, 0)),
                  pl.BlockSpec((1, t, 1), lambda h, i, j: (h, i, 0)),
                  pl.BlockSpec((1, 1, t), lambda h, i, j: (h, 0, jnp.minimum(j, i)))],
        out_specs=[pl.BlockSpec((t, DH), lambda h, i, j: (i, h)), pl.BlockSpec((1, t, 1), lambda h, i, j: (h, i, 0))],
        out_shape=[jax.ShapeDtypeStruct((s, HEADS * DH), BF16), jax.ShapeDtypeStruct((HEADS, s, 1), F32)],
        scratch_shapes=[pltpu.VMEM((t, DH), F32), pltpu.VMEM((t, 1), F32)],
        compiler_params=_cparams(("parallel", "parallel", "arbitrary")), name=name)(
            u, u, u, o, dout, lse, f_col, f_row)


def _ca_bias(rel_bias):
    band = (CA_LEFT + 1) * CHUNK
    n_clip = band - REL_CLIP
    gv = jnp.concatenate([jnp.broadcast_to(rel_bias[:, REL_TABLE - 1:], (rel_bias.shape[0], n_clip)),
                          rel_bias[:, REL_TABLE - 2::-1]], axis=1)
    rows = [gv[:, CHUNK - 1 - qi:CHUNK - 1 - qi + band] for qi in range(CHUNK)]
    b = jnp.stack(rows, axis=1)
    return jnp.concatenate([b, jnp.full((b.shape[0], CHUNK, CA_WIN - band), NEG, F32)], axis=2)


def _ca_bias_grad(db):
    band = (CA_LEFT + 1) * CHUNK
    n_clip = band - REL_CLIP
    dgv = None
    for qi in range(CHUNK):
        t = jnp.pad(db[:, qi, :band], ((0, 0), (CHUNK - 1 - qi, qi)))
        dgv = t if dgv is None else dgv + t
    return jnp.concatenate([dgv[:, n_clip:][:, ::-1], jnp.sum(dgv[:, :n_clip], axis=1, keepdims=True)], axis=1)


def _ca_specs():
    cur = lambda base: pl.BlockSpec((CA_BLK, DH), lambda h, i: (i, base + h))
    prev = lambda base: pl.BlockSpec((CA_BLK, DH), lambda h, i: (jnp.maximum(i - 1, 0), base + h))
    bias = pl.BlockSpec((1, CHUNK, CA_WIN), lambda h, i: (h, 0, 0))
    return cur, prev, bias


def _ca_fill(kcat, vcat, kp_ref, kc_ref, vp_ref, vc_ref):
    kcat[0:CA_BLK, :] = kp_ref[...]
    kcat[CA_BLK:2 * CA_BLK, :] = kc_ref[...]
    kcat[2 * CA_BLK:, :] = jnp.zeros((CHUNK, DH), F32)
    vcat[0:CA_BLK, :] = vp_ref[...]
    vcat[CA_BLK:2 * CA_BLK, :] = vc_ref[...]
    vcat[2 * CA_BLK:, :] = jnp.zeros((CHUNK, DH), F32)


def _ca_probs(i, c, q, kw, bias):
    sc = _nt(q, kw) * (DH ** -0.5) + bias
    col = lax.broadcasted_iota(jnp.int32, sc.shape, 1)
    sc = jnp.where((i - 1) * CA_BLK + c * CHUNK + col >= 0, sc, NEG)
    p = jnp.exp(sc - jnp.max(sc, axis=-1, keepdims=True))
    return p / jnp.sum(p, axis=-1, keepdims=True)


def _ca_fwd(u, bias, name):
    s = u.shape[0]
    cur, prev, bsp = _ca_specs()

    def body(q_ref, kp_ref, kc_ref, vp_ref, vc_ref, b_ref, o_ref, kcat, vcat):
        i = pl.program_id(1)
        _ca_fill(kcat, vcat, kp_ref, kc_ref, vp_ref, vc_ref)
        for c in range(CA_LEFT):
            rows = slice(c * CHUNK, (c + 1) * CHUNK)
            win = slice(c * CHUNK, c * CHUNK + CA_WIN)
            p = _ca_probs(i, c, q_ref[rows, :], kcat[win, :], b_ref[0])
            o_ref[rows, :] = _nn(p, vcat[win, :])

    return pl.pallas_call(
        body, grid=(HEADS, s // CA_BLK),
        in_specs=[cur(3 * HEADS), prev(4 * HEADS), cur(4 * HEADS), prev(5 * HEADS), cur(5 * HEADS), bsp],
        out_specs=pl.BlockSpec((CA_BLK, DH), lambda h, i: (i, h)),
        out_shape=jax.ShapeDtypeStruct((s, HEADS * DH), F32),
        scratch_shapes=[pltpu.VMEM((2 * CA_BLK + CHUNK, DH), F32)] * 2,
        compiler_params=_cparams(("parallel", "parallel")), name=name)(u, u, u, u, u, bias)


def _ca_bwd(u, bias, dout, name):
    s = u.shape[0]
    cur, prev, bsp = _ca_specs()
    rows_cat = 2 * CA_BLK + CHUNK

    def body(q_ref, kp_ref, kc_ref, vp_ref, vc_ref, b_ref, do_ref,
             dq_ref, dka_ref, dkb_ref, dva_ref, dvb_ref, db_ref, kcat, vcat, dkcat, dvcat):
        i = pl.program_id(1)

        @pl.when(i == 0)
        def _():
            db_ref[...] = jnp.zeros_like(db_ref)

        _ca_fill(kcat, vcat, kp_ref, kc_ref, vp_ref, vc_ref)
        dkcat[...] = jnp.zeros_like(dkcat)
        dvcat[...] = jnp.zeros_like(dvcat)
        for c in range(CA_LEFT):
            rows = slice(c * CHUNK, (c + 1) * CHUNK)
            win = slice(c * CHUNK, c * CHUNK + CA_WIN)
            q = q_ref[rows, :]
            kw = kcat[win, :]
            vw = vcat[win, :]
            do = do_ref[rows, :]
            p = _ca_probs(i, c, q, kw, b_ref[0])
            dp = _nt(do, vw)
            ds = p * (dp - jnp.sum(p * dp, axis=-1, keepdims=True))
            dq_ref[rows, :] = (_nn(ds, kw) * (DH ** -0.5)).astype(BF16)
            dkcat[win, :] += _tn(ds, q) * (DH ** -0.5)
            dvcat[win, :] += _tn(p, do)
            db_ref[0] += ds
        dkb_ref[...] = dkcat[0:CA_BLK, :]
        dka_ref[...] = dkcat[CA_BLK:2 * CA_BLK, :]
        dvb_ref[...] = dvcat[0:CA_BLK, :]
        dva_ref[...] = dvcat[CA_BLK:2 * CA_BLK, :]

    osp = pl.BlockSpec((CA_BLK, DH), lambda h, i: (i, h))
    shp = jax.ShapeDtypeStruct((s, HEADS * DH), F32)
    return pl.pallas_call(
        body, grid=(HEADS, s // CA_BLK),
        in_specs=[cur(3 * HEADS), prev(4 * HEADS), cur(4 * HEADS), prev(5 * HEADS), cur(5 * HEADS), bsp,
                  pl.BlockSpec((CA_BLK, DH), lambda h, i: (i, HEADS + h))],
        out_specs=[osp, osp, osp, osp, osp, bsp],
        out_shape=[jax.ShapeDtypeStruct((s, HEADS * DH), BF16), shp, shp, shp, shp,
                   jax.ShapeDtypeStruct((HEADS, CHUNK, CA_WIN), F32)],
        scratch_shapes=[pltpu.VMEM((rows_cat, DH), F32)] * 4,
        compiler_params=_cparams(("parallel", "arbitrary")), name=name)(u, u, u, u, u, bias, dout)


def _ca_merge(da, db):
    shifted = jnp.concatenate([db[CA_BLK:], jnp.zeros((CA_BLK, db.shape[1]), F32)], axis=0)
    return (da + shifted).astype(BF16)


def _place():
    x, y, c = lax.axis_index("x"), lax.axis_index("y"), lax.axis_index("c")
    return x, y, c, [(1 - x, y), (x, 1 - y), (1 - x, 1 - y)]


_ANY = pl.BlockSpec(memory_space=pl.ANY)


def _all_gather_chips(w, name):
    def body(w_ref, o_ref, send_sems, recv_sems, loc_sem):
        x, y, c, peers = _place()
        me = 2 * x + y
        loc = pltpu.make_async_copy(w_ref, o_ref.at[me], loc_sem)
        loc.start()
        sends = []
        for k, (px, py) in enumerate(peers):
            cp = pltpu.make_async_remote_copy(src_ref=w_ref, dst_ref=o_ref.at[me], send_sem=send_sems.at[k],
                                              recv_sem=recv_sems.at[k], device_id=(px, py, c), device_id_type=MESH)
            cp.start()
            sends.append(cp)
        for k, (px, py) in enumerate(peers):
            pltpu.make_async_remote_copy(src_ref=w_ref, dst_ref=o_ref.at[2 * px + py], send_sem=send_sems.at[k],
                                         recv_sem=recv_sems.at[k], device_id=(px, py, c),
                                         device_id_type=MESH).wait_recv()
        for cp in sends:
            cp.wait_send()
        loc.wait()

    return pl.pallas_call(
        body, in_specs=[_ANY], out_specs=_ANY, out_shape=jax.ShapeDtypeStruct((4,) + w.shape, w.dtype),
        scratch_shapes=[pltpu.SemaphoreType.DMA((3,)), pltpu.SemaphoreType.DMA((3,)), pltpu.SemaphoreType.DMA(())],
        name=name)(w)


def _all_to_all_chips(g, name):
    def body(g_ref, o_ref, send_sems, recv_sems, loc_sem):
        x, y, c, peers = _place()
        me = 2 * x + y
        loc = pltpu.make_async_copy(g_ref.at[me], o_ref.at[me], loc_sem)
        loc.start()
        sends = []
        for k, (px, py) in enumerate(peers):
            cp = pltpu.make_async_remote_copy(src_ref=g_ref.at[2 * px + py], dst_ref=o_ref.at[me],
                                              send_sem=send_sems.at[k], recv_sem=recv_sems.at[k],
                                              device_id=(px, py, c), device_id_type=MESH)
            cp.start()
            sends.append(cp)
        for k, (px, py) in enumerate(peers):
            pltpu.make_async_remote_copy(src_ref=g_ref.at[me], dst_ref=o_ref.at[2 * px + py],
                                         send_sem=send_sems.at[k], recv_sem=recv_sems.at[k],
                                         device_id=(px, py, c), device_id_type=MESH).wait_recv()
        for cp in sends:
            cp.wait_send()
        loc.wait()

    return pl.pallas_call(
        body, in_specs=[_ANY], out_specs=_ANY, out_shape=jax.ShapeDtypeStruct(g.shape, g.dtype),
        scratch_shapes=[pltpu.SemaphoreType.DMA((3,)), pltpu.SemaphoreType.DMA((3,)), pltpu.SemaphoreType.DMA(())],
        name=name)(g)


def _core_swap(a, name):
    def body(a_ref, o_ref, send_sem, recv_sem):
        x, y, c, _ = _place()
        cp = pltpu.make_async_remote_copy(src_ref=a_ref, dst_ref=o_ref, send_sem=send_sem, recv_sem=recv_sem,
                                          device_id=(x, y, 1 - c), device_id_type=MESH)
        cp.start()
        cp.wait()

    return pl.pallas_call(
        body, in_specs=[_ANY], out_specs=_ANY, out_shape=jax.ShapeDtypeStruct(a.shape, a.dtype),
        scratch_shapes=[pltpu.SemaphoreType.DMA(()), pltpu.SemaphoreType.DMA(())], name=name)(a)


def _all_reduce_small(v, name):
    r, wl = v.shape

    def body(v_ref, o_ref, buf, send_sems, recv_sems):
        x, y, c, _ = _place()
        me = 4 * x + 2 * y + c
        buf[me] = v_ref[...]
        flips = [(fx, fy, fc) for fx in (0, 1) for fy in (0, 1) for fc in (0, 1) if (fx, fy, fc) != (0, 0, 0)]
        peer = lambda f: (x ^ f[0], y ^ f[1], c ^ f[2])
        sends = []
        for k, f in enumerate(flips):
            cp = pltpu.make_async_remote_copy(src_ref=v_ref, dst_ref=buf.at[me], send_sem=send_sems.at[k],
                                              recv_sem=recv_sems.at[k], device_id=peer(f), device_id_type=MESH)
            cp.start()
            sends.append(cp)
        for k, f in enumerate(flips):
            px, py, pc = peer(f)
            pltpu.make_async_remote_copy(src_ref=v_ref, dst_ref=buf.at[4 * px + 2 * py + pc], send_sem=send_sems.at[k],
                                         recv_sem=recv_sems.at[k], device_id=peer(f), device_id_type=MESH).wait_recv()
        for cp in sends:
            cp.wait_send()
        acc = buf[0]
        for d in range(1, 8):
            acc = acc + buf[d]
        o_ref[...] = acc

    vm = pl.BlockSpec(memory_space=pltpu.VMEM)
    return pl.pallas_call(
        body, in_specs=[vm], out_specs=vm, out_shape=jax.ShapeDtypeStruct((r, wl), F32),
        scratch_shapes=[pltpu.VMEM((8, r, wl), F32), pltpu.SemaphoreType.DMA((7,)), pltpu.SemaphoreType.DMA((7,))],
        name=name)(v)


def _sum_slots(g, name):
    _, r, cdim = g.shape
    br = _pick(r, (256, 128, 64, 32, 8))

    def body(g_ref, o_ref):
        o_ref[...] = ((g_ref[0] + g_ref[1]) + g_ref[2]) + g_ref[3]

    return pl.pallas_call(
        body, grid=(r // br,), in_specs=[pl.BlockSpec((4, br, cdim), lambda i: (0, i, 0))],
        out_specs=pl.BlockSpec((br, cdim), lambda i: (i, 0)), out_shape=jax.ShapeDtypeStruct((r, cdim), F32),
        compiler_params=_cparams(("parallel",)), name=name)(g)


def _adamw(w, ga, gb, m, v, name):
    r, cdim = w.shape
    br = _pick(r, (256, 128, 64, 32, 8))
    c1 = 1.0 / (1.0 - ADAM_B1 ** ADAM_STEP)
    c2 = 1.0 / (1.0 - ADAM_B2 ** ADAM_STEP)
    two = gb is not None

    def body(*refs):
        w_ref, ga_ref = refs[0], refs[1]
        gb_ref = refs[2] if two else None
        m_ref, v_ref, g_out, d_out, m_out, v_out = refs[2 + two:]
        g = ga_ref[...] + gb_ref[...] if two else ga_ref[...]
        mn = ADAM_B1 * m_ref[...] + (1.0 - ADAM_B1) * g
        vn = ADAM_B2 * v_ref[...] + (1.0 - ADAM_B2) * (g * g)
        g_out[...] = g
        m_out[...] = mn
        v_out[...] = vn
        d_out[...] = -ADAM_LR * ((mn * c1) / (jnp.sqrt(vn * c2) + ADAM_EPS) + ADAM_WD * w_ref[...])

    blk = pl.BlockSpec((br, cdim), lambda i: (i, 0))
    args = (w, ga) + ((gb,) if two else ()) + (m, v)
    return pl.pallas_call(
        body, grid=(r // br,), in_specs=[blk] * len(args), out_specs=[blk] * 4,
        out_shape=[jax.ShapeDtypeStruct((r, cdim), F32)] * 4,
        compiler_params=_cparams(("parallel",)), name=name)(*args)


def _gather_cols(w, name):
    g = _all_gather_chips(w.astype(BF16), name)
    return jnp.transpose(g, (1, 0, 2)).reshape(w.shape[0], 4 * w.shape[1])


def _gather_rows(w, name):
    g = _all_gather_chips(w.astype(BF16), name)
    return g.reshape(4 * w.shape[0], w.shape[1])


def _shard_cols(g):
    k, n = g.shape
    return jnp.transpose(g.reshape(k, 4, n // 4), (1, 0, 2))


def _shard_rows(g):
    k, n = g.shape
    return g.reshape(4, k // 4, n)


def _reduce_shards(g4, name):
    mine = _sum_slots(_all_to_all_chips(g4, name + "_a2a"), name + "_sum")
    return mine, _core_swap(mine, name + "_swap")


def _lower_bound(logits):
    return jnp.cumsum(jax.nn.softmax(logits.astype(F32), axis=0), axis=0)[0]


def _pack(parts):
    flat = jnp.concatenate([p.reshape(-1) for p in parts])
    n = flat.shape[0]
    rows = -(-n // 1024) * 8
    return jnp.pad(flat, (0, rows * 128 - n)).reshape(rows, 128)


def _unpack(packed, shapes):
    flat = packed.reshape(-1)
    out, off = [], 0
    for shp in shapes:
        n = 1
        for d in shp:
            n *= d
        out.append(flat[off:off + n].reshape(shp))
        off += n
    return out


def kernel(x, ev_w_in, ev_conv_w, ev_conv_b, ev_conv_ln_g, ev_conv_ln_b, hgrn_lb_logits, ev_gnorm_g, ev_w_out, od_w_in, fox_b_f, rel_bias, od_w_out, ln_mix_g, ln_mix_b, mlp_w1, mlp_w2, ln_mlp_g, ln_mlp_b, loss_target, m_ev_w_in, m_ev_conv_w, m_ev_conv_b, m_ev_conv_ln_g, m_ev_conv_ln_b, m_hgrn_lb_logits, m_ev_gnorm_g, m_ev_w_out, m_od_w_in, m_fox_b_f, m_rel_bias, m_od_w_out, m_ln_mix_g, m_ln_mix_b, m_mlp_w1, m_mlp_w2, m_ln_mlp_g, m_ln_mlp_b, v_ev_w_in, v_ev_conv_w, v_ev_conv_b, v_ev_conv_ln_g, v_ev_conv_ln_b, v_hgrn_lb_logits, v_ev_gnorm_g, v_ev_w_out, v_od_w_in, v_fox_b_f, v_rel_bias, v_od_w_out, v_ln_mix_g, v_ln_mix_b, v_mlp_w1, v_mlp_w2, v_ln_mlp_g, v_ln_mlp_b):
    w_sharded = dict(ev_w_in=ev_w_in, ev_w_out=ev_w_out, od_w_in=od_w_in, od_w_out=od_w_out, mlp_w1=mlp_w1, mlp_w2=mlp_w2)
    m_sharded = dict(ev_w_in=m_ev_w_in, ev_w_out=m_ev_w_out, od_w_in=m_od_w_in, od_w_out=m_od_w_out, mlp_w1=m_mlp_w1, mlp_w2=m_mlp_w2)
    v_sharded = dict(ev_w_in=v_ev_w_in, ev_w_out=v_ev_w_out, od_w_in=v_od_w_in, od_w_out=v_od_w_out, mlp_w1=v_mlp_w1, mlp_w2=v_mlp_w2)
    small_names = ["ev_conv_w", "ev_conv_b", "ev_conv_ln_g", "ev_conv_ln_b", "hgrn_lb_logits", "ev_gnorm_g", "fox_b_f",
                   "rel_bias", "ln_mix_g", "ln_mix_b", "ln_mlp_g", "ln_mlp_b"]
    w_small = dict(ev_conv_w=ev_conv_w, ev_conv_b=ev_conv_b, ev_conv_ln_g=ev_conv_ln_g, ev_conv_ln_b=ev_conv_ln_b,
                   hgrn_lb_logits=hgrn_lb_logits, ev_gnorm_g=ev_gnorm_g, fox_b_f=fox_b_f, rel_bias=rel_bias,
                   ln_mix_g=ln_mix_g, ln_mix_b=ln_mix_b, ln_mlp_g=ln_mlp_g, ln_mlp_b=ln_mlp_b)
    m_small = dict(ev_conv_w=m_ev_conv_w, ev_conv_b=m_ev_conv_b, ev_conv_ln_g=m_ev_conv_ln_g, ev_conv_ln_b=m_ev_conv_ln_b,
                   hgrn_lb_logits=m_hgrn_lb_logits, ev_gnorm_g=m_ev_gnorm_g, fox_b_f=m_fox_b_f, rel_bias=m_rel_bias,
                   ln_mix_g=m_ln_mix_g, ln_mix_b=m_ln_mix_b, ln_mlp_g=m_ln_mlp_g, ln_mlp_b=m_ln_mlp_b)
    v_small = dict(ev_conv_w=v_ev_conv_w, ev_conv_b=v_ev_conv_b, ev_conv_ln_g=v_ev_conv_ln_g, ev_conv_ln_b=v_ev_conv_ln_b,
                   hgrn_lb_logits=v_hgrn_lb_logits, ev_gnorm_g=v_ev_gnorm_g, fox_b_f=v_fox_b_f, rel_bias=v_rel_bias,
                   ln_mix_g=v_ln_mix_g, ln_mix_b=v_ln_mix_b, ln_mlp_g=v_ln_mlp_g, ln_mlp_b=v_ln_mlp_b)

    chip = 2 * lax.axis_index("x") + lax.axis_index("y")
    hw = HEADS * DH

    w_ev_in = _gather_cols(ev_w_in[0], "ag_ev_w_in")
    w_ev_out = _gather_rows(ev_w_out[0], "ag_ev_w_out")
    w_od_in = _gather_cols(od_w_in[0], "ag_od_w_in")
    w_od_out = _gather_rows(od_w_out[0], "ag_od_w_out")
    w1 = [_gather_cols(mlp_w1[l], "ag_mlp_w1_%d" % l) for l in range(DEPTH)]
    w2 = [_gather_rows(mlp_w2[l], "ag_mlp_w2_%d" % l) for l in range(DEPTH)]
    tables = _all_gather_chips(jnp.concatenate(
        [ev_conv_w[0].reshape(1, -1), jnp.pad(rel_bias[0].reshape(1, -1), ((0, 0), (0, (-rel_bias[0].size) % 128)))],
        axis=1), "ag_tables")
    ncw = ev_conv_w[0].size
    cshard = ev_conv_w.shape[2]
    conv_w = jnp.transpose(tables[:, 0, :ncw].reshape(4, CONV_WIDTH, cshard), (1, 0, 2)).reshape(CONV_WIDTH, 4 * cshard)
    rshard = rel_bias.shape[2]
    rel_full = jnp.transpose(tables[:, 0, ncw:ncw + HEADS * rshard].reshape(4, HEADS, rshard), (1, 0, 2)).reshape(HEADS, 4 * rshard)

    loss_part, grad_x, g, small_partial = _local_step(
        x[0], loss_target[0], w_ev_in, w_ev_out, w_od_in, w_od_out, w1, w2, conv_w, rel_full, ev_conv_b, ev_conv_ln_g,
        ev_conv_ln_b, hgrn_lb_logits, ev_gnorm_g, fox_b_f, ln_mix_g, ln_mix_b, ln_mlp_g, ln_mlp_b)
    loss = lax.psum(loss_part, ("x", "y", "c"))

    full_shapes = [tuple(small_partial[n].shape) for n in small_names]
    reduced = _unpack(_all_reduce_small(_pack([small_partial[n] for n in small_names]), "ar_small"), full_shapes)
    g_small = {}
    for n, val in zip(small_names, reduced):
        if n == "ev_conv_w":
            val = lax.dynamic_slice_in_dim(val, chip * cshard, cshard, axis=1)
        elif n == "rel_bias":
            val = lax.dynamic_slice_in_dim(val, chip * rshard, rshard, axis=1)
        g_small[n] = val.reshape(w_small[n].shape)
    shard_shapes = [tuple(w_small[n].shape) for n in small_names]
    packed = [_pack([d[n] for n in small_names]) for d in (w_small, g_small, m_small, v_small)]
    sg_, sd_, sm_, sv_ = _adamw(packed[0], packed[1], None, packed[2], packed[3], "adamw_small")
    out_small = {k: dict(zip(small_names, _unpack(val, shard_shapes)))
                 for k, val in (("g", sg_), ("d", sd_), ("m", sm_), ("v", sv_))}

    out_big = {"g": {}, "d": {}, "m": {}, "v": {}}

    def finish(name, g4_list):
        w = w_sharded[name]
        res = {"g": [], "d": [], "m": [], "v": []}
        for l, g4 in enumerate(g4_list):
            mine, other = _reduce_shards(g4, "rs_%s_%d" % (name, l))
            outs = _adamw(w[l], mine, other, m_sharded[name][l], v_sharded[name][l], "adamw_%s_%d" % (name, l))
            for key, val in zip(("g", "d", "m", "v"), outs):
                res[key].append(val)
        for key in res:
            out_big[key][name] = jnp.stack(res[key])

    finish("ev_w_in", [_shard_cols(g["ev_w_in"])])
    finish("ev_w_out", [_shard_rows(g["ev_w_out"])])
    finish("od_w_in", [_shard_cols(g["od_w_in"])])
    finish("od_w_out", [_shard_rows(g["od_w_out"])])
    finish("mlp_w1", [_shard_cols(t) for t in g["mlp_w1"]])
    finish("mlp_w2", [_shard_rows(t) for t in g["mlp_w2"]])

    order = ["ev_w_in", "ev_conv_w", "ev_conv_b", "ev_conv_ln_g", "ev_conv_ln_b", "hgrn_lb_logits", "ev_gnorm_g", "ev_w_out",
             "od_w_in", "fox_b_f", "rel_bias", "od_w_out", "ln_mix_g", "ln_mix_b", "mlp_w1", "mlp_w2", "ln_mlp_g", "ln_mlp_b"]

    def pick(key, n):
        return out_big[key][n] if n in out_big[key] else out_small[key][n]

    outs = [loss, grad_x[None]]
    for key in ("g", "d", "m", "v"):
        outs.extend(pick(key, n) for n in order)
    return tuple(outs)


def _local_step(xin, tgt, w_ev_in, w_ev_out, w_od_in, w_od_out, w1, w2, conv_w, rel_full, ev_conv_b, ev_conv_ln_g,
                ev_conv_ln_b, hgrn_lb_logits, ev_gnorm_g, fox_b_f, ln_mix_g, ln_mix_b, ln_mlp_g, ln_mlp_b):
    hw = HEADS * DH
    w_od_main = jnp.concatenate([w_od_in[:, :3 * hw], w_od_in[:, 3 * hw + HEADS:]], axis=1)
    w_od_f = jnp.pad(w_od_in[:, 3 * hw:3 * hw + HEADS], ((0, 0), (0, 128 - HEADS)))
    bf_pad = jnp.pad(fox_b_f[0], (0, 128 - HEADS)).reshape(1, 128)
    lb0 = _lower_bound(hgrn_lb_logits)
    ca_bias = _ca_bias(rel_full)

    u0 = _matmul(xin, w_ev_in, name="l0_in")
    a_out = _conv_fwd(u0, conv_w, ev_conv_b[0], ev_conv_ln_g[0], ev_conv_ln_b[0], "l0_conv")
    o_raw, b_out, states = _hgrn_fwd(u0, lb0, ev_gnorm_g[0], "l0_hgrn")
    cat0 = jnp.concatenate([a_out, b_out], axis=1).astype(BF16)
    mix0 = _matmul(cat0, w_ev_out, name="l0_out")
    r0a, x0a, x0a_b = _ln_fwd(xin, mix0, ln_mix_g[0], ln_mix_b[0], "l0_ln_mix")
    z0, h0 = _matmul(x0a_b, w1[0], out_dtype=BF16, epi="relu2", name="l0_mlp1")
    f0 = _matmul(h0, w2[0], name="l0_mlp2")
    r0b, x1, x1_b = _ln_fwd(x0a, f0, ln_mlp_g[0], ln_mlp_b[0], "l0_ln_mlp")
    u1 = _matmul(x1_b, w_od_main, name="l1_in")
    cf = _matmul(x1_b, w_od_f, name="l1_in_f")
    fcum = _fgate_fwd(cf, bf_pad, "l1_fgate")
    f_col = jnp.transpose(fcum[:, :HEADS])[:, :, None]
    f_row = jnp.transpose(fcum[:, :HEADS])[:, None, :]
    c_out, lse = _fox_fwd(u1, f_col, f_row, "l1_fox")
    d_out = _ca_fwd(u1, ca_bias, "l1_ca")
    cat1 = jnp.concatenate([c_out, d_out], axis=1)
    mix1 = _matmul(cat1, w_od_out, name="l1_out")
    r1a, x1a, x1a_b = _ln_fwd(x1, mix1, ln_mix_g[1], ln_mix_b[1], "l1_ln_mix")
    z1, h1 = _matmul(x1a_b, w1[1], out_dtype=BF16, epi="relu2", name="l1_mlp1")
    f1 = _matmul(h1, w2[1], name="l1_mlp2")
    r1b, x2, _ = _ln_fwd(x1a, f1, ln_mlp_g[1], ln_mlp_b[1], "l1_ln_mlp")
    dy, loss_part = _loss_head(x2, tgt, "loss")

    g = {}
    dr, drb, dg_, db_ = _ln_bwd(dy, r1b, ln_mlp_g[1], "l1_ln_mlp_bwd")
    g_ln_mlp = [None, (dg_, db_)]
    g_w2 = [None, _matmul(h1, drb, ta=True, name="l1_dw2")]
    dz = _matmul(drb, w2[1], tb=True, out_dtype=BF16, epi="drelu2", extra=z1, name="l1_dz")
    g_w1 = [None, _matmul(x1a_b, dz, ta=True, name="l1_dw1")]
    dx = _matmul(dz, w1[1], tb=True, epi="add", extra=dr, name="l1_dx_mlp")
    dr, drb, dg_, db_ = _ln_bwd(dx, r1a, ln_mix_g[1], "l1_ln_mix_bwd")
    g_ln_mix = [None, (dg_, db_)]
    g["od_w_out"] = _matmul(cat1, drb, ta=True, name="l1_dwout")
    dcat = _matmul(drb, w_od_out, tb=True, name="l1_dcat")
    dk_c, dv_c, dfk = _fox_bwd_kv(u1, c_out, lse, dcat, f_col, f_row, "l1_fox_bwd_kv")
    dq_c, dfq = _fox_bwd_q(u1, c_out, lse, dcat, f_col, f_row, "l1_fox_bwd_q")
    d_f = jnp.pad(jnp.transpose(dfk[:, 0, :] + dfq[:, :, 0]), ((0, 0), (0, 128 - HEADS)))
    dcf, dbf = _fgate_bwd(d_f, cf, bf_pad, "l1_fgate_bwd")
    dq_d, dka, dkb, dva, dvb, dbias = _ca_bwd(u1, ca_bias, dcat, "l1_ca_bwd")
    du1 = jnp.concatenate([dq_c, dk_c, dv_c, dq_d, _ca_merge(dka, dkb), _ca_merge(dva, dvb)], axis=1)
    g_main = _matmul(x1_b, du1, ta=True, name="l1_dwin")
    g_f = _matmul(x1_b, dcf, ta=True, name="l1_dwin_f")
    g["od_w_in"] = jnp.concatenate([g_main[:, :3 * hw], g_f[:, :HEADS], g_main[:, 3 * hw:]], axis=1)
    dx_f = _matmul(dcf, w_od_f, tb=True, epi="add", extra=dr, name="l1_dx_f")
    dx = _matmul(du1, w_od_main, tb=True, epi="add", extra=dx_f, scale=1.0, name="l1_dx_in")
    dr, drb, dg_, db_ = _ln_bwd(dx, r0b, ln_mlp_g[0], "l0_ln_mlp_bwd")
    g_ln_mlp[0] = (dg_, db_)
    g_w2[0] = _matmul(h0, drb, ta=True, name="l0_dw2")
    dz = _matmul(drb, w2[0], tb=True, out_dtype=BF16, epi="drelu2", extra=z0, name="l0_dz")
    g_w1[0] = _matmul(x0a_b, dz, ta=True, name="l0_dw1")
    dx = _matmul(dz, w1[0], tb=True, epi="add", extra=dr, name="l0_dx_mlp")
    dr, drb, dg_, db_ = _ln_bwd(dx, r0a, ln_mix_g[0], "l0_ln_mix_bwd")
    g_ln_mix[0] = (dg_, db_)
    g["ev_w_out"] = _matmul(cat0, drb, ta=True, name="l0_dwout")
    dcat = _matmul(drb, w_ev_out, tb=True, name="l0_dcat")
    dc, g_conv_w, g_conv_b, g_conv_lg, g_conv_lb = _conv_bwd_params(
        u0, dcat, conv_w, ev_conv_b[0], ev_conv_ln_g[0], ev_conv_ln_b[0], "l0_conv_bwd_p")
    da, dgate = _conv_bwd_input(u0, dc, conv_w, "l0_conv_bwd_i")
    dhq, dhf, dhi, dhg, g_lb0, g_gnorm = _hgrn_bwd(u0, o_raw, states, dcat, lb0, ev_gnorm_g[0], "l0_hgrn_bwd")
    du0 = jnp.concatenate([da, dgate, dhq, dhf, dhi, dhg], axis=1)
    g["ev_w_in"] = _matmul(xin.astype(BF16), du0, ta=True, name="l0_dwin")
    grad_x = _matmul(du0, w_ev_in, tb=True, epi="add", extra=dr, name="l0_dx_in")
    g["mlp_w1"] = g_w1
    g["mlp_w2"] = g_w2

    g_lb_logits = jax.vjp(_lower_bound, hgrn_lb_logits)[1](g_lb0)[0]
    g_rel = _ca_bias_grad(dbias)
    small_partial = dict(
        ev_conv_w=g_conv_w, ev_conv_b=g_conv_b, ev_conv_ln_g=g_conv_lg, ev_conv_ln_b=g_conv_lb,
        hgrn_lb_logits=g_lb_logits, ev_gnorm_g=g_gnorm, fox_b_f=dbf[:HEADS], rel_bias=g_rel,
        ln_mix_g=jnp.stack([g_ln_mix[0][0], g_ln_mix[1][0]]), ln_mix_b=jnp.stack([g_ln_mix[0][1], g_ln_mix[1][1]]),
        ln_mlp_g=jnp.stack([g_ln_mlp[0][0], g_ln_mlp[1][0]]), ln_mlp_b=jnp.stack([g_ln_mlp[0][1], g_ln_mlp[1][1]]))
    return loss_part, grad_x, g, small_partial
```

```python
import functools

import jax
import jax.numpy as jnp
from jax import lax
from jax.experimental import pallas as pl
from jax.experimental.pallas import tpu as pltpu

F32 = jnp.float32
BF16 = jnp.bfloat16
MESH = pl.DeviceIdType.MESH

DEPTH = 2
ALPHA = (2 * DEPTH) ** 0.25
LN_EPS = 1e-5
HEADS = 8
DH = 128
CHUNK = 64
CONV_WIDTH = 31
HALO = 32
CA_LEFT = 8
CA_BLK = CA_LEFT * CHUNK
CA_WIN = (CA_LEFT + 2) * CHUNK
REL_CLIP = 256
REL_TABLE = (CHUNK - 1) + REL_CLIP + 1
NEG = -1e30

ADAM_LR = 0.001
ADAM_B1 = 0.9
ADAM_B2 = 0.999
ADAM_EPS = 1e-08
ADAM_WD = 0.01
ADAM_STEP = 10

VMEM_LIMIT = 48 * 1024 * 1024


def _cparams(sem):
    return pltpu.CompilerParams(dimension_semantics=sem, vmem_limit_bytes=VMEM_LIMIT)


def _pick(n, cands):
    for c in cands:
        if n % c == 0:
            return c
    return n


def _sigmoid(x):
    return 1.0 / (1.0 + jnp.exp(-x))


def _dot(a, b, dims):
    return lax.dot_general(a.astype(BF16), b.astype(BF16), (dims, ((), ())), preferred_element_type=F32)


def _nn(a, b):
    return _dot(a, b, ((1,), (0,)))


def _nt(a, b):
    return _dot(a, b, ((1,), (1,)))


def _tn(a, b):
    return _dot(a, b, ((0,), (0,)))


def _dot3(a, b, dims):
    a_hi = a.astype(BF16)
    b_hi = b.astype(BF16)
    a_lo = (a - a_hi.astype(F32)).astype(BF16)
    b_lo = (b - b_hi.astype(F32)).astype(BF16)
    dn = (dims, ((), ()))
    return (lax.dot_general(a_hi, b_hi, dn, preferred_element_type=F32)
            + (lax.dot_general(a_hi, b_lo, dn, preferred_element_type=F32)
               + lax.dot_general(a_lo, b_hi, dn, preferred_element_type=F32)))


def _nn3(a, b):
    return _dot3(a, b, ((1,), (0,)))


def _nt3(a, b):
    return _dot3(a, b, ((1,), (1,)))


def _tn3(a, b):
    return _dot3(a, b, ((0,), (0,)))


def _split3(x):
    hi = x.astype(BF16)
    r1 = x - hi.astype(F32)
    mid = r1.astype(BF16)
    lo = (r1 - mid.astype(F32)).astype(BF16)
    return hi, mid, lo


def _tri_matmul(tri, x, terms):
    parts = _split3(x)[:terms]
    acc = None
    for p in parts:
        t = lax.dot_general(tri, p, (((1,), (0,)), ((), ())), preferred_element_type=F32)
        acc = t if acc is None else acc + t
    return acc


def _tril(n, upper=False):
    r = lax.broadcasted_iota(jnp.int32, (n, n), 0)
    c = lax.broadcasted_iota(jnp.int32, (n, n), 1)
    m = (c >= r) if upper else (c <= r)
    return jnp.where(m, 1.0, 0.0).astype(BF16)


def _matmul(a, b, *, ta=False, tb=False, out_dtype=F32, epi=None, extra=None, scale=ALPHA, b_sharded=False,
            out_sharded=False, comm=(), name):
    m = a.shape[1] if ta else a.shape[0]
    kd = a.shape[0] if ta else a.shape[1]
    if b_sharded:
        shard = b.shape[2]
        n = b.shape[1] if tb else 4 * shard
        assert (b.shape[1] if not tb else 4 * shard) == kd
    else:
        n = b.shape[0] if tb else b.shape[1]
    bm = _pick(m, (1024, 512, 256, 128))
    bn = _pick(shard if (b_sharded and not tb) else (n // 4 if out_sharded else n), (1024, 768, 512, 256, 128))
    bk = _pick(shard if (b_sharded and tb) else kd, (512, 256, 128))
    ni, nj, nk = m // bm, n // bn, kd // bk
    n_out = 2 if epi == "relu2" else 1
    n_comm = len(comm)
    kinds = [c[0] for c in comm]

    def body(*refs):
        a_ref, b_ref = refs[0], refs[1]
        pos = 2
        e_ref = refs[pos] if extra is not None else None
        pos += extra is not None
        c_in = refs[pos:pos + n_comm]
        pos += n_comm
        outs = refs[pos:pos + n_out]
        pos += n_out
        c_out = refs[pos:pos + n_comm]
        pos += n_comm
        acc_ref = refs[pos]
        sems = refs[pos + 1:]
        i, j, k = pl.program_id(0), pl.program_id(1), pl.program_id(2)

        if n_comm:
            @pl.when(jnp.logical_and(jnp.logical_and(i == 0, j == 0), k == 0))
            def _():
                _exchange(kinds, c_in, c_out, *sems, start=True)

        @pl.when(k == 0)
        def _():
            acc_ref[...] = jnp.zeros_like(acc_ref)

        dims = ((0 if ta else 1,), (1 if tb else 0,))
        acc_ref[...] += _dot(a_ref[...], b_ref[...], dims)

        @pl.when(k == nk - 1)
        def _():
            r = acc_ref[...]
            if epi == "relu2":
                outs[0][...] = r.astype(out_dtype)
                outs[1][...] = jnp.square(jnp.maximum(r, 0.0)).astype(out_dtype)
            elif epi == "drelu2":
                outs[0][...] = (r * (2.0 * jnp.maximum(e_ref[...].astype(F32), 0.0))).astype(out_dtype)
            elif epi == "add":
                outs[0][...] = (r + scale * e_ref[...].astype(F32)).astype(out_dtype)
            else:
                outs[0][...] = r.astype(out_dtype)

        if n_comm:
            @pl.when(jnp.logical_and(jnp.logical_and(i == ni - 1, j == nj - 1), k == nk - 1))
            def _():
                _exchange(kinds, c_in, c_out, *sems, start=False)

    a_spec = pl.BlockSpec((bk, bm), lambda i, j, k: (k, i)) if ta else pl.BlockSpec((bm, bk), lambda i, j, k: (i, k))
    if b_sharded and tb:
        per = shard // bk
        b_spec = pl.BlockSpec((None, bn, bk), lambda i, j, k: (k // per, j, k % per))
    elif b_sharded:
        per = shard // bn
        b_spec = pl.BlockSpec((None, bk, bn), lambda i, j, k: (j // per, k, j % per))
    elif tb:
        b_spec = pl.BlockSpec((bn, bk), lambda i, j, k: (j, k))
    else:
        b_spec = pl.BlockSpec((bk, bn), lambda i, j, k: (k, j))
    e_spec = pl.BlockSpec((bm, bn), lambda i, j, k: (i, j))
    if out_sharded:
        per_o = (n // 4) // bn
        o_spec = pl.BlockSpec((None, bm, bn), lambda i, j, k: (j // per_o, i, j % per_o))
        o_shape = jax.ShapeDtypeStruct((4, m, n // 4), out_dtype)
    else:
        o_spec = e_spec
        o_shape = jax.ShapeDtypeStruct((m, n), out_dtype)
    in_specs = [a_spec, b_spec] + ([e_spec] if extra is not None else []) + [_ANY] * n_comm
    args = (a, b) + ((extra,) if extra is not None else ()) + tuple(c[1] for c in comm)
    c_shapes = [jax.ShapeDtypeStruct((4,) + c[1].shape if c[0] == "ag" else c[1].shape, c[1].dtype) for c in comm]
    scratch = [pltpu.VMEM((bm, bn), F32)]
    if n_comm:
        scratch += [pltpu.SemaphoreType.DMA((3 * n_comm,)), pltpu.SemaphoreType.DMA((3 * n_comm,)),
                    pltpu.SemaphoreType.DMA((n_comm,))]
    sem = ("arbitrary",) * 3 if n_comm else ("parallel", "parallel", "arbitrary")
    res = pl.pallas_call(
        body, grid=(ni, nj, nk), in_specs=in_specs,
        out_specs=[o_spec] * n_out + [_ANY] * n_comm, out_shape=[o_shape] * n_out + c_shapes,
        scratch_shapes=scratch, compiler_params=_cparams(sem), name=name)(*args)
    main = tuple(res[:n_out]) if n_out == 2 else res[0]
    return (main, list(res[n_out:])) if n_comm else main


def _ln_fwd(x, mix, g, b, name):
    s, d = x.shape
    br = _pick(s, (256, 128, 64, 8))

    def body(x_ref, m_ref, g_ref, b_ref, r_ref, y_ref, yb_ref):
        r = ALPHA * x_ref[...] + m_ref[...]
        mu = jnp.mean(r, axis=-1, keepdims=True)
        dlt = r - mu
        var = jnp.mean(dlt * dlt, axis=-1, keepdims=True)
        y = dlt * lax.rsqrt(var + LN_EPS) * g_ref[...] + b_ref[...]
        r_ref[...] = r
        y_ref[...] = y
        yb_ref[...] = y.astype(BF16)

    row = pl.BlockSpec((br, d), lambda i: (i, 0))
    vec = pl.BlockSpec((1, d), lambda i: (0, 0))
    return pl.pallas_call(
        body, grid=(s // br,), in_specs=[row, row, vec, vec], out_specs=[row, row, row],
        out_shape=[jax.ShapeDtypeStruct((s, d), F32), jax.ShapeDtypeStruct((s, d), F32),
                   jax.ShapeDtypeStruct((s, d), BF16)],
        compiler_params=_cparams(("parallel",)), name=name)(x, mix, g.reshape(1, d), b.reshape(1, d))


def _ln_bwd(dy, r, g, name):
    s, d = r.shape
    br = _pick(s, (256, 128, 64, 8))

    def body(dy_ref, r_ref, g_ref, dr_ref, drb_ref, dg_ref, db_ref):
        @pl.when(pl.program_id(0) == 0)
        def _():
            dg_ref[...] = jnp.zeros_like(dg_ref)
            db_ref[...] = jnp.zeros_like(db_ref)

        rv = r_ref[...]
        dyv = dy_ref[...]
        mu = jnp.mean(rv, axis=-1, keepdims=True)
        dlt = rv - mu
        var = jnp.mean(dlt * dlt, axis=-1, keepdims=True)
        rstd = lax.rsqrt(var + LN_EPS)
        xhat = dlt * rstd
        dxh = dyv * g_ref[...]
        m1 = jnp.mean(dxh, axis=-1, keepdims=True)
        m2 = jnp.mean(dxh * xhat, axis=-1, keepdims=True)
        dr = rstd * (dxh - m1 - xhat * m2)
        dr_ref[...] = dr
        drb_ref[...] = dr.astype(BF16)
        dg_ref[...] += jnp.sum(dyv * xhat, axis=0, keepdims=True)
        db_ref[...] += jnp.sum(dyv, axis=0, keepdims=True)

    row = pl.BlockSpec((br, d), lambda i: (i, 0))
    vec = pl.BlockSpec((1, d), lambda i: (0, 0))
    dr, drb, dg, db = pl.pallas_call(
        body, grid=(s // br,), in_specs=[row, row, vec], out_specs=[row, row, vec, vec],
        out_shape=[jax.ShapeDtypeStruct((s, d), F32), jax.ShapeDtypeStruct((s, d), BF16),
                   jax.ShapeDtypeStruct((1, d), F32), jax.ShapeDtypeStruct((1, d), F32)],
        compiler_params=_cparams(("arbitrary",)), name=name)(dy, r, g.reshape(1, d))
    return dr, drb, dg[0], db[0]


def _loss_head(y, tgt, name):
    s, d = y.shape
    br = _pick(s, (256, 128, 64, 8))

    def body(y_ref, t_ref, dy_ref, l_ref):
        @pl.when(pl.program_id(0) == 0)
        def _():
            l_ref[...] = jnp.zeros_like(l_ref)

        e = y_ref[...] - t_ref[...]
        dy_ref[...] = e * (1.0 / d)
        rows = jnp.sum(e * e, axis=-1, keepdims=True) * (0.5 / d)
        l_ref[...] += jnp.sum(rows, axis=0, keepdims=True)

    row = pl.BlockSpec((br, d), lambda i: (i, 0))
    dy, l = pl.pallas_call(
        body, grid=(s // br,), in_specs=[row, row],
        out_specs=[row, pl.BlockSpec((1, 1), lambda i: (0, 0))],
        out_shape=[jax.ShapeDtypeStruct((s, d), F32), jax.ShapeDtypeStruct((1, 1), F32)],
        compiler_params=_cparams(("arbitrary",)), name=name)(y, tgt)
    return dy, l[0, 0]


def _conv_recompute(i, a_ref, gt_ref, ah_ref, gh_ref, w_ref, cb_ref, hext_ref, tt):
    h = a_ref[...] * _sigmoid(gt_ref[...])
    hh = ah_ref[...] * _sigmoid(gh_ref[...])
    hh = jnp.where(i > 0, hh, 0.0)
    hext_ref[0:HALO, :] = hh
    hext_ref[HALO:HALO + tt, :] = h
    acc = jnp.zeros_like(h) + cb_ref[...]
    off = HALO - (CONV_WIDTH - 1)
    for j in range(CONV_WIDTH):
        acc = acc + w_ref[j:j + 1, :] * hext_ref[pl.ds(off + j, tt), :]
    mu = jnp.mean(acc, axis=-1, keepdims=True)
    dlt = acc - mu
    var = jnp.mean(dlt * dlt, axis=-1, keepdims=True)
    rstd = lax.rsqrt(var + LN_EPS)
    return dlt * rstd, rstd


def _conv_specs(tt, cc, s):
    nh = tt // HALO
    cur = lambda cb: pl.BlockSpec((tt, cc), lambda i: (i, cb))
    prev = lambda cb: pl.BlockSpec((HALO, cc), lambda i: (jnp.maximum(i * nh - 1, 0), cb))
    vec = pl.BlockSpec((1, cc), lambda i: (0, 0))
    wsp = pl.BlockSpec((HALO, cc), lambda i: (0, 0))
    return cur, prev, vec, wsp


def _pad_conv_w(w):
    return jnp.concatenate([w, jnp.zeros((HALO - CONV_WIDTH, w.shape[1]), F32)], axis=0)


def _conv_fwd(u, w, cb, lg, lb, name):
    s = u.shape[0]
    cc = w.shape[1]
    tt = _pick(s, (256, 128, 64))
    cur, prev, vec, wsp = _conv_specs(tt, cc, s)

    def body(a_ref, gt_ref, ah_ref, gh_ref, w_ref, cb_ref, lg_ref, lb_ref, o_ref, hext_ref):
        xhat, _ = _conv_recompute(pl.program_id(0), a_ref, gt_ref, ah_ref, gh_ref, w_ref, cb_ref, hext_ref, tt)
        nrm = xhat * lg_ref[...] + lb_ref[...]
        o_ref[...] = nrm * _sigmoid(nrm)

    return pl.pallas_call(
        body, grid=(s // tt,), in_specs=[cur(0), cur(1), prev(0), prev(1), wsp, vec, vec, vec],
        out_specs=pl.BlockSpec((tt, cc), lambda i: (i, 0)), out_shape=jax.ShapeDtypeStruct((s, cc), F32),
        scratch_shapes=[pltpu.VMEM((tt + HALO, cc), F32)],
        compiler_params=_cparams(("parallel",)), name=name)(
            u, u, u, u, _pad_conv_w(w), cb.reshape(1, cc), lg.reshape(1, cc), lb.reshape(1, cc))


def _conv_bwd_params(u, dout, w, cb, lg, lb, name):
    s = u.shape[0]
    cc = w.shape[1]
    tt = _pick(s, (256, 128, 64))
    cur, prev, vec, wsp = _conv_specs(tt, cc, s)

    def body(a_ref, gt_ref, ah_ref, gh_ref, w_ref, cb_ref, lg_ref, lb_ref, do_ref,
             dc_ref, dw_ref, dcb_ref, dlg_ref, dlb_ref, hext_ref):
        i = pl.program_id(0)

        @pl.when(i == 0)
        def _():
            dw_ref[...] = jnp.zeros_like(dw_ref)
            dcb_ref[...] = jnp.zeros_like(dcb_ref)
            dlg_ref[...] = jnp.zeros_like(dlg_ref)
            dlb_ref[...] = jnp.zeros_like(dlb_ref)

        xhat, rstd = _conv_recompute(i, a_ref, gt_ref, ah_ref, gh_ref, w_ref, cb_ref, hext_ref, tt)
        nrm = xhat * lg_ref[...] + lb_ref[...]
        sg = _sigmoid(nrm)
        dn = do_ref[...] * (sg * (1.0 + nrm * (1.0 - sg)))
        dxh = dn * lg_ref[...]
        m1 = jnp.mean(dxh, axis=-1, keepdims=True)
        m2 = jnp.mean(dxh * xhat, axis=-1, keepdims=True)
        dc = rstd * (dxh - m1 - xhat * m2)
        dc_ref[...] = dc
        dlg_ref[...] += jnp.sum(dn * xhat, axis=0, keepdims=True)
        dlb_ref[...] += jnp.sum(dn, axis=0, keepdims=True)
        dcb_ref[...] += jnp.sum(dc, axis=0, keepdims=True)
        off = HALO - (CONV_WIDTH - 1)
        for j in range(CONV_WIDTH):
            dw_ref[j:j + 1, :] += jnp.sum(dc * hext_ref[pl.ds(off + j, tt), :], axis=0, keepdims=True)

    dcol = pl.BlockSpec((tt, cc), lambda i: (i, 0))
    dc, dw, dcb, dlg, dlb = pl.pallas_call(
        body, grid=(s // tt,), in_specs=[cur(0), cur(1), prev(0), prev(1), wsp, vec, vec, vec, dcol],
        out_specs=[dcol, wsp, vec, vec, vec],
        out_shape=[jax.ShapeDtypeStruct((s, cc), F32), jax.ShapeDtypeStruct((HALO, cc), F32)]
        + [jax.ShapeDtypeStruct((1, cc), F32)] * 3,
        scratch_shapes=[pltpu.VMEM((tt + HALO, cc), F32)],
        compiler_params=_cparams(("arbitrary",)), name=name)(
            u, u, u, u, _pad_conv_w(w), cb.reshape(1, cc), lg.reshape(1, cc), lb.reshape(1, cc), dout)
    return dc, dw[:CONV_WIDTH], dcb[0], dlg[0], dlb[0]


def _conv_bwd_input(u, dc, w, name):
    s = u.shape[0]
    cc = w.shape[1]
    tt = _pick(s, (256, 128, 64))
    nh = tt // HALO
    nlast = s // HALO - 1
    cur = lambda cb: pl.BlockSpec((tt, cc), lambda i: (i, cb))
    nxt = pl.BlockSpec((HALO, cc), lambda i: (jnp.minimum((i + 1) * nh, nlast), 0))
    wsp = pl.BlockSpec((HALO, cc), lambda i: (0, 0))
    nblk = s // tt

    def body(a_ref, gt_ref, dc_ref, dn_ref, w_ref, da_ref, dg_ref, ext_ref):
        i = pl.program_id(0)
        ext_ref[0:tt, :] = dc_ref[...]
        ext_ref[tt:tt + HALO, :] = jnp.where(i < nblk - 1, dn_ref[...], 0.0)
        dh = jnp.zeros((tt, cc), F32)
        for j in range(CONV_WIDTH):
            dh = dh + w_ref[j:j + 1, :] * ext_ref[pl.ds(CONV_WIDTH - 1 - j, tt), :]
        a = a_ref[...]
        sg = _sigmoid(gt_ref[...])
        da_ref[...] = (dh * sg).astype(BF16)
        dg_ref[...] = (dh * a * sg * (1.0 - sg)).astype(BF16)

    ocol = pl.BlockSpec((tt, cc), lambda i: (i, 0))
    return pl.pallas_call(
        body, grid=(nblk,), in_specs=[cur(0), cur(1), ocol, nxt, wsp], out_specs=[ocol, ocol],
        out_shape=[jax.ShapeDtypeStruct((s, cc), BF16)] * 2,
        scratch_shapes=[pltpu.VMEM((tt + HALO, cc), F32)],
        compiler_params=_cparams(("parallel",)), name=name)(u, u, dc, dc, _pad_conv_w(w))


def _hgrn_gates(hq, hf, lb):
    sg = _sigmoid(hf)
    f = lb + (1.0 - lb) * sg
    lf = jnp.log(f)
    big_l = _tri_matmul(_tril(CHUNK), lf, 3)
    l_end = jnp.sum(lf, axis=0, keepdims=True)
    l_mid = jnp.sum(lf[0:CHUNK // 2, :], axis=0, keepdims=True)
    sq = _sigmoid(hq)
    q = hq * sq
    return sg, f, 1.0 - f, big_l, l_end, l_mid, sq, q


def _causal_mask(n):
    r = lax.broadcasted_iota(jnp.int32, (n, n), 0)
    c = lax.broadcasted_iota(jnp.int32, (n, n), 1)
    return c <= r


def _hgrn_fwd(u, lb, gg, name):
    s = u.shape[0]
    w = HEADS * DH
    nch = s // CHUNK

    def body(q_ref, f_ref, i_ref, g_ref, lb_ref, gg_ref, o_ref, out_ref, st_ref, state):
        @pl.when(pl.program_id(0) == 0)
        def _():
            state[...] = jnp.zeros_like(state)

        mask = _causal_mask(CHUNK)
        for hd in range(HEADS):
            sl = slice(hd * DH, (hd + 1) * DH)
            _, _, kk, big_l, l_end, l_mid, _, q = _hgrn_gates(q_ref[:, sl], f_ref[:, sl], lb_ref[:, sl])
            v = i_ref[:, sl]
            qs = q * jnp.exp(big_l - l_mid)
            ks = kk * jnp.exp(l_mid - big_l)
            att = jnp.where(mask, _nt3(qs, ks), 0.0)
            st0 = state[hd]
            st_ref[0, hd] = st0
            o = _nn3(att, v) + _nt3(q * jnp.exp(big_l), st0)
            state[hd] = st0 * jnp.exp(l_end) + _tn3(v, kk * jnp.exp(l_end - big_l))
            o_ref[:, sl] = o
            on = o * lax.rsqrt(jnp.mean(o * o, axis=-1, keepdims=True) + LN_EPS)
            gv = g_ref[:, sl]
            out_ref[:, sl] = on * gg_ref[:, sl] * (gv * _sigmoid(gv))

    col = lambda cb: pl.BlockSpec((CHUNK, w), lambda i: (i, cb))
    vec = pl.BlockSpec((1, w), lambda i: (0, 0))
    ocol = pl.BlockSpec((CHUNK, w), lambda i: (i, 0))
    return pl.pallas_call(
        body, grid=(nch,), in_specs=[col(2), col(3), col(4), col(5), vec, vec],
        out_specs=[ocol, ocol, pl.BlockSpec((1, HEADS, DH, DH), lambda i: (i, 0, 0, 0))],
        out_shape=[jax.ShapeDtypeStruct((s, w), F32), jax.ShapeDtypeStruct((s, w), F32),
                   jax.ShapeDtypeStruct((nch, HEADS, DH, DH), F32)],
        scratch_shapes=[pltpu.VMEM((HEADS, DH, DH), F32)],
        compiler_params=_cparams(("arbitrary",)), name=name)(u, u, u, u, lb.reshape(1, w), gg.reshape(1, w))


def _hgrn_bwd(u, o_raw, states, dout, lb, gg, name):
    s = u.shape[0]
    w = HEADS * DH
    nch = s // CHUNK

    def body(q_ref, f_ref, i_ref, g_ref, o_ref, st_ref, do_ref, lb_ref, gg_ref,
             dq_ref, df_ref, di_ref, dg_ref, dlb_ref, dgg_ref, dstate):
        @pl.when(pl.program_id(0) == 0)
        def _():
            dstate[...] = jnp.zeros_like(dstate)
            dlb_ref[...] = jnp.zeros_like(dlb_ref)
            dgg_ref[...] = jnp.zeros_like(dgg_ref)

        mask = _causal_mask(CHUNK)
        last_row = lax.broadcasted_iota(jnp.int32, (CHUNK, DH), 0) == CHUNK - 1
        tri_up = _tril(CHUNK, upper=True)
        for hd in range(HEADS):
            sl = slice(hd * DH, (hd + 1) * DH)
            hq = q_ref[:, sl]
            lbv = lb_ref[:, sl]
            sg, f, kk, big_l, l_end, l_mid, sq, q = _hgrn_gates(hq, f_ref[:, sl], lbv)
            v = i_ref[:, sl]
            e_l = jnp.exp(big_l)
            e_qm = jnp.exp(big_l - l_mid)
            e_km = jnp.exp(l_mid - big_l)
            e_ke = jnp.exp(l_end - big_l)
            e_end = jnp.exp(l_end)
            qs = q * e_qm
            ks = kk * e_km
            qe = q * e_l
            ke = kk * e_ke
            att = jnp.where(mask, _nt3(qs, ks), 0.0)
            st0 = st_ref[0, hd]
            dst1 = dstate[hd]
            o = o_ref[:, sl]
            rinv = lax.rsqrt(jnp.mean(o * o, axis=-1, keepdims=True) + LN_EPS)
            on = o * rinv
            gv = g_ref[:, sl]
            sgg = _sigmoid(gv)
            gsil = gv * sgg
            ggv = gg_ref[:, sl]
            dov = do_ref[:, sl]
            don = dov * ggv * gsil
            dg_ref[:, sl] = (dov * on * ggv * (sgg * (1.0 + gv * (1.0 - sgg)))).astype(BF16)
            dgg_ref[:, sl] += jnp.sum(dov * on * gsil, axis=0, keepdims=True)
            do = rinv * (don - on * jnp.mean(don * on, axis=-1, keepdims=True))
            datt = jnp.where(mask, _nt3(do, v), 0.0)
            dv = _tn3(att, do) + _nt3(ke, dst1)
            dqs = _nn3(datt, ks)
            dks = _tn3(datt, qs)
            dqe = _nn3(do, st0)
            dke = _nn3(v, dst1)
            dq = dqs * e_qm + dqe * e_l
            dk = dks * e_km + dke * e_ke
            dke_ke = dke * ke
            dl = dqs * qs - dks * ks + dqe * qe - dke_ke
            dl_end = jnp.sum(dke_ke, axis=0, keepdims=True) + jnp.sum(dst1 * st0, axis=0, keepdims=True) * e_end
            dl = dl + jnp.where(last_row, dl_end, 0.0)
            dlf = _tri_matmul(tri_up, dl, 2)
            dfv = dlf / f - dk
            df_ref[:, sl] = (dfv * (1.0 - lbv) * sg * (1.0 - sg)).astype(BF16)
            dlb_ref[:, sl] += jnp.sum(dfv * (1.0 - sg), axis=0, keepdims=True)
            dq_ref[:, sl] = (dq * (sq * (1.0 + hq * (1.0 - sq)))).astype(BF16)
            di_ref[:, sl] = dv.astype(BF16)
            dstate[hd] = dst1 * e_end + _tn3(do, qe)

    rev = lambda i: nch - 1 - i
    col = lambda cb: pl.BlockSpec((CHUNK, w), lambda i: (rev(i), cb))
    vec = pl.BlockSpec((1, w), lambda i: (0, 0))
    ocol = pl.BlockSpec((CHUNK, w), lambda i: (rev(i), 0))
    res = pl.pallas_call(
        body, grid=(nch,),
        in_specs=[col(2), col(3), col(4), col(5), ocol,
                  pl.BlockSpec((1, HEADS, DH, DH), lambda i: (rev(i), 0, 0, 0)), col(1), vec, vec],
        out_specs=[ocol, ocol, ocol, ocol, vec, vec],
        out_shape=[jax.ShapeDtypeStruct((s, w), BF16)] * 4 + [jax.ShapeDtypeStruct((1, w), F32)] * 2,
        scratch_shapes=[pltpu.VMEM((HEADS, DH, DH), F32)],
        compiler_params=_cparams(("arbitrary",)), name=name)(
            u, u, u, u, o_raw, states, dout, lb.reshape(1, w), gg.reshape(1, w))
    return res[0], res[1], res[2], res[3], res[4][0], res[5][0]


def _log_sigmoid(x):
    return jnp.minimum(x, 0.0) - jnp.log(1.0 + jnp.exp(-jnp.abs(x)))


def _fgate_fwd(cf, bf, name):
    s, wl = cf.shape
    tb = _pick(s, (512, 256, 128, 64))

    def body(c_ref, b_ref, f_ref, carry):
        @pl.when(pl.program_id(0) == 0)
        def _():
            carry[...] = jnp.zeros_like(carry)

        ls = _log_sigmoid(c_ref[...] + b_ref[...])
        f_ref[...] = _tri_matmul(_tril(tb), ls, 3) + carry[...]
        carry[...] += jnp.sum(ls, axis=0, keepdims=True)

    return pl.pallas_call(
        body, grid=(s // tb,), in_specs=[pl.BlockSpec((tb, wl), lambda i: (i, 0)), pl.BlockSpec((1, wl), lambda i: (0, 0))],
        out_specs=pl.BlockSpec((tb, wl), lambda i: (i, 0)), out_shape=jax.ShapeDtypeStruct((s, wl), F32),
        scratch_shapes=[pltpu.VMEM((1, wl), F32)],
        compiler_params=_cparams(("arbitrary",)), name=name)(cf, bf)


def _fgate_bwd(dF, cf, bf, name):
    s, wl = cf.shape
    tb = _pick(s, (512, 256, 128, 64))
    nb = s // tb

    def body(d_ref, c_ref, b_ref, dc_ref, db_ref, carry):
        @pl.when(pl.program_id(0) == 0)
        def _():
            carry[...] = jnp.zeros_like(carry)
            db_ref[...] = jnp.zeros_like(db_ref)

        dv = d_ref[...]
        dls = _tri_matmul(_tril(tb, upper=True), dv, 3) + carry[...]
        carry[...] += jnp.sum(dv, axis=0, keepdims=True)
        dc = dls * (1.0 - _sigmoid(c_ref[...] + b_ref[...]))
        dc_ref[...] = dc.astype(BF16)
        db_ref[...] += jnp.sum(dc, axis=0, keepdims=True)

    blk = pl.BlockSpec((tb, wl), lambda i: (nb - 1 - i, 0))
    vec = pl.BlockSpec((1, wl), lambda i: (0, 0))
    dc, db = pl.pallas_call(
        body, grid=(nb,), in_specs=[blk, blk, vec], out_specs=[blk, vec],
        out_shape=[jax.ShapeDtypeStruct((s, wl), BF16), jax.ShapeDtypeStruct((1, wl), F32)],
        scratch_shapes=[pltpu.VMEM((1, wl), F32)],
        compiler_params=_cparams(("arbitrary",)), name=name)(dF, cf, bf)
    return dc, db[0]


def _fox_scores(q, k, fq, fk, diag):
    sc = _nt(q, k) * (DH ** -0.5) + fq - fk
    t = sc.shape[0]
    r = lax.broadcasted_iota(jnp.int32, sc.shape, 0)
    c = lax.broadcasted_iota(jnp.int32, sc.shape, 1)
    return jnp.where(jnp.logical_or(jnp.logical_not(diag), c <= r), sc, NEG)


def _fox_fwd(u, f_col, f_row, name):
    s = u.shape[0]
    t = _pick(s, (512, 256, 128))
    nb = s // t

    def body(q_ref, k_ref, v_ref, fq_ref, fk_ref, o_ref, lse_ref, m_sc, l_sc, acc):
        i, j = pl.program_id(1), pl.program_id(2)

        @pl.when(j == 0)
        def _():
            m_sc[...] = jnp.full_like(m_sc, NEG)
            l_sc[...] = jnp.zeros_like(l_sc)
            acc[...] = jnp.zeros_like(acc)

        @pl.when(j <= i)
        def _():
            sc = _fox_scores(q_ref[...], k_ref[...], fq_ref[0], fk_ref[0], j == i)
            m_new = jnp.maximum(m_sc[...], jnp.max(sc, axis=-1, keepdims=True))
            a = jnp.exp(m_sc[...] - m_new)
            p = jnp.exp(sc - m_new)
            l_sc[...] = a * l_sc[...] + jnp.sum(p, axis=-1, keepdims=True)
            acc[...] = a * acc[...] + _nn(p, v_ref[...])
            m_sc[...] = m_new

        @pl.when(j == i)
        def _():
            o_ref[...] = acc[...] / l_sc[...]
            lse_ref[0] = m_sc[...] + jnp.log(l_sc[...])

    qs = pl.BlockSpec((t, DH), lambda h, i, j: (i, h))
    ks = lambda base: pl.BlockSpec((t, DH), lambda h, i, j: (jnp.minimum(j, i), base + h))
    return pl.pallas_call(
        body, grid=(HEADS, nb, nb),
        in_specs=[qs, ks(HEADS), ks(2 * HEADS),
                  pl.BlockSpec((1, t, 1), lambda h, i, j: (h, i, 0)),
                  pl.BlockSpec((1, 1, t), lambda h, i, j: (h, 0, jnp.minimum(j, i)))],
        out_specs=[pl.BlockSpec((t, DH), lambda h, i, j: (i, h)), pl.BlockSpec((1, t, 1), lambda h, i, j: (h, i, 0))],
        out_shape=[jax.ShapeDtypeStruct((s, HEADS * DH), F32), jax.ShapeDtypeStruct((HEADS, s, 1), F32)],
        scratch_shapes=[pltpu.VMEM((t, 1), F32), pltpu.VMEM((t, 1), F32), pltpu.VMEM((t, DH), F32)],
        compiler_params=_cparams(("parallel", "parallel", "arbitrary")), name=name)(u, u, u, f_col, f_row)


def _fox_bwd_kv(u, o, lse, dout, f_col, f_row, name):
    s = u.shape[0]
    t = _pick(s, (512, 256, 128))
    nb = s // t

    def body(q_ref, k_ref, v_ref, o_ref, do_ref, lse_ref, fq_ref, fk_ref, dk_ref, dv_ref, dfk_ref, dk_acc, dv_acc, df_acc):
        j, i = pl.program_id(1), pl.program_id(2)

        @pl.when(i == 0)
        def _():
            dk_acc[...] = jnp.zeros_like(dk_acc)
            dv_acc[...] = jnp.zeros_like(dv_acc)
            df_acc[...] = jnp.zeros_like(df_acc)

        @pl.when(i >= j)
        def _():
            q = q_ref[...]
            do = do_ref[...]
            sc = _fox_scores(q, k_ref[...], fq_ref[0], fk_ref[0], j == i)
            p = jnp.exp(sc - lse_ref[0])
            delta = jnp.sum(do * o_ref[...], axis=-1, keepdims=True)
            dv_acc[...] += _tn(p, do)
            ds = p * (_nt(do, v_ref[...]) - delta)
            dk_acc[...] += _tn(ds, q) * (DH ** -0.5)
            df_acc[...] -= jnp.sum(ds, axis=0, keepdims=True)

        @pl.when(i == nb - 1)
        def _():
            dk_ref[...] = dk_acc[...].astype(BF16)
            dv_ref[...] = dv_acc[...].astype(BF16)
            dfk_ref[0] = df_acc[...]

    qi = lambda j, i: jnp.maximum(i, j)
    qs = lambda base: pl.BlockSpec((t, DH), lambda h, j, i: (qi(j, i), base + h))
    ks = lambda base: pl.BlockSpec((t, DH), lambda h, j, i: (j, base + h))
    return pl.pallas_call(
        body, grid=(HEADS, nb, nb),
        in_specs=[qs(0), ks(HEADS), ks(2 * HEADS), qs(0), qs(0),
                  pl.BlockSpec((1, t, 1), lambda h, j, i: (h, qi(j, i), 0)),
                  pl.BlockSpec((1, t, 1), lambda h, j, i: (h, qi(j, i), 0)),
                  pl.BlockSpec((1, 1, t), lambda h, j, i: (h, 0, j))],
        out_specs=[pl.BlockSpec((t, DH), lambda h, j, i: (j, h)), pl.BlockSpec((t, DH), lambda h, j, i: (j, h)),
                   pl.BlockSpec((1, 1, t), lambda h, j, i: (h, 0, j))],
        out_shape=[jax.ShapeDtypeStruct((s, HEADS * DH), BF16)] * 2 + [jax.ShapeDtypeStruct((HEADS, 1, s), F32)],
        scratch_shapes=[pltpu.VMEM((t, DH), F32), pltpu.VMEM((t, DH), F32), pltpu.VMEM((1, t), F32)],
        compiler_params=_cparams(("parallel", "parallel", "arbitrary")), name=name)(
            u, u, u, o, dout, lse, f_col, f_row)


def _fox_bwd_q(u, o, lse, dout, f_col, f_row, name):
    s = u.shape[0]
    t = _pick(s, (512, 256, 128))
    nb = s // t

    def body(q_ref, k_ref, v_ref, o_ref, do_ref, lse_ref, fq_ref, fk_ref, dq_ref, dfq_ref, dq_acc, df_acc):
        i, j = pl.program_id(1), pl.program_id(2)

        @pl.when(j == 0)
        def _():
            dq_acc[...] = jnp.zeros_like(dq_acc)
            df_acc[...] = jnp.zeros_like(df_acc)

        @pl.when(j <= i)
        def _():
            do = do_ref[...]
            k = k_ref[...]
            sc = _fox_scores(q_ref[...], k, fq_ref[0], fk_ref[0], j == i)
            p = jnp.exp(sc - lse_ref[0])
            delta = jnp.sum(do * o_ref[...], axis=-1, keepdims=True)
            ds = p * (_nt(do, v_ref[...]) - delta)
            dq_acc[...] += _nn(ds, k) * (DH ** -0.5)
            df_acc[...] += jnp.sum(ds, axis=-1, keepdims=True)

        @pl.when(j == i)
        def _():
            dq_ref[...] = dq_acc[...].astype(BF16)
            dfq_ref[0] = df_acc[...]

    qs = lambda base: pl.BlockSpec((t, DH), lambda h, i, j: (i, base + h))
    ks = lambda base: pl.BlockSpec((t, DH), lambda h, i, j: (jnp.minimum(j, i), base + h))
    return pl.pallas_call(
        body, grid=(HEADS, nb, nb),
        in_specs=[qs(0), ks(HEADS), ks(2 * HEADS), qs(0), qs(0),
                  pl.BlockSpec((1, t, 1), lambda h, i, j: (h, i, 0)),
                  pl.BlockSpec((1, t, 1), lambda h, i, j: (h, i, 0)),
                  pl.BlockSpec((1, 1, t), lambda h, i, j: (h, 0, jnp.minimum(j, i)))],
        out_specs=[pl.BlockSpec((t, DH), lambda h, i, j: (i, h)), pl.BlockSpec((1, t, 1), lambda h, i, j: (h, i, 0))],
        out_shape=[jax.ShapeDtypeStruct((s, HEADS * DH), BF16), jax.ShapeDtypeStruct((HEADS, s, 1), F32)],
        scratch_shapes=[pltpu.VMEM((t, DH), F32), pltpu.VMEM((t, 1), F32)],
        compiler_params=_cparams(("parallel", "parallel", "arbitrary")), name=name)(
            u, u, u, o, dout, lse, f_col, f_row)


def _ca_bias(rel_bias):
    band = (CA_LEFT + 1) * CHUNK
    n_clip = band - REL_CLIP
    gv = jnp.concatenate([jnp.broadcast_to(rel_bias[:, REL_TABLE - 1:], (rel_bias.shape[0], n_clip)),
                          rel_bias[:, REL_TABLE - 2::-1]], axis=1)
    rows = [gv[:, CHUNK - 1 - qi:CHUNK - 1 - qi + band] for qi in range(CHUNK)]
    b = jnp.stack(rows, axis=1)
    return jnp.concatenate([b, jnp.full((b.shape[0], CHUNK, CA_WIN - band), NEG, F32)], axis=2)


def _ca_bias_grad(db):
    band = (CA_LEFT + 1) * CHUNK
    n_clip = band - REL_CLIP
    dgv = None
    for qi in range(CHUNK):
        t = jnp.pad(db[:, qi, :band], ((0, 0), (CHUNK - 1 - qi, qi)))
        dgv = t if dgv is None else dgv + t
    return jnp.concatenate([dgv[:, n_clip:][:, ::-1], jnp.sum(dgv[:, :n_clip], axis=1, keepdims=True)], axis=1)


def _ca_specs():
    cur = lambda base: pl.BlockSpec((CA_BLK, DH), lambda h, i: (i, base + h))
    prev = lambda base: pl.BlockSpec((CA_BLK, DH), lambda h, i: (jnp.maximum(i - 1, 0), base + h))
    bias = pl.BlockSpec((1, CHUNK, CA_WIN), lambda h, i: (h, 0, 0))
    return cur, prev, bias


def _ca_fill(kcat, vcat, kp_ref, kc_ref, vp_ref, vc_ref):
    kcat[0:CA_BLK, :] = kp_ref[...]
    kcat[CA_BLK:2 * CA_BLK, :] = kc_ref[...]
    kcat[2 * CA_BLK:, :] = jnp.zeros((CHUNK, DH), F32)
    vcat[0:CA_BLK, :] = vp_ref[...]
    vcat[CA_BLK:2 * CA_BLK, :] = vc_ref[...]
    vcat[2 * CA_BLK:, :] = jnp.zeros((CHUNK, DH), F32)


def _ca_probs(i, c, q, kw, bias):
    sc = _nt(q, kw) * (DH ** -0.5) + bias
    col = lax.broadcasted_iota(jnp.int32, sc.shape, 1)
    sc = jnp.where((i - 1) * CA_BLK + c * CHUNK + col >= 0, sc, NEG)
    p = jnp.exp(sc - jnp.max(sc, axis=-1, keepdims=True))
    return p / jnp.sum(p, axis=-1, keepdims=True)


def _ca_fwd(u, bias, name):
    s = u.shape[0]
    cur, prev, bsp = _ca_specs()

    def body(q_ref, kp_ref, kc_ref, vp_ref, vc_ref, b_ref, o_ref, kcat, vcat):
        i = pl.program_id(1)
        _ca_fill(kcat, vcat, kp_ref, kc_ref, vp_ref, vc_ref)
        for c in range(CA_LEFT):
            rows = slice(c * CHUNK, (c + 1) * CHUNK)
            win = slice(c * CHUNK, c * CHUNK + CA_WIN)
            p = _ca_probs(i, c, q_ref[rows, :], kcat[win, :], b_ref[0])
            o_ref[rows, :] = _nn(p, vcat[win, :])

    return pl.pallas_call(
        body, grid=(HEADS, s // CA_BLK),
        in_specs=[cur(3 * HEADS), prev(4 * HEADS), cur(4 * HEADS), prev(5 * HEADS), cur(5 * HEADS), bsp],
        out_specs=pl.BlockSpec((CA_BLK, DH), lambda h, i: (i, h)),
        out_shape=jax.ShapeDtypeStruct((s, HEADS * DH), F32),
        scratch_shapes=[pltpu.VMEM((2 * CA_BLK + CHUNK, DH), F32)] * 2,
        compiler_params=_cparams(("parallel", "parallel")), name=name)(u, u, u, u, u, bias)


def _ca_bwd(u, bias, dout, name):
    s = u.shape[0]
    cur, prev, bsp = _ca_specs()
    rows_cat = 2 * CA_BLK + CHUNK

    def body(q_ref, kp_ref, kc_ref, vp_ref, vc_ref, b_ref, do_ref,
             dq_ref, dka_ref, dkb_ref, dva_ref, dvb_ref, db_ref, kcat, vcat, dkcat, dvcat):
        i = pl.program_id(1)

        @pl.when(i == 0)
        def _():
            db_ref[...] = jnp.zeros_like(db_ref)

        _ca_fill(kcat, vcat, kp_ref, kc_ref, vp_ref, vc_ref)
        dkcat[...] = jnp.zeros_like(dkcat)
        dvcat[...] = jnp.zeros_like(dvcat)
        for c in range(CA_LEFT):
            rows = slice(c * CHUNK, (c + 1) * CHUNK)
            win = slice(c * CHUNK, c * CHUNK + CA_WIN)
            q = q_ref[rows, :]
            kw = kcat[win, :]
            vw = vcat[win, :]
            do = do_ref[rows, :]
            p = _ca_probs(i, c, q, kw, b_ref[0])
            dp = _nt(do, vw)
            ds = p * (dp - jnp.sum(p * dp, axis=-1, keepdims=True))
            dq_ref[rows, :] = (_nn(ds, kw) * (DH ** -0.5)).astype(BF16)
            dkcat[win, :] += _tn(ds, q) * (DH ** -0.5)
            dvcat[win, :] += _tn(p, do)
            db_ref[0] += ds
        dkb_ref[...] = dkcat[0:CA_BLK, :]
        dka_ref[...] = dkcat[CA_BLK:2 * CA_BLK, :]
        dvb_ref[...] = dvcat[0:CA_BLK, :]
        dva_ref[...] = dvcat[CA_BLK:2 * CA_BLK, :]

    osp = pl.BlockSpec((CA_BLK, DH), lambda h, i: (i, h))
    shp = jax.ShapeDtypeStruct((s, HEADS * DH), F32)
    return pl.pallas_call(
        body, grid=(HEADS, s // CA_BLK),
        in_specs=[cur(3 * HEADS), prev(4 * HEADS), cur(4 * HEADS), prev(5 * HEADS), cur(5 * HEADS), bsp,
                  pl.BlockSpec((CA_BLK, DH), lambda h, i: (i, HEADS + h))],
        out_specs=[osp, osp, osp, osp, osp, bsp],
        out_shape=[jax.ShapeDtypeStruct((s, HEADS * DH), BF16), shp, shp, shp, shp,
                   jax.ShapeDtypeStruct((HEADS, CHUNK, CA_WIN), F32)],
        scratch_shapes=[pltpu.VMEM((rows_cat, DH), F32)] * 4,
        compiler_params=_cparams(("parallel", "arbitrary")), name=name)(u, u, u, u, u, bias, dout)


def _ca_merge(da, db):
    shifted = jnp.concatenate([db[CA_BLK:], jnp.zeros((CA_BLK, db.shape[1]), F32)], axis=0)
    return (da + shifted).astype(BF16)


def _place():
    x, y, c = lax.axis_index("x"), lax.axis_index("y"), lax.axis_index("c")
    return x, y, c, [(1 - x, y), (x, 1 - y), (1 - x, 1 - y)]


_ANY = pl.BlockSpec(memory_space=pl.ANY)


def _all_gather_chips(w, name):
    def body(w_ref, o_ref, send_sems, recv_sems, loc_sems):
        _exchange(["ag"], [w_ref], [o_ref], send_sems, recv_sems, loc_sems, start=True)
        _exchange(["ag"], [w_ref], [o_ref], send_sems, recv_sems, loc_sems, start=False)

    return pl.pallas_call(
        body, in_specs=[_ANY], out_specs=_ANY, out_shape=jax.ShapeDtypeStruct((4,) + w.shape, w.dtype),
        scratch_shapes=[pltpu.SemaphoreType.DMA((3,)), pltpu.SemaphoreType.DMA((3,)), pltpu.SemaphoreType.DMA((1,))],
        name=name)(w)


def _exchange(kinds, srcs, dsts, send_sems, recv_sems, loc_sems, start):
    x, y, c, peers = _place()
    me = 2 * x + y
    for n, kind in enumerate(kinds):
        src, dst = srcs[n], dsts[n]
        mine = src if kind == "ag" else src.at[me]
        loc = pltpu.make_async_copy(mine, dst.at[me], loc_sems.at[n])
        copies = []
        for k, (px, py) in enumerate(peers):
            out_src = src if kind == "ag" else src.at[2 * px + py]
            send = pltpu.make_async_remote_copy(src_ref=out_src, dst_ref=dst.at[me], send_sem=send_sems.at[3 * n + k],
                                                recv_sem=recv_sems.at[3 * n + k], device_id=(px, py, c),
                                                device_id_type=MESH)
            recv = pltpu.make_async_remote_copy(src_ref=mine, dst_ref=dst.at[2 * px + py],
                                                send_sem=send_sems.at[3 * n + k], recv_sem=recv_sems.at[3 * n + k],
                                                device_id=(px, py, c), device_id_type=MESH)
            copies.append((send, recv))
        if start:
            loc.start()
            for send, _ in copies:
                send.start()
        else:
            for _, recv in copies:
                recv.wait_recv()
            for send, _ in copies:
                send.wait_send()
            loc.wait()


def _core_swap(a, name):
    def body(a_ref, o_ref, send_sem, recv_sem):
        x, y, c, _ = _place()
        cp = pltpu.make_async_remote_copy(src_ref=a_ref, dst_ref=o_ref, send_sem=send_sem, recv_sem=recv_sem,
                                          device_id=(x, y, 1 - c), device_id_type=MESH)
        cp.start()
        cp.wait()

    return pl.pallas_call(
        body, in_specs=[_ANY], out_specs=_ANY, out_shape=jax.ShapeDtypeStruct(a.shape, a.dtype),
        scratch_shapes=[pltpu.SemaphoreType.DMA(()), pltpu.SemaphoreType.DMA(())], name=name)(a)


def _all_reduce_small(v, name):
    r, wl = v.shape

    def body(v_ref, o_ref, buf, send_sems, recv_sems):
        x, y, c, _ = _place()
        me = 4 * x + 2 * y + c
        buf[me] = v_ref[...]
        flips = [(fx, fy, fc) for fx in (0, 1) for fy in (0, 1) for fc in (0, 1) if (fx, fy, fc) != (0, 0, 0)]
        peer = lambda f: (x ^ f[0], y ^ f[1], c ^ f[2])
        sends = []
        for k, f in enumerate(flips):
            cp = pltpu.make_async_remote_copy(src_ref=v_ref, dst_ref=buf.at[me], send_sem=send_sems.at[k],
                                              recv_sem=recv_sems.at[k], device_id=peer(f), device_id_type=MESH)
            cp.start()
            sends.append(cp)
        for k, f in enumerate(flips):
            px, py, pc = peer(f)
            pltpu.make_async_remote_copy(src_ref=v_ref, dst_ref=buf.at[4 * px + 2 * py + pc], send_sem=send_sems.at[k],
                                         recv_sem=recv_sems.at[k], device_id=peer(f), device_id_type=MESH).wait_recv()
        for cp in sends:
            cp.wait_send()
        acc = buf[0]
        for d in range(1, 8):
            acc = acc + buf[d]
        o_ref[...] = acc

    vm = pl.BlockSpec(memory_space=pltpu.VMEM)
    return pl.pallas_call(
        body, in_specs=[vm], out_specs=vm, out_shape=jax.ShapeDtypeStruct((r, wl), F32),
        scratch_shapes=[pltpu.VMEM((8, r, wl), F32), pltpu.SemaphoreType.DMA((7,)), pltpu.SemaphoreType.DMA((7,))],
        name=name)(v)


def _sum_slots(g, name):
    _, r, cdim = g.shape
    br = _pick(r, (256, 128, 64, 32, 8))

    def body(g_ref, o_ref):
        o_ref[...] = ((g_ref[0].astype(F32) + g_ref[1].astype(F32)) + g_ref[2].astype(F32)) + g_ref[3].astype(F32)

    return pl.pallas_call(
        body, grid=(r // br,), in_specs=[pl.BlockSpec((4, br, cdim), lambda i: (0, i, 0))],
        out_specs=pl.BlockSpec((br, cdim), lambda i: (i, 0)), out_shape=jax.ShapeDtypeStruct((r, cdim), F32),
        compiler_params=_cparams(("parallel",)), name=name)(g)


def _adamw(w, ga, gb, m, v, name):
    r, cdim = w.shape
    br = _pick(r, (256, 128, 64, 32, 8))
    c1 = 1.0 / (1.0 - ADAM_B1 ** ADAM_STEP)
    c2 = 1.0 / (1.0 - ADAM_B2 ** ADAM_STEP)
    two = gb is not None

    def body(*refs):
        w_ref, ga_ref = refs[0], refs[1]
        gb_ref = refs[2] if two else None
        m_ref, v_ref, g_out, d_out, m_out, v_out = refs[2 + two:]
        g = ga_ref[...] + gb_ref[...] if two else ga_ref[...]
        mn = ADAM_B1 * m_ref[...] + (1.0 - ADAM_B1) * g
        vn = ADAM_B2 * v_ref[...] + (1.0 - ADAM_B2) * (g * g)
        g_out[...] = g
        m_out[...] = mn
        v_out[...] = vn
        d_out[...] = -ADAM_LR * ((mn * c1) / (jnp.sqrt(vn * c2) + ADAM_EPS) + ADAM_WD * w_ref[...])

    blk = pl.BlockSpec((br, cdim), lambda i: (i, 0))
    args = (w, ga) + ((gb,) if two else ()) + (m, v)
    return pl.pallas_call(
        body, grid=(r // br,), in_specs=[blk] * len(args), out_specs=[blk] * 4,
        out_shape=[jax.ShapeDtypeStruct((r, cdim), F32)] * 4,
        compiler_params=_cparams(("parallel",)), name=name)(*args)


def _unshard_cols(g):
    return jnp.transpose(g, (1, 0, 2)).reshape(g.shape[1], 4 * g.shape[2])


def _unshard_rows(g):
    return g.reshape(4 * g.shape[1], g.shape[2])


def _shard_cols(g):
    k, n = g.shape
    return jnp.transpose(g.reshape(k, 4, n // 4), (1, 0, 2))


def _shard_rows(g):
    k, n = g.shape
    return g.reshape(4, k // 4, n)


def _lower_bound(logits):
    return jnp.cumsum(jax.nn.softmax(logits.astype(F32), axis=0), axis=0)[0]


def _pack(parts):
    flat = jnp.concatenate([p.reshape(-1) for p in parts])
    n = flat.shape[0]
    rows = -(-n // 1024) * 8
    return jnp.pad(flat, (0, rows * 128 - n)).reshape(rows, 128)


def _unpack(packed, shapes):
    flat = packed.reshape(-1)
    out, off = [], 0
    for shp in shapes:
        n = 1
        for d in shp:
            n *= d
        out.append(flat[off:off + n].reshape(shp))
        off += n
    return out


def kernel(x, ev_w_in, ev_conv_w, ev_conv_b, ev_conv_ln_g, ev_conv_ln_b, hgrn_lb_logits, ev_gnorm_g, ev_w_out, od_w_in, fox_b_f, rel_bias, od_w_out, ln_mix_g, ln_mix_b, mlp_w1, mlp_w2, ln_mlp_g, ln_mlp_b, loss_target, m_ev_w_in, m_ev_conv_w, m_ev_conv_b, m_ev_conv_ln_g, m_ev_conv_ln_b, m_hgrn_lb_logits, m_ev_gnorm_g, m_ev_w_out, m_od_w_in, m_fox_b_f, m_rel_bias, m_od_w_out, m_ln_mix_g, m_ln_mix_b, m_mlp_w1, m_mlp_w2, m_ln_mlp_g, m_ln_mlp_b, v_ev_w_in, v_ev_conv_w, v_ev_conv_b, v_ev_conv_ln_g, v_ev_conv_ln_b, v_hgrn_lb_logits, v_ev_gnorm_g, v_ev_w_out, v_od_w_in, v_fox_b_f, v_rel_bias, v_od_w_out, v_ln_mix_g, v_ln_mix_b, v_mlp_w1, v_mlp_w2, v_ln_mlp_g, v_ln_mlp_b):
    w_sharded = dict(ev_w_in=ev_w_in, ev_w_out=ev_w_out, od_w_in=od_w_in, od_w_out=od_w_out, mlp_w1=mlp_w1, mlp_w2=mlp_w2)
    m_sharded = dict(ev_w_in=m_ev_w_in, ev_w_out=m_ev_w_out, od_w_in=m_od_w_in, od_w_out=m_od_w_out, mlp_w1=m_mlp_w1, mlp_w2=m_mlp_w2)
    v_sharded = dict(ev_w_in=v_ev_w_in, ev_w_out=v_ev_w_out, od_w_in=v_od_w_in, od_w_out=v_od_w_out, mlp_w1=v_mlp_w1, mlp_w2=v_mlp_w2)
    small_names = ["ev_conv_w", "ev_conv_b", "ev_conv_ln_g", "ev_conv_ln_b", "hgrn_lb_logits", "ev_gnorm_g", "fox_b_f",
                   "rel_bias", "ln_mix_g", "ln_mix_b", "ln_mlp_g", "ln_mlp_b"]
    w_small = dict(ev_conv_w=ev_conv_w, ev_conv_b=ev_conv_b, ev_conv_ln_g=ev_conv_ln_g, ev_conv_ln_b=ev_conv_ln_b,
                   hgrn_lb_logits=hgrn_lb_logits, ev_gnorm_g=ev_gnorm_g, fox_b_f=fox_b_f, rel_bias=rel_bias,
                   ln_mix_g=ln_mix_g, ln_mix_b=ln_mix_b, ln_mlp_g=ln_mlp_g, ln_mlp_b=ln_mlp_b)
    m_small = dict(ev_conv_w=m_ev_conv_w, ev_conv_b=m_ev_conv_b, ev_conv_ln_g=m_ev_conv_ln_g, ev_conv_ln_b=m_ev_conv_ln_b,
                   hgrn_lb_logits=m_hgrn_lb_logits, ev_gnorm_g=m_ev_gnorm_g, fox_b_f=m_fox_b_f, rel_bias=m_rel_bias,
                   ln_mix_g=m_ln_mix_g, ln_mix_b=m_ln_mix_b, ln_mlp_g=m_ln_mlp_g, ln_mlp_b=m_ln_mlp_b)
    v_small = dict(ev_conv_w=v_ev_conv_w, ev_conv_b=v_ev_conv_b, ev_conv_ln_g=v_ev_conv_ln_g, ev_conv_ln_b=v_ev_conv_ln_b,
                   hgrn_lb_logits=v_hgrn_lb_logits, ev_gnorm_g=v_ev_gnorm_g, fox_b_f=v_fox_b_f, rel_bias=v_rel_bias,
                   ln_mix_g=v_ln_mix_g, ln_mix_b=v_ln_mix_b, ln_mlp_g=v_ln_mlp_g, ln_mlp_b=v_ln_mlp_b)

    chip = 2 * lax.axis_index("x") + lax.axis_index("y")
    hw = HEADS * DH

    w_ev_in = _all_gather_chips(ev_w_in[0].astype(BF16), "ag_ev_w_in")
    shards = dict(ev_w_out=ev_w_out[0].astype(BF16), od_w_in=od_w_in[0].astype(BF16), od_w_out=od_w_out[0].astype(BF16),
                  mlp_w1=[mlp_w1[l].astype(BF16) for l in range(DEPTH)],
                  mlp_w2=[mlp_w2[l].astype(BF16) for l in range(DEPTH)])
    tables = _all_gather_chips(jnp.concatenate(
        [ev_conv_w[0].reshape(1, -1), jnp.pad(rel_bias[0].reshape(1, -1), ((0, 0), (0, (-rel_bias[0].size) % 128)))],
        axis=1), "ag_tables")
    ncw = ev_conv_w[0].size
    cshard = ev_conv_w.shape[2]
    conv_w = jnp.transpose(tables[:, 0, :ncw].reshape(4, CONV_WIDTH, cshard), (1, 0, 2)).reshape(CONV_WIDTH, 4 * cshard)
    rshard = rel_bias.shape[2]
    rel_full = jnp.transpose(tables[:, 0, ncw:ncw + HEADS * rshard].reshape(4, HEADS, rshard), (1, 0, 2)).reshape(HEADS, 4 * rshard)

    loss_part, grad_x, g, small_partial = _local_step(
        x[0], loss_target[0], w_ev_in, shards, conv_w, rel_full, ev_conv_b, ev_conv_ln_g,
        ev_conv_ln_b, hgrn_lb_logits, ev_gnorm_g, fox_b_f, ln_mix_g, ln_mix_b, ln_mlp_g, ln_mlp_b)
    loss = lax.psum(loss_part, ("x", "y", "c"))

    full_shapes = [tuple(small_partial[n].shape) for n in small_names]
    reduced = _unpack(_all_reduce_small(_pack([small_partial[n] for n in small_names]), "ar_small"), full_shapes)
    g_small = {}
    for n, val in zip(small_names, reduced):
        if n == "ev_conv_w":
            val = lax.dynamic_slice_in_dim(val, chip * cshard, cshard, axis=1)
        elif n == "rel_bias":
            val = lax.dynamic_slice_in_dim(val, chip * rshard, rshard, axis=1)
        g_small[n] = val.reshape(w_small[n].shape)
    shard_shapes = [tuple(w_small[n].shape) for n in small_names]
    packed = [_pack([d[n] for n in small_names]) for d in (w_small, g_small, m_small, v_small)]
    sg_, sd_, sm_, sv_ = _adamw(packed[0], packed[1], None, packed[2], packed[3], "adamw_small")
    out_small = {k: dict(zip(small_names, _unpack(val, shard_shapes)))
                 for k, val in (("g", sg_), ("d", sd_), ("m", sm_), ("v", sv_))}

    out_big = {"g": {}, "d": {}, "m": {}, "v": {}}

    for name in ("ev_w_in", "ev_w_out", "od_w_in", "od_w_out", "mlp_w1", "mlp_w2"):
        res = {"g": [], "d": [], "m": [], "v": []}
        for l, slots in enumerate(g[name]):
            mine = _sum_slots(slots, "rs_%s_%d_sum" % (name, l))
            other = _core_swap(mine, "rs_%s_%d_swap" % (name, l))
            outs = _adamw(w_sharded[name][l], mine, other, m_sharded[name][l], v_sharded[name][l],
                          "adamw_%s_%d" % (name, l))
            for key, val in zip(("g", "d", "m", "v"), outs):
                res[key].append(val)
        for key in res:
            out_big[key][name] = jnp.stack(res[key])

    order = ["ev_w_in", "ev_conv_w", "ev_conv_b", "ev_conv_ln_g", "ev_conv_ln_b", "hgrn_lb_logits", "ev_gnorm_g", "ev_w_out",
             "od_w_in", "fox_b_f", "rel_bias", "od_w_out", "ln_mix_g", "ln_mix_b", "mlp_w1", "mlp_w2", "ln_mlp_g", "ln_mlp_b"]

    def pick(key, n):
        return out_big[key][n] if n in out_big[key] else out_small[key][n]

    outs = [loss, grad_x[None]]
    for key in ("g", "d", "m", "v"):
        outs.extend(pick(key, n) for n in order)
    return tuple(outs)


def _local_step(xin, tgt, w_ev_in, shards, conv_w, rel_full, ev_conv_b, ev_conv_ln_g,
                ev_conv_ln_b, hgrn_lb_logits, ev_gnorm_g, fox_b_f, ln_mix_g, ln_mix_b, ln_mlp_g, ln_mlp_b):
    hw = HEADS * DH
    bf_pad = jnp.pad(fox_b_f[0], (0, 128 - HEADS)).reshape(1, 128)
    lb0 = _lower_bound(hgrn_lb_logits)
    ca_bias = _ca_bias(rel_full)
    ag = lambda w: ("ag", w)
    a2a = lambda g4: ("a2a", g4)

    u0, (w_ev_out4, w1_0) = _matmul(xin, w_ev_in, b_sharded=True, comm=(ag(shards["ev_w_out"]), ag(shards["mlp_w1"][0])),
                                    name="l0_in")
    w_ev_out = _unshard_rows(w_ev_out4)
    a_out = _conv_fwd(u0, conv_w, ev_conv_b[0], ev_conv_ln_g[0], ev_conv_ln_b[0], "l0_conv")
    o_raw, b_out, states = _hgrn_fwd(u0, lb0, ev_gnorm_g[0], "l0_hgrn")
    cat0 = jnp.concatenate([a_out, b_out], axis=1).astype(BF16)
    mix0 = _matmul(cat0, w_ev_out, name="l0_out")
    r0a, x0a, x0a_b = _ln_fwd(xin, mix0, ln_mix_g[0], ln_mix_b[0], "l0_ln_mix")
    (z0, h0), (w2_0,) = _matmul(x0a_b, w1_0, b_sharded=True, out_dtype=BF16, epi="relu2",
                                comm=(ag(shards["mlp_w2"][0]),), name="l0_mlp1")
    w2 = [_unshard_rows(w2_0), None]
    f0, (w_od_in4, w_od_out4) = _matmul(h0, w2[0], comm=(ag(shards["od_w_in"]), ag(shards["od_w_out"])), name="l0_mlp2")
    w_od_in = _unshard_cols(w_od_in4)
    w_od_out = _unshard_rows(w_od_out4)
    w_od_main = jnp.concatenate([w_od_in[:, :3 * hw], w_od_in[:, 3 * hw + HEADS:]], axis=1)
    w_od_f = jnp.pad(w_od_in[:, 3 * hw:3 * hw + HEADS], ((0, 0), (0, 128 - HEADS)))
    r0b, x1, x1_b = _ln_fwd(x0a, f0, ln_mlp_g[0], ln_mlp_b[0], "l0_ln_mlp")
    u1, (w1_1,) = _matmul(x1_b, w_od_main, comm=(ag(shards["mlp_w1"][1]),), name="l1_in")
    w1 = [w1_0, w1_1]
    cf = _matmul(x1_b, w_od_f, name="l1_in_f")
    fcum = _fgate_fwd(cf, bf_pad, "l1_fgate")
    f_col = jnp.transpose(fcum[:, :HEADS])[:, :, None]
    f_row = jnp.transpose(fcum[:, :HEADS])[:, None, :]
    c_out, lse = _fox_fwd(u1, f_col, f_row, "l1_fox")
    d_out = _ca_fwd(u1, ca_bias, "l1_ca")
    cat1 = jnp.concatenate([c_out, d_out], axis=1)
    mix1 = _matmul(cat1, w_od_out, name="l1_out")
    r1a, x1a, x1a_b = _ln_fwd(x1, mix1, ln_mix_g[1], ln_mix_b[1], "l1_ln_mix")
    (z1, h1), (w2_1,) = _matmul(x1a_b, w1[1], b_sharded=True, out_dtype=BF16, epi="relu2",
                                comm=(ag(shards["mlp_w2"][1]),), name="l1_mlp1")
    w2[1] = _unshard_rows(w2_1)
    f1 = _matmul(h1, w2[1], name="l1_mlp2")
    r1b, x2, _ = _ln_fwd(x1a, f1, ln_mlp_g[1], ln_mlp_b[1], "l1_ln_mlp")
    dy, loss_part = _loss_head(x2, tgt, "loss")

    g = {}
    dr, drb, dg_, db_ = _ln_bwd(dy, r1b, ln_mlp_g[1], "l1_ln_mlp_bwd")
    g_ln_mlp = [None, (dg_, db_)]
    gw = _shard_rows(_matmul(h1, drb, ta=True, out_dtype=BF16, name="l1_dw2"))
    dz, (gw2_1,) = _matmul(drb, w2[1], tb=True, out_dtype=BF16, epi="drelu2", extra=z1, comm=(a2a(gw),), name="l1_dz")
    gw = _matmul(x1a_b, dz, ta=True, out_dtype=BF16, out_sharded=True, name="l1_dw1")
    dx, (gw1_1,) = _matmul(dz, w1[1], tb=True, b_sharded=True, epi="add", extra=dr, comm=(a2a(gw),), name="l1_dx_mlp")
    dr, drb, dg_, db_ = _ln_bwd(dx, r1a, ln_mix_g[1], "l1_ln_mix_bwd")
    g_ln_mix = [None, (dg_, db_)]
    gw = _shard_rows(_matmul(cat1, drb, ta=True, out_dtype=BF16, name="l1_dwout"))
    dcat, g["od_w_out"] = _matmul(drb, w_od_out, tb=True, comm=(a2a(gw),), name="l1_dcat")
    dk_c, dv_c, dfk = _fox_bwd_kv(u1, c_out, lse, dcat, f_col, f_row, "l1_fox_bwd_kv")
    dq_c, dfq = _fox_bwd_q(u1, c_out, lse, dcat, f_col, f_row, "l1_fox_bwd_q")
    d_f = jnp.pad(jnp.transpose(dfk[:, 0, :] + dfq[:, :, 0]), ((0, 0), (0, 128 - HEADS)))
    dcf, dbf = _fgate_bwd(d_f, cf, bf_pad, "l1_fgate_bwd")
    dq_d, dka, dkb, dva, dvb, dbias = _ca_bwd(u1, ca_bias, dcat, "l1_ca_bwd")
    du1 = jnp.concatenate([dq_c, dk_c, dv_c, dq_d, _ca_merge(dka, dkb), _ca_merge(dva, dvb)], axis=1)
    g_main = _matmul(x1_b, du1, ta=True, out_dtype=BF16, name="l1_dwin")
    g_f = _matmul(x1_b, dcf, ta=True, out_dtype=BF16, name="l1_dwin_f")
    gw = _shard_cols(jnp.concatenate([g_main[:, :3 * hw], g_f[:, :HEADS], g_main[:, 3 * hw:]], axis=1))
    dx_f = _matmul(dcf, w_od_f, tb=True, epi="add", extra=dr, name="l1_dx_f")
    dx, g["od_w_in"] = _matmul(du1, w_od_main, tb=True, epi="add", extra=dx_f, scale=1.0, comm=(a2a(gw),),
                               name="l1_dx_in")
    dr, drb, dg_, db_ = _ln_bwd(dx, r0b, ln_mlp_g[0], "l0_ln_mlp_bwd")
    g_ln_mlp[0] = (dg_, db_)
    gw = _shard_rows(_matmul(h0, drb, ta=True, out_dtype=BF16, name="l0_dw2"))
    dz, (gw2_0,) = _matmul(drb, w2[0], tb=True, out_dtype=BF16, epi="drelu2", extra=z0, comm=(a2a(gw),), name="l0_dz")
    gw = _matmul(x0a_b, dz, ta=True, out_dtype=BF16, out_sharded=True, name="l0_dw1")
    dx, (gw1_0,) = _matmul(dz, w1[0], tb=True, b_sharded=True, epi="add", extra=dr, comm=(a2a(gw),), name="l0_dx_mlp")
    dr, drb, dg_, db_ = _ln_bwd(dx, r0a, ln_mix_g[0], "l0_ln_mix_bwd")
    g_ln_mix[0] = (dg_, db_)
    gw = _shard_rows(_matmul(cat0, drb, ta=True, out_dtype=BF16, name="l0_dwout"))
    dcat, g["ev_w_out"] = _matmul(drb, w_ev_out, tb=True, comm=(a2a(gw),), name="l0_dcat")
    dc, g_conv_w, g_conv_b, g_conv_lg, g_conv_lb = _conv_bwd_params(
        u0, dcat, conv_w, ev_conv_b[0], ev_conv_ln_g[0], ev_conv_ln_b[0], "l0_conv_bwd_p")
    da, dgate = _conv_bwd_input(u0, dc, conv_w, "l0_conv_bwd_i")
    dhq, dhf, dhi, dhg, g_lb0, g_gnorm = _hgrn_bwd(u0, o_raw, states, dcat, lb0, ev_gnorm_g[0], "l0_hgrn_bwd")
    du0 = jnp.concatenate([da, dgate, dhq, dhf, dhi, dhg], axis=1)
    gw = _matmul(xin.astype(BF16), du0, ta=True, out_dtype=BF16, out_sharded=True, name="l0_dwin")
    grad_x, g["ev_w_in"] = _matmul(du0, w_ev_in, tb=True, b_sharded=True, epi="add", extra=dr, comm=(a2a(gw),),
                                   name="l0_dx_in")
    g["mlp_w1"] = [gw1_0, gw1_1]
    g["mlp_w2"] = [gw2_0, gw2_1]

    g_lb_logits = jax.vjp(_lower_bound, hgrn_lb_logits)[1](g_lb0)[0]
    g_rel = _ca_bias_grad(dbias)
    small_partial = dict(
        ev_conv_w=g_conv_w, ev_conv_b=g_conv_b, ev_conv_ln_g=g_conv_lg, ev_conv_ln_b=g_conv_lb,
        hgrn_lb_logits=g_lb_logits, ev_gnorm_g=g_gnorm, fox_b_f=dbf[:HEADS], rel_bias=g_rel,
        ln_mix_g=jnp.stack([g_ln_mix[0][0], g_ln_mix[1][0]]), ln_mix_b=jnp.stack([g_ln_mix[0][1], g_ln_mix[1][1]]),
        ln_mlp_g=jnp.stack([g_ln_mlp[0][0], g_ln_mlp[1][0]]), ln_mlp_b=jnp.stack([g_ln_mlp[0][1], g_ln_mlp[1][1]]))
    return loss_part, grad_x, g, small_partial
```

```python
import functools

import jax
import jax.numpy as jnp
from jax import lax
from jax.experimental import pallas as pl
from jax.experimental.pallas import tpu as pltpu

F32 = jnp.float32
BF16 = jnp.bfloat16
MESH = pl.DeviceIdType.MESH

DEPTH = 2
ALPHA = (2 * DEPTH) ** 0.25
LN_EPS = 1e-5
HEADS = 8
DH = 128
CHUNK = 64
CONV_WIDTH = 31
HALO = 32
CA_LEFT = 8
CA_BLK = CA_LEFT * CHUNK
CA_WIN = (CA_LEFT + 2) * CHUNK
REL_CLIP = 256
REL_TABLE = (CHUNK - 1) + REL_CLIP + 1
NEG = -1e30

ADAM_LR = 0.001
ADAM_B1 = 0.9
ADAM_B2 = 0.999
ADAM_EPS = 1e-08
ADAM_WD = 0.01
ADAM_STEP = 10

VMEM_LIMIT = 48 * 1024 * 1024


def _cparams(sem):
    return pltpu.CompilerParams(dimension_semantics=sem, vmem_limit_bytes=VMEM_LIMIT)


def _pick(n, cands):
    for c in cands:
        if n % c == 0:
            return c
    return n


def _sigmoid(x):
    return 1.0 / (1.0 + jnp.exp(-x))


def _dot(a, b, dims):
    return lax.dot_general(a.astype(BF16), b.astype(BF16), (dims, ((), ())), preferred_element_type=F32)


def _nn(a, b):
    return _dot(a, b, ((1,), (0,)))


def _nt(a, b):
    return _dot(a, b, ((1,), (1,)))


def _tn(a, b):
    return _dot(a, b, ((0,), (0,)))


def _dot3(a, b, dims):
    a_hi = a.astype(BF16)
    b_hi = b.astype(BF16)
    a_lo = (a - a_hi.astype(F32)).astype(BF16)
    b_lo = (b - b_hi.astype(F32)).astype(BF16)
    dn = (dims, ((), ()))
    return (lax.dot_general(a_hi, b_hi, dn, preferred_element_type=F32)
            + (lax.dot_general(a_hi, b_lo, dn, preferred_element_type=F32)
               + lax.dot_general(a_lo, b_hi, dn, preferred_element_type=F32)))


def _nn3(a, b):
    return _dot3(a, b, ((1,), (0,)))


def _nt3(a, b):
    return _dot3(a, b, ((1,), (1,)))


def _tn3(a, b):
    return _dot3(a, b, ((0,), (0,)))


def _split3(x):
    hi = x.astype(BF16)
    r1 = x - hi.astype(F32)
    mid = r1.astype(BF16)
    lo = (r1 - mid.astype(F32)).astype(BF16)
    return hi, mid, lo


def _tri_matmul(tri, x, terms):
    parts = _split3(x)[:terms]
    acc = None
    for p in parts:
        t = lax.dot_general(tri, p, (((1,), (0,)), ((), ())), preferred_element_type=F32)
        acc = t if acc is None else acc + t
    return acc


def _tril(n, upper=False):
    r = lax.broadcasted_iota(jnp.int32, (n, n), 0)
    c = lax.broadcasted_iota(jnp.int32, (n, n), 1)
    m = (c >= r) if upper else (c <= r)
    return jnp.where(m, 1.0, 0.0).astype(BF16)


def _matmul(a, b, *, ta=False, tb=False, out_dtype=F32, epi=None, extra=None, scale=ALPHA, b_sharded=False,
            out_sharded=False, comm=(), name):
    m = a.shape[1] if ta else a.shape[0]
    kd = a.shape[0] if ta else a.shape[1]
    if b_sharded:
        shard = b.shape[2]
        n = b.shape[1] if tb else 4 * shard
        assert (b.shape[1] if not tb else 4 * shard) == kd
    else:
        n = b.shape[0] if tb else b.shape[1]
    bm = _pick(m, (1024, 512, 256, 128))
    bn = _pick(shard if (b_sharded and not tb) else (n // 4 if out_sharded else n), (1024, 768, 512, 256, 128))
    bk = _pick(shard if (b_sharded and tb) else kd, (2048, 1536, 1024, 768, 512, 256, 128))
    ni, nj, nk = m // bm, n // bn, kd // bk
    n_out = 2 if epi == "relu2" else 1
    n_comm = len(comm)
    kinds = [c[0] for c in comm]

    def body(*refs):
        a_ref, b_ref = refs[0], refs[1]
        pos = 2
        e_ref = refs[pos] if extra is not None else None
        pos += extra is not None
        c_in = refs[pos:pos + n_comm]
        pos += n_comm
        outs = refs[pos:pos + n_out]
        pos += n_out
        c_out = refs[pos:pos + n_comm]
        pos += n_comm
        acc_ref = refs[pos]
        sems = refs[pos + 1:]
        i, j, k = pl.program_id(0), pl.program_id(1), pl.program_id(2)

        if n_comm:
            @pl.when(jnp.logical_and(jnp.logical_and(i == 0, j == 0), k == 0))
            def _():
                _exchange(kinds, c_in, c_out, *sems, start=True)

        dims = ((0 if ta else 1,), (1 if tb else 0,))
        part = _dot(a_ref[...], b_ref[...], dims)
        if nk > 1:
            @pl.when(k == 0)
            def _():
                acc_ref[...] = part

            @pl.when(jnp.logical_and(k > 0, k < nk - 1))
            def _():
                acc_ref[...] += part

        @pl.when(k == nk - 1)
        def _():
            r = part + acc_ref[...] if nk > 1 else part
            if epi == "relu2":
                outs[0][...] = r.astype(out_dtype)
                outs[1][...] = jnp.square(jnp.maximum(r, 0.0)).astype(out_dtype)
            elif epi == "drelu2":
                outs[0][...] = (r * (2.0 * jnp.maximum(e_ref[...].astype(F32), 0.0))).astype(out_dtype)
            elif epi == "add":
                outs[0][...] = (r + scale * e_ref[...].astype(F32)).astype(out_dtype)
            else:
                outs[0][...] = r.astype(out_dtype)

        if n_comm:
            @pl.when(jnp.logical_and(jnp.logical_and(i == ni - 1, j == nj - 1), k == nk - 1))
            def _():
                _exchange(kinds, c_in, c_out, *sems, start=False)

    a_spec = pl.BlockSpec((bk, bm), lambda i, j, k: (k, i)) if ta else pl.BlockSpec((bm, bk), lambda i, j, k: (i, k))
    if b_sharded and tb:
        per = shard // bk
        b_spec = pl.BlockSpec((None, bn, bk), lambda i, j, k: (k // per, j, k % per))
    elif b_sharded:
        per = shard // bn
        b_spec = pl.BlockSpec((None, bk, bn), lambda i, j, k: (j // per, k, j % per))
    elif tb:
        b_spec = pl.BlockSpec((bn, bk), lambda i, j, k: (j, k))
    else:
        b_spec = pl.BlockSpec((bk, bn), lambda i, j, k: (k, j))
    e_spec = pl.BlockSpec((bm, bn), lambda i, j, k: (i, j))
    if out_sharded:
        per_o = (n // 4) // bn
        o_spec = pl.BlockSpec((None, bm, bn), lambda i, j, k: (j // per_o, i, j % per_o))
        o_shape = jax.ShapeDtypeStruct((4, m, n // 4), out_dtype)
    else:
        o_spec = e_spec
        o_shape = jax.ShapeDtypeStruct((m, n), out_dtype)
    in_specs = [a_spec, b_spec] + ([e_spec] if extra is not None else []) + [_ANY] * n_comm
    args = (a, b) + ((extra,) if extra is not None else ()) + tuple(c[1] for c in comm)
    c_shapes = [jax.ShapeDtypeStruct((4,) + c[1].shape if c[0] == "ag" else c[1].shape, c[1].dtype) for c in comm]
    scratch = [pltpu.VMEM((bm, bn) if nk > 1 else (8, 128), F32)]
    if n_comm:
        scratch += [pltpu.SemaphoreType.DMA((3 * n_comm,)), pltpu.SemaphoreType.DMA((3 * n_comm,)),
                    pltpu.SemaphoreType.DMA((n_comm,))]
    sem = ("arbitrary",) * 3 if n_comm else ("parallel", "parallel", "arbitrary")
    res = pl.pallas_call(
        body, grid=(ni, nj, nk), in_specs=in_specs,
        out_specs=[o_spec] * n_out + [_ANY] * n_comm, out_shape=[o_shape] * n_out + c_shapes,
        scratch_shapes=scratch, compiler_params=_cparams(sem), name=name)(*args)
    main = tuple(res[:n_out]) if n_out == 2 else res[0]
    return (main, list(res[n_out:])) if n_comm else main


def _ln_fwd(x, mix, g, b, name):
    s, d = x.shape
    br = _pick(s, (256, 128, 64, 8))

    def body(x_ref, m_ref, g_ref, b_ref, r_ref, y_ref, yb_ref):
        r = ALPHA * x_ref[...] + m_ref[...]
        mu = jnp.mean(r, axis=-1, keepdims=True)
        dlt = r - mu
        var = jnp.mean(dlt * dlt, axis=-1, keepdims=True)
        y = dlt * lax.rsqrt(var + LN_EPS) * g_ref[...] + b_ref[...]
        r_ref[...] = r
        y_ref[...] = y
        yb_ref[...] = y.astype(BF16)

    row = pl.BlockSpec((br, d), lambda i: (i, 0))
    vec = pl.BlockSpec((1, d), lambda i: (0, 0))
    return pl.pallas_call(
        body, grid=(s // br,), in_specs=[row, row, vec, vec], out_specs=[row, row, row],
        out_shape=[jax.ShapeDtypeStruct((s, d), F32), jax.ShapeDtypeStruct((s, d), F32),
                   jax.ShapeDtypeStruct((s, d), BF16)],
        compiler_params=_cparams(("parallel",)), name=name)(x, mix, g.reshape(1, d), b.reshape(1, d))


def _ln_bwd(dy, r, g, name):
    s, d = r.shape
    br = _pick(s, (256, 128, 64, 8))

    def body(dy_ref, r_ref, g_ref, dr_ref, drb_ref, dg_ref, db_ref):
        @pl.when(pl.program_id(0) == 0)
        def _():
            dg_ref[...] = jnp.zeros_like(dg_ref)
            db_ref[...] = jnp.zeros_like(db_ref)

        rv = r_ref[...]
        dyv = dy_ref[...]
        mu = jnp.mean(rv, axis=-1, keepdims=True)
        dlt = rv - mu
        var = jnp.mean(dlt * dlt, axis=-1, keepdims=True)
        rstd = lax.rsqrt(var + LN_EPS)
        xhat = dlt * rstd
        dxh = dyv * g_ref[...]
        m1 = jnp.mean(dxh, axis=-1, keepdims=True)
        m2 = jnp.mean(dxh * xhat, axis=-1, keepdims=True)
        dr = rstd * (dxh - m1 - xhat * m2)
        dr_ref[...] = dr
        drb_ref[...] = dr.astype(BF16)
        dg_ref[...] += jnp.sum(dyv * xhat, axis=0, keepdims=True)
        db_ref[...] += jnp.sum(dyv, axis=0, keepdims=True)

    row = pl.BlockSpec((br, d), lambda i: (i, 0))
    vec = pl.BlockSpec((1, d), lambda i: (0, 0))
    dr, drb, dg, db = pl.pallas_call(
        body, grid=(s // br,), in_specs=[row, row, vec], out_specs=[row, row, vec, vec],
        out_shape=[jax.ShapeDtypeStruct((s, d), F32), jax.ShapeDtypeStruct((s, d), BF16),
                   jax.ShapeDtypeStruct((1, d), F32), jax.ShapeDtypeStruct((1, d), F32)],
        compiler_params=_cparams(("arbitrary",)), name=name)(dy, r, g.reshape(1, d))
    return dr, drb, dg[0], db[0]


def _loss_head(y, tgt, name):
    s, d = y.shape
    br = _pick(s, (256, 128, 64, 8))

    def body(y_ref, t_ref, dy_ref, l_ref):
        @pl.when(pl.program_id(0) == 0)
        def _():
            l_ref[...] = jnp.zeros_like(l_ref)

        e = y_ref[...] - t_ref[...]
        dy_ref[...] = e * (1.0 / d)
        rows = jnp.sum(e * e, axis=-1, keepdims=True) * (0.5 / d)
        l_ref[...] += jnp.sum(rows, axis=0, keepdims=True)

    row = pl.BlockSpec((br, d), lambda i: (i, 0))
    dy, l = pl.pallas_call(
        body, grid=(s // br,), in_specs=[row, row],
        out_specs=[row, pl.BlockSpec((1, 1), lambda i: (0, 0))],
        out_shape=[jax.ShapeDtypeStruct((s, d), F32), jax.ShapeDtypeStruct((1, 1), F32)],
        compiler_params=_cparams(("arbitrary",)), name=name)(y, tgt)
    return dy, l[0, 0]


def _conv_recompute(i, a_ref, gt_ref, ah_ref, gh_ref, w_ref, cb_ref, hext_ref, tt):
    h = a_ref[...] * _sigmoid(gt_ref[...])
    hh = ah_ref[...] * _sigmoid(gh_ref[...])
    hh = jnp.where(i > 0, hh, 0.0)
    hext_ref[0:HALO, :] = hh
    hext_ref[HALO:HALO + tt, :] = h
    acc = jnp.zeros_like(h) + cb_ref[...]
    off = HALO - (CONV_WIDTH - 1)
    for j in range(CONV_WIDTH):
        acc = acc + w_ref[j:j + 1, :] * hext_ref[pl.ds(off + j, tt), :]
    mu = jnp.mean(acc, axis=-1, keepdims=True)
    dlt = acc - mu
    var = jnp.mean(dlt * dlt, axis=-1, keepdims=True)
    rstd = lax.rsqrt(var + LN_EPS)
    return dlt * rstd, rstd


def _conv_specs(tt, cc, s):
    nh = tt // HALO
    cur = lambda cb: pl.BlockSpec((tt, cc), lambda i: (i, cb))
    prev = lambda cb: pl.BlockSpec((HALO, cc), lambda i: (jnp.maximum(i * nh - 1, 0), cb))
    vec = pl.BlockSpec((1, cc), lambda i: (0, 0))
    wsp = pl.BlockSpec((HALO, cc), lambda i: (0, 0))
    return cur, prev, vec, wsp


def _pad_conv_w(w):
    return jnp.concatenate([w, jnp.zeros((HALO - CONV_WIDTH, w.shape[1]), F32)], axis=0)


def _conv_fwd(u, w, cb, lg, lb, name):
    s = u.shape[0]
    cc = w.shape[1]
    tt = _pick(s, (256, 128, 64))
    cur, prev, vec, wsp = _conv_specs(tt, cc, s)

    def body(a_ref, gt_ref, ah_ref, gh_ref, w_ref, cb_ref, lg_ref, lb_ref, o_ref, hext_ref):
        xhat, _ = _conv_recompute(pl.program_id(0), a_ref, gt_ref, ah_ref, gh_ref, w_ref, cb_ref, hext_ref, tt)
        nrm = xhat * lg_ref[...] + lb_ref[...]
        o_ref[...] = nrm * _sigmoid(nrm)

    return pl.pallas_call(
        body, grid=(s // tt,), in_specs=[cur(0), cur(1), prev(0), prev(1), wsp, vec, vec, vec],
        out_specs=pl.BlockSpec((tt, cc), lambda i: (i, 0)), out_shape=jax.ShapeDtypeStruct((s, cc), F32),
        scratch_shapes=[pltpu.VMEM((tt + HALO, cc), F32)],
        compiler_params=_cparams(("parallel",)), name=name)(
            u, u, u, u, _pad_conv_w(w), cb.reshape(1, cc), lg.reshape(1, cc), lb.reshape(1, cc))


def _conv_bwd_params(u, dout, w, cb, lg, lb, name):
    s = u.shape[0]
    cc = w.shape[1]
    tt = _pick(s, (256, 128, 64))
    cur, prev, vec, wsp = _conv_specs(tt, cc, s)

    def body(a_ref, gt_ref, ah_ref, gh_ref, w_ref, cb_ref, lg_ref, lb_ref, do_ref,
             dc_ref, dw_ref, dcb_ref, dlg_ref, dlb_ref, hext_ref):
        i = pl.program_id(0)

        @pl.when(i == 0)
        def _():
            dw_ref[...] = jnp.zeros_like(dw_ref)
            dcb_ref[...] = jnp.zeros_like(dcb_ref)
            dlg_ref[...] = jnp.zeros_like(dlg_ref)
            dlb_ref[...] = jnp.zeros_like(dlb_ref)

        xhat, rstd = _conv_recompute(i, a_ref, gt_ref, ah_ref, gh_ref, w_ref, cb_ref, hext_ref, tt)
        nrm = xhat * lg_ref[...] + lb_ref[...]
        sg = _sigmoid(nrm)
        dn = do_ref[...] * (sg * (1.0 + nrm * (1.0 - sg)))
        dxh = dn * lg_ref[...]
        m1 = jnp.mean(dxh, axis=-1, keepdims=True)
        m2 = jnp.mean(dxh * xhat, axis=-1, keepdims=True)
        dc = rstd * (dxh - m1 - xhat * m2)
        dc_ref[...] = dc
        dlg_ref[...] += jnp.sum(dn * xhat, axis=0, keepdims=True)
        dlb_ref[...] += jnp.sum(dn, axis=0, keepdims=True)
        dcb_ref[...] += jnp.sum(dc, axis=0, keepdims=True)
        off = HALO - (CONV_WIDTH - 1)
        for j in range(CONV_WIDTH):
            dw_ref[j:j + 1, :] += jnp.sum(dc * hext_ref[pl.ds(off + j, tt), :], axis=0, keepdims=True)

    dcol = pl.BlockSpec((tt, cc), lambda i: (i, 0))
    dc, dw, dcb, dlg, dlb = pl.pallas_call(
        body, grid=(s // tt,), in_specs=[cur(0), cur(1), prev(0), prev(1), wsp, vec, vec, vec, dcol],
        out_specs=[dcol, wsp, vec, vec, vec],
        out_shape=[jax.ShapeDtypeStruct((s, cc), F32), jax.ShapeDtypeStruct((HALO, cc), F32)]
        + [jax.ShapeDtypeStruct((1, cc), F32)] * 3,
        scratch_shapes=[pltpu.VMEM((tt + HALO, cc), F32)],
        compiler_params=_cparams(("arbitrary",)), name=name)(
            u, u, u, u, _pad_conv_w(w), cb.reshape(1, cc), lg.reshape(1, cc), lb.reshape(1, cc), dout)
    return dc, dw[:CONV_WIDTH], dcb[0], dlg[0], dlb[0]


def _conv_bwd_input(u, dc, w, name):
    s = u.shape[0]
    cc = w.shape[1]
    tt = _pick(s, (256, 128, 64))
    nh = tt // HALO
    nlast = s // HALO - 1
    cur = lambda cb: pl.BlockSpec((tt, cc), lambda i: (i, cb))
    nxt = pl.BlockSpec((HALO, cc), lambda i: (jnp.minimum((i + 1) * nh, nlast), 0))
    wsp = pl.BlockSpec((HALO, cc), lambda i: (0, 0))
    nblk = s // tt

    def body(a_ref, gt_ref, dc_ref, dn_ref, w_ref, da_ref, dg_ref, ext_ref):
        i = pl.program_id(0)
        ext_ref[0:tt, :] = dc_ref[...]
        ext_ref[tt:tt + HALO, :] = jnp.where(i < nblk - 1, dn_ref[...], 0.0)
        dh = jnp.zeros((tt, cc), F32)
        for j in range(CONV_WIDTH):
            dh = dh + w_ref[j:j + 1, :] * ext_ref[pl.ds(CONV_WIDTH - 1 - j, tt), :]
        a = a_ref[...]
        sg = _sigmoid(gt_ref[...])
        da_ref[...] = (dh * sg).astype(BF16)
        dg_ref[...] = (dh * a * sg * (1.0 - sg)).astype(BF16)

    ocol = pl.BlockSpec((tt, cc), lambda i: (i, 0))
    return pl.pallas_call(
        body, grid=(nblk,), in_specs=[cur(0), cur(1), ocol, nxt, wsp], out_specs=[ocol, ocol],
        out_shape=[jax.ShapeDtypeStruct((s, cc), BF16)] * 2,
        scratch_shapes=[pltpu.VMEM((tt + HALO, cc), F32)],
        compiler_params=_cparams(("parallel",)), name=name)(u, u, dc, dc, _pad_conv_w(w))


def _hgrn_gates(hq, hf, lb):
    sg = _sigmoid(hf)
    f = lb + (1.0 - lb) * sg
    lf = jnp.log(f)
    big_l = _tri_matmul(_tril(CHUNK), lf, 3)
    l_end = jnp.sum(lf, axis=0, keepdims=True)
    l_mid = jnp.sum(lf[0:CHUNK // 2, :], axis=0, keepdims=True)
    sq = _sigmoid(hq)
    q = hq * sq
    return sg, f, 1.0 - f, big_l, l_end, l_mid, sq, q


def _causal_mask(n):
    r = lax.broadcasted_iota(jnp.int32, (n, n), 0)
    c = lax.broadcasted_iota(jnp.int32, (n, n), 1)
    return c <= r


def _hgrn_fwd(u, lb, gg, name):
    s = u.shape[0]
    w = HEADS * DH
    nch = s // CHUNK

    def body(q_ref, f_ref, i_ref, g_ref, lb_ref, gg_ref, o_ref, out_ref, st_ref, state):
        @pl.when(pl.program_id(0) == 0)
        def _():
            state[...] = jnp.zeros_like(state)

        mask = _causal_mask(CHUNK)
        for hd in range(HEADS):
            sl = slice(hd * DH, (hd + 1) * DH)
            _, _, kk, big_l, l_end, l_mid, _, q = _hgrn_gates(q_ref[:, sl], f_ref[:, sl], lb_ref[:, sl])
            v = i_ref[:, sl]
            qs = q * jnp.exp(big_l - l_mid)
            ks = kk * jnp.exp(l_mid - big_l)
            att = jnp.where(mask, _nt3(qs, ks), 0.0)
            st0 = state[hd]
            st_ref[0, hd] = st0
            o = _nn3(att, v) + _nt3(q * jnp.exp(big_l), st0)
            state[hd] = st0 * jnp.exp(l_end) + _tn3(v, kk * jnp.exp(l_end - big_l))
            o_ref[:, sl] = o
            on = o * lax.rsqrt(jnp.mean(o * o, axis=-1, keepdims=True) + LN_EPS)
            gv = g_ref[:, sl]
            out_ref[:, sl] = on * gg_ref[:, sl] * (gv * _sigmoid(gv))

    col = lambda cb: pl.BlockSpec((CHUNK, w), lambda i: (i, cb))
    vec = pl.BlockSpec((1, w), lambda i: (0, 0))
    ocol = pl.BlockSpec((CHUNK, w), lambda i: (i, 0))
    return pl.pallas_call(
        body, grid=(nch,), in_specs=[col(2), col(3), col(4), col(5), vec, vec],
        out_specs=[ocol, ocol, pl.BlockSpec((1, HEADS, DH, DH), lambda i: (i, 0, 0, 0))],
        out_shape=[jax.ShapeDtypeStruct((s, w), F32), jax.ShapeDtypeStruct((s, w), F32),
                   jax.ShapeDtypeStruct((nch, HEADS, DH, DH), F32)],
        scratch_shapes=[pltpu.VMEM((HEADS, DH, DH), F32)],
        compiler_params=_cparams(("arbitrary",)), name=name)(u, u, u, u, lb.reshape(1, w), gg.reshape(1, w))


def _hgrn_bwd(u, o_raw, states, dout, lb, gg, name):
    s = u.shape[0]
    w = HEADS * DH
    nch = s // CHUNK

    def body(q_ref, f_ref, i_ref, g_ref, o_ref, st_ref, do_ref, lb_ref, gg_ref,
             dq_ref, df_ref, di_ref, dg_ref, dlb_ref, dgg_ref, dstate):
        @pl.when(pl.program_id(0) == 0)
        def _():
            dstate[...] = jnp.zeros_like(dstate)
            dlb_ref[...] = jnp.zeros_like(dlb_ref)
            dgg_ref[...] = jnp.zeros_like(dgg_ref)

        mask = _causal_mask(CHUNK)
        last_row = lax.broadcasted_iota(jnp.int32, (CHUNK, DH), 0) == CHUNK - 1
        tri_up = _tril(CHUNK, upper=True)
        for hd in range(HEADS):
            sl = slice(hd * DH, (hd + 1) * DH)
            hq = q_ref[:, sl]
            lbv = lb_ref[:, sl]
            sg, f, kk, big_l, l_end, l_mid, sq, q = _hgrn_gates(hq, f_ref[:, sl], lbv)
            v = i_ref[:, sl]
            e_l = jnp.exp(big_l)
            e_qm = jnp.exp(big_l - l_mid)
            e_km = jnp.exp(l_mid - big_l)
            e_ke = jnp.exp(l_end - big_l)
            e_end = jnp.exp(l_end)
            qs = q * e_qm
            ks = kk * e_km
            qe = q * e_l
            ke = kk * e_ke
            att = jnp.where(mask, _nt3(qs, ks), 0.0)
            st0 = st_ref[0, hd]
            dst1 = dstate[hd]
            o = o_ref[:, sl]
            rinv = lax.rsqrt(jnp.mean(o * o, axis=-1, keepdims=True) + LN_EPS)
            on = o * rinv
            gv = g_ref[:, sl]
            sgg = _sigmoid(gv)
            gsil = gv * sgg
            ggv = gg_ref[:, sl]
            dov = do_ref[:, sl]
            don = dov * ggv * gsil
            dg_ref[:, sl] = (dov * on * ggv * (sgg * (1.0 + gv * (1.0 - sgg)))).astype(BF16)
            dgg_ref[:, sl] += jnp.sum(dov * on * gsil, axis=0, keepdims=True)
            do = rinv * (don - on * jnp.mean(don * on, axis=-1, keepdims=True))
            datt = jnp.where(mask, _nt3(do, v), 0.0)
            dv = _tn3(att, do) + _nt3(ke, dst1)
            dqs = _nn3(datt, ks)
            dks = _tn3(datt, qs)
            dqe = _nn3(do, st0)
            dke = _nn3(v, dst1)
            dq = dqs * e_qm + dqe * e_l
            dk = dks * e_km + dke * e_ke
            dke_ke = dke * ke
            dl = dqs * qs - dks * ks + dqe * qe - dke_ke
            dl_end = jnp.sum(dke_ke, axis=0, keepdims=True) + jnp.sum(dst1 * st0, axis=0, keepdims=True) * e_end
            dl = dl + jnp.where(last_row, dl_end, 0.0)
            dlf = _tri_matmul(tri_up, dl, 2)
            dfv = dlf / f - dk
            df_ref[:, sl] = (dfv * (1.0 - lbv) * sg * (1.0 - sg)).astype(BF16)
            dlb_ref[:, sl] += jnp.sum(dfv * (1.0 - sg), axis=0, keepdims=True)
            dq_ref[:, sl] = (dq * (sq * (1.0 + hq * (1.0 - sq)))).astype(BF16)
            di_ref[:, sl] = dv.astype(BF16)
            dstate[hd] = dst1 * e_end + _tn3(do, qe)

    rev = lambda i: nch - 1 - i
    col = lambda cb: pl.BlockSpec((CHUNK, w), lambda i: (rev(i), cb))
    vec = pl.BlockSpec((1, w), lambda i: (0, 0))
    ocol = pl.BlockSpec((CHUNK, w), lambda i: (rev(i), 0))
    res = pl.pallas_call(
        body, grid=(nch,),
        in_specs=[col(2), col(3), col(4), col(5), ocol,
                  pl.BlockSpec((1, HEADS, DH, DH), lambda i: (rev(i), 0, 0, 0)), col(1), vec, vec],
        out_specs=[ocol, ocol, ocol, ocol, vec, vec],
        out_shape=[jax.ShapeDtypeStruct((s, w), BF16)] * 4 + [jax.ShapeDtypeStruct((1, w), F32)] * 2,
        scratch_shapes=[pltpu.VMEM((HEADS, DH, DH), F32)],
        compiler_params=_cparams(("arbitrary",)), name=name)(
            u, u, u, u, o_raw, states, dout, lb.reshape(1, w), gg.reshape(1, w))
    return res[0], res[1], res[2], res[3], res[4][0], res[5][0]


def _log_sigmoid(x):
    return jnp.minimum(x, 0.0) - jnp.log(1.0 + jnp.exp(-jnp.abs(x)))


def _fgate_fwd(cf, bf, name):
    s, wl = cf.shape
    tb = _pick(s, (512, 256, 128, 64))

    def body(c_ref, b_ref, f_ref, carry):
        @pl.when(pl.program_id(0) == 0)
        def _():
            carry[...] = jnp.zeros_like(carry)

        ls = _log_sigmoid(c_ref[...] + b_ref[...])
        f_ref[...] = _tri_matmul(_tril(tb), ls, 3) + carry[...]
        carry[...] += jnp.sum(ls, axis=0, keepdims=True)

    return pl.pallas_call(
        body, grid=(s // tb,), in_specs=[pl.BlockSpec((tb, wl), lambda i: (i, 0)), pl.BlockSpec((1, wl), lambda i: (0, 0))],
        out_specs=pl.BlockSpec((tb, wl), lambda i: (i, 0)), out_shape=jax.ShapeDtypeStruct((s, wl), F32),
        scratch_shapes=[pltpu.VMEM((1, wl), F32)],
        compiler_params=_cparams(("arbitrary",)), name=name)(cf, bf)


def _fgate_bwd(dF, cf, bf, name):
    s, wl = cf.shape
    tb = _pick(s, (512, 256, 128, 64))
    nb = s // tb

    def body(d_ref, c_ref, b_ref, dc_ref, db_ref, carry):
        @pl.when(pl.program_id(0) == 0)
        def _():
            carry[...] = jnp.zeros_like(carry)
            db_ref[...] = jnp.zeros_like(db_ref)

        dv = d_ref[...]
        dls = _tri_matmul(_tril(tb, upper=True), dv, 3) + carry[...]
        carry[...] += jnp.sum(dv, axis=0, keepdims=True)
        dc = dls * (1.0 - _sigmoid(c_ref[...] + b_ref[...]))
        dc_ref[...] = dc.astype(BF16)
        db_ref[...] += jnp.sum(dc, axis=0, keepdims=True)

    blk = pl.BlockSpec((tb, wl), lambda i: (nb - 1 - i, 0))
    vec = pl.BlockSpec((1, wl), lambda i: (0, 0))
    dc, db = pl.pallas_call(
        body, grid=(nb,), in_specs=[blk, blk, vec], out_specs=[blk, vec],
        out_shape=[jax.ShapeDtypeStruct((s, wl), BF16), jax.ShapeDtypeStruct((1, wl), F32)],
        scratch_shapes=[pltpu.VMEM((1, wl), F32)],
        compiler_params=_cparams(("arbitrary",)), name=name)(dF, cf, bf)
    return dc, db[0]


FOX_HPB = 2


def _fox_scores(q, k, fq, fk, diag):
    sc = _nt(q, k) * (DH ** -0.5) + fq - fk
    if not diag:
        return sc
    r = lax.broadcasted_iota(jnp.int32, sc.shape, 0)
    c = lax.broadcasted_iota(jnp.int32, sc.shape, 1)
    return jnp.where(c <= r, sc, NEG)


def _fox_when_needed(q_blk, k_blk, step):
    @pl.when(k_blk < q_blk)
    def _():
        step(False)

    @pl.when(k_blk == q_blk)
    def _():
        step(True)


def _fox_fwd(u, f_col, f_row, name):
    s = u.shape[0]
    t = _pick(s, (512, 256, 128))
    nb = s // t
    hpb = FOX_HPB
    wb = hpb * DH
    ng = HEADS // hpb

    def body(q_ref, k_ref, v_ref, fq_ref, fk_ref, o_ref, lse_ref, m_sc, l_sc, acc):
        i, j = pl.program_id(1), pl.program_id(2)

        @pl.when(j == 0)
        def _():
            m_sc[...] = jnp.full_like(m_sc, NEG)
            l_sc[...] = jnp.zeros_like(l_sc)
            acc[...] = jnp.zeros_like(acc)

        def step(diag):
            for hh in range(hpb):
                sl = slice(hh * DH, (hh + 1) * DH)
                sc = _fox_scores(q_ref[:, sl], k_ref[:, sl], fq_ref[hh], fk_ref[hh], diag)
                m_new = jnp.maximum(m_sc[hh], jnp.max(sc, axis=-1, keepdims=True))
                a = jnp.exp(m_sc[hh] - m_new)
                p = jnp.exp(sc - m_new)
                l_sc[hh] = a * l_sc[hh] + jnp.sum(p, axis=-1, keepdims=True)
                acc[:, sl] = a * acc[:, sl] + _nn(p, v_ref[:, sl])
                m_sc[hh] = m_new

        _fox_when_needed(i, j, step)

        @pl.when(j == i)
        def _():
            for hh in range(hpb):
                sl = slice(hh * DH, (hh + 1) * DH)
                o_ref[:, sl] = acc[:, sl] / l_sc[hh]
                lse_ref[hh] = m_sc[hh] + jnp.log(l_sc[hh])

    qs = pl.BlockSpec((t, wb), lambda h, i, j: (i, h))
    ks = lambda base: pl.BlockSpec((t, wb), lambda h, i, j: (jnp.minimum(j, i), base + h))
    return pl.pallas_call(
        body, grid=(ng, nb, nb),
        in_specs=[qs, ks(ng), ks(2 * ng),
                  pl.BlockSpec((hpb, t, 1), lambda h, i, j: (h, i, 0)),
                  pl.BlockSpec((hpb, 1, t), lambda h, i, j: (h, 0, jnp.minimum(j, i)))],
        out_specs=[pl.BlockSpec((t, wb), lambda h, i, j: (i, h)), pl.BlockSpec((hpb, t, 1), lambda h, i, j: (h, i, 0))],
        out_shape=[jax.ShapeDtypeStruct((s, HEADS * DH), F32), jax.ShapeDtypeStruct((HEADS, s, 1), F32)],
        scratch_shapes=[pltpu.VMEM((hpb, t, 1), F32), pltpu.VMEM((hpb, t, 1), F32), pltpu.VMEM((t, wb), F32)],
        compiler_params=_cparams(("parallel", "parallel", "arbitrary")), name=name)(u, u, u, f_col, f_row)


def _fox_bwd_kv(u, o, lse, dout, f_col, f_row, name):
    s = u.shape[0]
    t = _pick(s, (512, 256, 128))
    nb = s // t
    hpb = FOX_HPB
    wb = hpb * DH
    ng = HEADS // hpb

    def body(q_ref, k_ref, v_ref, o_ref, do_ref, lse_ref, fq_ref, fk_ref, dk_ref, dv_ref, dfk_ref, dk_acc, dv_acc, df_acc):
        j, i = pl.program_id(1), pl.program_id(2)

        @pl.when(i == 0)
        def _():
            dk_acc[...] = jnp.zeros_like(dk_acc)
            dv_acc[...] = jnp.zeros_like(dv_acc)
            df_acc[...] = jnp.zeros_like(df_acc)

        def step(diag):
            for hh in range(hpb):
                sl = slice(hh * DH, (hh + 1) * DH)
                q = q_ref[:, sl]
                do = do_ref[:, sl]
                sc = _fox_scores(q, k_ref[:, sl], fq_ref[hh], fk_ref[hh], diag)
                p = jnp.exp(sc - lse_ref[hh])
                delta = jnp.sum(do * o_ref[:, sl], axis=-1, keepdims=True)
                dv_acc[:, sl] += _tn(p, do)
                ds = p * (_nt(do, v_ref[:, sl]) - delta)
                dk_acc[:, sl] += _tn(ds, q) * (DH ** -0.5)
                df_acc[hh] -= jnp.sum(ds, axis=0, keepdims=True)

        _fox_when_needed(i, j, step)

        @pl.when(i == nb - 1)
        def _():
            dk_ref[...] = dk_acc[...].astype(BF16)
            dv_ref[...] = dv_acc[...].astype(BF16)
            dfk_ref[...] = df_acc[...]

    qi = lambda j, i: jnp.maximum(i, j)
    qs = lambda base: pl.BlockSpec((t, wb), lambda h, j, i: (qi(j, i), base + h))
    ks = lambda base: pl.BlockSpec((t, wb), lambda h, j, i: (j, base + h))
    return pl.pallas_call(
        body, grid=(ng, nb, nb),
        in_specs=[qs(0), ks(ng), ks(2 * ng), qs(0), qs(0),
                  pl.BlockSpec((hpb, t, 1), lambda h, j, i: (h, qi(j, i), 0)),
                  pl.BlockSpec((hpb, t, 1), lambda h, j, i: (h, qi(j, i), 0)),
                  pl.BlockSpec((hpb, 1, t), lambda h, j, i: (h, 0, j))],
        out_specs=[pl.BlockSpec((t, wb), lambda h, j, i: (j, h)), pl.BlockSpec((t, wb), lambda h, j, i: (j, h)),
                   pl.BlockSpec((hpb, 1, t), lambda h, j, i: (h, 0, j))],
        out_shape=[jax.ShapeDtypeStruct((s, HEADS * DH), BF16)] * 2 + [jax.ShapeDtypeStruct((HEADS, 1, s), F32)],
        scratch_shapes=[pltpu.VMEM((t, wb), F32), pltpu.VMEM((t, wb), F32), pltpu.VMEM((hpb, 1, t), F32)],
        compiler_params=_cparams(("parallel", "parallel", "arbitrary")), name=name)(
            u, u, u, o, dout, lse, f_col, f_row)


def _fox_bwd_q(u, o, lse, dout, f_col, f_row, name):
    s = u.shape[0]
    t = _pick(s, (512, 256, 128))
    nb = s // t
    hpb = FOX_HPB
    wb = hpb * DH
    ng = HEADS // hpb

    def body(q_ref, k_ref, v_ref, o_ref, do_ref, lse_ref, fq_ref, fk_ref, dq_ref, dfq_ref, dq_acc, df_acc):
        i, j = pl.program_id(1), pl.program_id(2)

        @pl.when(j == 0)
        def _():
            dq_acc[...] = jnp.zeros_like(dq_acc)
            df_acc[...] = jnp.zeros_like(df_acc)

        def step(diag):
            for hh in range(hpb):
                sl = slice(hh * DH, (hh + 1) * DH)
                do = do_ref[:, sl]
                k = k_ref[:, sl]
                sc = _fox_scores(q_ref[:, sl], k, fq_ref[hh], fk_ref[hh], diag)
                p = jnp.exp(sc - lse_ref[hh])
                delta = jnp.sum(do * o_ref[:, sl], axis=-1, keepdims=True)
                ds = p * (_nt(do, v_ref[:, sl]) - delta)
                dq_acc[:, sl] += _nn(ds, k) * (DH ** -0.5)
                df_acc[hh] += jnp.sum(ds, axis=-1, keepdims=True)

        _fox_when_needed(i, j, step)

        @pl.when(j == i)
        def _():
            dq_ref[...] = dq_acc[...].astype(BF16)
            dfq_ref[...] = df_acc[...]

    qs = lambda base: pl.BlockSpec((t, wb), lambda h, i, j: (i, base + h))
    ks = lambda base: pl.BlockSpec((t, wb), lambda h, i, j: (jnp.minimum(j, i), base + h))
    return pl.pallas_call(
        body, grid=(ng, nb, nb),
        in_specs=[qs(0), ks(ng), ks(2 * ng), qs(0), qs(0),
                  pl.BlockSpec((hpb, t, 1), lambda h, i, j: (h, i, 0)),
                  pl.BlockSpec((hpb, t, 1), lambda h, i, j: (h, i, 0)),
                  pl.BlockSpec((hpb, 1, t), lambda h, i, j: (h, 0, jnp.minimum(j, i)))],
        out_specs=[pl.BlockSpec((t, wb), lambda h, i, j: (i, h)), pl.BlockSpec((hpb, t, 1), lambda h, i, j: (h, i, 0))],
        out_shape=[jax.ShapeDtypeStruct((s, HEADS * DH), BF16), jax.ShapeDtypeStruct((HEADS, s, 1), F32)],
        scratch_shapes=[pltpu.VMEM((t, wb), F32), pltpu.VMEM((hpb, t, 1), F32)],
        compiler_params=_cparams(("parallel", "parallel", "arbitrary")), name=name)(
            u, u, u, o, dout, lse, f_col, f_row)


def _ca_bias(rel_bias):
    band = (CA_LEFT + 1) * CHUNK
    n_clip = band - REL_CLIP
    gv = jnp.concatenate([jnp.broadcast_to(rel_bias[:, REL_TABLE - 1:], (rel_bias.shape[0], n_clip)),
                          rel_bias[:, REL_TABLE - 2::-1]], axis=1)
    rows = [gv[:, CHUNK - 1 - qi:CHUNK - 1 - qi + band] for qi in range(CHUNK)]
    b = jnp.stack(rows, axis=1)
    return jnp.concatenate([b, jnp.full((b.shape[0], CHUNK, CA_WIN - band), NEG, F32)], axis=2)


def _ca_bias_grad(db):
    band = (CA_LEFT + 1) * CHUNK
    n_clip = band - REL_CLIP
    dgv = None
    for qi in range(CHUNK):
        t = jnp.pad(db[:, qi, :band], ((0, 0), (CHUNK - 1 - qi, qi)))
        dgv = t if dgv is None else dgv + t
    return jnp.concatenate([dgv[:, n_clip:][:, ::-1], jnp.sum(dgv[:, :n_clip], axis=1, keepdims=True)], axis=1)


def _ca_specs():
    cur = lambda base: pl.BlockSpec((CA_BLK, DH), lambda h, i: (i, base + h))
    prev = lambda base: pl.BlockSpec((CA_BLK, DH), lambda h, i: (jnp.maximum(i - 1, 0), base + h))
    bias = pl.BlockSpec((1, CHUNK, CA_WIN), lambda h, i: (h, 0, 0))
    return cur, prev, bias


def _ca_fill(kcat, vcat, kp_ref, kc_ref, vp_ref, vc_ref):
    kcat[0:CA_BLK, :] = kp_ref[...]
    kcat[CA_BLK:2 * CA_BLK, :] = kc_ref[...]
    kcat[2 * CA_BLK:, :] = jnp.zeros((CHUNK, DH), F32)
    vcat[0:CA_BLK, :] = vp_ref[...]
    vcat[CA_BLK:2 * CA_BLK, :] = vc_ref[...]
    vcat[2 * CA_BLK:, :] = jnp.zeros((CHUNK, DH), F32)


def _ca_probs(i, c, q, kw, bias):
    sc = _nt(q, kw) * (DH ** -0.5) + bias
    col = lax.broadcasted_iota(jnp.int32, sc.shape, 1)
    sc = jnp.where((i - 1) * CA_BLK + c * CHUNK + col >= 0, sc, NEG)
    p = jnp.exp(sc - jnp.max(sc, axis=-1, keepdims=True))
    return p / jnp.sum(p, axis=-1, keepdims=True)


def _ca_fwd(u, bias, name):
    s = u.shape[0]
    cur, prev, bsp = _ca_specs()

    def body(q_ref, kp_ref, kc_ref, vp_ref, vc_ref, b_ref, o_ref, kcat, vcat):
        i = pl.program_id(1)
        _ca_fill(kcat, vcat, kp_ref, kc_ref, vp_ref, vc_ref)
        for c in range(CA_LEFT):
            rows = slice(c * CHUNK, (c + 1) * CHUNK)
            win = slice(c * CHUNK, c * CHUNK + CA_WIN)
            p = _ca_probs(i, c, q_ref[rows, :], kcat[win, :], b_ref[0])
            o_ref[rows, :] = _nn(p, vcat[win, :])

    return pl.pallas_call(
        body, grid=(HEADS, s // CA_BLK),
        in_specs=[cur(3 * HEADS), prev(4 * HEADS), cur(4 * HEADS), prev(5 * HEADS), cur(5 * HEADS), bsp],
        out_specs=pl.BlockSpec((CA_BLK, DH), lambda h, i: (i, h)),
        out_shape=jax.ShapeDtypeStruct((s, HEADS * DH), F32),
        scratch_shapes=[pltpu.VMEM((2 * CA_BLK + CHUNK, DH), F32)] * 2,
        compiler_params=_cparams(("parallel", "parallel")), name=name)(u, u, u, u, u, bias)


def _ca_bwd(u, bias, dout, name):
    s = u.shape[0]
    cur, prev, bsp = _ca_specs()
    rows_cat = 2 * CA_BLK + CHUNK

    def body(q_ref, kp_ref, kc_ref, vp_ref, vc_ref, b_ref, do_ref,
             dq_ref, dka_ref, dkb_ref, dva_ref, dvb_ref, db_ref, kcat, vcat, dkcat, dvcat):
        i = pl.program_id(1)

        @pl.when(i == 0)
        def _():
            db_ref[...] = jnp.zeros_like(db_ref)

        _ca_fill(kcat, vcat, kp_ref, kc_ref, vp_ref, vc_ref)
        dkcat[...] = jnp.zeros_like(dkcat)
        dvcat[...] = jnp.zeros_like(dvcat)
        for c in range(CA_LEFT):
            rows = slice(c * CHUNK, (c + 1) * CHUNK)
            win = slice(c * CHUNK, c * CHUNK + CA_WIN)
            q = q_ref[rows, :]
            kw = kcat[win, :]
            vw = vcat[win, :]
            do = do_ref[rows, :]
            p = _ca_probs(i, c, q, kw, b_ref[0])
            dp = _nt(do, vw)
            ds = p * (dp - jnp.sum(p * dp, axis=-1, keepdims=True))
            dq_ref[rows, :] = (_nn(ds, kw) * (DH ** -0.5)).astype(BF16)
            dkcat[win, :] += _tn(ds, q) * (DH ** -0.5)
            dvcat[win, :] += _tn(p, do)
            db_ref[0] += ds
        dkb_ref[...] = dkcat[0:CA_BLK, :]
        dka_ref[...] = dkcat[CA_BLK:2 * CA_BLK, :]
        dvb_ref[...] = dvcat[0:CA_BLK, :]
        dva_ref[...] = dvcat[CA_BLK:2 * CA_BLK, :]

    osp = pl.BlockSpec((CA_BLK, DH), lambda h, i: (i, h))
    shp = jax.ShapeDtypeStruct((s, HEADS * DH), F32)
    return pl.pallas_call(
        body, grid=(HEADS, s // CA_BLK),
        in_specs=[cur(3 * HEADS), prev(4 * HEADS), cur(4 * HEADS), prev(5 * HEADS), cur(5 * HEADS), bsp,
                  pl.BlockSpec((CA_BLK, DH), lambda h, i: (i, HEADS + h))],
        out_specs=[osp, osp, osp, osp, osp, bsp],
        out_shape=[jax.ShapeDtypeStruct((s, HEADS * DH), BF16), shp, shp, shp, shp,
                   jax.ShapeDtypeStruct((HEADS, CHUNK, CA_WIN), F32)],
        scratch_shapes=[pltpu.VMEM((rows_cat, DH), F32)] * 4,
        compiler_params=_cparams(("parallel", "arbitrary")), name=name)(u, u, u, u, u, bias, dout)


def _ca_merge(da, db):
    shifted = jnp.concatenate([db[CA_BLK:], jnp.zeros((CA_BLK, db.shape[1]), F32)], axis=0)
    return (da + shifted).astype(BF16)


def _place():
    x, y, c = lax.axis_index("x"), lax.axis_index("y"), lax.axis_index("c")
    return x, y, c, [(1 - x, y), (x, 1 - y), (1 - x, 1 - y)]


_ANY = pl.BlockSpec(memory_space=pl.ANY)


def _all_gather_chips(w, name):
    def body(w_ref, o_ref, send_sems, recv_sems, loc_sems):
        _exchange(["ag"], [w_ref], [o_ref], send_sems, recv_sems, loc_sems, start=True)
        _exchange(["ag"], [w_ref], [o_ref], send_sems, recv_sems, loc_sems, start=False)

    return pl.pallas_call(
        body, in_specs=[_ANY], out_specs=_ANY, out_shape=jax.ShapeDtypeStruct((4,) + w.shape, w.dtype),
        scratch_shapes=[pltpu.SemaphoreType.DMA((3,)), pltpu.SemaphoreType.DMA((3,)), pltpu.SemaphoreType.DMA((1,))],
        name=name)(w)


def _exchange(kinds, srcs, dsts, send_sems, recv_sems, loc_sems, start):
    x, y, c, peers = _place()
    me = 2 * x + y
    for n, kind in enumerate(kinds):
        src, dst = srcs[n], dsts[n]
        mine = src if kind == "ag" else src.at[me]
        loc = pltpu.make_async_copy(mine, dst.at[me], loc_sems.at[n])
        copies = []
        for k, (px, py) in enumerate(peers):
            out_src = src if kind == "ag" else src.at[2 * px + py]
            send = pltpu.make_async_remote_copy(src_ref=out_src, dst_ref=dst.at[me], send_sem=send_sems.at[3 * n + k],
                                                recv_sem=recv_sems.at[3 * n + k], device_id=(px, py, c),
                                                device_id_type=MESH)
            recv = pltpu.make_async_remote_copy(src_ref=mine, dst_ref=dst.at[2 * px + py],
                                                send_sem=send_sems.at[3 * n + k], recv_sem=recv_sems.at[3 * n + k],
                                                device_id=(px, py, c), device_id_type=MESH)
            copies.append((send, recv))
        if start:
            loc.start()
            for send, _ in copies:
                send.start()
        else:
            for _, recv in copies:
                recv.wait_recv()
            for send, _ in copies:
                send.wait_send()
            loc.wait()


def _core_swap(a, name):
    def body(a_ref, o_ref, send_sem, recv_sem):
        x, y, c, _ = _place()
        cp = pltpu.make_async_remote_copy(src_ref=a_ref, dst_ref=o_ref, send_sem=send_sem, recv_sem=recv_sem,
                                          device_id=(x, y, 1 - c), device_id_type=MESH)
        cp.start()
        cp.wait()

    return pl.pallas_call(
        body, in_specs=[_ANY], out_specs=_ANY, out_shape=jax.ShapeDtypeStruct(a.shape, a.dtype),
        scratch_shapes=[pltpu.SemaphoreType.DMA(()), pltpu.SemaphoreType.DMA(())], name=name)(a)


def _all_reduce_small(v, name):
    r, wl = v.shape

    def body(v_ref, o_ref, buf, send_sems, recv_sems):
        x, y, c, _ = _place()
        me = 4 * x + 2 * y + c
        buf[me] = v_ref[...]
        flips = [(fx, fy, fc) for fx in (0, 1) for fy in (0, 1) for fc in (0, 1) if (fx, fy, fc) != (0, 0, 0)]
        peer = lambda f: (x ^ f[0], y ^ f[1], c ^ f[2])
        sends = []
        for k, f in enumerate(flips):
            cp = pltpu.make_async_remote_copy(src_ref=v_ref, dst_ref=buf.at[me], send_sem=send_sems.at[k],
                                              recv_sem=recv_sems.at[k], device_id=peer(f), device_id_type=MESH)
            cp.start()
            sends.append(cp)
        for k, f in enumerate(flips):
            px, py, pc = peer(f)
            pltpu.make_async_remote_copy(src_ref=v_ref, dst_ref=buf.at[4 * px + 2 * py + pc], send_sem=send_sems.at[k],
                                         recv_sem=recv_sems.at[k], device_id=peer(f), device_id_type=MESH).wait_recv()
        for cp in sends:
            cp.wait_send()
        acc = buf[0]
        for d in range(1, 8):
            acc = acc + buf[d]
        o_ref[...] = acc

    vm = pl.BlockSpec(memory_space=pltpu.VMEM)
    return pl.pallas_call(
        body, in_specs=[vm], out_specs=vm, out_shape=jax.ShapeDtypeStruct((r, wl), F32),
        scratch_shapes=[pltpu.VMEM((8, r, wl), F32), pltpu.SemaphoreType.DMA((7,)), pltpu.SemaphoreType.DMA((7,))],
        name=name)(v)


def _sum_slots(g, name):
    _, r, cdim = g.shape
    br = _pick(r, (256, 128, 64, 32, 8))

    def body(g_ref, o_ref):
        o_ref[...] = ((g_ref[0].astype(F32) + g_ref[1].astype(F32)) + g_ref[2].astype(F32)) + g_ref[3].astype(F32)

    return pl.pallas_call(
        body, grid=(r // br,), in_specs=[pl.BlockSpec((4, br, cdim), lambda i: (0, i, 0))],
        out_specs=pl.BlockSpec((br, cdim), lambda i: (i, 0)), out_shape=jax.ShapeDtypeStruct((r, cdim), F32),
        compiler_params=_cparams(("parallel",)), name=name)(g)


def _adamw(w, ga, gb, m, v, name):
    r, cdim = w.shape
    br = _pick(r, (256, 128, 64, 32, 8))
    c1 = 1.0 / (1.0 - ADAM_B1 ** ADAM_STEP)
    c2 = 1.0 / (1.0 - ADAM_B2 ** ADAM_STEP)
    two = gb is not None

    def body(*refs):
        w_ref, ga_ref = refs[0], refs[1]
        gb_ref = refs[2] if two else None
        m_ref, v_ref, g_out, d_out, m_out, v_out = refs[2 + two:]
        g = ga_ref[...] + gb_ref[...] if two else ga_ref[...]
        mn = ADAM_B1 * m_ref[...] + (1.0 - ADAM_B1) * g
        vn = ADAM_B2 * v_ref[...] + (1.0 - ADAM_B2) * (g * g)
        g_out[...] = g
        m_out[...] = mn
        v_out[...] = vn
        d_out[...] = -ADAM_LR * ((mn * c1) / (jnp.sqrt(vn * c2) + ADAM_EPS) + ADAM_WD * w_ref[...])

    blk = pl.BlockSpec((br, cdim), lambda i: (i, 0))
    args = (w, ga) + ((gb,) if two else ()) + (m, v)
    return pl.pallas_call(
        body, grid=(r // br,), in_specs=[blk] * len(args), out_specs=[blk] * 4,
        out_shape=[jax.ShapeDtypeStruct((r, cdim), F32)] * 4,
        compiler_params=_cparams(("parallel",)), name=name)(*args)


def _unshard_cols(g):
    return jnp.transpose(g, (1, 0, 2)).reshape(g.shape[1], 4 * g.shape[2])


def _unshard_rows(g):
    return g.reshape(4 * g.shape[1], g.shape[2])


def _shard_cols(g):
    k, n = g.shape
    return jnp.transpose(g.reshape(k, 4, n // 4), (1, 0, 2))


def _shard_rows(g):
    k, n = g.shape
    return g.reshape(4, k // 4, n)


def _lower_bound(logits):
    return jnp.cumsum(jax.nn.softmax(logits.astype(F32), axis=0), axis=0)[0]


def _pack(parts):
    flat = jnp.concatenate([p.reshape(-1) for p in parts])
    n = flat.shape[0]
    rows = -(-n // 1024) * 8
    return jnp.pad(flat, (0, rows * 128 - n)).reshape(rows, 128)


def _unpack(packed, shapes):
    flat = packed.reshape(-1)
    out, off = [], 0
    for shp in shapes:
        n = 1
        for d in shp:
            n *= d
        out.append(flat[off:off + n].reshape(shp))
        off += n
    return out


def kernel(x, ev_w_in, ev_conv_w, ev_conv_b, ev_conv_ln_g, ev_conv_ln_b, hgrn_lb_logits, ev_gnorm_g, ev_w_out, od_w_in, fox_b_f, rel_bias, od_w_out, ln_mix_g, ln_mix_b, mlp_w1, mlp_w2, ln_mlp_g, ln_mlp_b, loss_target, m_ev_w_in, m_ev_conv_w, m_ev_conv_b, m_ev_conv_ln_g, m_ev_conv_ln_b, m_hgrn_lb_logits, m_ev_gnorm_g, m_ev_w_out, m_od_w_in, m_fox_b_f, m_rel_bias, m_od_w_out, m_ln_mix_g, m_ln_mix_b, m_mlp_w1, m_mlp_w2, m_ln_mlp_g, m_ln_mlp_b, v_ev_w_in, v_ev_conv_w, v_ev_conv_b, v_ev_conv_ln_g, v_ev_conv_ln_b, v_hgrn_lb_logits, v_ev_gnorm_g, v_ev_w_out, v_od_w_in, v_fox_b_f, v_rel_bias, v_od_w_out, v_ln_mix_g, v_ln_mix_b, v_mlp_w1, v_mlp_w2, v_ln_mlp_g, v_ln_mlp_b):
    w_sharded = dict(ev_w_in=ev_w_in, ev_w_out=ev_w_out, od_w_in=od_w_in, od_w_out=od_w_out, mlp_w1=mlp_w1, mlp_w2=mlp_w2)
    m_sharded = dict(ev_w_in=m_ev_w_in, ev_w_out=m_ev_w_out, od_w_in=m_od_w_in, od_w_out=m_od_w_out, mlp_w1=m_mlp_w1, mlp_w2=m_mlp_w2)
    v_sharded = dict(ev_w_in=v_ev_w_in, ev_w_out=v_ev_w_out, od_w_in=v_od_w_in, od_w_out=v_od_w_out, mlp_w1=v_mlp_w1, mlp_w2=v_mlp_w2)
    small_names = ["ev_conv_w", "ev_conv_b", "ev_conv_ln_g", "ev_conv_ln_b", "hgrn_lb_logits", "ev_gnorm_g", "fox_b_f",
                   "rel_bias", "ln_mix_g", "ln_mix_b", "ln_mlp_g", "ln_mlp_b"]
    w_small = dict(ev_conv_w=ev_conv_w, ev_conv_b=ev_conv_b, ev_conv_ln_g=ev_conv_ln_g, ev_conv_ln_b=ev_conv_ln_b,
                   hgrn_lb_logits=hgrn_lb_logits, ev_gnorm_g=ev_gnorm_g, fox_b_f=fox_b_f, rel_bias=rel_bias,
                   ln_mix_g=ln_mix_g, ln_mix_b=ln_mix_b, ln_mlp_g=ln_mlp_g, ln_mlp_b=ln_mlp_b)
    m_small = dict(ev_conv_w=m_ev_conv_w, ev_conv_b=m_ev_conv_b, ev_conv_ln_g=m_ev_conv_ln_g, ev_conv_ln_b=m_ev_conv_ln_b,
                   hgrn_lb_logits=m_hgrn_lb_logits, ev_gnorm_g=m_ev_gnorm_g, fox_b_f=m_fox_b_f, rel_bias=m_rel_bias,
                   ln_mix_g=m_ln_mix_g, ln_mix_b=m_ln_mix_b, ln_mlp_g=m_ln_mlp_g, ln_mlp_b=m_ln_mlp_b)
    v_small = dict(ev_conv_w=v_ev_conv_w, ev_conv_b=v_ev_conv_b, ev_conv_ln_g=v_ev_conv_ln_g, ev_conv_ln_b=v_ev_conv_ln_b,
                   hgrn_lb_logits=v_hgrn_lb_logits, ev_gnorm_g=v_ev_gnorm_g, fox_b_f=v_fox_b_f, rel_bias=v_rel_bias,
                   ln_mix_g=v_ln_mix_g, ln_mix_b=v_ln_mix_b, ln_mlp_g=v_ln_mlp_g, ln_mlp_b=v_ln_mlp_b)

    chip = 2 * lax.axis_index("x") + lax.axis_index("y")
    hw = HEADS * DH

    w_ev_in = _all_gather_chips(ev_w_in[0].astype(BF16), "ag_ev_w_in")
    shards = dict(ev_w_out=ev_w_out[0].astype(BF16), od_w_in=od_w_in[0].astype(BF16), od_w_out=od_w_out[0].astype(BF16),
                  mlp_w1=[mlp_w1[l].astype(BF16) for l in range(DEPTH)],
                  mlp_w2=[mlp_w2[l].astype(BF16) for l in range(DEPTH)])
    tables = _all_gather_chips(jnp.concatenate(
        [ev_conv_w[0].reshape(1, -1), jnp.pad(rel_bias[0].reshape(1, -1), ((0, 0), (0, (-rel_bias[0].size) % 128)))],
        axis=1), "ag_tables")
    ncw = ev_conv_w[0].size
    cshard = ev_conv_w.shape[2]
    conv_w = jnp.transpose(tables[:, 0, :ncw].reshape(4, CONV_WIDTH, cshard), (1, 0, 2)).reshape(CONV_WIDTH, 4 * cshard)
    rshard = rel_bias.shape[2]
    rel_full = jnp.transpose(tables[:, 0, ncw:ncw + HEADS * rshard].reshape(4, HEADS, rshard), (1, 0, 2)).reshape(HEADS, 4 * rshard)

    loss_part, grad_x, g, small_partial = _local_step(
        x[0], loss_target[0], w_ev_in, shards, conv_w, rel_full, ev_conv_b, ev_conv_ln_g,
        ev_conv_ln_b, hgrn_lb_logits, ev_gnorm_g, fox_b_f, ln_mix_g, ln_mix_b, ln_mlp_g, ln_mlp_b)
    loss = lax.psum(loss_part, ("x", "y", "c"))

    full_shapes = [tuple(small_partial[n].shape) for n in small_names]
    reduced = _unpack(_all_reduce_small(_pack([small_partial[n] for n in small_names]), "ar_small"), full_shapes)
    g_small = {}
    for n, val in zip(small_names, reduced):
        if n == "ev_conv_w":
            val = lax.dynamic_slice_in_dim(val, chip * cshard, cshard, axis=1)
        elif n == "rel_bias":
            val = lax.dynamic_slice_in_dim(val, chip * rshard, rshard, axis=1)
        g_small[n] = val.reshape(w_small[n].shape)
    shard_shapes = [tuple(w_small[n].shape) for n in small_names]
    packed = [_pack([d[n] for n in small_names]) for d in (w_small, g_small, m_small, v_small)]
    sg_, sd_, sm_, sv_ = _adamw(packed[0], packed[1], None, packed[2], packed[3], "adamw_small")
    out_small = {k: dict(zip(small_names, _unpack(val, shard_shapes)))
                 for k, val in (("g", sg_), ("d", sd_), ("m", sm_), ("v", sv_))}

    out_big = {"g": {}, "d": {}, "m": {}, "v": {}}

    for name in ("ev_w_in", "ev_w_out", "od_w_in", "od_w_out", "mlp_w1", "mlp_w2"):
        res = {"g": [], "d": [], "m": [], "v": []}
        for l, slots in enumerate(g[name]):
            mine = _sum_slots(slots, "rs_%s_%d_sum" % (name, l))
            other = _core_swap(mine, "rs_%s_%d_swap" % (name, l))
            outs = _adamw(w_sharded[name][l], mine, other, m_sharded[name][l], v_sharded[name][l],
                          "adamw_%s_%d" % (name, l))
            for key, val in zip(("g", "d", "m", "v"), outs):
                res[key].append(val)
        for key in res:
            out_big[key][name] = jnp.stack(res[key])

    order = ["ev_w_in", "ev_conv_w", "ev_conv_b", "ev_conv_ln_g", "ev_conv_ln_b", "hgrn_lb_logits", "ev_gnorm_g", "ev_w_out",
             "od_w_in", "fox_b_f", "rel_bias", "od_w_out", "ln_mix_g", "ln_mix_b", "mlp_w1", "mlp_w2", "ln_mlp_g", "ln_mlp_b"]

    def pick(key, n):
        return out_big[key][n] if n in out_big[key] else out_small[key][n]

    outs = [loss, grad_x[None]]
    for key in ("g", "d", "m", "v"):
        outs.extend(pick(key, n) for n in order)
    return tuple(outs)


def _local_step(xin, tgt, w_ev_in, shards, conv_w, rel_full, ev_conv_b, ev_conv_ln_g,
                ev_conv_ln_b, hgrn_lb_logits, ev_gnorm_g, fox_b_f, ln_mix_g, ln_mix_b, ln_mlp_g, ln_mlp_b):
    hw = HEADS * DH
    bf_pad = jnp.pad(fox_b_f[0], (0, 128 - HEADS)).reshape(1, 128)
    lb0 = _lower_bound(hgrn_lb_logits)
    ca_bias = _ca_bias(rel_full)
    ag = lambda w: ("ag", w)
    a2a = lambda g4: ("a2a", g4)

    xin_b = xin.astype(BF16)
    u0, (w_ev_out4, w1_0) = _matmul(xin_b, w_ev_in, b_sharded=True, comm=(ag(shards["ev_w_out"]), ag(shards["mlp_w1"][0])),
                                    name="l0_in")
    w_ev_out = _unshard_rows(w_ev_out4)
    a_out = _conv_fwd(u0, conv_w, ev_conv_b[0], ev_conv_ln_g[0], ev_conv_ln_b[0], "l0_conv")
    o_raw, b_out, states = _hgrn_fwd(u0, lb0, ev_gnorm_g[0], "l0_hgrn")
    cat0 = jnp.concatenate([a_out, b_out], axis=1).astype(BF16)
    mix0 = _matmul(cat0, w_ev_out, name="l0_out")
    r0a, x0a, x0a_b = _ln_fwd(xin, mix0, ln_mix_g[0], ln_mix_b[0], "l0_ln_mix")
    (z0, h0), (w2_0,) = _matmul(x0a_b, w1_0, b_sharded=True, out_dtype=BF16, epi="relu2",
                                comm=(ag(shards["mlp_w2"][0]),), name="l0_mlp1")
    w2 = [_unshard_rows(w2_0), None]
    f0, (w_od_in4, w_od_out4) = _matmul(h0, w2[0], comm=(ag(shards["od_w_in"]), ag(shards["od_w_out"])), name="l0_mlp2")
    w_od_in = _unshard_cols(w_od_in4)
    w_od_out = _unshard_rows(w_od_out4)
    w_od_main = jnp.concatenate([w_od_in[:, :3 * hw], w_od_in[:, 3 * hw + HEADS:]], axis=1)
    w_od_f = jnp.pad(w_od_in[:, 3 * hw:3 * hw + HEADS], ((0, 0), (0, 128 - HEADS)))
    r0b, x1, x1_b = _ln_fwd(x0a, f0, ln_mlp_g[0], ln_mlp_b[0], "l0_ln_mlp")
    u1, (w1_1,) = _matmul(x1_b, w_od_main, comm=(ag(shards["mlp_w1"][1]),), name="l1_in")
    w1 = [w1_0, w1_1]
    cf = _matmul(x1_b, w_od_f, name="l1_in_f")
    fcum = _fgate_fwd(cf, bf_pad, "l1_fgate")
    f_col = jnp.transpose(fcum[:, :HEADS])[:, :, None]
    f_row = jnp.transpose(fcum[:, :HEADS])[:, None, :]
    c_out, lse = _fox_fwd(u1, f_col, f_row, "l1_fox")
    d_out = _ca_fwd(u1, ca_bias, "l1_ca")
    cat1 = jnp.concatenate([c_out, d_out], axis=1).astype(BF16)
    mix1 = _matmul(cat1, w_od_out, name="l1_out")
    r1a, x1a, x1a_b = _ln_fwd(x1, mix1, ln_mix_g[1], ln_mix_b[1], "l1_ln_mix")
    (z1, h1), (w2_1,) = _matmul(x1a_b, w1[1], b_sharded=True, out_dtype=BF16, epi="relu2",
                                comm=(ag(shards["mlp_w2"][1]),), name="l1_mlp1")
    w2[1] = _unshard_rows(w2_1)
    f1 = _matmul(h1, w2[1], name="l1_mlp2")
    r1b, x2, _ = _ln_fwd(x1a, f1, ln_mlp_g[1], ln_mlp_b[1], "l1_ln_mlp")
    dy, loss_part = _loss_head(x2, tgt, "loss")

    g = {}
    dr, drb, dg_, db_ = _ln_bwd(dy, r1b, ln_mlp_g[1], "l1_ln_mlp_bwd")
    g_ln_mlp = [None, (dg_, db_)]
    gw = _shard_rows(_matmul(h1, drb, ta=True, out_dtype=BF16, name="l1_dw2"))
    dz, (gw2_1,) = _matmul(drb, w2[1], tb=True, out_dtype=BF16, epi="drelu2", extra=z1, comm=(a2a(gw),), name="l1_dz")
    gw = _matmul(x1a_b, dz, ta=True, out_dtype=BF16, out_sharded=True, name="l1_dw1")
    dx, (gw1_1,) = _matmul(dz, w1[1], tb=True, b_sharded=True, epi="add", extra=dr, comm=(a2a(gw),), name="l1_dx_mlp")
    dr, drb, dg_, db_ = _ln_bwd(dx, r1a, ln_mix_g[1], "l1_ln_mix_bwd")
    g_ln_mix = [None, (dg_, db_)]
    gw = _shard_rows(_matmul(cat1, drb, ta=True, out_dtype=BF16, name="l1_dwout"))
    dcat, g["od_w_out"] = _matmul(drb, w_od_out, tb=True, comm=(a2a(gw),), name="l1_dcat")
    dk_c, dv_c, dfk = _fox_bwd_kv(u1, c_out, lse, dcat, f_col, f_row, "l1_fox_bwd_kv")
    dq_c, dfq = _fox_bwd_q(u1, c_out, lse, dcat, f_col, f_row, "l1_fox_bwd_q")
    d_f = jnp.pad(jnp.transpose(dfk[:, 0, :] + dfq[:, :, 0]), ((0, 0), (0, 128 - HEADS)))
    dcf, dbf = _fgate_bwd(d_f, cf, bf_pad, "l1_fgate_bwd")
    dq_d, dka, dkb, dva, dvb, dbias = _ca_bwd(u1, ca_bias, dcat, "l1_ca_bwd")
    du1 = jnp.concatenate([dq_c, dk_c, dv_c, dq_d, _ca_merge(dka, dkb), _ca_merge(dva, dvb)], axis=1)
    g_main = _matmul(x1_b, du1, ta=True, out_dtype=BF16, name="l1_dwin")
    g_f = _matmul(x1_b, dcf, ta=True, out_dtype=BF16, name="l1_dwin_f")
    gw = _shard_cols(jnp.concatenate([g_main[:, :3 * hw], g_f[:, :HEADS], g_main[:, 3 * hw:]], axis=1))
    dx_f = _matmul(dcf, w_od_f, tb=True, epi="add", extra=dr, name="l1_dx_f")
    dx, g["od_w_in"] = _matmul(du1, w_od_main, tb=True, epi="add", extra=dx_f, scale=1.0, comm=(a2a(gw),),
                               name="l1_dx_in")
    dr, drb, dg_, db_ = _ln_bwd(dx, r0b, ln_mlp_g[0], "l0_ln_mlp_bwd")
    g_ln_mlp[0] = (dg_, db_)
    gw = _shard_rows(_matmul(h0, drb, ta=True, out_dtype=BF16, name="l0_dw2"))
    dz, (gw2_0,) = _matmul(drb, w2[0], tb=True, out_dtype=BF16, epi="drelu2", extra=z0, comm=(a2a(gw),), name="l0_dz")
    gw = _matmul(x0a_b, dz, ta=True, out_dtype=BF16, out_sharded=True, name="l0_dw1")
    dx, (gw1_0,) = _matmul(dz, w1[0], tb=True, b_sharded=True, epi="add", extra=dr, comm=(a2a(gw),), name="l0_dx_mlp")
    dr, drb, dg_, db_ = _ln_bwd(dx, r0a, ln_mix_g[0], "l0_ln_mix_bwd")
    g_ln_mix[0] = (dg_, db_)
    gw = _shard_rows(_matmul(cat0, drb, ta=True, out_dtype=BF16, name="l0_dwout"))
    dcat, g["ev_w_out"] = _matmul(drb, w_ev_out, tb=True, comm=(a2a(gw),), name="l0_dcat")
    dc, g_conv_w, g_conv_b, g_conv_lg, g_conv_lb = _conv_bwd_params(
        u0, dcat, conv_w, ev_conv_b[0], ev_conv_ln_g[0], ev_conv_ln_b[0], "l0_conv_bwd_p")
    da, dgate = _conv_bwd_input(u0, dc, conv_w, "l0_conv_bwd_i")
    dhq, dhf, dhi, dhg, g_lb0, g_gnorm = _hgrn_bwd(u0, o_raw, states, dcat, lb0, ev_gnorm_g[0], "l0_hgrn_bwd")
    du0 = jnp.concatenate([da, dgate, dhq, dhf, dhi, dhg], axis=1)
    gw = _matmul(xin_b, du0, ta=True, out_dtype=BF16, out_sharded=True, name="l0_dwin")
    grad_x, g["ev_w_in"] = _matmul(du0, w_ev_in, tb=True, b_sharded=True, epi="add", extra=dr, comm=(a2a(gw),),
                                   name="l0_dx_in")
    g["mlp_w1"] = [gw1_0, gw1_1]
    g["mlp_w2"] = [gw2_0, gw2_1]

    g_lb_logits = jax.vjp(_lower_bound, hgrn_lb_logits)[1](g_lb0)[0]
    g_rel = _ca_bias_grad(dbias)
    small_partial = dict(
        ev_conv_w=g_conv_w, ev_conv_b=g_conv_b, ev_conv_ln_g=g_conv_lg, ev_conv_ln_b=g_conv_lb,
        hgrn_lb_logits=g_lb_logits, ev_gnorm_g=g_gnorm, fox_b_f=dbf[:HEADS], rel_bias=g_rel,
        ln_mix_g=jnp.stack([g_ln_mix[0][0], g_ln_mix[1][0]]), ln_mix_b=jnp.stack([g_ln_mix[0][1], g_ln_mix[1][1]]),
        ln_mlp_g=jnp.stack([g_ln_mlp[0][0], g_ln_mlp[1][0]]), ln_mlp_b=jnp.stack([g_ln_mlp[0][1], g_ln_mlp[1][1]]))
    return loss_part, grad_x, g, small_partial
```

```python
import functools

import jax
import jax.numpy as jnp
from jax import lax
from jax.experimental import pallas as pl
from jax.experimental.pallas import tpu as pltpu

F32 = jnp.float32
BF16 = jnp.bfloat16
MESH = pl.DeviceIdType.MESH

DEPTH = 2
ALPHA = (2 * DEPTH) ** 0.25
LN_EPS = 1e-5
HEADS = 8
DH = 128
CHUNK = 64
CONV_WIDTH = 31
HALO = 32
CA_LEFT = 8
CA_BLK = CA_LEFT * CHUNK
CA_WIN = (CA_LEFT + 2) * CHUNK
REL_CLIP = 256
REL_TABLE = (CHUNK - 1) + REL_CLIP + 1
NEG = -1e30

ADAM_LR = 0.001
ADAM_B1 = 0.9
ADAM_B2 = 0.999
ADAM_EPS = 1e-08
ADAM_WD = 0.01
ADAM_STEP = 10

VMEM_LIMIT = 48 * 1024 * 1024


def _cparams(sem):
    return pltpu.CompilerParams(dimension_semantics=sem, vmem_limit_bytes=VMEM_LIMIT)


def _pick(n, cands):
    for c in cands:
        if n % c == 0:
            return c
    return n


def _sigmoid(x):
    return 1.0 / (1.0 + jnp.exp(-x))


def _dot(a, b, dims):
    return lax.dot_general(a.astype(BF16), b.astype(BF16), (dims, ((), ())), preferred_element_type=F32)


def _nn(a, b):
    return _dot(a, b, ((1,), (0,)))


def _nt(a, b):
    return _dot(a, b, ((1,), (1,)))


def _tn(a, b):
    return _dot(a, b, ((0,), (0,)))


def _dot3(a, b, dims):
    a_hi = a.astype(BF16)
    b_hi = b.astype(BF16)
    a_lo = (a - a_hi.astype(F32)).astype(BF16)
    b_lo = (b - b_hi.astype(F32)).astype(BF16)
    dn = (dims, ((), ()))
    return (lax.dot_general(a_hi, b_hi, dn, preferred_element_type=F32)
            + (lax.dot_general(a_hi, b_lo, dn, preferred_element_type=F32)
               + lax.dot_general(a_lo, b_hi, dn, preferred_element_type=F32)))


def _nn3(a, b):
    return _dot3(a, b, ((1,), (0,)))


def _nt3(a, b):
    return _dot3(a, b, ((1,), (1,)))


def _tn3(a, b):
    return _dot3(a, b, ((0,), (0,)))


def _split3(x):
    hi = x.astype(BF16)
    r1 = x - hi.astype(F32)
    mid = r1.astype(BF16)
    lo = (r1 - mid.astype(F32)).astype(BF16)
    return hi, mid, lo


def _tri_matmul(tri, x, terms):
    parts = _split3(x)[:terms]
    acc = None
    for p in parts:
        t = lax.dot_general(tri, p, (((1,), (0,)), ((), ())), preferred_element_type=F32)
        acc = t if acc is None else acc + t
    return acc


def _tril(n, upper=False):
    r = lax.broadcasted_iota(jnp.int32, (n, n), 0)
    c = lax.broadcasted_iota(jnp.int32, (n, n), 1)
    m = (c >= r) if upper else (c <= r)
    return jnp.where(m, 1.0, 0.0).astype(BF16)


def _matmul(a, b, *, ta=False, tb=False, out_dtype=F32, epi=None, extra=None, scale=ALPHA, b_sharded=False,
            out_sharded=False, comm=(), name):
    m = a.shape[1] if ta else a.shape[0]
    kd = a.shape[0] if ta else a.shape[1]
    if b_sharded:
        shard = b.shape[2]
        n = b.shape[1] if tb else 4 * shard
        assert (b.shape[1] if not tb else 4 * shard) == kd
    else:
        n = b.shape[0] if tb else b.shape[1]
    bm = _pick(m, (1024, 512, 256, 128))
    bn = _pick(shard if (b_sharded and not tb) else (n // 4 if out_sharded else n), (1024, 768, 512, 256, 128))
    bk = _pick(shard if (b_sharded and tb) else kd, (2048, 1536, 1024, 768, 512, 256, 128))
    ni, nj, nk = m // bm, n // bn, kd // bk
    n_out = 2 if epi == "relu2" else 1
    n_comm = len(comm)
    kinds = [c[0] for c in comm]

    def body(*refs):
        a_ref, b_ref = refs[0], refs[1]
        pos = 2
        e_ref = refs[pos] if extra is not None else None
        pos += extra is not None
        c_in = refs[pos:pos + n_comm]
        pos += n_comm
        outs = refs[pos:pos + n_out]
        pos += n_out
        c_out = refs[pos:pos + n_comm]
        pos += n_comm
        acc_ref = refs[pos]
        sems = refs[pos + 1:]
        i, j, k = pl.program_id(0), pl.program_id(1), pl.program_id(2)

        if n_comm:
            @pl.when(jnp.logical_and(jnp.logical_and(i == 0, j == 0), k == 0))
            def _():
                _exchange(kinds, c_in, c_out, *sems, start=True)

        dims = ((0 if ta else 1,), (1 if tb else 0,))
        part = _dot(a_ref[...], b_ref[...], dims)
        if nk > 1:
            @pl.when(k == 0)
            def _():
                acc_ref[...] = part

            @pl.when(jnp.logical_and(k > 0, k < nk - 1))
            def _():
                acc_ref[...] += part

        @pl.when(k == nk - 1)
        def _():
            r = part + acc_ref[...] if nk > 1 else part
            if epi == "relu2":
                outs[0][...] = r.astype(out_dtype)
                outs[1][...] = jnp.square(jnp.maximum(r, 0.0)).astype(out_dtype)
            elif epi == "drelu2":
                outs[0][...] = (r * (2.0 * jnp.maximum(e_ref[...].astype(F32), 0.0))).astype(out_dtype)
            elif epi == "add":
                outs[0][...] = (r + scale * e_ref[...].astype(F32)).astype(out_dtype)
            else:
                outs[0][...] = r.astype(out_dtype)

        if n_comm:
            @pl.when(jnp.logical_and(jnp.logical_and(i == ni - 1, j == nj - 1), k == nk - 1))
            def _():
                _exchange(kinds, c_in, c_out, *sems, start=False)

    a_spec = pl.BlockSpec((bk, bm), lambda i, j, k: (k, i)) if ta else pl.BlockSpec((bm, bk), lambda i, j, k: (i, k))
    if b_sharded and tb:
        per = shard // bk
        b_spec = pl.BlockSpec((None, bn, bk), lambda i, j, k: (k // per, j, k % per))
    elif b_sharded:
        per = shard // bn
        b_spec = pl.BlockSpec((None, bk, bn), lambda i, j, k: (j // per, k, j % per))
    elif tb:
        b_spec = pl.BlockSpec((bn, bk), lambda i, j, k: (j, k))
    else:
        b_spec = pl.BlockSpec((bk, bn), lambda i, j, k: (k, j))
    e_spec = pl.BlockSpec((bm, bn), lambda i, j, k: (i, j))
    if out_sharded:
        per_o = (n // 4) // bn
        o_spec = pl.BlockSpec((None, bm, bn), lambda i, j, k: (j // per_o, i, j % per_o))
        o_shape = jax.ShapeDtypeStruct((4, m, n // 4), out_dtype)
    else:
        o_spec = e_spec
        o_shape = jax.ShapeDtypeStruct((m, n), out_dtype)
    in_specs = [a_spec, b_spec] + ([e_spec] if extra is not None else []) + [_ANY] * n_comm
    args = (a, b) + ((extra,) if extra is not None else ()) + tuple(c[1] for c in comm)
    c_shapes = [jax.ShapeDtypeStruct((4,) + c[1].shape if c[0] == "ag" else c[1].shape, c[1].dtype) for c in comm]
    scratch = [pltpu.VMEM((bm, bn) if nk > 1 else (8, 128), F32)]
    if n_comm:
        scratch += [pltpu.SemaphoreType.DMA((3 * n_comm,)), pltpu.SemaphoreType.DMA((3 * n_comm,)),
                    pltpu.SemaphoreType.DMA((n_comm,))]
    sem = ("arbitrary",) * 3 if n_comm else ("parallel", "parallel", "arbitrary")
    res = pl.pallas_call(
        body, grid=(ni, nj, nk), in_specs=in_specs,
        out_specs=[o_spec] * n_out + [_ANY] * n_comm, out_shape=[o_shape] * n_out + c_shapes,
        scratch_shapes=scratch, compiler_params=_cparams(sem), name=name)(*args)
    main = tuple(res[:n_out]) if n_out == 2 else res[0]
    return (main, list(res[n_out:])) if n_comm else main


def _ln_fwd(x, mix, g, b, name):
    s, d = x.shape
    br = _pick(s, (256, 128, 64, 8))

    def body(x_ref, m_ref, g_ref, b_ref, r_ref, y_ref, yb_ref):
        r = ALPHA * x_ref[...] + m_ref[...]
        mu = jnp.mean(r, axis=-1, keepdims=True)
        dlt = r - mu
        var = jnp.mean(dlt * dlt, axis=-1, keepdims=True)
        y = dlt * lax.rsqrt(var + LN_EPS) * g_ref[...] + b_ref[...]
        r_ref[...] = r
        y_ref[...] = y
        yb_ref[...] = y.astype(BF16)

    row = pl.BlockSpec((br, d), lambda i: (i, 0))
    vec = pl.BlockSpec((1, d), lambda i: (0, 0))
    return pl.pallas_call(
        body, grid=(s // br,), in_specs=[row, row, vec, vec], out_specs=[row, row, row],
        out_shape=[jax.ShapeDtypeStruct((s, d), F32), jax.ShapeDtypeStruct((s, d), F32),
                   jax.ShapeDtypeStruct((s, d), BF16)],
        compiler_params=_cparams(("parallel",)), name=name)(x, mix, g.reshape(1, d), b.reshape(1, d))


def _ln_bwd(dy, r, g, name):
    s, d = r.shape
    br = _pick(s, (256, 128, 64, 8))

    def body(dy_ref, r_ref, g_ref, dr_ref, drb_ref, dg_ref, db_ref):
        @pl.when(pl.program_id(0) == 0)
        def _():
            dg_ref[...] = jnp.zeros_like(dg_ref)
            db_ref[...] = jnp.zeros_like(db_ref)

        rv = r_ref[...]
        dyv = dy_ref[...]
        mu = jnp.mean(rv, axis=-1, keepdims=True)
        dlt = rv - mu
        var = jnp.mean(dlt * dlt, axis=-1, keepdims=True)
        rstd = lax.rsqrt(var + LN_EPS)
        xhat = dlt * rstd
        dxh = dyv * g_ref[...]
        m1 = jnp.mean(dxh, axis=-1, keepdims=True)
        m2 = jnp.mean(dxh * xhat, axis=-1, keepdims=True)
        dr = rstd * (dxh - m1 - xhat * m2)
        dr_ref[...] = dr
        drb_ref[...] = dr.astype(BF16)
        dg_ref[...] += jnp.sum(dyv * xhat, axis=0, keepdims=True)
        db_ref[...] += jnp.sum(dyv, axis=0, keepdims=True)

    row = pl.BlockSpec((br, d), lambda i: (i, 0))
    vec = pl.BlockSpec((1, d), lambda i: (0, 0))
    dr, drb, dg, db = pl.pallas_call(
        body, grid=(s // br,), in_specs=[row, row, vec], out_specs=[row, row, vec, vec],
        out_shape=[jax.ShapeDtypeStruct((s, d), F32), jax.ShapeDtypeStruct((s, d), BF16),
                   jax.ShapeDtypeStruct((1, d), F32), jax.ShapeDtypeStruct((1, d), F32)],
        compiler_params=_cparams(("arbitrary",)), name=name)(dy, r, g.reshape(1, d))
    return dr, drb, dg[0], db[0]


def _loss_head(y, tgt, name):
    s, d = y.shape
    br = _pick(s, (256, 128, 64, 8))

    def body(y_ref, t_ref, dy_ref, l_ref):
        @pl.when(pl.program_id(0) == 0)
        def _():
            l_ref[...] = jnp.zeros_like(l_ref)

        e = y_ref[...] - t_ref[...]
        dy_ref[...] = e * (1.0 / d)
        rows = jnp.sum(e * e, axis=-1, keepdims=True) * (0.5 / d)
        l_ref[...] += jnp.sum(rows, axis=0, keepdims=True)

    row = pl.BlockSpec((br, d), lambda i: (i, 0))
    dy, l = pl.pallas_call(
        body, grid=(s // br,), in_specs=[row, row],
        out_specs=[row, pl.BlockSpec((1, 1), lambda i: (0, 0))],
        out_shape=[jax.ShapeDtypeStruct((s, d), F32), jax.ShapeDtypeStruct((1, 1), F32)],
        compiler_params=_cparams(("arbitrary",)), name=name)(y, tgt)
    return dy, l[0, 0]


SUBLANES = 8


def _shifted_rows(ext_ref, sh_ref, tt):
    n = tt + HALO - SUBLANES
    for r in range(1, SUBLANES):
        sh_ref[r - 1] = ext_ref[pl.ds(r, n), :]

    def tap(o):
        r = o % SUBLANES
        if r == 0:
            return ext_ref[pl.ds(o, tt), :]
        return sh_ref[r - 1, pl.ds(o - r, tt), :]

    return tap


def _conv_scratch(tt, cc):
    return [pltpu.VMEM((tt + HALO, cc), F32), pltpu.VMEM((SUBLANES - 1, tt + HALO - SUBLANES, cc), F32)]


def _conv_recompute(i, a_ref, gt_ref, ah_ref, gh_ref, w_ref, cb_ref, hext_ref, sh_ref, tt):
    h = a_ref[...] * _sigmoid(gt_ref[...])
    hh = ah_ref[...] * _sigmoid(gh_ref[...])
    hh = jnp.where(i > 0, hh, 0.0)
    hext_ref[0:HALO, :] = hh
    hext_ref[HALO:HALO + tt, :] = h
    tap = _shifted_rows(hext_ref, sh_ref, tt)
    acc = jnp.zeros_like(h) + cb_ref[...]
    off = HALO - (CONV_WIDTH - 1)
    for j in range(CONV_WIDTH):
        acc = acc + w_ref[j:j + 1, :] * tap(off + j)
    mu = jnp.mean(acc, axis=-1, keepdims=True)
    dlt = acc - mu
    var = jnp.mean(dlt * dlt, axis=-1, keepdims=True)
    rstd = lax.rsqrt(var + LN_EPS)
    return dlt * rstd, rstd, tap


def _conv_specs(tt, cc, s):
    nh = tt // HALO
    cur = lambda cb: pl.BlockSpec((tt, cc), lambda i: (i, cb))
    prev = lambda cb: pl.BlockSpec((HALO, cc), lambda i: (jnp.maximum(i * nh - 1, 0), cb))
    vec = pl.BlockSpec((1, cc), lambda i: (0, 0))
    wsp = pl.BlockSpec((HALO, cc), lambda i: (0, 0))
    return cur, prev, vec, wsp


def _pad_conv_w(w):
    return jnp.concatenate([w, jnp.zeros((HALO - CONV_WIDTH, w.shape[1]), F32)], axis=0)


def _conv_fwd(u, w, cb, lg, lb, name):
    s = u.shape[0]
    cc = w.shape[1]
    tt = _pick(s, (256, 128, 64))
    cur, prev, vec, wsp = _conv_specs(tt, cc, s)

    def body(a_ref, gt_ref, ah_ref, gh_ref, w_ref, cb_ref, lg_ref, lb_ref, o_ref, hext_ref, sh_ref):
        xhat, _, _ = _conv_recompute(pl.program_id(0), a_ref, gt_ref, ah_ref, gh_ref, w_ref, cb_ref, hext_ref,
                                     sh_ref, tt)
        nrm = xhat * lg_ref[...] + lb_ref[...]
        o_ref[...] = nrm * _sigmoid(nrm)

    return pl.pallas_call(
        body, grid=(s // tt,), in_specs=[cur(0), cur(1), prev(0), prev(1), wsp, vec, vec, vec],
        out_specs=pl.BlockSpec((tt, cc), lambda i: (i, 0)), out_shape=jax.ShapeDtypeStruct((s, cc), F32),
        scratch_shapes=_conv_scratch(tt, cc),
        compiler_params=_cparams(("parallel",)), name=name)(
            u, u, u, u, _pad_conv_w(w), cb.reshape(1, cc), lg.reshape(1, cc), lb.reshape(1, cc))


def _conv_bwd_params(u, dout, w, cb, lg, lb, name):
    s = u.shape[0]
    cc = w.shape[1]
    tt = _pick(s, (256, 128, 64))
    cur, prev, vec, wsp = _conv_specs(tt, cc, s)

    def body(a_ref, gt_ref, ah_ref, gh_ref, w_ref, cb_ref, lg_ref, lb_ref, do_ref,
             dc_ref, dw_ref, dcb_ref, dlg_ref, dlb_ref, hext_ref, sh_ref, dw_acc):
        i = pl.program_id(0)

        @pl.when(i == 0)
        def _():
            dw_acc[...] = jnp.zeros_like(dw_acc)
            dcb_ref[...] = jnp.zeros_like(dcb_ref)
            dlg_ref[...] = jnp.zeros_like(dlg_ref)
            dlb_ref[...] = jnp.zeros_like(dlb_ref)

        xhat, rstd, tap = _conv_recompute(i, a_ref, gt_ref, ah_ref, gh_ref, w_ref, cb_ref, hext_ref, sh_ref, tt)
        nrm = xhat * lg_ref[...] + lb_ref[...]
        sg = _sigmoid(nrm)
        dn = do_ref[...] * (sg * (1.0 + nrm * (1.0 - sg)))
        dxh = dn * lg_ref[...]
        m1 = jnp.mean(dxh, axis=-1, keepdims=True)
        m2 = jnp.mean(dxh * xhat, axis=-1, keepdims=True)
        dc = rstd * (dxh - m1 - xhat * m2)
        dc_ref[...] = dc
        dlg_ref[...] += jnp.sum(dn * xhat, axis=0, keepdims=True)
        dlb_ref[...] += jnp.sum(dn, axis=0, keepdims=True)
        dcb_ref[...] += jnp.sum(dc, axis=0, keepdims=True)
        off = HALO - (CONV_WIDTH - 1)
        for j in range(CONV_WIDTH):
            dw_acc[j] += jnp.sum((dc * tap(off + j)).reshape(tt // SUBLANES, SUBLANES, cc), axis=0)

        @pl.when(i == nblk - 1)
        def _():
            dw_ref[...] = jnp.zeros_like(dw_ref)
            for j in range(CONV_WIDTH):
                dw_ref[j:j + 1, :] = jnp.sum(dw_acc[j], axis=0, keepdims=True)

    nblk = s // tt
    dcol = pl.BlockSpec((tt, cc), lambda i: (i, 0))
    dc, dw, dcb, dlg, dlb = pl.pallas_call(
        body, grid=(nblk,), in_specs=[cur(0), cur(1), prev(0), prev(1), wsp, vec, vec, vec, dcol],
        out_specs=[dcol, wsp, vec, vec, vec],
        out_shape=[jax.ShapeDtypeStruct((s, cc), F32), jax.ShapeDtypeStruct((HALO, cc), F32)]
        + [jax.ShapeDtypeStruct((1, cc), F32)] * 3,
        scratch_shapes=_conv_scratch(tt, cc) + [pltpu.VMEM((HALO, SUBLANES, cc), F32)],
        compiler_params=_cparams(("arbitrary",)), name=name)(
            u, u, u, u, _pad_conv_w(w), cb.reshape(1, cc), lg.reshape(1, cc), lb.reshape(1, cc), dout)
    return dc, dw[:CONV_WIDTH], dcb[0], dlg[0], dlb[0]


def _conv_bwd_input(u, dc, w, name):
    s = u.shape[0]
    cc = w.shape[1]
    tt = _pick(s, (256, 128, 64))
    nh = tt // HALO
    nlast = s // HALO - 1
    cur = lambda cb: pl.BlockSpec((tt, cc), lambda i: (i, cb))
    nxt = pl.BlockSpec((HALO, cc), lambda i: (jnp.minimum((i + 1) * nh, nlast), 0))
    wsp = pl.BlockSpec((HALO, cc), lambda i: (0, 0))
    nblk = s // tt

    def body(a_ref, gt_ref, dc_ref, dn_ref, w_ref, da_ref, dg_ref, ext_ref, sh_ref):
        i = pl.program_id(0)
        ext_ref[0:tt, :] = dc_ref[...]
        ext_ref[tt:tt + HALO, :] = jnp.where(i < nblk - 1, dn_ref[...], 0.0)
        tap = _shifted_rows(ext_ref, sh_ref, tt)
        dh = jnp.zeros((tt, cc), F32)
        for j in range(CONV_WIDTH):
            dh = dh + w_ref[j:j + 1, :] * tap(CONV_WIDTH - 1 - j)
        a = a_ref[...]
        sg = _sigmoid(gt_ref[...])
        da_ref[...] = (dh * sg).astype(BF16)
        dg_ref[...] = (dh * a * sg * (1.0 - sg)).astype(BF16)

    ocol = pl.BlockSpec((tt, cc), lambda i: (i, 0))
    return pl.pallas_call(
        body, grid=(nblk,), in_specs=[cur(0), cur(1), ocol, nxt, wsp], out_specs=[ocol, ocol],
        out_shape=[jax.ShapeDtypeStruct((s, cc), BF16)] * 2,
        scratch_shapes=_conv_scratch(tt, cc),
        compiler_params=_cparams(("parallel",)), name=name)(u, u, dc, dc, _pad_conv_w(w))


def _hgrn_gates(hq, hf, lb):
    sg = _sigmoid(hf)
    f = lb + (1.0 - lb) * sg
    lf = jnp.log(f)
    big_l = _tri_matmul(_tril(CHUNK), lf, 3)
    l_end = jnp.sum(lf, axis=0, keepdims=True)
    l_mid = jnp.sum(lf[0:CHUNK // 2, :], axis=0, keepdims=True)
    sq = _sigmoid(hq)
    q = hq * sq
    return sg, f, 1.0 - f, big_l, l_end, l_mid, sq, q


def _causal_mask(n):
    r = lax.broadcasted_iota(jnp.int32, (n, n), 0)
    c = lax.broadcasted_iota(jnp.int32, (n, n), 1)
    return c <= r


def _hgrn_fwd(u, lb, gg, name):
    s = u.shape[0]
    w = HEADS * DH
    nch = s // CHUNK

    def body(q_ref, f_ref, i_ref, g_ref, lb_ref, gg_ref, o_ref, out_ref, st_ref, state):
        @pl.when(pl.program_id(0) == 0)
        def _():
            state[...] = jnp.zeros_like(state)

        mask = _causal_mask(CHUNK)
        for hd in range(HEADS):
            sl = slice(hd * DH, (hd + 1) * DH)
            _, _, kk, big_l, l_end, l_mid, _, q = _hgrn_gates(q_ref[:, sl], f_ref[:, sl], lb_ref[:, sl])
            v = i_ref[:, sl]
            qs = q * jnp.exp(big_l - l_mid)
            ks = kk * jnp.exp(l_mid - big_l)
            att = jnp.where(mask, _nt3(qs, ks), 0.0)
            st0 = state[hd]
            st_ref[0, hd] = st0
            o = _nn3(att, v) + _nt3(q * jnp.exp(big_l), st0)
            state[hd] = st0 * jnp.exp(l_end) + _tn3(v, kk * jnp.exp(l_end - big_l))
            o_ref[:, sl] = o
            on = o * lax.rsqrt(jnp.mean(o * o, axis=-1, keepdims=True) + LN_EPS)
            gv = g_ref[:, sl]
            out_ref[:, sl] = on * gg_ref[:, sl] * (gv * _sigmoid(gv))

    col = lambda cb: pl.BlockSpec((CHUNK, w), lambda i: (i, cb))
    vec = pl.BlockSpec((1, w), lambda i: (0, 0))
    ocol = pl.BlockSpec((CHUNK, w), lambda i: (i, 0))
    return pl.pallas_call(
        body, grid=(nch,), in_specs=[col(2), col(3), col(4), col(5), vec, vec],
        out_specs=[ocol, ocol, pl.BlockSpec((1, HEADS, DH, DH), lambda i: (i, 0, 0, 0))],
        out_shape=[jax.ShapeDtypeStruct((s, w), F32), jax.ShapeDtypeStruct((s, w), F32),
                   jax.ShapeDtypeStruct((nch, HEADS, DH, DH), F32)],
        scratch_shapes=[pltpu.VMEM((HEADS, DH, DH), F32)],
        compiler_params=_cparams(("arbitrary",)), name=name)(u, u, u, u, lb.reshape(1, w), gg.reshape(1, w))


def _hgrn_bwd(u, o_raw, states, dout, lb, gg, name):
    s = u.shape[0]
    w = HEADS * DH
    nch = s // CHUNK

    def body(q_ref, f_ref, i_ref, g_ref, o_ref, st_ref, do_ref, lb_ref, gg_ref,
             dq_ref, df_ref, di_ref, dg_ref, dlb_ref, dgg_ref, dstate):
        @pl.when(pl.program_id(0) == 0)
        def _():
            dstate[...] = jnp.zeros_like(dstate)
            dlb_ref[...] = jnp.zeros_like(dlb_ref)
            dgg_ref[...] = jnp.zeros_like(dgg_ref)

        mask = _causal_mask(CHUNK)
        last_row = lax.broadcasted_iota(jnp.int32, (CHUNK, DH), 0) == CHUNK - 1
        tri_up = _tril(CHUNK, upper=True)
        for hd in range(HEADS):
            sl = slice(hd * DH, (hd + 1) * DH)
            hq = q_ref[:, sl]
            lbv = lb_ref[:, sl]
            sg, f, kk, big_l, l_end, l_mid, sq, q = _hgrn_gates(hq, f_ref[:, sl], lbv)
            v = i_ref[:, sl]
            e_l = jnp.exp(big_l)
            e_qm = jnp.exp(big_l - l_mid)
            e_km = jnp.exp(l_mid - big_l)
            e_ke = jnp.exp(l_end - big_l)
            e_end = jnp.exp(l_end)
            qs = q * e_qm
            ks = kk * e_km
            qe = q * e_l
            ke = kk * e_ke
            att = jnp.where(mask, _nt3(qs, ks), 0.0)
            st0 = st_ref[0, hd]
            dst1 = dstate[hd]
            o = o_ref[:, sl]
            rinv = lax.rsqrt(jnp.mean(o * o, axis=-1, keepdims=True) + LN_EPS)
            on = o * rinv
            gv = g_ref[:, sl]
            sgg = _sigmoid(gv)
            gsil = gv * sgg
            ggv = gg_ref[:, sl]
            dov = do_ref[:, sl]
            don = dov * ggv * gsil
            dg_ref[:, sl] = (dov * on * ggv * (sgg * (1.0 + gv * (1.0 - sgg)))).astype(BF16)
            dgg_ref[:, sl] += jnp.sum(dov * on * gsil, axis=0, keepdims=True)
            do = rinv * (don - on * jnp.mean(don * on, axis=-1, keepdims=True))
            datt = jnp.where(mask, _nt3(do, v), 0.0)
            dv = _tn3(att, do) + _nt3(ke, dst1)
            dqs = _nn3(datt, ks)
            dks = _tn3(datt, qs)
            dqe = _nn3(do, st0)
            dke = _nn3(v, dst1)
            dq = dqs * e_qm + dqe * e_l
            dk = dks * e_km + dke * e_ke
            dke_ke = dke * ke
            dl = dqs * qs - dks * ks + dqe * qe - dke_ke
            dl_end = jnp.sum(dke_ke, axis=0, keepdims=True) + jnp.sum(dst1 * st0, axis=0, keepdims=True) * e_end
            dl = dl + jnp.where(last_row, dl_end, 0.0)
            dlf = _tri_matmul(tri_up, dl, 2)
            dfv = dlf / f - dk
            df_ref[:, sl] = (dfv * (1.0 - lbv) * sg * (1.0 - sg)).astype(BF16)
            dlb_ref[:, sl] += jnp.sum(dfv * (1.0 - sg), axis=0, keepdims=True)
            dq_ref[:, sl] = (dq * (sq * (1.0 + hq * (1.0 - sq)))).astype(BF16)
            di_ref[:, sl] = dv.astype(BF16)
            dstate[hd] = dst1 * e_end + _tn3(do, qe)

    rev = lambda i: nch - 1 - i
    col = lambda cb: pl.BlockSpec((CHUNK, w), lambda i: (rev(i), cb))
    vec = pl.BlockSpec((1, w), lambda i: (0, 0))
    ocol = pl.BlockSpec((CHUNK, w), lambda i: (rev(i), 0))
    res = pl.pallas_call(
        body, grid=(nch,),
        in_specs=[col(2), col(3), col(4), col(5), ocol,
                  pl.BlockSpec((1, HEADS, DH, DH), lambda i: (rev(i), 0, 0, 0)), col(1), vec, vec],
        out_specs=[ocol, ocol, ocol, ocol, vec, vec],
        out_shape=[jax.ShapeDtypeStruct((s, w), BF16)] * 4 + [jax.ShapeDtypeStruct((1, w), F32)] * 2,
        scratch_shapes=[pltpu.VMEM((HEADS, DH, DH), F32)],
        compiler_params=_cparams(("arbitrary",)), name=name)(
            u, u, u, u, o_raw, states, dout, lb.reshape(1, w), gg.reshape(1, w))
    return res[0], res[1], res[2], res[3], res[4][0], res[5][0]


def _log_sigmoid(x):
    return jnp.minimum(x, 0.0) - jnp.log(1.0 + jnp.exp(-jnp.abs(x)))


def _fgate_fwd(cf, bf, name):
    s, wl = cf.shape
    tb = _pick(s, (512, 256, 128, 64))

    def body(c_ref, b_ref, f_ref, carry):
        @pl.when(pl.program_id(0) == 0)
        def _():
            carry[...] = jnp.zeros_like(carry)

        ls = _log_sigmoid(c_ref[...] + b_ref[...])
        f_ref[...] = _tri_matmul(_tril(tb), ls, 3) + carry[...]
        carry[...] += jnp.sum(ls, axis=0, keepdims=True)

    return pl.pallas_call(
        body, grid=(s // tb,), in_specs=[pl.BlockSpec((tb, wl), lambda i: (i, 0)), pl.BlockSpec((1, wl), lambda i: (0, 0))],
        out_specs=pl.BlockSpec((tb, wl), lambda i: (i, 0)), out_shape=jax.ShapeDtypeStruct((s, wl), F32),
        scratch_shapes=[pltpu.VMEM((1, wl), F32)],
        compiler_params=_cparams(("arbitrary",)), name=name)(cf, bf)


def _fgate_bwd(dF, cf, bf, name):
    s, wl = cf.shape
    tb = _pick(s, (512, 256, 128, 64))
    nb = s // tb

    def body(d_ref, c_ref, b_ref, dc_ref, db_ref, carry):
        @pl.when(pl.program_id(0) == 0)
        def _():
            carry[...] = jnp.zeros_like(carry)
            db_ref[...] = jnp.zeros_like(db_ref)

        dv = d_ref[...]
        dls = _tri_matmul(_tril(tb, upper=True), dv, 3) + carry[...]
        carry[...] += jnp.sum(dv, axis=0, keepdims=True)
        dc = dls * (1.0 - _sigmoid(c_ref[...] + b_ref[...]))
        dc_ref[...] = dc.astype(BF16)
        db_ref[...] += jnp.sum(dc, axis=0, keepdims=True)

    blk = pl.BlockSpec((tb, wl), lambda i: (nb - 1 - i, 0))
    vec = pl.BlockSpec((1, wl), lambda i: (0, 0))
    dc, db = pl.pallas_call(
        body, grid=(nb,), in_specs=[blk, blk, vec], out_specs=[blk, vec],
        out_shape=[jax.ShapeDtypeStruct((s, wl), BF16), jax.ShapeDtypeStruct((1, wl), F32)],
        scratch_shapes=[pltpu.VMEM((1, wl), F32)],
        compiler_params=_cparams(("arbitrary",)), name=name)(dF, cf, bf)
    return dc, db[0]


FOX_HPB = 2


def _fox_scores(q, k, fq, fk, diag):
    sc = _nt(q, k) * (DH ** -0.5) + fq - fk
    if not diag:
        return sc
    r = lax.broadcasted_iota(jnp.int32, sc.shape, 0)
    c = lax.broadcasted_iota(jnp.int32, sc.shape, 1)
    return jnp.where(c <= r, sc, NEG)


def _fox_when_needed(q_blk, k_blk, step):
    @pl.when(k_blk < q_blk)
    def _():
        step(False)

    @pl.when(k_blk == q_blk)
    def _():
        step(True)


def _fox_fwd(u, f_col, f_row, name):
    s = u.shape[0]
    t = _pick(s, (512, 256, 128))
    nb = s // t
    hpb = FOX_HPB
    wb = hpb * DH
    ng = HEADS // hpb

    v_t = jnp.transpose(u[:, 2 * HEADS * DH:3 * HEADS * DH]).astype(BF16)

    def body(q_ref, k_ref, vt_ref, fq_ref, fk_ref, o_ref, lse_ref, m_sc, l_sc, acc):
        i, j = pl.program_id(1), pl.program_id(2)

        @pl.when(j == 0)
        def _():
            m_sc[...] = jnp.full_like(m_sc, NEG)
            l_sc[...] = jnp.zeros_like(l_sc)
            acc[...] = jnp.zeros_like(acc)

        def step(diag):
            for hh in range(hpb):
                sl = slice(hh * DH, (hh + 1) * DH)
                sc = _nt(k_ref[:, sl], q_ref[:, sl]) * (DH ** -0.5) + fq_ref[hh] - fk_ref[hh]
                if diag:
                    r = lax.broadcasted_iota(jnp.int32, sc.shape, 0)
                    c = lax.broadcasted_iota(jnp.int32, sc.shape, 1)
                    sc = jnp.where(r <= c, sc, NEG)
                m_new = jnp.maximum(m_sc[hh], jnp.max(sc, axis=0, keepdims=True))
                a = jnp.exp(m_sc[hh] - m_new)
                p = jnp.exp(sc - m_new)
                l_sc[hh] = a * l_sc[hh] + jnp.sum(p, axis=0, keepdims=True)
                acc[sl, :] = a * acc[sl, :] + _nn(vt_ref[sl, :], p)
                m_sc[hh] = m_new

        _fox_when_needed(i, j, step)

        @pl.when(j == i)
        def _():
            for hh in range(hpb):
                sl = slice(hh * DH, (hh + 1) * DH)
                o_ref[:, sl] = jnp.transpose(acc[sl, :] / l_sc[hh])
                lse_ref[hh] = m_sc[hh] + jnp.log(l_sc[hh])

    qs = pl.BlockSpec((t, wb), lambda h, i, j: (i, h))
    ks = pl.BlockSpec((t, wb), lambda h, i, j: (jnp.minimum(j, i), ng + h))
    o, lse_row = pl.pallas_call(
        body, grid=(ng, nb, nb),
        in_specs=[qs, ks, pl.BlockSpec((wb, t), lambda h, i, j: (h, jnp.minimum(j, i))),
                  pl.BlockSpec((hpb, 1, t), lambda h, i, j: (h, 0, i)),
                  pl.BlockSpec((hpb, t, 1), lambda h, i, j: (h, jnp.minimum(j, i), 0))],
        out_specs=[pl.BlockSpec((t, wb), lambda h, i, j: (i, h)), pl.BlockSpec((hpb, 1, t), lambda h, i, j: (h, 0, i))],
        out_shape=[jax.ShapeDtypeStruct((s, HEADS * DH), F32), jax.ShapeDtypeStruct((HEADS, 1, s), F32)],
        scratch_shapes=[pltpu.VMEM((hpb, 1, t), F32), pltpu.VMEM((hpb, 1, t), F32), pltpu.VMEM((wb, t), F32)],
        compiler_params=_cparams(("parallel", "parallel", "arbitrary")), name=name)(u, u, v_t, f_row, f_col)
    return o, jnp.transpose(lse_row, (0, 2, 1)), lse_row


def _fox_bwd_kv(u, o, lse_row, dout, f_col, f_row, name):
    s = u.shape[0]
    t = _pick(s, (512, 256, 128))
    nb = s // t
    hpb = FOX_HPB
    wb = hpb * DH
    ng = HEADS // hpb

    def body(q_ref, k_ref, v_ref, o_ref, do_ref, lse_ref, fq_ref, fk_ref, dk_ref, dv_ref, dfk_ref, dk_acc, dv_acc, df_acc):
        j, i = pl.program_id(1), pl.program_id(2)

        @pl.when(i == 0)
        def _():
            dk_acc[...] = jnp.zeros_like(dk_acc)
            dv_acc[...] = jnp.zeros_like(dv_acc)
            df_acc[...] = jnp.zeros_like(df_acc)

        ones = jnp.ones((8, DH), BF16)

        def step(diag):
            for hh in range(hpb):
                sl = slice(hh * DH, (hh + 1) * DH)
                q = q_ref[:, sl]
                do = do_ref[:, sl]
                sc = _nt(k_ref[:, sl], q) * (DH ** -0.5) + fq_ref[hh] - fk_ref[hh]
                if diag:
                    r = lax.broadcasted_iota(jnp.int32, sc.shape, 0)
                    c = lax.broadcasted_iota(jnp.int32, sc.shape, 1)
                    sc = jnp.where(r <= c, sc, NEG)
                p = jnp.exp(sc - lse_ref[hh])
                prod = do * o_ref[:, sl]
                hi = prod.astype(BF16)
                lo = (prod - hi.astype(F32)).astype(BF16)
                dims = (((1,), (1,)), ((), ()))
                delta = (lax.dot_general(ones, hi, dims, preferred_element_type=F32)
                         + lax.dot_general(ones, lo, dims, preferred_element_type=F32))[0:1, :]
                dv_acc[:, sl] += _nn(p, do)
                ds = p * (_nt(v_ref[:, sl], do) - delta)
                dk_acc[:, sl] += _nn(ds, q) * (DH ** -0.5)
                df_acc[hh] -= jnp.sum(ds, axis=1, keepdims=True)

        _fox_when_needed(i, j, step)

        @pl.when(i == nb - 1)
        def _():
            dk_ref[...] = dk_acc[...].astype(BF16)
            dv_ref[...] = dv_acc[...].astype(BF16)
            dfk_ref[...] = df_acc[...]

    qi = lambda j, i: jnp.maximum(i, j)
    qs = lambda base: pl.BlockSpec((t, wb), lambda h, j, i: (qi(j, i), base + h))
    ks = lambda base: pl.BlockSpec((t, wb), lambda h, j, i: (j, base + h))
    return pl.pallas_call(
        body, grid=(ng, nb, nb),
        in_specs=[qs(0), ks(ng), ks(2 * ng), qs(0), qs(0),
                  pl.BlockSpec((hpb, 1, t), lambda h, j, i: (h, 0, qi(j, i))),
                  pl.BlockSpec((hpb, 1, t), lambda h, j, i: (h, 0, qi(j, i))),
                  pl.BlockSpec((hpb, t, 1), lambda h, j, i: (h, j, 0))],
        out_specs=[pl.BlockSpec((t, wb), lambda h, j, i: (j, h)), pl.BlockSpec((t, wb), lambda h, j, i: (j, h)),
                   pl.BlockSpec((hpb, t, 1), lambda h, j, i: (h, j, 0))],
        out_shape=[jax.ShapeDtypeStruct((s, HEADS * DH), BF16)] * 2 + [jax.ShapeDtypeStruct((HEADS, s, 1), F32)],
        scratch_shapes=[pltpu.VMEM((t, wb), F32), pltpu.VMEM((t, wb), F32), pltpu.VMEM((hpb, t, 1), F32)],
        compiler_params=_cparams(("parallel", "parallel", "arbitrary")), name=name)(
            u, u, u, o, dout, lse_row, f_row, f_col)


def _fox_bwd_q(u, o, lse, dout, f_col, f_row, name):
    s = u.shape[0]
    t = _pick(s, (512, 256, 128))
    nb = s // t
    hpb = FOX_HPB
    wb = hpb * DH
    ng = HEADS // hpb

    def body(q_ref, k_ref, v_ref, o_ref, do_ref, lse_ref, fq_ref, fk_ref, dq_ref, dfq_ref, dq_acc, df_acc):
        i, j = pl.program_id(1), pl.program_id(2)

        @pl.when(j == 0)
        def _():
            dq_acc[...] = jnp.zeros_like(dq_acc)
            df_acc[...] = jnp.zeros_like(df_acc)

        def step(diag):
            for hh in range(hpb):
                sl = slice(hh * DH, (hh + 1) * DH)
                do = do_ref[:, sl]
                k = k_ref[:, sl]
                sc = _fox_scores(q_ref[:, sl], k, fq_ref[hh], fk_ref[hh], diag)
                p = jnp.exp(sc - lse_ref[hh])
                delta = jnp.sum(do * o_ref[:, sl], axis=-1, keepdims=True)
                ds = p * (_nt(do, v_ref[:, sl]) - delta)
                dq_acc[:, sl] += _nn(ds, k) * (DH ** -0.5)
                df_acc[hh] += jnp.sum(ds, axis=-1, keepdims=True)

        _fox_when_needed(i, j, step)

        @pl.when(j == i)
        def _():
            dq_ref[...] = dq_acc[...].astype(BF16)
            dfq_ref[...] = df_acc[...]

    qs = lambda base: pl.BlockSpec((t, wb), lambda h, i, j: (i, base + h))
    ks = lambda base: pl.BlockSpec((t, wb), lambda h, i, j: (jnp.minimum(j, i), base + h))
    return pl.pallas_call(
        body, grid=(ng, nb, nb),
        in_specs=[qs(0), ks(ng), ks(2 * ng), qs(0), qs(0),
                  pl.BlockSpec((hpb, t, 1), lambda h, i, j: (h, i, 0)),
                  pl.BlockSpec((hpb, t, 1), lambda h, i, j: (h, i, 0)),
                  pl.BlockSpec((hpb, 1, t), lambda h, i, j: (h, 0, jnp.minimum(j, i)))],
        out_specs=[pl.BlockSpec((t, wb), lambda h, i, j: (i, h)), pl.BlockSpec((hpb, t, 1), lambda h, i, j: (h, i, 0))],
        out_shape=[jax.ShapeDtypeStruct((s, HEADS * DH), BF16), jax.ShapeDtypeStruct((HEADS, s, 1), F32)],
        scratch_shapes=[pltpu.VMEM((t, wb), F32), pltpu.VMEM((hpb, t, 1), F32)],
        compiler_params=_cparams(("parallel", "parallel", "arbitrary")), name=name)(
            u, u, u, o, dout, lse, f_col, f_row)


def _ca_bias(rel_bias):
    band = (CA_LEFT + 1) * CHUNK
    n_clip = band - REL_CLIP
    gv = jnp.concatenate([jnp.broadcast_to(rel_bias[:, REL_TABLE - 1:], (rel_bias.shape[0], n_clip)),
                          rel_bias[:, REL_TABLE - 2::-1]], axis=1)
    rows = [gv[:, CHUNK - 1 - qi:CHUNK - 1 - qi + band] for qi in range(CHUNK)]
    b = jnp.stack(rows, axis=1)
    return jnp.concatenate([b, jnp.full((b.shape[0], CHUNK, CA_WIN - band), NEG, F32)], axis=2)


def _ca_bias_grad(db):
    band = (CA_LEFT + 1) * CHUNK
    n_clip = band - REL_CLIP
    dgv = None
    for qi in range(CHUNK):
        t = jnp.pad(db[:, qi, :band], ((0, 0), (CHUNK - 1 - qi, qi)))
        dgv = t if dgv is None else dgv + t
    return jnp.concatenate([dgv[:, n_clip:][:, ::-1], jnp.sum(dgv[:, :n_clip], axis=1, keepdims=True)], axis=1)


def _ca_specs():
    cur = lambda base: pl.BlockSpec((CA_BLK, DH), lambda h, i: (i, base + h))
    prev = lambda base: pl.BlockSpec((CA_BLK, DH), lambda h, i: (jnp.maximum(i - 1, 0), base + h))
    bias = pl.BlockSpec((1, CHUNK, CA_WIN), lambda h, i: (h, 0, 0))
    return cur, prev, bias


def _ca_fill(kcat, vcat, kp_ref, kc_ref, vp_ref, vc_ref):
    kcat[0:CA_BLK, :] = kp_ref[...]
    kcat[CA_BLK:2 * CA_BLK, :] = kc_ref[...]
    kcat[2 * CA_BLK:, :] = jnp.zeros((CHUNK, DH), F32)
    vcat[0:CA_BLK, :] = vp_ref[...]
    vcat[CA_BLK:2 * CA_BLK, :] = vc_ref[...]
    vcat[2 * CA_BLK:, :] = jnp.zeros((CHUNK, DH), F32)


def _ca_probs(i, c, q, kw, bias):
    sc = _nt(q, kw) * (DH ** -0.5) + bias
    col = lax.broadcasted_iota(jnp.int32, sc.shape, 1)
    sc = jnp.where((i - 1) * CA_BLK + c * CHUNK + col >= 0, sc, NEG)
    p = jnp.exp(sc - jnp.max(sc, axis=-1, keepdims=True))
    return p / jnp.sum(p, axis=-1, keepdims=True)


def _ca_fwd(u, bias, name):
    s = u.shape[0]
    cur, prev, bsp = _ca_specs()

    def body(q_ref, kp_ref, kc_ref, vp_ref, vc_ref, b_ref, o_ref, kcat, vcat):
        i = pl.program_id(1)
        _ca_fill(kcat, vcat, kp_ref, kc_ref, vp_ref, vc_ref)
        for c in range(CA_LEFT):
            rows = slice(c * CHUNK, (c + 1) * CHUNK)
            win = slice(c * CHUNK, c * CHUNK + CA_WIN)
            p = _ca_probs(i, c, q_ref[rows, :], kcat[win, :], b_ref[0])
            o_ref[rows, :] = _nn(p, vcat[win, :])

    return pl.pallas_call(
        body, grid=(HEADS, s // CA_BLK),
        in_specs=[cur(3 * HEADS), prev(4 * HEADS), cur(4 * HEADS), prev(5 * HEADS), cur(5 * HEADS), bsp],
        out_specs=pl.BlockSpec((CA_BLK, DH), lambda h, i: (i, h)),
        out_shape=jax.ShapeDtypeStruct((s, HEADS * DH), F32),
        scratch_shapes=[pltpu.VMEM((2 * CA_BLK + CHUNK, DH), F32)] * 2,
        compiler_params=_cparams(("parallel", "parallel")), name=name)(u, u, u, u, u, bias)


def _ca_bwd(u, bias, dout, name):
    s = u.shape[0]
    cur, prev, bsp = _ca_specs()
    rows_cat = 2 * CA_BLK + CHUNK

    def body(q_ref, kp_ref, kc_ref, vp_ref, vc_ref, b_ref, do_ref,
             dq_ref, dka_ref, dkb_ref, dva_ref, dvb_ref, db_ref, kcat, vcat, dkcat, dvcat):
        i = pl.program_id(1)

        @pl.when(i == 0)
        def _():
            db_ref[...] = jnp.zeros_like(db_ref)

        _ca_fill(kcat, vcat, kp_ref, kc_ref, vp_ref, vc_ref)
        dkcat[...] = jnp.zeros_like(dkcat)
        dvcat[...] = jnp.zeros_like(dvcat)
        for c in range(CA_LEFT):
            rows = slice(c * CHUNK, (c + 1) * CHUNK)
            win = slice(c * CHUNK, c * CHUNK + CA_WIN)
            q = q_ref[rows, :]
            kw = kcat[win, :]
            vw = vcat[win, :]
            do = do_ref[rows, :]
            p = _ca_probs(i, c, q, kw, b_ref[0])
            dp = _nt(do, vw)
            ds = p * (dp - jnp.sum(p * dp, axis=-1, keepdims=True))
            dq_ref[rows, :] = (_nn(ds, kw) * (DH ** -0.5)).astype(BF16)
            dkcat[win, :] += _tn(ds, q) * (DH ** -0.5)
            dvcat[win, :] += _tn(p, do)
            db_ref[0] += ds
        dkb_ref[...] = dkcat[0:CA_BLK, :]
        dka_ref[...] = dkcat[CA_BLK:2 * CA_BLK, :]
        dvb_ref[...] = dvcat[0:CA_BLK, :]
        dva_ref[...] = dvcat[CA_BLK:2 * CA_BLK, :]

    osp = pl.BlockSpec((CA_BLK, DH), lambda h, i: (i, h))
    shp = jax.ShapeDtypeStruct((s, HEADS * DH), F32)
    return pl.pallas_call(
        body, grid=(HEADS, s // CA_BLK),
        in_specs=[cur(3 * HEADS), prev(4 * HEADS), cur(4 * HEADS), prev(5 * HEADS), cur(5 * HEADS), bsp,
                  pl.BlockSpec((CA_BLK, DH), lambda h, i: (i, HEADS + h))],
        out_specs=[osp, osp, osp, osp, osp, bsp],
        out_shape=[jax.ShapeDtypeStruct((s, HEADS * DH), BF16), shp, shp, shp, shp,
                   jax.ShapeDtypeStruct((HEADS, CHUNK, CA_WIN), F32)],
        scratch_shapes=[pltpu.VMEM((rows_cat, DH), F32)] * 4,
        compiler_params=_cparams(("parallel", "arbitrary")), name=name)(u, u, u, u, u, bias, dout)


def _ca_merge(da, db):
    shifted = jnp.concatenate([db[CA_BLK:], jnp.zeros((CA_BLK, db.shape[1]), F32)], axis=0)
    return (da + shifted).astype(BF16)


def _place():
    x, y, c = lax.axis_index("x"), lax.axis_index("y"), lax.axis_index("c")
    return x, y, c, [(1 - x, y), (x, 1 - y), (1 - x, 1 - y)]


_ANY = pl.BlockSpec(memory_space=pl.ANY)


def _all_gather_chips(w, name):
    def body(w_ref, o_ref, send_sems, recv_sems, loc_sems):
        _exchange(["ag"], [w_ref], [o_ref], send_sems, recv_sems, loc_sems, start=True)
        _exchange(["ag"], [w_ref], [o_ref], send_sems, recv_sems, loc_sems, start=False)

    return pl.pallas_call(
        body, in_specs=[_ANY], out_specs=_ANY, out_shape=jax.ShapeDtypeStruct((4,) + w.shape, w.dtype),
        scratch_shapes=[pltpu.SemaphoreType.DMA((3,)), pltpu.SemaphoreType.DMA((3,)), pltpu.SemaphoreType.DMA((1,))],
        name=name)(w)


def _exchange(kinds, srcs, dsts, send_sems, recv_sems, loc_sems, start):
    x, y, c, peers = _place()
    me = 2 * x + y
    for n, kind in enumerate(kinds):
        src, dst = srcs[n], dsts[n]
        mine = src if kind == "ag" else src.at[me]
        loc = pltpu.make_async_copy(mine, dst.at[me], loc_sems.at[n])
        copies = []
        for k, (px, py) in enumerate(peers):
            out_src = src if kind == "ag" else src.at[2 * px + py]
            send = pltpu.make_async_remote_copy(src_ref=out_src, dst_ref=dst.at[me], send_sem=send_sems.at[3 * n + k],
                                                recv_sem=recv_sems.at[3 * n + k], device_id=(px, py, c),
                                                device_id_type=MESH)
            recv = pltpu.make_async_remote_copy(src_ref=mine, dst_ref=dst.at[2 * px + py],
                                                send_sem=send_sems.at[3 * n + k], recv_sem=recv_sems.at[3 * n + k],
                                                device_id=(px, py, c), device_id_type=MESH)
            copies.append((send, recv))
        if start:
            loc.start()
            for send, _ in copies:
                send.start()
        else:
            for _, recv in copies:
                recv.wait_recv()
            for send, _ in copies:
                send.wait_send()
            loc.wait()


def _core_swap(a, name):
    def body(a_ref, o_ref, send_sem, recv_sem):
        x, y, c, _ = _place()
        cp = pltpu.make_async_remote_copy(src_ref=a_ref, dst_ref=o_ref, send_sem=send_sem, recv_sem=recv_sem,
                                          device_id=(x, y, 1 - c), device_id_type=MESH)
        cp.start()
        cp.wait()

    return pl.pallas_call(
        body, in_specs=[_ANY], out_specs=_ANY, out_shape=jax.ShapeDtypeStruct(a.shape, a.dtype),
        scratch_shapes=[pltpu.SemaphoreType.DMA(()), pltpu.SemaphoreType.DMA(())], name=name)(a)


def _all_reduce_small(v, name):
    r, wl = v.shape

    def body(v_ref, o_ref, buf, send_sems, recv_sems):
        x, y, c, _ = _place()
        me = 4 * x + 2 * y + c
        buf[me] = v_ref[...]
        flips = [(fx, fy, fc) for fx in (0, 1) for fy in (0, 1) for fc in (0, 1) if (fx, fy, fc) != (0, 0, 0)]
        peer = lambda f: (x ^ f[0], y ^ f[1], c ^ f[2])
        sends = []
        for k, f in enumerate(flips):
            cp = pltpu.make_async_remote_copy(src_ref=v_ref, dst_ref=buf.at[me], send_sem=send_sems.at[k],
                                              recv_sem=recv_sems.at[k], device_id=peer(f), device_id_type=MESH)
            cp.start()
            sends.append(cp)
        for k, f in enumerate(flips):
            px, py, pc = peer(f)
            pltpu.make_async_remote_copy(src_ref=v_ref, dst_ref=buf.at[4 * px + 2 * py + pc], send_sem=send_sems.at[k],
                                         recv_sem=recv_sems.at[k], device_id=peer(f), device_id_type=MESH).wait_recv()
        for cp in sends:
            cp.wait_send()
        acc = buf[0]
        for d in range(1, 8):
            acc = acc + buf[d]
        o_ref[...] = acc

    vm = pl.BlockSpec(memory_space=pltpu.VMEM)
    return pl.pallas_call(
        body, in_specs=[vm], out_specs=vm, out_shape=jax.ShapeDtypeStruct((r, wl), F32),
        scratch_shapes=[pltpu.VMEM((8, r, wl), F32), pltpu.SemaphoreType.DMA((7,)), pltpu.SemaphoreType.DMA((7,))],
        name=name)(v)


def _sum_slots(g, name):
    _, r, cdim = g.shape
    br = _pick(r, (256, 128, 64, 32, 8))

    def body(g_ref, o_ref):
        o_ref[...] = ((g_ref[0].astype(F32) + g_ref[1].astype(F32)) + g_ref[2].astype(F32)) + g_ref[3].astype(F32)

    return pl.pallas_call(
        body, grid=(r // br,), in_specs=[pl.BlockSpec((4, br, cdim), lambda i: (0, i, 0))],
        out_specs=pl.BlockSpec((br, cdim), lambda i: (i, 0)), out_shape=jax.ShapeDtypeStruct((r, cdim), F32),
        compiler_params=_cparams(("parallel",)), name=name)(g)


def _adamw(w, ga, gb, m, v, name):
    r, cdim = w.shape
    br = _pick(r, (256, 128, 64, 32, 8))
    c1 = 1.0 / (1.0 - ADAM_B1 ** ADAM_STEP)
    c2 = 1.0 / (1.0 - ADAM_B2 ** ADAM_STEP)
    two = gb is not None

    def body(*refs):
        w_ref, ga_ref = refs[0], refs[1]
        gb_ref = refs[2] if two else None
        m_ref, v_ref, g_out, d_out, m_out, v_out = refs[2 + two:]
        g = ga_ref[...] + gb_ref[...] if two else ga_ref[...]
        mn = ADAM_B1 * m_ref[...] + (1.0 - ADAM_B1) * g
        vn = ADAM_B2 * v_ref[...] + (1.0 - ADAM_B2) * (g * g)
        g_out[...] = g
        m_out[...] = mn
        v_out[...] = vn
        d_out[...] = -ADAM_LR * ((mn * c1) / (jnp.sqrt(vn * c2) + ADAM_EPS) + ADAM_WD * w_ref[...])

    blk = pl.BlockSpec((br, cdim), lambda i: (i, 0))
    args = (w, ga) + ((gb,) if two else ()) + (m, v)
    return pl.pallas_call(
        body, grid=(r // br,), in_specs=[blk] * len(args), out_specs=[blk] * 4,
        out_shape=[jax.ShapeDtypeStruct((r, cdim), F32)] * 4,
        compiler_params=_cparams(("parallel",)), name=name)(*args)


def _unshard_cols(g):
    return jnp.transpose(g, (1, 0, 2)).reshape(g.shape[1], 4 * g.shape[2])


def _unshard_rows(g):
    return g.reshape(4 * g.shape[1], g.shape[2])


def _shard_cols(g):
    k, n = g.shape
    return jnp.transpose(g.reshape(k, 4, n // 4), (1, 0, 2))


def _shard_rows(g):
    k, n = g.shape
    return g.reshape(4, k // 4, n)


def _lower_bound(logits):
    return jnp.cumsum(jax.nn.softmax(logits.astype(F32), axis=0), axis=0)[0]


def _pack(parts):
    flat = jnp.concatenate([p.reshape(-1) for p in parts])
    n = flat.shape[0]
    rows = -(-n // 1024) * 8
    return jnp.pad(flat, (0, rows * 128 - n)).reshape(rows, 128)


def _unpack(packed, shapes):
    flat = packed.reshape(-1)
    out, off = [], 0
    for shp in shapes:
        n = 1
        for d in shp:
            n *= d
        out.append(flat[off:off + n].reshape(shp))
        off += n
    return out


def kernel(x, ev_w_in, ev_conv_w, ev_conv_b, ev_conv_ln_g, ev_conv_ln_b, hgrn_lb_logits, ev_gnorm_g, ev_w_out, od_w_in, fox_b_f, rel_bias, od_w_out, ln_mix_g, ln_mix_b, mlp_w1, mlp_w2, ln_mlp_g, ln_mlp_b, loss_target, m_ev_w_in, m_ev_conv_w, m_ev_conv_b, m_ev_conv_ln_g, m_ev_conv_ln_b, m_hgrn_lb_logits, m_ev_gnorm_g, m_ev_w_out, m_od_w_in, m_fox_b_f, m_rel_bias, m_od_w_out, m_ln_mix_g, m_ln_mix_b, m_mlp_w1, m_mlp_w2, m_ln_mlp_g, m_ln_mlp_b, v_ev_w_in, v_ev_conv_w, v_ev_conv_b, v_ev_conv_ln_g, v_ev_conv_ln_b, v_hgrn_lb_logits, v_ev_gnorm_g, v_ev_w_out, v_od_w_in, v_fox_b_f, v_rel_bias, v_od_w_out, v_ln_mix_g, v_ln_mix_b, v_mlp_w1, v_mlp_w2, v_ln_mlp_g, v_ln_mlp_b):
    w_sharded = dict(ev_w_in=ev_w_in, ev_w_out=ev_w_out, od_w_in=od_w_in, od_w_out=od_w_out, mlp_w1=mlp_w1, mlp_w2=mlp_w2)
    m_sharded = dict(ev_w_in=m_ev_w_in, ev_w_out=m_ev_w_out, od_w_in=m_od_w_in, od_w_out=m_od_w_out, mlp_w1=m_mlp_w1, mlp_w2=m_mlp_w2)
    v_sharded = dict(ev_w_in=v_ev_w_in, ev_w_out=v_ev_w_out, od_w_in=v_od_w_in, od_w_out=v_od_w_out, mlp_w1=v_mlp_w1, mlp_w2=v_mlp_w2)
    small_names = ["ev_conv_w", "ev_conv_b", "ev_conv_ln_g", "ev_conv_ln_b", "hgrn_lb_logits", "ev_gnorm_g", "fox_b_f",
                   "rel_bias", "ln_mix_g", "ln_mix_b", "ln_mlp_g", "ln_mlp_b"]
    w_small = dict(ev_conv_w=ev_conv_w, ev_conv_b=ev_conv_b, ev_conv_ln_g=ev_conv_ln_g, ev_conv_ln_b=ev_conv_ln_b,
                   hgrn_lb_logits=hgrn_lb_logits, ev_gnorm_g=ev_gnorm_g, fox_b_f=fox_b_f, rel_bias=rel_bias,
                   ln_mix_g=ln_mix_g, ln_mix_b=ln_mix_b, ln_mlp_g=ln_mlp_g, ln_mlp_b=ln_mlp_b)
    m_small = dict(ev_conv_w=m_ev_conv_w, ev_conv_b=m_ev_conv_b, ev_conv_ln_g=m_ev_conv_ln_g, ev_conv_ln_b=m_ev_conv_ln_b,
                   hgrn_lb_logits=m_hgrn_lb_logits, ev_gnorm_g=m_ev_gnorm_g, fox_b_f=m_fox_b_f, rel_bias=m_rel_bias,
                   ln_mix_g=m_ln_mix_g, ln_mix_b=m_ln_mix_b, ln_mlp_g=m_ln_mlp_g, ln_mlp_b=m_ln_mlp_b)
    v_small = dict(ev_conv_w=v_ev_conv_w, ev_conv_b=v_ev_conv_b, ev_conv_ln_g=v_ev_conv_ln_g, ev_conv_ln_b=v_ev_conv_ln_b,
                   hgrn_lb_logits=v_hgrn_lb_logits, ev_gnorm_g=v_ev_gnorm_g, fox_b_f=v_fox_b_f, rel_bias=v_rel_bias,
                   ln_mix_g=v_ln_mix_g, ln_mix_b=v_ln_mix_b, ln_mlp_g=v_ln_mlp_g, ln_mlp_b=v_ln_mlp_b)

    chip = 2 * lax.axis_index("x") + lax.axis_index("y")
    hw = HEADS * DH

    w_ev_in = _all_gather_chips(ev_w_in[0].astype(BF16), "ag_ev_w_in")
    shards = dict(ev_w_out=ev_w_out[0].astype(BF16), od_w_in=od_w_in[0].astype(BF16), od_w_out=od_w_out[0].astype(BF16),
                  mlp_w1=[mlp_w1[l].astype(BF16) for l in range(DEPTH)],
                  mlp_w2=[mlp_w2[l].astype(BF16) for l in range(DEPTH)])
    tables = _all_gather_chips(jnp.concatenate(
        [ev_conv_w[0].reshape(1, -1), jnp.pad(rel_bias[0].reshape(1, -1), ((0, 0), (0, (-rel_bias[0].size) % 128)))],
        axis=1), "ag_tables")
    ncw = ev_conv_w[0].size
    cshard = ev_conv_w.shape[2]
    conv_w = jnp.transpose(tables[:, 0, :ncw].reshape(4, CONV_WIDTH, cshard), (1, 0, 2)).reshape(CONV_WIDTH, 4 * cshard)
    rshard = rel_bias.shape[2]
    rel_full = jnp.transpose(tables[:, 0, ncw:ncw + HEADS * rshard].reshape(4, HEADS, rshard), (1, 0, 2)).reshape(HEADS, 4 * rshard)

    loss_part, grad_x, g, small_partial = _local_step(
        x[0], loss_target[0], w_ev_in, shards, conv_w, rel_full, ev_conv_b, ev_conv_ln_g,
        ev_conv_ln_b, hgrn_lb_logits, ev_gnorm_g, fox_b_f, ln_mix_g, ln_mix_b, ln_mlp_g, ln_mlp_b)
    loss = lax.psum(loss_part, ("x", "y", "c"))

    full_shapes = [tuple(small_partial[n].shape) for n in small_names]
    reduced = _unpack(_all_reduce_small(_pack([small_partial[n] for n in small_names]), "ar_small"), full_shapes)
    g_small = {}
    for n, val in zip(small_names, reduced):
        if n == "ev_conv_w":
            val = lax.dynamic_slice_in_dim(val, chip * cshard, cshard, axis=1)
        elif n == "rel_bias":
            val = lax.dynamic_slice_in_dim(val, chip * rshard, rshard, axis=1)
        g_small[n] = val.reshape(w_small[n].shape)
    shard_shapes = [tuple(w_small[n].shape) for n in small_names]
    packed = [_pack([d[n] for n in small_names]) for d in (w_small, g_small, m_small, v_small)]
    sg_, sd_, sm_, sv_ = _adamw(packed[0], packed[1], None, packed[2], packed[3], "adamw_small")
    out_small = {k: dict(zip(small_names, _unpack(val, shard_shapes)))
                 for k, val in (("g", sg_), ("d", sd_), ("m", sm_), ("v", sv_))}

    out_big = {"g": {}, "d": {}, "m": {}, "v": {}}

    for name in ("ev_w_in", "ev_w_out", "od_w_in", "od_w_out", "mlp_w1", "mlp_w2"):
        res = {"g": [], "d": [], "m": [], "v": []}
        for l, slots in enumerate(g[name]):
            mine = _sum_slots(slots, "rs_%s_%d_sum" % (name, l))
            other = _core_swap(mine, "rs_%s_%d_swap" % (name, l))
            outs = _adamw(w_sharded[name][l], mine, other, m_sharded[name][l], v_sharded[name][l],
                          "adamw_%s_%d" % (name, l))
            for key, val in zip(("g", "d", "m", "v"), outs):
                res[key].append(val)
        for key in res:
            out_big[key][name] = jnp.stack(res[key])

    order = ["ev_w_in", "ev_conv_w", "ev_conv_b", "ev_conv_ln_g", "ev_conv_ln_b", "hgrn_lb_logits", "ev_gnorm_g", "ev_w_out",
             "od_w_in", "fox_b_f", "rel_bias", "od_w_out", "ln_mix_g", "ln_mix_b", "mlp_w1", "mlp_w2", "ln_mlp_g", "ln_mlp_b"]

    def pick(key, n):
        return out_big[key][n] if n in out_big[key] else out_small[key][n]

    outs = [loss, grad_x[None]]
    for key in ("g", "d", "m", "v"):
        outs.extend(pick(key, n) for n in order)
    return tuple(outs)


def _local_step(xin, tgt, w_ev_in, shards, conv_w, rel_full, ev_conv_b, ev_conv_ln_g,
                ev_conv_ln_b, hgrn_lb_logits, ev_gnorm_g, fox_b_f, ln_mix_g, ln_mix_b, ln_mlp_g, ln_mlp_b):
    hw = HEADS * DH
    bf_pad = jnp.pad(fox_b_f[0], (0, 128 - HEADS)).reshape(1, 128)
    lb0 = _lower_bound(hgrn_lb_logits)
    ca_bias = _ca_bias(rel_full)
    ag = lambda w: ("ag", w)
    a2a = lambda g4: ("a2a", g4)

    xin_b = xin.astype(BF16)
    u0, (w_ev_out4, w1_0) = _matmul(xin_b, w_ev_in, b_sharded=True, comm=(ag(shards["ev_w_out"]), ag(shards["mlp_w1"][0])),
                                    name="l0_in")
    w_ev_out = _unshard_rows(w_ev_out4)
    a_out = _conv_fwd(u0, conv_w, ev_conv_b[0], ev_conv_ln_g[0], ev_conv_ln_b[0], "l0_conv")
    o_raw, b_out, states = _hgrn_fwd(u0, lb0, ev_gnorm_g[0], "l0_hgrn")
    cat0 = jnp.concatenate([a_out, b_out], axis=1).astype(BF16)
    mix0 = _matmul(cat0, w_ev_out, name="l0_out")
    r0a, x0a, x0a_b = _ln_fwd(xin, mix0, ln_mix_g[0], ln_mix_b[0], "l0_ln_mix")
    (z0, h0), (w2_0,) = _matmul(x0a_b, w1_0, b_sharded=True, out_dtype=BF16, epi="relu2",
                                comm=(ag(shards["mlp_w2"][0]),), name="l0_mlp1")
    w2 = [_unshard_rows(w2_0), None]
    f0, (w_od_in4, w_od_out4) = _matmul(h0, w2[0], comm=(ag(shards["od_w_in"]), ag(shards["od_w_out"])), name="l0_mlp2")
    w_od_in = _unshard_cols(w_od_in4)
    w_od_out = _unshard_rows(w_od_out4)
    w_od_main = jnp.concatenate([w_od_in[:, :3 * hw], w_od_in[:, 3 * hw + HEADS:]], axis=1)
    w_od_f = jnp.pad(w_od_in[:, 3 * hw:3 * hw + HEADS], ((0, 0), (0, 128 - HEADS)))
    r0b, x1, x1_b = _ln_fwd(x0a, f0, ln_mlp_g[0], ln_mlp_b[0], "l0_ln_mlp")
    u1, (w1_1,) = _matmul(x1_b, w_od_main, comm=(ag(shards["mlp_w1"][1]),), name="l1_in")
    w1 = [w1_0, w1_1]
    cf = _matmul(x1_b, w_od_f, name="l1_in_f")
    fcum = _fgate_fwd(cf, bf_pad, "l1_fgate")
    f_col = jnp.transpose(fcum[:, :HEADS])[:, :, None]
    f_row = jnp.transpose(fcum[:, :HEADS])[:, None, :]
    c_out, lse, lse_row = _fox_fwd(u1, f_col, f_row, "l1_fox")
    d_out = _ca_fwd(u1, ca_bias, "l1_ca")
    cat1 = jnp.concatenate([c_out, d_out], axis=1).astype(BF16)
    mix1 = _matmul(cat1, w_od_out, name="l1_out")
    r1a, x1a, x1a_b = _ln_fwd(x1, mix1, ln_mix_g[1], ln_mix_b[1], "l1_ln_mix")
    (z1, h1), (w2_1,) = _matmul(x1a_b, w1[1], b_sharded=True, out_dtype=BF16, epi="relu2",
                                comm=(ag(shards["mlp_w2"][1]),), name="l1_mlp1")
    w2[1] = _unshard_rows(w2_1)
    f1 = _matmul(h1, w2[1], name="l1_mlp2")
    r1b, x2, _ = _ln_fwd(x1a, f1, ln_mlp_g[1], ln_mlp_b[1], "l1_ln_mlp")
    dy, loss_part = _loss_head(x2, tgt, "loss")

    g = {}
    dr, drb, dg_, db_ = _ln_bwd(dy, r1b, ln_mlp_g[1], "l1_ln_mlp_bwd")
    g_ln_mlp = [None, (dg_, db_)]
    gw = _shard_rows(_matmul(h1, drb, ta=True, out_dtype=BF16, name="l1_dw2"))
    dz, (gw2_1,) = _matmul(drb, w2[1], tb=True, out_dtype=BF16, epi="drelu2", extra=z1, comm=(a2a(gw),), name="l1_dz")
    gw = _matmul(x1a_b, dz, ta=True, out_dtype=BF16, out_sharded=True, name="l1_dw1")
    dx, (gw1_1,) = _matmul(dz, w1[1], tb=True, b_sharded=True, epi="add", extra=dr, comm=(a2a(gw),), name="l1_dx_mlp")
    dr, drb, dg_, db_ = _ln_bwd(dx, r1a, ln_mix_g[1], "l1_ln_mix_bwd")
    g_ln_mix = [None, (dg_, db_)]
    gw = _shard_rows(_matmul(cat1, drb, ta=True, out_dtype=BF16, name="l1_dwout"))
    dcat, g["od_w_out"] = _matmul(drb, w_od_out, tb=True, comm=(a2a(gw),), name="l1_dcat")
    dk_c, dv_c, dfk = _fox_bwd_kv(u1, c_out, lse_row, dcat, f_col, f_row, "l1_fox_bwd_kv")
    dq_c, dfq = _fox_bwd_q(u1, c_out, lse, dcat, f_col, f_row, "l1_fox_bwd_q")
    d_f = jnp.pad(jnp.transpose(dfk[:, :, 0] + dfq[:, :, 0]), ((0, 0), (0, 128 - HEADS)))
    dcf, dbf = _fgate_bwd(d_f, cf, bf_pad, "l1_fgate_bwd")
    dq_d, dka, dkb, dva, dvb, dbias = _ca_bwd(u1, ca_bias, dcat, "l1_ca_bwd")
    du1 = jnp.concatenate([dq_c, dk_c, dv_c, dq_d, _ca_merge(dka, dkb), _ca_merge(dva, dvb)], axis=1)
    g_main = _matmul(x1_b, du1, ta=True, out_dtype=BF16, name="l1_dwin")
    g_f = _matmul(x1_b, dcf, ta=True, out_dtype=BF16, name="l1_dwin_f")
    gw = _shard_cols(jnp.concatenate([g_main[:, :3 * hw], g_f[:, :HEADS], g_main[:, 3 * hw:]], axis=1))
    dx_f = _matmul(dcf, w_od_f, tb=True, epi="add", extra=dr, name="l1_dx_f")
    dx, g["od_w_in"] = _matmul(du1, w_od_main, tb=True, epi="add", extra=dx_f, scale=1.0, comm=(a2a(gw),),
                               name="l1_dx_in")
    dr, drb, dg_, db_ = _ln_bwd(dx, r0b, ln_mlp_g[0], "l0_ln_mlp_bwd")
    g_ln_mlp[0] = (dg_, db_)
    gw = _shard_rows(_matmul(h0, drb, ta=True, out_dtype=BF16, name="l0_dw2"))
    dz, (gw2_0,) = _matmul(drb, w2[0], tb=True, out_dtype=BF16, epi="drelu2", extra=z0, comm=(a2a(gw),), name="l0_dz")
    gw = _matmul(x0a_b, dz, ta=True, out_dtype=BF16, out_sharded=True, name="l0_dw1")
    dx, (gw1_0,) = _matmul(dz, w1[0], tb=True, b_sharded=True, epi="add", extra=dr, comm=(a2a(gw),), name="l0_dx_mlp")
    dr, drb, dg_, db_ = _ln_bwd(dx, r0a, ln_mix_g[0], "l0_ln_mix_bwd")
    g_ln_mix[0] = (dg_, db_)
    gw = _shard_rows(_matmul(cat0, drb, ta=True, out_dtype=BF16, name="l0_dwout"))
    dcat, g["ev_w_out"] = _matmul(drb, w_ev_out, tb=True, comm=(a2a(gw),), name="l0_dcat")
    dc, g_conv_w, g_conv_b, g_conv_lg, g_conv_lb = _conv_bwd_params(
        u0, dcat, conv_w, ev_conv_b[0], ev_conv_ln_g[0], ev_conv_ln_b[0], "l0_conv_bwd_p")
    da, dgate = _conv_bwd_input(u0, dc, conv_w, "l0_conv_bwd_i")
    dhq, dhf, dhi, dhg, g_lb0, g_gnorm = _hgrn_bwd(u0, o_raw, states, dcat, lb0, ev_gnorm_g[0], "l0_hgrn_bwd")
    du0 = jnp.concatenate([da, dgate, dhq, dhf, dhi, dhg], axis=1)
    gw = _matmul(xin_b, du0, ta=True, out_dtype=BF16, out_sharded=True, name="l0_dwin")
    grad_x, g["ev_w_in"] = _matmul(du0, w_ev_in, tb=True, b_sharded=True, epi="add", extra=dr, comm=(a2a(gw),),
                                   name="l0_dx_in")
    g["mlp_w1"] = [gw1_0, gw1_1]
    g["mlp_w2"] = [gw2_0, gw2_1]

    g_lb_logits = jax.vjp(_lower_bound, hgrn_lb_logits)[1](g_lb0)[0]
    g_rel = _ca_bias_grad(dbias)
    small_partial = dict(
        ev_conv_w=g_conv_w, ev_conv_b=g_conv_b, ev_conv_ln_g=g_conv_lg, ev_conv_ln_b=g_conv_lb,
        hgrn_lb_logits=g_lb_logits, ev_gnorm_g=g_gnorm, fox_b_f=dbf[:HEADS], rel_bias=g_rel,
        ln_mix_g=jnp.stack([g_ln_mix[0][0], g_ln_mix[1][0]]), ln_mix_b=jnp.stack([g_ln_mix[0][1], g_ln_mix[1][1]]),
        ln_mlp_g=jnp.stack([g_ln_mlp[0][0], g_ln_mlp[1][0]]), ln_mlp_b=jnp.stack([g_ln_mlp[0][1], g_ln_mlp[1][1]]))
    return loss_part, grad_x, g, small_partial
```

```python
import functools

import jax
import jax.numpy as jnp
from jax import lax
from jax.experimental import pallas as pl
from jax.experimental.pallas import tpu as pltpu

F32 = jnp.float32
BF16 = jnp.bfloat16
MESH = pl.DeviceIdType.MESH

DEPTH = 2
ALPHA = (2 * DEPTH) ** 0.25
LN_EPS = 1e-5
HEADS = 8
DH = 128
CHUNK = 64
CONV_WIDTH = 31
HALO = 32
CA_LEFT = 8
CA_BLK = CA_LEFT * CHUNK
CA_BAND = (CA_LEFT + 1) * CHUNK
CA_TILE = 4 * CHUNK
CA_WIN = CA_TILE + CA_LEFT * CHUNK
REL_CLIP = 256
REL_TABLE = (CHUNK - 1) + REL_CLIP + 1
NEG = -1e30

ADAM_LR = 0.001
ADAM_B1 = 0.9
ADAM_B2 = 0.999
ADAM_EPS = 1e-08
ADAM_WD = 0.01
ADAM_STEP = 10

VMEM_LIMIT = 48 * 1024 * 1024


def _cparams(sem):
    return pltpu.CompilerParams(dimension_semantics=sem, vmem_limit_bytes=VMEM_LIMIT)


def _pick(n, cands):
    for c in cands:
        if n % c == 0:
            return c
    return n


def _sigmoid(x):
    return 1.0 / (1.0 + jnp.exp(-x))


def _dot(a, b, dims):
    return lax.dot_general(a.astype(BF16), b.astype(BF16), (dims, ((), ())), preferred_element_type=F32)


def _nn(a, b):
    return _dot(a, b, ((1,), (0,)))


def _nt(a, b):
    return _dot(a, b, ((1,), (1,)))


def _tn(a, b):
    return _dot(a, b, ((0,), (0,)))


def _dot3(a, b, dims):
    a_hi = a.astype(BF16)
    b_hi = b.astype(BF16)
    a_lo = (a - a_hi.astype(F32)).astype(BF16)
    b_lo = (b - b_hi.astype(F32)).astype(BF16)
    dn = (dims, ((), ()))
    return (lax.dot_general(a_hi, b_hi, dn, preferred_element_type=F32)
            + (lax.dot_general(a_hi, b_lo, dn, preferred_element_type=F32)
               + lax.dot_general(a_lo, b_hi, dn, preferred_element_type=F32)))


def _nn3(a, b):
    return _dot3(a, b, ((1,), (0,)))


def _nt3(a, b):
    return _dot3(a, b, ((1,), (1,)))


def _tn3(a, b):
    return _dot3(a, b, ((0,), (0,)))


def _split3(x):
    hi = x.astype(BF16)
    r1 = x - hi.astype(F32)
    mid = r1.astype(BF16)
    lo = (r1 - mid.astype(F32)).astype(BF16)
    return hi, mid, lo


def _tri_matmul(tri, x, terms):
    parts = _split3(x)[:terms]
    acc = None
    for p in parts:
        t = lax.dot_general(tri, p, (((1,), (0,)), ((), ())), preferred_element_type=F32)
        acc = t if acc is None else acc + t
    return acc


def _tril(n, upper=False):
    r = lax.broadcasted_iota(jnp.int32, (n, n), 0)
    c = lax.broadcasted_iota(jnp.int32, (n, n), 1)
    m = (c >= r) if upper else (c <= r)
    return jnp.where(m, 1.0, 0.0).astype(BF16)


def _matmul(a, b, *, ta=False, tb=False, out_dtype=F32, epi=None, extra=None, scale=ALPHA, b_sharded=False,
            out_sharded=False, comm=(), name):
    m = a.shape[1] if ta else a.shape[0]
    kd = a.shape[0] if ta else a.shape[1]
    if b_sharded:
        shard = b.shape[2]
        n = b.shape[1] if tb else 4 * shard
        assert (b.shape[1] if not tb else 4 * shard) == kd
    else:
        n = b.shape[0] if tb else b.shape[1]
    bm = _pick(m, (1024, 512, 256, 128))
    bn = _pick(shard if (b_sharded and not tb) else (n // 4 if out_sharded else n), (1024, 768, 512, 256, 128))
    bk = _pick(shard if (b_sharded and tb) else kd, (2048, 1536, 1024, 768, 512, 256, 128))
    ni, nj, nk = m // bm, n // bn, kd // bk
    n_out = 2 if epi == "relu2" else 1
    n_comm = len(comm)
    kinds = [c[0] for c in comm]

    def body(*refs):
        a_ref, b_ref = refs[0], refs[1]
        pos = 2
        e_ref = refs[pos] if extra is not None else None
        pos += extra is not None
        c_in = refs[pos:pos + n_comm]
        pos += n_comm
        outs = refs[pos:pos + n_out]
        pos += n_out
        c_out = refs[pos:pos + n_comm]
        pos += n_comm
        acc_ref = refs[pos]
        sems = refs[pos + 1:]
        i, j, k = pl.program_id(0), pl.program_id(1), pl.program_id(2)

        if n_comm:
            @pl.when(jnp.logical_and(jnp.logical_and(i == 0, j == 0), k == 0))
            def _():
                _exchange(kinds, c_in, c_out, *sems, start=True)

        dims = ((0 if ta else 1,), (1 if tb else 0,))
        part = _dot(a_ref[...], b_ref[...], dims)
        if nk > 1:
            @pl.when(k == 0)
            def _():
                acc_ref[...] = part

            @pl.when(jnp.logical_and(k > 0, k < nk - 1))
            def _():
                acc_ref[...] += part

        @pl.when(k == nk - 1)
        def _():
            r = part + acc_ref[...] if nk > 1 else part
            if epi == "relu2":
                outs[0][...] = r.astype(out_dtype)
                outs[1][...] = jnp.square(jnp.maximum(r, 0.0)).astype(out_dtype)
            elif epi == "drelu2":
                outs[0][...] = (r * (2.0 * jnp.maximum(e_ref[...].astype(F32), 0.0))).astype(out_dtype)
            elif epi == "add":
                outs[0][...] = (r + scale * e_ref[...].astype(F32)).astype(out_dtype)
            else:
                outs[0][...] = r.astype(out_dtype)

        if n_comm:
            @pl.when(jnp.logical_and(jnp.logical_and(i == ni - 1, j == nj - 1), k == nk - 1))
            def _():
                _exchange(kinds, c_in, c_out, *sems, start=False)

    a_spec = pl.BlockSpec((bk, bm), lambda i, j, k: (k, i)) if ta else pl.BlockSpec((bm, bk), lambda i, j, k: (i, k))
    if b_sharded and tb:
        per = shard // bk
        b_spec = pl.BlockSpec((None, bn, bk), lambda i, j, k: (k // per, j, k % per))
    elif b_sharded:
        per = shard // bn
        b_spec = pl.BlockSpec((None, bk, bn), lambda i, j, k: (j // per, k, j % per))
    elif tb:
        b_spec = pl.BlockSpec((bn, bk), lambda i, j, k: (j, k))
    else:
        b_spec = pl.BlockSpec((bk, bn), lambda i, j, k: (k, j))
    e_spec = pl.BlockSpec((bm, bn), lambda i, j, k: (i, j))
    if out_sharded:
        per_o = (n // 4) // bn
        o_spec = pl.BlockSpec((None, bm, bn), lambda i, j, k: (j // per_o, i, j % per_o))
        o_shape = jax.ShapeDtypeStruct((4, m, n // 4), out_dtype)
    else:
        o_spec = e_spec
        o_shape = jax.ShapeDtypeStruct((m, n), out_dtype)
    in_specs = [a_spec, b_spec] + ([e_spec] if extra is not None else []) + [_ANY] * n_comm
    args = (a, b) + ((extra,) if extra is not None else ()) + tuple(c[1] for c in comm)
    c_shapes = [jax.ShapeDtypeStruct((4,) + c[1].shape if c[0] == "ag" else c[1].shape, c[1].dtype) for c in comm]
    scratch = [pltpu.VMEM((bm, bn) if nk > 1 else (8, 128), F32)]
    if n_comm:
        scratch += [pltpu.SemaphoreType.DMA((3 * n_comm,)), pltpu.SemaphoreType.DMA((3 * n_comm,)),
                    pltpu.SemaphoreType.DMA((n_comm,))]
    sem = ("arbitrary",) * 3 if n_comm else ("parallel", "parallel", "arbitrary")
    res = pl.pallas_call(
        body, grid=(ni, nj, nk), in_specs=in_specs,
        out_specs=[o_spec] * n_out + [_ANY] * n_comm, out_shape=[o_shape] * n_out + c_shapes,
        scratch_shapes=scratch, compiler_params=_cparams(sem), name=name)(*args)
    main = tuple(res[:n_out]) if n_out == 2 else res[0]
    return (main, list(res[n_out:])) if n_comm else main


def _ln_fwd(x, mix, g, b, name):
    s, d = x.shape
    br = _pick(s, (256, 128, 64, 8))

    def body(x_ref, m_ref, g_ref, b_ref, r_ref, y_ref, yb_ref):
        r = ALPHA * x_ref[...] + m_ref[...]
        mu = jnp.mean(r, axis=-1, keepdims=True)
        dlt = r - mu
        var = jnp.mean(dlt * dlt, axis=-1, keepdims=True)
        y = dlt * lax.rsqrt(var + LN_EPS) * g_ref[...] + b_ref[...]
        r_ref[...] = r
        y_ref[...] = y
        yb_ref[...] = y.astype(BF16)

    row = pl.BlockSpec((br, d), lambda i: (i, 0))
    vec = pl.BlockSpec((1, d), lambda i: (0, 0))
    return pl.pallas_call(
        body, grid=(s // br,), in_specs=[row, row, vec, vec], out_specs=[row, row, row],
        out_shape=[jax.ShapeDtypeStruct((s, d), F32), jax.ShapeDtypeStruct((s, d), F32),
                   jax.ShapeDtypeStruct((s, d), BF16)],
        compiler_params=_cparams(("parallel",)), name=name)(x, mix, g.reshape(1, d), b.reshape(1, d))


def _ln_bwd(dy, r, g, name):
    s, d = r.shape
    br = _pick(s, (256, 128, 64, 8))

    def body(dy_ref, r_ref, g_ref, dr_ref, drb_ref, dg_ref, db_ref):
        @pl.when(pl.program_id(0) == 0)
        def _():
            dg_ref[...] = jnp.zeros_like(dg_ref)
            db_ref[...] = jnp.zeros_like(db_ref)

        rv = r_ref[...]
        dyv = dy_ref[...]
        mu = jnp.mean(rv, axis=-1, keepdims=True)
        dlt = rv - mu
        var = jnp.mean(dlt * dlt, axis=-1, keepdims=True)
        rstd = lax.rsqrt(var + LN_EPS)
        xhat = dlt * rstd
        dxh = dyv * g_ref[...]
        m1 = jnp.mean(dxh, axis=-1, keepdims=True)
        m2 = jnp.mean(dxh * xhat, axis=-1, keepdims=True)
        dr = rstd * (dxh - m1 - xhat * m2)
        dr_ref[...] = dr
        drb_ref[...] = dr.astype(BF16)
        dg_ref[...] += jnp.sum(dyv * xhat, axis=0, keepdims=True)
        db_ref[...] += jnp.sum(dyv, axis=0, keepdims=True)

    row = pl.BlockSpec((br, d), lambda i: (i, 0))
    vec = pl.BlockSpec((1, d), lambda i: (0, 0))
    dr, drb, dg, db = pl.pallas_call(
        body, grid=(s // br,), in_specs=[row, row, vec], out_specs=[row, row, vec, vec],
        out_shape=[jax.ShapeDtypeStruct((s, d), F32), jax.ShapeDtypeStruct((s, d), BF16),
                   jax.ShapeDtypeStruct((1, d), F32), jax.ShapeDtypeStruct((1, d), F32)],
        compiler_params=_cparams(("arbitrary",)), name=name)(dy, r, g.reshape(1, d))
    return dr, drb, dg[0], db[0]


def _loss_head(y, tgt, name):
    s, d = y.shape
    br = _pick(s, (256, 128, 64, 8))

    def body(y_ref, t_ref, dy_ref, l_ref):
        @pl.when(pl.program_id(0) == 0)
        def _():
            l_ref[...] = jnp.zeros_like(l_ref)

        e = y_ref[...] - t_ref[...]
        dy_ref[...] = e * (1.0 / d)
        rows = jnp.sum(e * e, axis=-1, keepdims=True) * (0.5 / d)
        l_ref[...] += jnp.sum(rows, axis=0, keepdims=True)

    row = pl.BlockSpec((br, d), lambda i: (i, 0))
    dy, l = pl.pallas_call(
        body, grid=(s // br,), in_specs=[row, row],
        out_specs=[row, pl.BlockSpec((1, 1), lambda i: (0, 0))],
        out_shape=[jax.ShapeDtypeStruct((s, d), F32), jax.ShapeDtypeStruct((1, 1), F32)],
        compiler_params=_cparams(("arbitrary",)), name=name)(y, tgt)
    return dy, l[0, 0]


SUBLANES = 8


def _shifted_rows(ext_ref, sh_ref, tt):
    n = tt + HALO - SUBLANES
    for r in range(1, SUBLANES):
        sh_ref[r - 1] = ext_ref[pl.ds(r, n), :]

    def tap(o):
        r = o % SUBLANES
        if r == 0:
            return ext_ref[pl.ds(o, tt), :]
        return sh_ref[r - 1, pl.ds(o - r, tt), :]

    return tap


def _conv_scratch(tt, cc):
    return [pltpu.VMEM((tt + HALO, cc), F32), pltpu.VMEM((SUBLANES - 1, tt + HALO - SUBLANES, cc), F32)]


def _conv_recompute(i, a_ref, gt_ref, ah_ref, gh_ref, w_ref, cb_ref, hext_ref, sh_ref, tt):
    h = a_ref[...] * _sigmoid(gt_ref[...])
    hh = ah_ref[...] * _sigmoid(gh_ref[...])
    hh = jnp.where(i > 0, hh, 0.0)
    hext_ref[0:HALO, :] = hh
    hext_ref[HALO:HALO + tt, :] = h
    tap = _shifted_rows(hext_ref, sh_ref, tt)
    acc = jnp.zeros_like(h) + cb_ref[...]
    off = HALO - (CONV_WIDTH - 1)
    for j in range(CONV_WIDTH):
        acc = acc + w_ref[j:j + 1, :] * tap(off + j)
    mu = jnp.mean(acc, axis=-1, keepdims=True)
    dlt = acc - mu
    var = jnp.mean(dlt * dlt, axis=-1, keepdims=True)
    rstd = lax.rsqrt(var + LN_EPS)
    return dlt * rstd, rstd, tap


def _conv_specs(tt, cc, s):
    nh = tt // HALO
    cur = lambda cb: pl.BlockSpec((tt, cc), lambda i: (i, cb))
    prev = lambda cb: pl.BlockSpec((HALO, cc), lambda i: (jnp.maximum(i * nh - 1, 0), cb))
    vec = pl.BlockSpec((1, cc), lambda i: (0, 0))
    wsp = pl.BlockSpec((HALO, cc), lambda i: (0, 0))
    return cur, prev, vec, wsp


def _pad_conv_w(w):
    return jnp.concatenate([w, jnp.zeros((HALO - CONV_WIDTH, w.shape[1]), F32)], axis=0)


def _conv_fwd(u, w, cb, lg, lb, name):
    s = u.shape[0]
    cc = w.shape[1]
    tt = _pick(s, (256, 128, 64))
    cur, prev, vec, wsp = _conv_specs(tt, cc, s)

    def body(a_ref, gt_ref, ah_ref, gh_ref, w_ref, cb_ref, lg_ref, lb_ref, o_ref, hext_ref, sh_ref):
        xhat, _, _ = _conv_recompute(pl.program_id(0), a_ref, gt_ref, ah_ref, gh_ref, w_ref, cb_ref, hext_ref,
                                     sh_ref, tt)
        nrm = xhat * lg_ref[...] + lb_ref[...]
        o_ref[...] = nrm * _sigmoid(nrm)

    return pl.pallas_call(
        body, grid=(s // tt,), in_specs=[cur(0), cur(1), prev(0), prev(1), wsp, vec, vec, vec],
        out_specs=pl.BlockSpec((tt, cc), lambda i: (i, 0)), out_shape=jax.ShapeDtypeStruct((s, cc), F32),
        scratch_shapes=_conv_scratch(tt, cc),
        compiler_params=_cparams(("parallel",)), name=name)(
            u, u, u, u, _pad_conv_w(w), cb.reshape(1, cc), lg.reshape(1, cc), lb.reshape(1, cc))


def _conv_bwd_params(u, dout, w, cb, lg, lb, name):
    s = u.shape[0]
    cc = w.shape[1]
    tt = _pick(s, (256, 128, 64))
    cur, prev, vec, wsp = _conv_specs(tt, cc, s)

    def body(a_ref, gt_ref, ah_ref, gh_ref, w_ref, cb_ref, lg_ref, lb_ref, do_ref,
             dc_ref, dw_ref, dcb_ref, dlg_ref, dlb_ref, hext_ref, sh_ref, dw_acc):
        i = pl.program_id(0)

        @pl.when(i == 0)
        def _():
            dw_acc[...] = jnp.zeros_like(dw_acc)
            dcb_ref[...] = jnp.zeros_like(dcb_ref)
            dlg_ref[...] = jnp.zeros_like(dlg_ref)
            dlb_ref[...] = jnp.zeros_like(dlb_ref)

        xhat, rstd, tap = _conv_recompute(i, a_ref, gt_ref, ah_ref, gh_ref, w_ref, cb_ref, hext_ref, sh_ref, tt)
        nrm = xhat * lg_ref[...] + lb_ref[...]
        sg = _sigmoid(nrm)
        dn = do_ref[...] * (sg * (1.0 + nrm * (1.0 - sg)))
        dxh = dn * lg_ref[...]
        m1 = jnp.mean(dxh, axis=-1, keepdims=True)
        m2 = jnp.mean(dxh * xhat, axis=-1, keepdims=True)
        dc = rstd * (dxh - m1 - xhat * m2)
        dc_ref[...] = dc
        dlg_ref[...] += jnp.sum(dn * xhat, axis=0, keepdims=True)
        dlb_ref[...] += jnp.sum(dn, axis=0, keepdims=True)
        dcb_ref[...] += jnp.sum(dc, axis=0, keepdims=True)
        off = HALO - (CONV_WIDTH - 1)
        for j in range(CONV_WIDTH):
            dw_acc[j] += jnp.sum((dc * tap(off + j)).reshape(tt // SUBLANES, SUBLANES, cc), axis=0)

        @pl.when(i == nblk - 1)
        def _():
            dw_ref[...] = jnp.zeros_like(dw_ref)
            for j in range(CONV_WIDTH):
                dw_ref[j:j + 1, :] = jnp.sum(dw_acc[j], axis=0, keepdims=True)

    nblk = s // tt
    dcol = pl.BlockSpec((tt, cc), lambda i: (i, 0))
    dc, dw, dcb, dlg, dlb = pl.pallas_call(
        body, grid=(nblk,), in_specs=[cur(0), cur(1), prev(0), prev(1), wsp, vec, vec, vec, dcol],
        out_specs=[dcol, wsp, vec, vec, vec],
        out_shape=[jax.ShapeDtypeStruct((s, cc), F32), jax.ShapeDtypeStruct((HALO, cc), F32)]
        + [jax.ShapeDtypeStruct((1, cc), F32)] * 3,
        scratch_shapes=_conv_scratch(tt, cc) + [pltpu.VMEM((HALO, SUBLANES, cc), F32)],
        compiler_params=_cparams(("arbitrary",)), name=name)(
            u, u, u, u, _pad_conv_w(w), cb.reshape(1, cc), lg.reshape(1, cc), lb.reshape(1, cc), dout)
    return dc, dw[:CONV_WIDTH], dcb[0], dlg[0], dlb[0]


def _conv_bwd_input(u, dc, w, name):
    s = u.shape[0]
    cc = w.shape[1]
    tt = _pick(s, (256, 128, 64))
    nh = tt // HALO
    nlast = s // HALO - 1
    cur = lambda cb: pl.BlockSpec((tt, cc), lambda i: (i, cb))
    nxt = pl.BlockSpec((HALO, cc), lambda i: (jnp.minimum((i + 1) * nh, nlast), 0))
    wsp = pl.BlockSpec((HALO, cc), lambda i: (0, 0))
    nblk = s // tt

    def body(a_ref, gt_ref, dc_ref, dn_ref, w_ref, da_ref, dg_ref, ext_ref, sh_ref):
        i = pl.program_id(0)
        ext_ref[0:tt, :] = dc_ref[...]
        ext_ref[tt:tt + HALO, :] = jnp.where(i < nblk - 1, dn_ref[...], 0.0)
        tap = _shifted_rows(ext_ref, sh_ref, tt)
        dh = jnp.zeros((tt, cc), F32)
        for j in range(CONV_WIDTH):
            dh = dh + w_ref[j:j + 1, :] * tap(CONV_WIDTH - 1 - j)
        a = a_ref[...]
        sg = _sigmoid(gt_ref[...])
        da_ref[...] = (dh * sg).astype(BF16)
        dg_ref[...] = (dh * a * sg * (1.0 - sg)).astype(BF16)

    ocol = pl.BlockSpec((tt, cc), lambda i: (i, 0))
    return pl.pallas_call(
        body, grid=(nblk,), in_specs=[cur(0), cur(1), ocol, nxt, wsp], out_specs=[ocol, ocol],
        out_shape=[jax.ShapeDtypeStruct((s, cc), BF16)] * 2,
        scratch_shapes=_conv_scratch(tt, cc),
        compiler_params=_cparams(("parallel",)), name=name)(u, u, dc, dc, _pad_conv_w(w))


def _hgrn_gates(hq, hf, lb):
    sg = _sigmoid(hf)
    f = lb + (1.0 - lb) * sg
    lf = jnp.log(f)
    big_l = _tri_matmul(_tril(CHUNK), lf, 3)
    l_end = jnp.sum(lf, axis=0, keepdims=True)
    l_mid = jnp.sum(lf[0:CHUNK // 2, :], axis=0, keepdims=True)
    sq = _sigmoid(hq)
    q = hq * sq
    return sg, f, 1.0 - f, big_l, l_end, l_mid, sq, q


def _causal_mask(n):
    r = lax.broadcasted_iota(jnp.int32, (n, n), 0)
    c = lax.broadcasted_iota(jnp.int32, (n, n), 1)
    return c <= r


def _hgrn_fwd(u, lb, gg, name):
    s = u.shape[0]
    w = HEADS * DH
    nch = s // CHUNK

    def body(q_ref, f_ref, i_ref, g_ref, lb_ref, gg_ref, o_ref, out_ref, st_ref, state):
        @pl.when(pl.program_id(0) == 0)
        def _():
            state[...] = jnp.zeros_like(state)

        mask = _causal_mask(CHUNK)
        for hd in range(HEADS):
            sl = slice(hd * DH, (hd + 1) * DH)
            _, _, kk, big_l, l_end, l_mid, _, q = _hgrn_gates(q_ref[:, sl], f_ref[:, sl], lb_ref[:, sl])
            v = i_ref[:, sl]
            qs = q * jnp.exp(big_l - l_mid)
            ks = kk * jnp.exp(l_mid - big_l)
            att = jnp.where(mask, _nt3(qs, ks), 0.0)
            st0 = state[hd]
            st_ref[0, hd] = st0
            o = _nn3(att, v) + _nt3(q * jnp.exp(big_l), st0)
            state[hd] = st0 * jnp.exp(l_end) + _tn3(v, kk * jnp.exp(l_end - big_l))
            o_ref[:, sl] = o
            on = o * lax.rsqrt(jnp.mean(o * o, axis=-1, keepdims=True) + LN_EPS)
            gv = g_ref[:, sl]
            out_ref[:, sl] = on * gg_ref[:, sl] * (gv * _sigmoid(gv))

    col = lambda cb: pl.BlockSpec((CHUNK, w), lambda i: (i, cb))
    vec = pl.BlockSpec((1, w), lambda i: (0, 0))
    ocol = pl.BlockSpec((CHUNK, w), lambda i: (i, 0))
    return pl.pallas_call(
        body, grid=(nch,), in_specs=[col(2), col(3), col(4), col(5), vec, vec],
        out_specs=[ocol, ocol, pl.BlockSpec((1, HEADS, DH, DH), lambda i: (i, 0, 0, 0))],
        out_shape=[jax.ShapeDtypeStruct((s, w), F32), jax.ShapeDtypeStruct((s, w), F32),
                   jax.ShapeDtypeStruct((nch, HEADS, DH, DH), F32)],
        scratch_shapes=[pltpu.VMEM((HEADS, DH, DH), F32)],
        compiler_params=_cparams(("arbitrary",)), name=name)(u, u, u, u, lb.reshape(1, w), gg.reshape(1, w))


def _hgrn_bwd(u, o_raw, states, dout, lb, gg, name):
    s = u.shape[0]
    w = HEADS * DH
    nch = s // CHUNK

    def body(q_ref, f_ref, i_ref, g_ref, o_ref, st_ref, do_ref, lb_ref, gg_ref,
             dq_ref, df_ref, di_ref, dg_ref, dlb_ref, dgg_ref, dstate):
        @pl.when(pl.program_id(0) == 0)
        def _():
            dstate[...] = jnp.zeros_like(dstate)
            dlb_ref[...] = jnp.zeros_like(dlb_ref)
            dgg_ref[...] = jnp.zeros_like(dgg_ref)

        mask = _causal_mask(CHUNK)
        last_row = lax.broadcasted_iota(jnp.int32, (CHUNK, DH), 0) == CHUNK - 1
        tri_up = _tril(CHUNK, upper=True)
        for hd in range(HEADS):
            sl = slice(hd * DH, (hd + 1) * DH)
            hq = q_ref[:, sl]
            lbv = lb_ref[:, sl]
            sg, f, kk, big_l, l_end, l_mid, sq, q = _hgrn_gates(hq, f_ref[:, sl], lbv)
            v = i_ref[:, sl]
            e_l = jnp.exp(big_l)
            e_qm = jnp.exp(big_l - l_mid)
            e_km = jnp.exp(l_mid - big_l)
            e_ke = jnp.exp(l_end - big_l)
            e_end = jnp.exp(l_end)
            qs = q * e_qm
            ks = kk * e_km
            qe = q * e_l
            ke = kk * e_ke
            att = jnp.where(mask, _nt3(qs, ks), 0.0)
            st0 = st_ref[0, hd]
            dst1 = dstate[hd]
            o = o_ref[:, sl]
            rinv = lax.rsqrt(jnp.mean(o * o, axis=-1, keepdims=True) + LN_EPS)
            on = o * rinv
            gv = g_ref[:, sl]
            sgg = _sigmoid(gv)
            gsil = gv * sgg
            ggv = gg_ref[:, sl]
            dov = do_ref[:, sl]
            don = dov * ggv * gsil
            dg_ref[:, sl] = (dov * on * ggv * (sgg * (1.0 + gv * (1.0 - sgg)))).astype(BF16)
            dgg_ref[:, sl] += jnp.sum(dov * on * gsil, axis=0, keepdims=True)
            do = rinv * (don - on * jnp.mean(don * on, axis=-1, keepdims=True))
            datt = jnp.where(mask, _nt3(do, v), 0.0)
            dv = _tn3(att, do) + _nt3(ke, dst1)
            dqs = _nn3(datt, ks)
            dks = _tn3(datt, qs)
            dqe = _nn3(do, st0)
            dke = _nn3(v, dst1)
            dq = dqs * e_qm + dqe * e_l
            dk = dks * e_km + dke * e_ke
            dke_ke = dke * ke
            dl = dqs * qs - dks * ks + dqe * qe - dke_ke
            dl_end = jnp.sum(dke_ke, axis=0, keepdims=True) + jnp.sum(dst1 * st0, axis=0, keepdims=True) * e_end
            dl = dl + jnp.where(last_row, dl_end, 0.0)
            dlf = _tri_matmul(tri_up, dl, 2)
            dfv = dlf / f - dk
            df_ref[:, sl] = (dfv * (1.0 - lbv) * sg * (1.0 - sg)).astype(BF16)
            dlb_ref[:, sl] += jnp.sum(dfv * (1.0 - sg), axis=0, keepdims=True)
            dq_ref[:, sl] = (dq * (sq * (1.0 + hq * (1.0 - sq)))).astype(BF16)
            di_ref[:, sl] = dv.astype(BF16)
            dstate[hd] = dst1 * e_end + _tn3(do, qe)

    rev = lambda i: nch - 1 - i
    col = lambda cb: pl.BlockSpec((CHUNK, w), lambda i: (rev(i), cb))
    vec = pl.BlockSpec((1, w), lambda i: (0, 0))
    ocol = pl.BlockSpec((CHUNK, w), lambda i: (rev(i), 0))
    res = pl.pallas_call(
        body, grid=(nch,),
        in_specs=[col(2), col(3), col(4), col(5), ocol,
                  pl.BlockSpec((1, HEADS, DH, DH), lambda i: (rev(i), 0, 0, 0)), col(1), vec, vec],
        out_specs=[ocol, ocol, ocol, ocol, vec, vec],
        out_shape=[jax.ShapeDtypeStruct((s, w), BF16)] * 4 + [jax.ShapeDtypeStruct((1, w), F32)] * 2,
        scratch_shapes=[pltpu.VMEM((HEADS, DH, DH), F32)],
        compiler_params=_cparams(("arbitrary",)), name=name)(
            u, u, u, u, o_raw, states, dout, lb.reshape(1, w), gg.reshape(1, w))
    return res[0], res[1], res[2], res[3], res[4][0], res[5][0]


def _log_sigmoid(x):
    return jnp.minimum(x, 0.0) - jnp.log(1.0 + jnp.exp(-jnp.abs(x)))


def _fgate_fwd(cf, bf, name):
    s, wl = cf.shape
    tb = _pick(s, (512, 256, 128, 64))

    def body(c_ref, b_ref, f_ref, carry):
        @pl.when(pl.program_id(0) == 0)
        def _():
            carry[...] = jnp.zeros_like(carry)

        ls = _log_sigmoid(c_ref[...] + b_ref[...])
        f_ref[...] = _tri_matmul(_tril(tb), ls, 3) + carry[...]
        carry[...] += jnp.sum(ls, axis=0, keepdims=True)

    return pl.pallas_call(
        body, grid=(s // tb,), in_specs=[pl.BlockSpec((tb, wl), lambda i: (i, 0)), pl.BlockSpec((1, wl), lambda i: (0, 0))],
        out_specs=pl.BlockSpec((tb, wl), lambda i: (i, 0)), out_shape=jax.ShapeDtypeStruct((s, wl), F32),
        scratch_shapes=[pltpu.VMEM((1, wl), F32)],
        compiler_params=_cparams(("arbitrary",)), name=name)(cf, bf)


def _fgate_bwd(dF, cf, bf, name):
    s, wl = cf.shape
    tb = _pick(s, (512, 256, 128, 64))
    nb = s // tb

    def body(d_ref, c_ref, b_ref, dc_ref, db_ref, carry):
        @pl.when(pl.program_id(0) == 0)
        def _():
            carry[...] = jnp.zeros_like(carry)
            db_ref[...] = jnp.zeros_like(db_ref)

        dv = d_ref[...]
        dls = _tri_matmul(_tril(tb, upper=True), dv, 3) + carry[...]
        carry[...] += jnp.sum(dv, axis=0, keepdims=True)
        dc = dls * (1.0 - _sigmoid(c_ref[...] + b_ref[...]))
        dc_ref[...] = dc.astype(BF16)
        db_ref[...] += jnp.sum(dc, axis=0, keepdims=True)

    blk = pl.BlockSpec((tb, wl), lambda i: (nb - 1 - i, 0))
    vec = pl.BlockSpec((1, wl), lambda i: (0, 0))
    dc, db = pl.pallas_call(
        body, grid=(nb,), in_specs=[blk, blk, vec], out_specs=[blk, vec],
        out_shape=[jax.ShapeDtypeStruct((s, wl), BF16), jax.ShapeDtypeStruct((1, wl), F32)],
        scratch_shapes=[pltpu.VMEM((1, wl), F32)],
        compiler_params=_cparams(("arbitrary",)), name=name)(dF, cf, bf)
    return dc, db[0]


FOX_HPB = 2


def _fox_scores_t(q, k, fq_row, fk_col, diag):
    sc = _nt(k, q) * (DH ** -0.5) + fq_row - fk_col
    if not diag:
        return sc
    r = lax.broadcasted_iota(jnp.int32, sc.shape, 0)
    c = lax.broadcasted_iota(jnp.int32, sc.shape, 1)
    return jnp.where(r <= c, sc, NEG)


def _fox_probs_t(q, k, fq_row, fk_col, lse_row, diag):
    return jnp.exp(_fox_scores_t(q, k, fq_row, fk_col, diag) - lse_row)


def _fox_delta_row(do, o):
    prod = do * o
    hi = prod.astype(BF16)
    lo = (prod - hi.astype(F32)).astype(BF16)
    ones = jnp.ones((SUBLANES, DH), BF16)
    dims = (((1,), (1,)), ((), ()))
    return (lax.dot_general(ones, hi, dims, preferred_element_type=F32)
            + lax.dot_general(ones, lo, dims, preferred_element_type=F32))[0:1, :]


def _fox_when_needed(q_blk, k_blk, step):
    @pl.when(k_blk < q_blk)
    def _():
        step(False)

    @pl.when(k_blk == q_blk)
    def _():
        step(True)


def _fox_fwd(u, f_col, f_row, name):
    s = u.shape[0]
    t = _pick(s, (512, 256, 128))
    nb = s // t
    hpb = FOX_HPB
    wb = hpb * DH
    ng = HEADS // hpb

    def body(q_ref, k_ref, v_ref, fq_ref, fk_ref, o_ref, lse_ref, m_sc, l_sc, acc):
        i, j = pl.program_id(1), pl.program_id(2)

        @pl.when(j == 0)
        def _():
            m_sc[...] = jnp.full_like(m_sc, NEG)
            l_sc[...] = jnp.zeros_like(l_sc)
            acc[...] = jnp.zeros_like(acc)

        def step(diag):
            for hh in range(hpb):
                sl = slice(hh * DH, (hh + 1) * DH)
                sc = _fox_scores_t(q_ref[:, sl], k_ref[:, sl], fq_ref[hh], fk_ref[hh], diag)
                m_new = jnp.maximum(m_sc[hh], jnp.max(sc, axis=0, keepdims=True))
                a = jnp.exp(m_sc[hh] - m_new)
                p = jnp.exp(sc - m_new)
                l_sc[hh] = a * l_sc[hh] + jnp.sum(p, axis=0, keepdims=True)
                acc[sl, :] = a * acc[sl, :] + _tn(v_ref[:, sl], p)
                m_sc[hh] = m_new

        _fox_when_needed(i, j, step)

        @pl.when(j == i)
        def _():
            for hh in range(hpb):
                sl = slice(hh * DH, (hh + 1) * DH)
                o_ref[:, sl] = jnp.transpose(acc[sl, :] / l_sc[hh])
                lse_ref[hh] = m_sc[hh] + jnp.log(l_sc[hh])

    qs = pl.BlockSpec((t, wb), lambda h, i, j: (i, h))
    ks = lambda base: pl.BlockSpec((t, wb), lambda h, i, j: (jnp.minimum(j, i), base + h))
    return pl.pallas_call(
        body, grid=(ng, nb, nb),
        in_specs=[qs, ks(ng), ks(2 * ng),
                  pl.BlockSpec((hpb, 1, t), lambda h, i, j: (h, 0, i)),
                  pl.BlockSpec((hpb, t, 1), lambda h, i, j: (h, jnp.minimum(j, i), 0))],
        out_specs=[pl.BlockSpec((t, wb), lambda h, i, j: (i, h)), pl.BlockSpec((hpb, 1, t), lambda h, i, j: (h, 0, i))],
        out_shape=[jax.ShapeDtypeStruct((s, HEADS * DH), F32), jax.ShapeDtypeStruct((HEADS, 1, s), F32)],
        scratch_shapes=[pltpu.VMEM((hpb, 1, t), F32), pltpu.VMEM((hpb, 1, t), F32), pltpu.VMEM((wb, t), F32)],
        compiler_params=_cparams(("parallel", "parallel", "arbitrary")), name=name)(u, u, u, f_row, f_col)


def _fox_bwd_kv(u, o, lse_row, dout, f_col, f_row, name):
    s = u.shape[0]
    t = _pick(s, (512, 256, 128))
    nb = s // t
    hpb = FOX_HPB
    wb = hpb * DH
    ng = HEADS // hpb

    def body(q_ref, k_ref, v_ref, o_ref, do_ref, lse_ref, fq_ref, fk_ref, dk_ref, dv_ref, dfk_ref, dk_acc, dv_acc, df_acc):
        j, i = pl.program_id(1), pl.program_id(2)

        @pl.when(i == 0)
        def _():
            dk_acc[...] = jnp.zeros_like(dk_acc)
            dv_acc[...] = jnp.zeros_like(dv_acc)
            df_acc[...] = jnp.zeros_like(df_acc)

        def step(diag):
            for hh in range(hpb):
                sl = slice(hh * DH, (hh + 1) * DH)
                q = q_ref[:, sl]
                do = do_ref[:, sl]
                p = _fox_probs_t(q, k_ref[:, sl], fq_ref[hh], fk_ref[hh], lse_ref[hh], diag)
                dv_acc[:, sl] += _nn(p, do)
                ds = p * (_nt(v_ref[:, sl], do) - _fox_delta_row(do, o_ref[:, sl]))
                dk_acc[:, sl] += _nn(ds, q) * (DH ** -0.5)
                df_acc[hh] -= jnp.sum(ds, axis=1, keepdims=True)

        _fox_when_needed(i, j, step)

        @pl.when(i == nb - 1)
        def _():
            dk_ref[...] = dk_acc[...].astype(BF16)
            dv_ref[...] = dv_acc[...].astype(BF16)
            dfk_ref[...] = df_acc[...]

    qi = lambda j, i: jnp.maximum(i, j)
    qs = lambda base: pl.BlockSpec((t, wb), lambda h, j, i: (qi(j, i), base + h))
    ks = lambda base: pl.BlockSpec((t, wb), lambda h, j, i: (j, base + h))
    return pl.pallas_call(
        body, grid=(ng, nb, nb),
        in_specs=[qs(0), ks(ng), ks(2 * ng), qs(0), qs(0),
                  pl.BlockSpec((hpb, 1, t), lambda h, j, i: (h, 0, qi(j, i))),
                  pl.BlockSpec((hpb, 1, t), lambda h, j, i: (h, 0, qi(j, i))),
                  pl.BlockSpec((hpb, t, 1), lambda h, j, i: (h, j, 0))],
        out_specs=[pl.BlockSpec((t, wb), lambda h, j, i: (j, h)), pl.BlockSpec((t, wb), lambda h, j, i: (j, h)),
                   pl.BlockSpec((hpb, t, 1), lambda h, j, i: (h, j, 0))],
        out_shape=[jax.ShapeDtypeStruct((s, HEADS * DH), BF16)] * 2 + [jax.ShapeDtypeStruct((HEADS, s, 1), F32)],
        scratch_shapes=[pltpu.VMEM((t, wb), F32), pltpu.VMEM((t, wb), F32), pltpu.VMEM((hpb, t, 1), F32)],
        compiler_params=_cparams(("parallel", "parallel", "arbitrary")), name=name)(
            u, u, u, o, dout, lse_row, f_row, f_col)


def _fox_bwd_q(u, o, lse_row, dout, f_col, f_row, name):
    s = u.shape[0]
    t = _pick(s, (512, 256, 128))
    nb = s // t
    hpb = FOX_HPB
    wb = hpb * DH
    ng = HEADS // hpb

    def body(q_ref, k_ref, v_ref, o_ref, do_ref, lse_ref, fq_ref, fk_ref, dq_ref, dfq_ref, dq_acc, df_acc):
        i, j = pl.program_id(1), pl.program_id(2)

        @pl.when(j == 0)
        def _():
            dq_acc[...] = jnp.zeros_like(dq_acc)
            df_acc[...] = jnp.zeros_like(df_acc)

        def step(diag):
            for hh in range(hpb):
                sl = slice(hh * DH, (hh + 1) * DH)
                do = do_ref[:, sl]
                k = k_ref[:, sl]
                p = _fox_probs_t(q_ref[:, sl], k, fq_ref[hh], fk_ref[hh], lse_ref[hh], diag)
                ds = p * (_nt(v_ref[:, sl], do) - _fox_delta_row(do, o_ref[:, sl]))
                dq_acc[sl, :] += _tn(k, ds) * (DH ** -0.5)
                df_acc[hh] += jnp.sum(ds, axis=0, keepdims=True)

        _fox_when_needed(i, j, step)

        @pl.when(j == i)
        def _():
            for hh in range(hpb):
                sl = slice(hh * DH, (hh + 1) * DH)
                dq_ref[:, sl] = jnp.transpose(dq_acc[sl, :]).astype(BF16)
            dfq_ref[...] = df_acc[...]

    qs = lambda base: pl.BlockSpec((t, wb), lambda h, i, j: (i, base + h))
    ks = lambda base: pl.BlockSpec((t, wb), lambda h, i, j: (jnp.minimum(j, i), base + h))
    row = pl.BlockSpec((hpb, 1, t), lambda h, i, j: (h, 0, i))
    return pl.pallas_call(
        body, grid=(ng, nb, nb),
        in_specs=[qs(0), ks(ng), ks(2 * ng), qs(0), qs(0), row, row,
                  pl.BlockSpec((hpb, t, 1), lambda h, i, j: (h, jnp.minimum(j, i), 0))],
        out_specs=[pl.BlockSpec((t, wb), lambda h, i, j: (i, h)), row],
        out_shape=[jax.ShapeDtypeStruct((s, HEADS * DH), BF16), jax.ShapeDtypeStruct((HEADS, 1, s), F32)],
        scratch_shapes=[pltpu.VMEM((wb, t), F32), pltpu.VMEM((hpb, 1, t), F32)],
        compiler_params=_cparams(("parallel", "parallel", "arbitrary")), name=name)(
            u, u, u, o, dout, lse_row, f_row, f_col)


def _ca_bias(rel_bias):
    nh = rel_bias.shape[0]
    n_clip = CA_BAND - REL_CLIP
    gv = jnp.concatenate([jnp.broadcast_to(rel_bias[:, REL_TABLE - 1:], (nh, n_clip)),
                          rel_bias[:, REL_TABLE - 2::-1]], axis=1)
    b = jnp.stack([gv[:, CHUNK - 1 - qi:CHUNK - 1 - qi + CA_BAND] for qi in range(CHUNK)], axis=1)
    neg = lambda n: jnp.full((nh, CHUNK, n), NEG, F32)
    blocks = [jnp.concatenate([neg(a * CHUNK), b, neg(CA_WIN - CA_BAND - a * CHUNK)], axis=2)
              for a in range(CA_TILE // CHUNK)]
    return jnp.concatenate(blocks, axis=1)


def _ca_bias_grad(db):
    nh = db.shape[0]
    n_clip = CA_BAND - REL_CLIP
    d64 = sum(db[:, a * CHUNK:(a + 1) * CHUNK, a * CHUNK:a * CHUNK + CA_BAND] for a in range(CA_TILE // CHUNK))
    wide = jnp.pad(d64[:, ::-1, :], ((0, 0), (0, 0), (0, CHUNK))).reshape(nh, CHUNK * (CA_BAND + CHUNK))
    n_gv = CA_BAND + CHUNK - 1
    dgv = jnp.sum(wide[:, :CHUNK * n_gv].reshape(nh, CHUNK, n_gv), axis=1)
    return jnp.concatenate([dgv[:, n_clip:][:, ::-1], jnp.sum(dgv[:, :n_clip], axis=1, keepdims=True)], axis=1)


def _ca_specs():
    cur = lambda base: pl.BlockSpec((CA_BLK, DH), lambda h, i: (i, base + h))
    prev = lambda base: pl.BlockSpec((CA_BLK, DH), lambda h, i: (jnp.maximum(i - 1, 0), base + h))
    bias = pl.BlockSpec((1, CA_TILE, CA_WIN), lambda h, i: (h, 0, 0))
    return cur, prev, bias


def _ca_fill(kcat, vcat, kp_ref, kc_ref, vp_ref, vc_ref):
    kcat[0:CA_BLK, :] = kp_ref[...]
    kcat[CA_BLK:2 * CA_BLK, :] = kc_ref[...]
    vcat[0:CA_BLK, :] = vp_ref[...]
    vcat[CA_BLK:2 * CA_BLK, :] = vc_ref[...]


def _ca_probs(i, tl, q, kw, bias):
    sc = _nt(q, kw) * (DH ** -0.5) + bias
    col = lax.broadcasted_iota(jnp.int32, sc.shape, 1)
    sc = jnp.where((i - 1) * CA_BLK + tl * CA_TILE + col >= 0, sc, NEG)
    p = jnp.exp(sc - jnp.max(sc, axis=-1, keepdims=True))
    return p / jnp.sum(p, axis=-1, keepdims=True)


def _ca_fwd(u, bias, name):
    s = u.shape[0]
    cur, prev, bsp = _ca_specs()

    def body(q_ref, kp_ref, kc_ref, vp_ref, vc_ref, b_ref, o_ref, kcat, vcat):
        i = pl.program_id(1)
        _ca_fill(kcat, vcat, kp_ref, kc_ref, vp_ref, vc_ref)
        for tl in range(CA_BLK // CA_TILE):
            rows = slice(tl * CA_TILE, (tl + 1) * CA_TILE)
            win = slice(tl * CA_TILE, tl * CA_TILE + CA_WIN)
            p = _ca_probs(i, tl, q_ref[rows, :], kcat[win, :], b_ref[0])
            o_ref[rows, :] = _nn(p, vcat[win, :])

    return pl.pallas_call(
        body, grid=(HEADS, s // CA_BLK),
        in_specs=[cur(3 * HEADS), prev(4 * HEADS), cur(4 * HEADS), prev(5 * HEADS), cur(5 * HEADS), bsp],
        out_specs=pl.BlockSpec((CA_BLK, DH), lambda h, i: (i, h)),
        out_shape=jax.ShapeDtypeStruct((s, HEADS * DH), F32),
        scratch_shapes=[pltpu.VMEM((2 * CA_BLK, DH), F32)] * 2,
        compiler_params=_cparams(("parallel", "parallel")), name=name)(u, u, u, u, u, bias)


def _ca_bwd(u, bias, dout, name):
    s = u.shape[0]
    cur, prev, bsp = _ca_specs()
    rows_cat = 2 * CA_BLK

    def body(q_ref, kp_ref, kc_ref, vp_ref, vc_ref, b_ref, do_ref,
             dq_ref, dka_ref, dkb_ref, dva_ref, dvb_ref, db_ref, kcat, vcat, dkcat, dvcat):
        i = pl.program_id(1)

        @pl.when(i == 0)
        def _():
            db_ref[...] = jnp.zeros_like(db_ref)

        _ca_fill(kcat, vcat, kp_ref, kc_ref, vp_ref, vc_ref)
        dkcat[...] = jnp.zeros_like(dkcat)
        dvcat[...] = jnp.zeros_like(dvcat)
        for tl in range(CA_BLK // CA_TILE):
            rows = slice(tl * CA_TILE, (tl + 1) * CA_TILE)
            win = slice(tl * CA_TILE, tl * CA_TILE + CA_WIN)
            q = q_ref[rows, :]
            kw = kcat[win, :]
            vw = vcat[win, :]
            do = do_ref[rows, :]
            p = _ca_probs(i, tl, q, kw, b_ref[0])
            dp = _nt(do, vw)
            ds = p * (dp - jnp.sum(p * dp, axis=-1, keepdims=True))
            dq_ref[rows, :] = (_nn(ds, kw) * (DH ** -0.5)).astype(BF16)
            dkcat[win, :] += _tn(ds, q) * (DH ** -0.5)
            dvcat[win, :] += _tn(p, do)
            db_ref[0] += ds
        dkb_ref[...] = dkcat[0:CA_BLK, :]
        dka_ref[...] = dkcat[CA_BLK:2 * CA_BLK, :]
        dvb_ref[...] = dvcat[0:CA_BLK, :]
        dva_ref[...] = dvcat[CA_BLK:2 * CA_BLK, :]

    osp = pl.BlockSpec((CA_BLK, DH), lambda h, i: (i, h))
    shp = jax.ShapeDtypeStruct((s, HEADS * DH), F32)
    return pl.pallas_call(
        body, grid=(HEADS, s // CA_BLK),
        in_specs=[cur(3 * HEADS), prev(4 * HEADS), cur(4 * HEADS), prev(5 * HEADS), cur(5 * HEADS), bsp,
                  pl.BlockSpec((CA_BLK, DH), lambda h, i: (i, HEADS + h))],
        out_specs=[osp, osp, osp, osp, osp, bsp],
        out_shape=[jax.ShapeDtypeStruct((s, HEADS * DH), BF16), shp, shp, shp, shp,
                   jax.ShapeDtypeStruct((HEADS, CA_TILE, CA_WIN), F32)],
        scratch_shapes=[pltpu.VMEM((rows_cat, DH), F32)] * 4,
        compiler_params=_cparams(("parallel", "arbitrary")), name=name)(u, u, u, u, u, bias, dout)


def _ca_merge(da, db):
    shifted = jnp.concatenate([db[CA_BLK:], jnp.zeros((CA_BLK, db.shape[1]), F32)], axis=0)
    return (da + shifted).astype(BF16)


def _place():
    x, y, c = lax.axis_index("x"), lax.axis_index("y"), lax.axis_index("c")
    return x, y, c, [(1 - x, y), (x, 1 - y), (1 - x, 1 - y)]


_ANY = pl.BlockSpec(memory_space=pl.ANY)


def _all_gather_chips(w, name):
    def body(w_ref, o_ref, send_sems, recv_sems, loc_sems):
        _exchange(["ag"], [w_ref], [o_ref], send_sems, recv_sems, loc_sems, start=True)
        _exchange(["ag"], [w_ref], [o_ref], send_sems, recv_sems, loc_sems, start=False)

    return pl.pallas_call(
        body, in_specs=[_ANY], out_specs=_ANY, out_shape=jax.ShapeDtypeStruct((4,) + w.shape, w.dtype),
        scratch_shapes=[pltpu.SemaphoreType.DMA((3,)), pltpu.SemaphoreType.DMA((3,)), pltpu.SemaphoreType.DMA((1,))],
        name=name)(w)


def _exchange(kinds, srcs, dsts, send_sems, recv_sems, loc_sems, start):
    x, y, c, peers = _place()
    me = 2 * x + y
    for n, kind in enumerate(kinds):
        src, dst = srcs[n], dsts[n]
        mine = src if kind == "ag" else src.at[me]
        loc = pltpu.make_async_copy(mine, dst.at[me], loc_sems.at[n])
        copies = []
        for k, (px, py) in enumerate(peers):
            out_src = src if kind == "ag" else src.at[2 * px + py]
            send = pltpu.make_async_remote_copy(src_ref=out_src, dst_ref=dst.at[me], send_sem=send_sems.at[3 * n + k],
                                                recv_sem=recv_sems.at[3 * n + k], device_id=(px, py, c),
                                                device_id_type=MESH)
            recv = pltpu.make_async_remote_copy(src_ref=mine, dst_ref=dst.at[2 * px + py],
                                                send_sem=send_sems.at[3 * n + k], recv_sem=recv_sems.at[3 * n + k],
                                                device_id=(px, py, c), device_id_type=MESH)
            copies.append((send, recv))
        if start:
            loc.start()
            for send, _ in copies:
                send.start()
        else:
            for _, recv in copies:
                recv.wait_recv()
            for send, _ in copies:
                send.wait_send()
            loc.wait()


def _core_swap(a, name):
    def body(a_ref, o_ref, send_sem, recv_sem):
        x, y, c, _ = _place()
        cp = pltpu.make_async_remote_copy(src_ref=a_ref, dst_ref=o_ref, send_sem=send_sem, recv_sem=recv_sem,
                                          device_id=(x, y, 1 - c), device_id_type=MESH)
        cp.start()
        cp.wait()

    return pl.pallas_call(
        body, in_specs=[_ANY], out_specs=_ANY, out_shape=jax.ShapeDtypeStruct(a.shape, a.dtype),
        scratch_shapes=[pltpu.SemaphoreType.DMA(()), pltpu.SemaphoreType.DMA(())], name=name)(a)


def _all_reduce_small(v, name):
    r, wl = v.shape

    def body(v_ref, o_ref, buf, send_sems, recv_sems):
        x, y, c, _ = _place()
        me = 4 * x + 2 * y + c
        buf[me] = v_ref[...]
        flips = [(fx, fy, fc) for fx in (0, 1) for fy in (0, 1) for fc in (0, 1) if (fx, fy, fc) != (0, 0, 0)]
        peer = lambda f: (x ^ f[0], y ^ f[1], c ^ f[2])
        sends = []
        for k, f in enumerate(flips):
            cp = pltpu.make_async_remote_copy(src_ref=v_ref, dst_ref=buf.at[me], send_sem=send_sems.at[k],
                                              recv_sem=recv_sems.at[k], device_id=peer(f), device_id_type=MESH)
            cp.start()
            sends.append(cp)
        for k, f in enumerate(flips):
            px, py, pc = peer(f)
            pltpu.make_async_remote_copy(src_ref=v_ref, dst_ref=buf.at[4 * px + 2 * py + pc], send_sem=send_sems.at[k],
                                         recv_sem=recv_sems.at[k], device_id=peer(f), device_id_type=MESH).wait_recv()
        for cp in sends:
            cp.wait_send()
        acc = buf[0]
        for d in range(1, 8):
            acc = acc + buf[d]
        o_ref[...] = acc

    vm = pl.BlockSpec(memory_space=pltpu.VMEM)
    return pl.pallas_call(
        body, in_specs=[vm], out_specs=vm, out_shape=jax.ShapeDtypeStruct((r, wl), F32),
        scratch_shapes=[pltpu.VMEM((8, r, wl), F32), pltpu.SemaphoreType.DMA((7,)), pltpu.SemaphoreType.DMA((7,))],
        name=name)(v)


def _sum_slots(g, name):
    _, r, cdim = g.shape
    br = _pick(r, (256, 128, 64, 32, 8))

    def body(g_ref, o_ref):
        o_ref[...] = ((g_ref[0].astype(F32) + g_ref[1].astype(F32)) + g_ref[2].astype(F32)) + g_ref[3].astype(F32)

    return pl.pallas_call(
        body, grid=(r // br,), in_specs=[pl.BlockSpec((4, br, cdim), lambda i: (0, i, 0))],
        out_specs=pl.BlockSpec((br, cdim), lambda i: (i, 0)), out_shape=jax.ShapeDtypeStruct((r, cdim), F32),
        compiler_params=_cparams(("parallel",)), name=name)(g)


def _adamw(w, grads, m, v, name):
    nl, r, cdim = w.shape
    br = _pick(r, (128, 64, 32, 8))
    c1 = 1.0 / (1.0 - ADAM_B1 ** ADAM_STEP)
    c2 = 1.0 / (1.0 - ADAM_B2 ** ADAM_STEP)
    counts = [len(t) for t in grads]
    flat = [a for t in grads for a in t]

    def body(*refs):
        w_ref = refs[0]
        g_refs = refs[1:1 + len(flat)]
        m_ref, v_ref, g_out, d_out, m_out, v_out = refs[1 + len(flat):]
        layer = pl.program_id(0)
        g, pos = None, 0
        for l, n in enumerate(counts):
            gl = g_refs[pos][...]
            for a in g_refs[pos + 1:pos + n]:
                gl = gl + a[...]
            pos += n
            g = gl if g is None else jnp.where(layer == l, gl, g)
        mn = ADAM_B1 * m_ref[...] + (1.0 - ADAM_B1) * g
        vn = ADAM_B2 * v_ref[...] + (1.0 - ADAM_B2) * (g * g)
        g_out[...] = g
        m_out[...] = mn
        v_out[...] = vn
        d_out[...] = -ADAM_LR * ((mn * c1) / (jnp.sqrt(vn * c2) + ADAM_EPS) + ADAM_WD * w_ref[...])

    blk = pl.BlockSpec((None, br, cdim), lambda l, i: (l, i, 0))
    gblk = pl.BlockSpec((br, cdim), lambda l, i: (i, 0))
    return pl.pallas_call(
        body, grid=(nl, r // br), in_specs=[blk] + [gblk] * len(flat) + [blk, blk], out_specs=[blk] * 4,
        out_shape=[jax.ShapeDtypeStruct((nl, r, cdim), F32)] * 4,
        compiler_params=_cparams(("parallel", "parallel")), name=name)(w, *flat, m, v)


def _unshard_cols(g):
    return jnp.transpose(g, (1, 0, 2)).reshape(g.shape[1], 4 * g.shape[2])


def _unshard_rows(g):
    return g.reshape(4 * g.shape[1], g.shape[2])


def _shard_cols(g):
    k, n = g.shape
    return jnp.transpose(g.reshape(k, 4, n // 4), (1, 0, 2))


def _shard_rows(g):
    k, n = g.shape
    return g.reshape(4, k // 4, n)


def _lower_bound(logits):
    return jnp.cumsum(jax.nn.softmax(logits.astype(F32), axis=0), axis=0)[0]


def _pack(parts):
    flat = jnp.concatenate([p.reshape(-1) for p in parts])
    n = flat.shape[0]
    rows = -(-n // 1024) * 8
    return jnp.pad(flat, (0, rows * 128 - n)).reshape(rows, 128)


def _unpack(packed, shapes):
    flat = packed.reshape(-1)
    out, off = [], 0
    for shp in shapes:
        n = 1
        for d in shp:
            n *= d
        out.append(flat[off:off + n].reshape(shp))
        off += n
    return out


def kernel(x, ev_w_in, ev_conv_w, ev_conv_b, ev_conv_ln_g, ev_conv_ln_b, hgrn_lb_logits, ev_gnorm_g, ev_w_out, od_w_in, fox_b_f, rel_bias, od_w_out, ln_mix_g, ln_mix_b, mlp_w1, mlp_w2, ln_mlp_g, ln_mlp_b, loss_target, m_ev_w_in, m_ev_conv_w, m_ev_conv_b, m_ev_conv_ln_g, m_ev_conv_ln_b, m_hgrn_lb_logits, m_ev_gnorm_g, m_ev_w_out, m_od_w_in, m_fox_b_f, m_rel_bias, m_od_w_out, m_ln_mix_g, m_ln_mix_b, m_mlp_w1, m_mlp_w2, m_ln_mlp_g, m_ln_mlp_b, v_ev_w_in, v_ev_conv_w, v_ev_conv_b, v_ev_conv_ln_g, v_ev_conv_ln_b, v_hgrn_lb_logits, v_ev_gnorm_g, v_ev_w_out, v_od_w_in, v_fox_b_f, v_rel_bias, v_od_w_out, v_ln_mix_g, v_ln_mix_b, v_mlp_w1, v_mlp_w2, v_ln_mlp_g, v_ln_mlp_b):
    w_sharded = dict(ev_w_in=ev_w_in, ev_w_out=ev_w_out, od_w_in=od_w_in, od_w_out=od_w_out, mlp_w1=mlp_w1, mlp_w2=mlp_w2)
    m_sharded = dict(ev_w_in=m_ev_w_in, ev_w_out=m_ev_w_out, od_w_in=m_od_w_in, od_w_out=m_od_w_out, mlp_w1=m_mlp_w1, mlp_w2=m_mlp_w2)
    v_sharded = dict(ev_w_in=v_ev_w_in, ev_w_out=v_ev_w_out, od_w_in=v_od_w_in, od_w_out=v_od_w_out, mlp_w1=v_mlp_w1, mlp_w2=v_mlp_w2)
    small_names = ["ev_conv_w", "ev_conv_b", "ev_conv_ln_g", "ev_conv_ln_b", "hgrn_lb_logits", "ev_gnorm_g", "fox_b_f",
                   "rel_bias", "ln_mix_g", "ln_mix_b", "ln_mlp_g", "ln_mlp_b"]
    w_small = dict(ev_conv_w=ev_conv_w, ev_conv_b=ev_conv_b, ev_conv_ln_g=ev_conv_ln_g, ev_conv_ln_b=ev_conv_ln_b,
                   hgrn_lb_logits=hgrn_lb_logits, ev_gnorm_g=ev_gnorm_g, fox_b_f=fox_b_f, rel_bias=rel_bias,
                   ln_mix_g=ln_mix_g, ln_mix_b=ln_mix_b, ln_mlp_g=ln_mlp_g, ln_mlp_b=ln_mlp_b)
    m_small = dict(ev_conv_w=m_ev_conv_w, ev_conv_b=m_ev_conv_b, ev_conv_ln_g=m_ev_conv_ln_g, ev_conv_ln_b=m_ev_conv_ln_b,
                   hgrn_lb_logits=m_hgrn_lb_logits, ev_gnorm_g=m_ev_gnorm_g, fox_b_f=m_fox_b_f, rel_bias=m_rel_bias,
                   ln_mix_g=m_ln_mix_g, ln_mix_b=m_ln_mix_b, ln_mlp_g=m_ln_mlp_g, ln_mlp_b=m_ln_mlp_b)
    v_small = dict(ev_conv_w=v_ev_conv_w, ev_conv_b=v_ev_conv_b, ev_conv_ln_g=v_ev_conv_ln_g, ev_conv_ln_b=v_ev_conv_ln_b,
                   hgrn_lb_logits=v_hgrn_lb_logits, ev_gnorm_g=v_ev_gnorm_g, fox_b_f=v_fox_b_f, rel_bias=v_rel_bias,
                   ln_mix_g=v_ln_mix_g, ln_mix_b=v_ln_mix_b, ln_mlp_g=v_ln_mlp_g, ln_mlp_b=v_ln_mlp_b)

    chip = 2 * lax.axis_index("x") + lax.axis_index("y")
    hw = HEADS * DH

    w_ev_in = _all_gather_chips(ev_w_in[0].astype(BF16), "ag_ev_w_in")
    shards = dict(ev_w_out=ev_w_out[0].astype(BF16), od_w_in=od_w_in[0].astype(BF16), od_w_out=od_w_out[0].astype(BF16),
                  mlp_w1=[mlp_w1[l].astype(BF16) for l in range(DEPTH)],
                  mlp_w2=[mlp_w2[l].astype(BF16) for l in range(DEPTH)])
    tables = _all_gather_chips(jnp.concatenate(
        [ev_conv_w[0].reshape(1, -1), jnp.pad(rel_bias[0].reshape(1, -1), ((0, 0), (0, (-rel_bias[0].size) % 128)))],
        axis=1), "ag_tables")
    ncw = ev_conv_w[0].size
    cshard = ev_conv_w.shape[2]
    conv_w = jnp.transpose(tables[:, 0, :ncw].reshape(4, CONV_WIDTH, cshard), (1, 0, 2)).reshape(CONV_WIDTH, 4 * cshard)
    rshard = rel_bias.shape[2]
    rel_full = jnp.transpose(tables[:, 0, ncw:ncw + HEADS * rshard].reshape(4, HEADS, rshard), (1, 0, 2)).reshape(HEADS, 4 * rshard)

    loss_part, grad_x, g, small_partial = _local_step(
        x[0], loss_target[0], w_ev_in, shards, conv_w, rel_full, ev_conv_b, ev_conv_ln_g,
        ev_conv_ln_b, hgrn_lb_logits, ev_gnorm_g, fox_b_f, ln_mix_g, ln_mix_b, ln_mlp_g, ln_mlp_b)
    loss = lax.psum(loss_part, ("x", "y", "c"))

    full_shapes = [tuple(small_partial[n].shape) for n in small_names]
    reduced = _unpack(_all_reduce_small(_pack([small_partial[n] for n in small_names]), "ar_small"), full_shapes)
    g_small = {}
    for n, val in zip(small_names, reduced):
        if n == "ev_conv_w":
            val = lax.dynamic_slice_in_dim(val, chip * cshard, cshard, axis=1)
        elif n == "rel_bias":
            val = lax.dynamic_slice_in_dim(val, chip * rshard, rshard, axis=1)
        g_small[n] = val.reshape(w_small[n].shape)
    shard_shapes = [tuple(w_small[n].shape) for n in small_names]
    packed = [_pack([d[n] for n in small_names]) for d in (w_small, g_small, m_small, v_small)]
    small_out = _adamw(packed[0][None], [(packed[1],)], packed[2][None], packed[3][None], "adamw_small")
    out_small = {k: dict(zip(small_names, _unpack(val[0], shard_shapes))) for k, val in zip("gdmv", small_out)}

    out_big = {"g": {}, "d": {}, "m": {}, "v": {}}

    for name in ("ev_w_in", "ev_w_out", "od_w_in", "od_w_out", "mlp_w1", "mlp_w2"):
        pairs = []
        for l, slots in enumerate(g[name]):
            mine = _sum_slots(slots, "rs_%s_%d_sum" % (name, l))
            pairs.append((mine, _core_swap(mine, "rs_%s_%d_swap" % (name, l))))
        outs = _adamw(w_sharded[name], pairs, m_sharded[name], v_sharded[name], "adamw_%s" % name)
        for key, val in zip("gdmv", outs):
            out_big[key][name] = val

    order = ["ev_w_in", "ev_conv_w", "ev_conv_b", "ev_conv_ln_g", "ev_conv_ln_b", "hgrn_lb_logits", "ev_gnorm_g", "ev_w_out",
             "od_w_in", "fox_b_f", "rel_bias", "od_w_out", "ln_mix_g", "ln_mix_b", "mlp_w1", "mlp_w2", "ln_mlp_g", "ln_mlp_b"]

    def pick(key, n):
        return out_big[key][n] if n in out_big[key] else out_small[key][n]

    outs = [loss, grad_x[None]]
    for key in ("g", "d", "m", "v"):
        outs.extend(pick(key, n) for n in order)
    return tuple(outs)


def _local_step(xin, tgt, w_ev_in, shards, conv_w, rel_full, ev_conv_b, ev_conv_ln_g,
                ev_conv_ln_b, hgrn_lb_logits, ev_gnorm_g, fox_b_f, ln_mix_g, ln_mix_b, ln_mlp_g, ln_mlp_b):
    hw = HEADS * DH
    bf_pad = jnp.pad(fox_b_f[0], (0, 128 - HEADS)).reshape(1, 128)
    lb0 = _lower_bound(hgrn_lb_logits)
    ca_bias = _ca_bias(rel_full)
    ag = lambda w: ("ag", w)
    a2a = lambda g4: ("a2a", g4)

    xin_b = xin.astype(BF16)
    u0, (w_ev_out4, w1_0) = _matmul(xin_b, w_ev_in, b_sharded=True, comm=(ag(shards["ev_w_out"]), ag(shards["mlp_w1"][0])),
                                    name="l0_in")
    w_ev_out = _unshard_rows(w_ev_out4)
    a_out = _conv_fwd(u0, conv_w, ev_conv_b[0], ev_conv_ln_g[0], ev_conv_ln_b[0], "l0_conv")
    o_raw, b_out, states = _hgrn_fwd(u0, lb0, ev_gnorm_g[0], "l0_hgrn")
    cat0 = jnp.concatenate([a_out, b_out], axis=1).astype(BF16)
    mix0 = _matmul(cat0, w_ev_out, name="l0_out")
    r0a, x0a, x0a_b = _ln_fwd(xin, mix0, ln_mix_g[0], ln_mix_b[0], "l0_ln_mix")
    (z0, h0), (w2_0,) = _matmul(x0a_b, w1_0, b_sharded=True, out_dtype=BF16, epi="relu2",
                                comm=(ag(shards["mlp_w2"][0]),), name="l0_mlp1")
    w2 = [_unshard_rows(w2_0), None]
    f0, (w_od_in4, w_od_out4) = _matmul(h0, w2[0], comm=(ag(shards["od_w_in"]), ag(shards["od_w_out"])), name="l0_mlp2")
    w_od_in = _unshard_cols(w_od_in4)
    w_od_out = _unshard_rows(w_od_out4)
    w_od_main = jnp.concatenate([w_od_in[:, :3 * hw], w_od_in[:, 3 * hw + HEADS:]], axis=1)
    w_od_f = jnp.pad(w_od_in[:, 3 * hw:3 * hw + HEADS], ((0, 0), (0, 128 - HEADS)))
    r0b, x1, x1_b = _ln_fwd(x0a, f0, ln_mlp_g[0], ln_mlp_b[0], "l0_ln_mlp")
    u1, (w1_1,) = _matmul(x1_b, w_od_main, comm=(ag(shards["mlp_w1"][1]),), name="l1_in")
    w1 = [w1_0, w1_1]
    cf = _matmul(x1_b, w_od_f, name="l1_in_f")
    fcum = _fgate_fwd(cf, bf_pad, "l1_fgate")
    f_col = jnp.transpose(fcum[:, :HEADS])[:, :, None]
    f_row = jnp.transpose(fcum[:, :HEADS])[:, None, :]
    c_out, lse_row = _fox_fwd(u1, f_col, f_row, "l1_fox")
    d_out = _ca_fwd(u1, ca_bias, "l1_ca")
    cat1 = jnp.concatenate([c_out, d_out], axis=1).astype(BF16)
    mix1 = _matmul(cat1, w_od_out, name="l1_out")
    r1a, x1a, x1a_b = _ln_fwd(x1, mix1, ln_mix_g[1], ln_mix_b[1], "l1_ln_mix")
    (z1, h1), (w2_1,) = _matmul(x1a_b, w1[1], b_sharded=True, out_dtype=BF16, epi="relu2",
                                comm=(ag(shards["mlp_w2"][1]),), name="l1_mlp1")
    w2[1] = _unshard_rows(w2_1)
    f1 = _matmul(h1, w2[1], name="l1_mlp2")
    r1b, x2, _ = _ln_fwd(x1a, f1, ln_mlp_g[1], ln_mlp_b[1], "l1_ln_mlp")
    dy, loss_part = _loss_head(x2, tgt, "loss")

    g = {}
    dr, drb, dg_, db_ = _ln_bwd(dy, r1b, ln_mlp_g[1], "l1_ln_mlp_bwd")
    g_ln_mlp = [None, (dg_, db_)]
    gw = _shard_rows(_matmul(h1, drb, ta=True, out_dtype=BF16, name="l1_dw2"))
    dz, (gw2_1,) = _matmul(drb, w2[1], tb=True, out_dtype=BF16, epi="drelu2", extra=z1, comm=(a2a(gw),), name="l1_dz")
    gw = _matmul(x1a_b, dz, ta=True, out_dtype=BF16, out_sharded=True, name="l1_dw1")
    dx, (gw1_1,) = _matmul(dz, w1[1], tb=True, b_sharded=True, epi="add", extra=dr, comm=(a2a(gw),), name="l1_dx_mlp")
    dr, drb, dg_, db_ = _ln_bwd(dx, r1a, ln_mix_g[1], "l1_ln_mix_bwd")
    g_ln_mix = [None, (dg_, db_)]
    gw = _shard_rows(_matmul(cat1, drb, ta=True, out_dtype=BF16, name="l1_dwout"))
    dcat, g["od_w_out"] = _matmul(drb, w_od_out, tb=True, comm=(a2a(gw),), name="l1_dcat")
    dk_c, dv_c, dfk = _fox_bwd_kv(u1, c_out, lse_row, dcat, f_col, f_row, "l1_fox_bwd_kv")
    dq_c, dfq = _fox_bwd_q(u1, c_out, lse_row, dcat, f_col, f_row, "l1_fox_bwd_q")
    d_f = jnp.pad(jnp.transpose(dfk[:, :, 0] + dfq[:, 0, :]), ((0, 0), (0, 128 - HEADS)))
    dcf, dbf = _fgate_bwd(d_f, cf, bf_pad, "l1_fgate_bwd")
    dq_d, dka, dkb, dva, dvb, dbias = _ca_bwd(u1, ca_bias, dcat, "l1_ca_bwd")
    du1 = jnp.concatenate([dq_c, dk_c, dv_c, dq_d, _ca_merge(dka, dkb), _ca_merge(dva, dvb)], axis=1)
    g_main = _matmul(x1_b, du1, ta=True, out_dtype=BF16, name="l1_dwin")
    g_f = _matmul(x1_b, dcf, ta=True, out_dtype=BF16, name="l1_dwin_f")
    gw = _shard_cols(jnp.concatenate([g_main[:, :3 * hw], g_f[:, :HEADS], g_main[:, 3 * hw:]], axis=1))
    dx_f = _matmul(dcf, w_od_f, tb=True, epi="add", extra=dr, name="l1_dx_f")
    dx, g["od_w_in"] = _matmul(du1, w_od_main, tb=True, epi="add", extra=dx_f, scale=1.0, comm=(a2a(gw),),
                               name="l1_dx_in")
    dr, drb, dg_, db_ = _ln_bwd(dx, r0b, ln_mlp_g[0], "l0_ln_mlp_bwd")
    g_ln_mlp[0] = (dg_, db_)
    gw = _shard_rows(_matmul(h0, drb, ta=True, out_dtype=BF16, name="l0_dw2"))
    dz, (gw2_0,) = _matmul(drb, w2[0], tb=True, out_dtype=BF16, epi="drelu2", extra=z0, comm=(a2a(gw),), name="l0_dz")
    gw = _matmul(x0a_b, dz, ta=True, out_dtype=BF16, out_sharded=True, name="l0_dw1")
    dx, (gw1_0,) = _matmul(dz, w1[0], tb=True, b_sharded=True, epi="add", extra=dr, comm=(a2a(gw),), name="l0_dx_mlp")
    dr, drb, dg_, db_ = _ln_bwd(dx, r0a, ln_mix_g[0], "l0_ln_mix_bwd")
    g_ln_mix[0] = (dg_, db_)
    gw = _shard_rows(_matmul(cat0, drb, ta=True, out_dtype=BF16, name="l0_dwout"))
    dcat, g["ev_w_out"] = _matmul(drb, w_ev_out, tb=True, comm=(a2a(gw),), name="l0_dcat")
    dc, g_conv_w, g_conv_b, g_conv_lg, g_conv_lb = _conv_bwd_params(
        u0, dcat, conv_w, ev_conv_b[0], ev_conv_ln_g[0], ev_conv_ln_b[0], "l0_conv_bwd_p")
    da, dgate = _conv_bwd_input(u0, dc, conv_w, "l0_conv_bwd_i")
    dhq, dhf, dhi, dhg, g_lb0, g_gnorm = _hgrn_bwd(u0, o_raw, states, dcat, lb0, ev_gnorm_g[0], "l0_hgrn_bwd")
    du0 = jnp.concatenate([da, dgate, dhq, dhf, dhi, dhg], axis=1)
    gw = _matmul(xin_b, du0, ta=True, out_dtype=BF16, out_sharded=True, name="l0_dwin")
    grad_x, g["ev_w_in"] = _matmul(du0, w_ev_in, tb=True, b_sharded=True, epi="add", extra=dr, comm=(a2a(gw),),
                                   name="l0_dx_in")
    g["mlp_w1"] = [gw1_0, gw1_1]
    g["mlp_w2"] = [gw2_0, gw2_1]

    g_lb_logits = jax.vjp(_lower_bound, hgrn_lb_logits)[1](g_lb0)[0]
    g_rel = _ca_bias_grad(dbias)
    small_partial = dict(
        ev_conv_w=g_conv_w, ev_conv_b=g_conv_b, ev_conv_ln_g=g_conv_lg, ev_conv_ln_b=g_conv_lb,
        hgrn_lb_logits=g_lb_logits, ev_gnorm_g=g_gnorm, fox_b_f=dbf[:HEADS], rel_bias=g_rel,
        ln_mix_g=jnp.stack([g_ln_mix[0][0], g_ln_mix[1][0]]), ln_mix_b=jnp.stack([g_ln_mix[0][1], g_ln_mix[1][1]]),
        ln_mlp_g=jnp.stack([g_ln_mlp[0][0], g_ln_mlp[1][0]]), ln_mlp_b=jnp.stack([g_ln_mlp[0][1], g_ln_mlp[1][1]]))
    return loss_part, grad_x, g, small_partial
```

```python
import functools

import jax
import jax.numpy as jnp
from jax import lax
from jax.experimental import pallas as pl
from jax.experimental.pallas import tpu as pltpu

F32 = jnp.float32
BF16 = jnp.bfloat16
MESH = pl.DeviceIdType.MESH

DEPTH = 2
ALPHA = (2 * DEPTH) ** 0.25
LN_EPS = 1e-5
HEADS = 8
DH = 128
CHUNK = 64
CONV_WIDTH = 31
HALO = 32
CA_LEFT = 8
CA_BLK = CA_LEFT * CHUNK
CA_BAND = (CA_LEFT + 1) * CHUNK
CA_TILE = 4 * CHUNK
CA_WIN = CA_TILE + CA_LEFT * CHUNK
REL_CLIP = 256
REL_TABLE = (CHUNK - 1) + REL_CLIP + 1
NEG = -1e30

ADAM_LR = 0.001
ADAM_B1 = 0.9
ADAM_B2 = 0.999
ADAM_EPS = 1e-08
ADAM_WD = 0.01
ADAM_STEP = 10

VMEM_LIMIT = 48 * 1024 * 1024


def _cparams(sem):
    return pltpu.CompilerParams(dimension_semantics=sem, vmem_limit_bytes=VMEM_LIMIT)


def _pick(n, cands):
    for c in cands:
        if n % c == 0:
            return c
    return n


def _sigmoid(x):
    return 1.0 / (1.0 + jnp.exp(-x))


def _dot(a, b, dims):
    return lax.dot_general(a.astype(BF16), b.astype(BF16), (dims, ((), ())), preferred_element_type=F32)


def _nn(a, b):
    return _dot(a, b, ((1,), (0,)))


def _nt(a, b):
    return _dot(a, b, ((1,), (1,)))


def _tn(a, b):
    return _dot(a, b, ((0,), (0,)))


def _dot3(a, b, dims):
    a_hi = a.astype(BF16)
    b_hi = b.astype(BF16)
    a_lo = (a - a_hi.astype(F32)).astype(BF16)
    b_lo = (b - b_hi.astype(F32)).astype(BF16)
    dn = (dims, ((), ()))
    return (lax.dot_general(a_hi, b_hi, dn, preferred_element_type=F32)
            + (lax.dot_general(a_hi, b_lo, dn, preferred_element_type=F32)
               + lax.dot_general(a_lo, b_hi, dn, preferred_element_type=F32)))


def _nn3(a, b):
    return _dot3(a, b, ((1,), (0,)))


def _nt3(a, b):
    return _dot3(a, b, ((1,), (1,)))


def _tn3(a, b):
    return _dot3(a, b, ((0,), (0,)))


def _split3(x):
    hi = x.astype(BF16)
    r1 = x - hi.astype(F32)
    mid = r1.astype(BF16)
    lo = (r1 - mid.astype(F32)).astype(BF16)
    return hi, mid, lo


def _tri_matmul(tri, x, terms):
    parts = _split3(x)[:terms]
    acc = None
    for p in parts:
        t = lax.dot_general(tri, p, (((1,), (0,)), ((), ())), preferred_element_type=F32)
        acc = t if acc is None else acc + t
    return acc


def _tril(n, upper=False):
    r = lax.broadcasted_iota(jnp.int32, (n, n), 0)
    c = lax.broadcasted_iota(jnp.int32, (n, n), 1)
    m = (c >= r) if upper else (c <= r)
    return jnp.where(m, 1.0, 0.0).astype(BF16)


def _matmul(a, b, *, ta=False, tb=False, out_dtype=F32, epi=None, extra=None, scale=ALPHA, b_sharded=False,
            out_sharded=False, comm=(), name):
    m = a.shape[1] if ta else a.shape[0]
    kd = a.shape[0] if ta else a.shape[1]
    if b_sharded:
        shard = b.shape[2]
        n = b.shape[1] if tb else 4 * shard
        assert (b.shape[1] if not tb else 4 * shard) == kd
    else:
        n = b.shape[0] if tb else b.shape[1]
    bm = _pick(m, (1024, 512, 256, 128))
    bn = _pick(shard if (b_sharded and not tb) else (n // 4 if out_sharded else n), (1024, 768, 512, 256, 128))
    bk = _pick(shard if (b_sharded and tb) else kd, (2048, 1536, 1024, 768, 512, 256, 128))
    ni, nj, nk = m // bm, n // bn, kd // bk
    n_out = 2 if epi == "relu2" else 1
    n_comm = len(comm)
    kinds = [c[0] for c in comm]

    def body(*refs):
        a_ref, b_ref = refs[0], refs[1]
        pos = 2
        e_ref = refs[pos] if extra is not None else None
        pos += extra is not None
        c_in = refs[pos:pos + n_comm]
        pos += n_comm
        outs = refs[pos:pos + n_out]
        pos += n_out
        c_out = refs[pos:pos + n_comm]
        pos += n_comm
        acc_ref = refs[pos]
        sems = refs[pos + 1:]
        i, j, k = pl.program_id(0), pl.program_id(1), pl.program_id(2)

        if n_comm:
            @pl.when(jnp.logical_and(jnp.logical_and(i == 0, j == 0), k == 0))
            def _():
                _exchange(kinds, c_in, c_out, *sems, start=True)

        dims = ((0 if ta else 1,), (1 if tb else 0,))
        part = _dot(a_ref[...], b_ref[...], dims)
        if nk > 1:
            @pl.when(k == 0)
            def _():
                acc_ref[...] = part

            @pl.when(jnp.logical_and(k > 0, k < nk - 1))
            def _():
                acc_ref[...] += part

        @pl.when(k == nk - 1)
        def _():
            r = part + acc_ref[...] if nk > 1 else part
            if epi == "relu2":
                outs[0][...] = r.astype(out_dtype)
                outs[1][...] = jnp.square(jnp.maximum(r, 0.0)).astype(out_dtype)
            elif epi == "drelu2":
                outs[0][...] = (r * (2.0 * jnp.maximum(e_ref[...].astype(F32), 0.0))).astype(out_dtype)
            elif epi == "add":
                outs[0][...] = (r + scale * e_ref[...].astype(F32)).astype(out_dtype)
            else:
                outs[0][...] = r.astype(out_dtype)

        if n_comm:
            @pl.when(jnp.logical_and(jnp.logical_and(i == ni - 1, j == nj - 1), k == nk - 1))
            def _():
                _exchange(kinds, c_in, c_out, *sems, start=False)

    a_spec = pl.BlockSpec((bk, bm), lambda i, j, k: (k, i)) if ta else pl.BlockSpec((bm, bk), lambda i, j, k: (i, k))
    if b_sharded and tb:
        per = shard // bk
        b_spec = pl.BlockSpec((None, bn, bk), lambda i, j, k: (k // per, j, k % per))
    elif b_sharded:
        per = shard // bn
        b_spec = pl.BlockSpec((None, bk, bn), lambda i, j, k: (j // per, k, j % per))
    elif tb:
        b_spec = pl.BlockSpec((bn, bk), lambda i, j, k: (j, k))
    else:
        b_spec = pl.BlockSpec((bk, bn), lambda i, j, k: (k, j))
    e_spec = pl.BlockSpec((bm, bn), lambda i, j, k: (i, j))
    if out_sharded:
        per_o = (n // 4) // bn
        o_spec = pl.BlockSpec((None, bm, bn), lambda i, j, k: (j // per_o, i, j % per_o))
        o_shape = jax.ShapeDtypeStruct((4, m, n // 4), out_dtype)
    else:
        o_spec = e_spec
        o_shape = jax.ShapeDtypeStruct((m, n), out_dtype)
    in_specs = [a_spec, b_spec] + ([e_spec] if extra is not None else []) + [_ANY] * n_comm
    args = (a, b) + ((extra,) if extra is not None else ()) + tuple(c[1] for c in comm)
    c_shapes = [jax.ShapeDtypeStruct((4,) + c[1].shape if c[0] == "ag" else c[1].shape, c[1].dtype) for c in comm]
    scratch = [pltpu.VMEM((bm, bn) if nk > 1 else (8, 128), F32)]
    if n_comm:
        scratch += [pltpu.SemaphoreType.DMA((3 * n_comm,)), pltpu.SemaphoreType.DMA((3 * n_comm,)),
                    pltpu.SemaphoreType.DMA((n_comm,))]
    sem = ("arbitrary",) * 3 if n_comm else ("parallel", "parallel", "arbitrary")
    res = pl.pallas_call(
        body, grid=(ni, nj, nk), in_specs=in_specs,
        out_specs=[o_spec] * n_out + [_ANY] * n_comm, out_shape=[o_shape] * n_out + c_shapes,
        scratch_shapes=scratch, compiler_params=_cparams(sem), name=name)(*args)
    main = tuple(res[:n_out]) if n_out == 2 else res[0]
    return (main, list(res[n_out:])) if n_comm else main


def _ln_fwd(x, mix, g, b, name):
    s, d = x.shape
    br = _pick(s, (256, 128, 64, 8))

    def body(x_ref, m_ref, g_ref, b_ref, r_ref, y_ref, yb_ref):
        r = ALPHA * x_ref[...] + m_ref[...]
        mu = jnp.mean(r, axis=-1, keepdims=True)
        dlt = r - mu
        var = jnp.mean(dlt * dlt, axis=-1, keepdims=True)
        y = dlt * lax.rsqrt(var + LN_EPS) * g_ref[...] + b_ref[...]
        r_ref[...] = r
        y_ref[...] = y
        yb_ref[...] = y.astype(BF16)

    row = pl.BlockSpec((br, d), lambda i: (i, 0))
    vec = pl.BlockSpec((1, d), lambda i: (0, 0))
    return pl.pallas_call(
        body, grid=(s // br,), in_specs=[row, row, vec, vec], out_specs=[row, row, row],
        out_shape=[jax.ShapeDtypeStruct((s, d), F32), jax.ShapeDtypeStruct((s, d), F32),
                   jax.ShapeDtypeStruct((s, d), BF16)],
        compiler_params=_cparams(("parallel",)), name=name)(x, mix, g.reshape(1, d), b.reshape(1, d))


def _ln_bwd(dy, r, g, name):
    s, d = r.shape
    br = _pick(s, (256, 128, 64, 8))

    def body(dy_ref, r_ref, g_ref, dr_ref, drb_ref, dg_ref, db_ref):
        @pl.when(pl.program_id(0) == 0)
        def _():
            dg_ref[...] = jnp.zeros_like(dg_ref)
            db_ref[...] = jnp.zeros_like(db_ref)

        rv = r_ref[...]
        dyv = dy_ref[...]
        mu = jnp.mean(rv, axis=-1, keepdims=True)
        dlt = rv - mu
        var = jnp.mean(dlt * dlt, axis=-1, keepdims=True)
        rstd = lax.rsqrt(var + LN_EPS)
        xhat = dlt * rstd
        dxh = dyv * g_ref[...]
        m1 = jnp.mean(dxh, axis=-1, keepdims=True)
        m2 = jnp.mean(dxh * xhat, axis=-1, keepdims=True)
        dr = rstd * (dxh - m1 - xhat * m2)
        dr_ref[...] = dr
        drb_ref[...] = dr.astype(BF16)
        dg_ref[...] += jnp.sum(dyv * xhat, axis=0, keepdims=True)
        db_ref[...] += jnp.sum(dyv, axis=0, keepdims=True)

    row = pl.BlockSpec((br, d), lambda i: (i, 0))
    vec = pl.BlockSpec((1, d), lambda i: (0, 0))
    dr, drb, dg, db = pl.pallas_call(
        body, grid=(s // br,), in_specs=[row, row, vec], out_specs=[row, row, vec, vec],
        out_shape=[jax.ShapeDtypeStruct((s, d), F32), jax.ShapeDtypeStruct((s, d), BF16),
                   jax.ShapeDtypeStruct((1, d), F32), jax.ShapeDtypeStruct((1, d), F32)],
        compiler_params=_cparams(("arbitrary",)), name=name)(dy, r, g.reshape(1, d))
    return dr, drb, dg[0], db[0]


def _loss_head(y, tgt, name):
    s, d = y.shape
    br = _pick(s, (256, 128, 64, 8))

    def body(y_ref, t_ref, dy_ref, l_ref):
        @pl.when(pl.program_id(0) == 0)
        def _():
            l_ref[...] = jnp.zeros_like(l_ref)

        e = y_ref[...] - t_ref[...]
        dy_ref[...] = e * (1.0 / d)
        rows = jnp.sum(e * e, axis=-1, keepdims=True) * (0.5 / d)
        l_ref[...] += jnp.sum(rows, axis=0, keepdims=True)

    row = pl.BlockSpec((br, d), lambda i: (i, 0))
    dy, l = pl.pallas_call(
        body, grid=(s // br,), in_specs=[row, row],
        out_specs=[row, pl.BlockSpec((1, 1), lambda i: (0, 0))],
        out_shape=[jax.ShapeDtypeStruct((s, d), F32), jax.ShapeDtypeStruct((1, 1), F32)],
        compiler_params=_cparams(("arbitrary",)), name=name)(y, tgt)
    return dy, l[0, 0]


SUBLANES = 8


def _shifted_rows(ext_ref, sh_ref, tt):
    n = tt + HALO - SUBLANES
    for r in range(1, SUBLANES):
        sh_ref[r - 1] = ext_ref[pl.ds(r, n), :]

    def tap(o):
        r = o % SUBLANES
        if r == 0:
            return ext_ref[pl.ds(o, tt), :]
        return sh_ref[r - 1, pl.ds(o - r, tt), :]

    return tap


def _conv_scratch(tt, cc):
    return [pltpu.VMEM((tt + HALO, cc), F32), pltpu.VMEM((SUBLANES - 1, tt + HALO - SUBLANES, cc), F32)]


def _conv_recompute(i, a_ref, gt_ref, ah_ref, gh_ref, w_ref, cb_ref, hext_ref, sh_ref, tt):
    h = a_ref[...] * _sigmoid(gt_ref[...])
    hh = ah_ref[...] * _sigmoid(gh_ref[...])
    hh = jnp.where(i > 0, hh, 0.0)
    hext_ref[0:HALO, :] = hh
    hext_ref[HALO:HALO + tt, :] = h
    tap = _shifted_rows(hext_ref, sh_ref, tt)
    acc = jnp.zeros_like(h) + cb_ref[...]
    off = HALO - (CONV_WIDTH - 1)
    for j in range(CONV_WIDTH):
        acc = acc + w_ref[j:j + 1, :] * tap(off + j)
    mu = jnp.mean(acc, axis=-1, keepdims=True)
    dlt = acc - mu
    var = jnp.mean(dlt * dlt, axis=-1, keepdims=True)
    rstd = lax.rsqrt(var + LN_EPS)
    return dlt * rstd, rstd, tap


def _conv_specs(tt, cc, s):
    nh = tt // HALO
    cur = lambda cb: pl.BlockSpec((tt, cc), lambda i: (i, cb))
    prev = lambda cb: pl.BlockSpec((HALO, cc), lambda i: (jnp.maximum(i * nh - 1, 0), cb))
    vec = pl.BlockSpec((1, cc), lambda i: (0, 0))
    wsp = pl.BlockSpec((HALO, cc), lambda i: (0, 0))
    return cur, prev, vec, wsp


def _pad_conv_w(w):
    return jnp.concatenate([w, jnp.zeros((HALO - CONV_WIDTH, w.shape[1]), F32)], axis=0)


def _conv_fwd(u, w, cb, lg, lb, name):
    s = u.shape[0]
    cc = w.shape[1]
    tt = _pick(s, (256, 128, 64))
    cur, prev, vec, wsp = _conv_specs(tt, cc, s)

    def body(a_ref, gt_ref, ah_ref, gh_ref, w_ref, cb_ref, lg_ref, lb_ref, o_ref, hext_ref, sh_ref):
        xhat, _, _ = _conv_recompute(pl.program_id(0), a_ref, gt_ref, ah_ref, gh_ref, w_ref, cb_ref, hext_ref,
                                     sh_ref, tt)
        nrm = xhat * lg_ref[...] + lb_ref[...]
        o_ref[...] = nrm * _sigmoid(nrm)

    return pl.pallas_call(
        body, grid=(s // tt,), in_specs=[cur(0), cur(1), prev(0), prev(1), wsp, vec, vec, vec],
        out_specs=pl.BlockSpec((tt, cc), lambda i: (i, 0)), out_shape=jax.ShapeDtypeStruct((s, cc), F32),
        scratch_shapes=_conv_scratch(tt, cc),
        compiler_params=_cparams(("parallel",)), name=name)(
            u, u, u, u, _pad_conv_w(w), cb.reshape(1, cc), lg.reshape(1, cc), lb.reshape(1, cc))


def _conv_bwd_params(u, dout, w, cb, lg, lb, name):
    s = u.shape[0]
    cc = w.shape[1]
    tt = _pick(s, (256, 128, 64))
    cur, prev, vec, wsp = _conv_specs(tt, cc, s)

    def body(a_ref, gt_ref, ah_ref, gh_ref, w_ref, cb_ref, lg_ref, lb_ref, do_ref,
             dc_ref, dw_ref, dcb_ref, dlg_ref, dlb_ref, hext_ref, sh_ref, dw_acc):
        i = pl.program_id(0)

        @pl.when(i == 0)
        def _():
            dw_acc[...] = jnp.zeros_like(dw_acc)
            dcb_ref[...] = jnp.zeros_like(dcb_ref)
            dlg_ref[...] = jnp.zeros_like(dlg_ref)
            dlb_ref[...] = jnp.zeros_like(dlb_ref)

        xhat, rstd, tap = _conv_recompute(i, a_ref, gt_ref, ah_ref, gh_ref, w_ref, cb_ref, hext_ref, sh_ref, tt)
        nrm = xhat * lg_ref[...] + lb_ref[...]
        sg = _sigmoid(nrm)
        dn = do_ref[...] * (sg * (1.0 + nrm * (1.0 - sg)))
        dxh = dn * lg_ref[...]
        m1 = jnp.mean(dxh, axis=-1, keepdims=True)
        m2 = jnp.mean(dxh * xhat, axis=-1, keepdims=True)
        dc = rstd * (dxh - m1 - xhat * m2)
        dc_ref[...] = dc
        dlg_ref[...] += jnp.sum(dn * xhat, axis=0, keepdims=True)
        dlb_ref[...] += jnp.sum(dn, axis=0, keepdims=True)
        dcb_ref[...] += jnp.sum(dc, axis=0, keepdims=True)
        off = HALO - (CONV_WIDTH - 1)
        for j in range(CONV_WIDTH):
            dw_acc[j] += jnp.sum((dc * tap(off + j)).reshape(tt // SUBLANES, SUBLANES, cc), axis=0)

        @pl.when(i == nblk - 1)
        def _():
            dw_ref[...] = jnp.zeros_like(dw_ref)
            for j in range(CONV_WIDTH):
                dw_ref[j:j + 1, :] = jnp.sum(dw_acc[j], axis=0, keepdims=True)

    nblk = s // tt
    dcol = pl.BlockSpec((tt, cc), lambda i: (i, 0))
    dc, dw, dcb, dlg, dlb = pl.pallas_call(
        body, grid=(nblk,), in_specs=[cur(0), cur(1), prev(0), prev(1), wsp, vec, vec, vec, dcol],
        out_specs=[dcol, wsp, vec, vec, vec],
        out_shape=[jax.ShapeDtypeStruct((s, cc), F32), jax.ShapeDtypeStruct((HALO, cc), F32)]
        + [jax.ShapeDtypeStruct((1, cc), F32)] * 3,
        scratch_shapes=_conv_scratch(tt, cc) + [pltpu.VMEM((HALO, SUBLANES, cc), F32)],
        compiler_params=_cparams(("arbitrary",)), name=name)(
            u, u, u, u, _pad_conv_w(w), cb.reshape(1, cc), lg.reshape(1, cc), lb.reshape(1, cc), dout)
    return dc, dw[:CONV_WIDTH], dcb[0], dlg[0], dlb[0]


def _conv_bwd_input(u, dc, w, name):
    s = u.shape[0]
    cc = w.shape[1]
    tt = _pick(s, (256, 128, 64))
    nh = tt // HALO
    nlast = s // HALO - 1
    cur = lambda cb: pl.BlockSpec((tt, cc), lambda i: (i, cb))
    nxt = pl.BlockSpec((HALO, cc), lambda i: (jnp.minimum((i + 1) * nh, nlast), 0))
    wsp = pl.BlockSpec((HALO, cc), lambda i: (0, 0))
    nblk = s // tt

    def body(a_ref, gt_ref, dc_ref, dn_ref, w_ref, da_ref, dg_ref, ext_ref, sh_ref):
        i = pl.program_id(0)
        ext_ref[0:tt, :] = dc_ref[...]
        ext_ref[tt:tt + HALO, :] = jnp.where(i < nblk - 1, dn_ref[...], 0.0)
        tap = _shifted_rows(ext_ref, sh_ref, tt)
        dh = jnp.zeros((tt, cc), F32)
        for j in range(CONV_WIDTH):
            dh = dh + w_ref[j:j + 1, :] * tap(CONV_WIDTH - 1 - j)
        a = a_ref[...]
        sg = _sigmoid(gt_ref[...])
        da_ref[...] = (dh * sg).astype(BF16)
        dg_ref[...] = (dh * a * sg * (1.0 - sg)).astype(BF16)

    ocol = pl.BlockSpec((tt, cc), lambda i: (i, 0))
    return pl.pallas_call(
        body, grid=(nblk,), in_specs=[cur(0), cur(1), ocol, nxt, wsp], out_specs=[ocol, ocol],
        out_shape=[jax.ShapeDtypeStruct((s, cc), BF16)] * 2,
        scratch_shapes=_conv_scratch(tt, cc),
        compiler_params=_cparams(("parallel",)), name=name)(u, u, dc, dc, _pad_conv_w(w))


def _hgrn_gates(hq, hf, lb):
    sg = _sigmoid(hf)
    f = lb + (1.0 - lb) * sg
    lf = jnp.log(f)
    big_l = _tri_matmul(_tril(CHUNK), lf, 3)
    l_end = jnp.sum(lf, axis=0, keepdims=True)
    l_mid = jnp.sum(lf[0:CHUNK // 2, :], axis=0, keepdims=True)
    sq = _sigmoid(hq)
    q = hq * sq
    return sg, f, 1.0 - f, big_l, l_end, l_mid, sq, q


def _causal_mask(n):
    r = lax.broadcasted_iota(jnp.int32, (n, n), 0)
    c = lax.broadcasted_iota(jnp.int32, (n, n), 1)
    return c <= r


def _hgrn_fwd(u, lb, gg, comm, name):
    s = u.shape[0]
    w = HEADS * DH
    nch = s // CHUNK
    n_comm = len(comm)
    kinds = [c[0] for c in comm]

    def body(*refs):
        q_ref, f_ref, i_ref, g_ref, lb_ref, gg_ref = refs[:6]
        c_in = refs[6:6 + n_comm]
        o_ref, out_ref, st_ref = refs[6 + n_comm:9 + n_comm]
        c_out = refs[9 + n_comm:9 + 2 * n_comm]
        state = refs[9 + 2 * n_comm]
        sems = refs[10 + 2 * n_comm:]

        @pl.when(pl.program_id(0) == 0)
        def _():
            state[...] = jnp.zeros_like(state)
            _exchange(kinds, c_in, c_out, *sems, start=True)

        mask = _causal_mask(CHUNK)
        for hd in range(HEADS):
            sl = slice(hd * DH, (hd + 1) * DH)
            _, _, kk, big_l, l_end, l_mid, _, q = _hgrn_gates(q_ref[:, sl], f_ref[:, sl], lb_ref[:, sl])
            v = i_ref[:, sl]
            qs = q * jnp.exp(big_l - l_mid)
            ks = kk * jnp.exp(l_mid - big_l)
            att = jnp.where(mask, _nt3(qs, ks), 0.0)
            st0 = state[hd]
            st_ref[0, hd] = st0
            o = _nn3(att, v) + _nt3(q * jnp.exp(big_l), st0)
            state[hd] = st0 * jnp.exp(l_end) + _tn3(v, kk * jnp.exp(l_end - big_l))
            o_ref[:, sl] = o
            on = o * lax.rsqrt(jnp.mean(o * o, axis=-1, keepdims=True) + LN_EPS)
            gv = g_ref[:, sl]
            out_ref[:, sl] = on * gg_ref[:, sl] * (gv * _sigmoid(gv))

        @pl.when(pl.program_id(0) == nch - 1)
        def _():
            _exchange(kinds, c_in, c_out, *sems, start=False)

    col = lambda cb: pl.BlockSpec((CHUNK, w), lambda i: (i, cb))
    vec = pl.BlockSpec((1, w), lambda i: (0, 0))
    ocol = pl.BlockSpec((CHUNK, w), lambda i: (i, 0))
    c_shapes = [jax.ShapeDtypeStruct((4,) + c[1].shape if c[0] == "ag" else c[1].shape, c[1].dtype) for c in comm]
    res = pl.pallas_call(
        body, grid=(nch,), in_specs=[col(2), col(3), col(4), col(5), vec, vec] + [_ANY] * n_comm,
        out_specs=[ocol, ocol, pl.BlockSpec((1, HEADS, DH, DH), lambda i: (i, 0, 0, 0))] + [_ANY] * n_comm,
        out_shape=[jax.ShapeDtypeStruct((s, w), F32), jax.ShapeDtypeStruct((s, w), F32),
                   jax.ShapeDtypeStruct((nch, HEADS, DH, DH), F32)] + c_shapes,
        scratch_shapes=[pltpu.VMEM((HEADS, DH, DH), F32), pltpu.SemaphoreType.DMA((3 * n_comm,)),
                        pltpu.SemaphoreType.DMA((3 * n_comm,)), pltpu.SemaphoreType.DMA((n_comm,))],
        compiler_params=_cparams(("arbitrary",)), name=name)(
            u, u, u, u, lb.reshape(1, w), gg.reshape(1, w), *[c[1] for c in comm])
    return res[0], res[1], res[2], list(res[3:])


def _hgrn_bwd(u, o_raw, states, dout, lb, gg, name):
    s = u.shape[0]
    w = HEADS * DH
    nch = s // CHUNK

    def body(q_ref, f_ref, i_ref, g_ref, o_ref, st_ref, do_ref, lb_ref, gg_ref,
             dq_ref, df_ref, di_ref, dg_ref, dlb_ref, dgg_ref, dstate):
        @pl.when(pl.program_id(0) == 0)
        def _():
            dstate[...] = jnp.zeros_like(dstate)
            dlb_ref[...] = jnp.zeros_like(dlb_ref)
            dgg_ref[...] = jnp.zeros_like(dgg_ref)

        mask = _causal_mask(CHUNK)
        last_row = lax.broadcasted_iota(jnp.int32, (CHUNK, DH), 0) == CHUNK - 1
        tri_up = _tril(CHUNK, upper=True)
        for hd in range(HEADS):
            sl = slice(hd * DH, (hd + 1) * DH)
            hq = q_ref[:, sl]
            lbv = lb_ref[:, sl]
            sg, f, kk, big_l, l_end, l_mid, sq, q = _hgrn_gates(hq, f_ref[:, sl], lbv)
            v = i_ref[:, sl]
            e_l = jnp.exp(big_l)
            e_qm = jnp.exp(big_l - l_mid)
            e_km = jnp.exp(l_mid - big_l)
            e_ke = jnp.exp(l_end - big_l)
            e_end = jnp.exp(l_end)
            qs = q * e_qm
            ks = kk * e_km
            qe = q * e_l
            ke = kk * e_ke
            att = jnp.where(mask, _nt3(qs, ks), 0.0)
            st0 = st_ref[0, hd]
            dst1 = dstate[hd]
            o = o_ref[:, sl]
            rinv = lax.rsqrt(jnp.mean(o * o, axis=-1, keepdims=True) + LN_EPS)
            on = o * rinv
            gv = g_ref[:, sl]
            sgg = _sigmoid(gv)
            gsil = gv * sgg
            ggv = gg_ref[:, sl]
            dov = do_ref[:, sl]
            don = dov * ggv * gsil
            dg_ref[:, sl] = (dov * on * ggv * (sgg * (1.0 + gv * (1.0 - sgg)))).astype(BF16)
            dgg_ref[:, sl] += jnp.sum(dov * on * gsil, axis=0, keepdims=True)
            do = rinv * (don - on * jnp.mean(don * on, axis=-1, keepdims=True))
            datt = jnp.where(mask, _nt3(do, v), 0.0)
            dv = _tn3(att, do) + _nt3(ke, dst1)
            dqs = _nn3(datt, ks)
            dks = _tn3(datt, qs)
            dqe = _nn3(do, st0)
            dke = _nn3(v, dst1)
            dq = dqs * e_qm + dqe * e_l
            dk = dks * e_km + dke * e_ke
            dke_ke = dke * ke
            dl = dqs * qs - dks * ks + dqe * qe - dke_ke
            dl_end = jnp.sum(dke_ke, axis=0, keepdims=True) + jnp.sum(dst1 * st0, axis=0, keepdims=True) * e_end
            dl = dl + jnp.where(last_row, dl_end, 0.0)
            dlf = _tri_matmul(tri_up, dl, 2)
            dfv = dlf / f - dk
            df_ref[:, sl] = (dfv * (1.0 - lbv) * sg * (1.0 - sg)).astype(BF16)
            dlb_ref[:, sl] += jnp.sum(dfv * (1.0 - sg), axis=0, keepdims=True)
            dq_ref[:, sl] = (dq * (sq * (1.0 + hq * (1.0 - sq)))).astype(BF16)
            di_ref[:, sl] = dv.astype(BF16)
            dstate[hd] = dst1 * e_end + _tn3(do, qe)

    rev = lambda i: nch - 1 - i
    col = lambda cb: pl.BlockSpec((CHUNK, w), lambda i: (rev(i), cb))
    vec = pl.BlockSpec((1, w), lambda i: (0, 0))
    ocol = pl.BlockSpec((CHUNK, w), lambda i: (rev(i), 0))
    res = pl.pallas_call(
        body, grid=(nch,),
        in_specs=[col(2), col(3), col(4), col(5), ocol,
                  pl.BlockSpec((1, HEADS, DH, DH), lambda i: (rev(i), 0, 0, 0)), col(1), vec, vec],
        out_specs=[ocol, ocol, ocol, ocol, vec, vec],
        out_shape=[jax.ShapeDtypeStruct((s, w), BF16)] * 4 + [jax.ShapeDtypeStruct((1, w), F32)] * 2,
        scratch_shapes=[pltpu.VMEM((HEADS, DH, DH), F32)],
        compiler_params=_cparams(("arbitrary",)), name=name)(
            u, u, u, u, o_raw, states, dout, lb.reshape(1, w), gg.reshape(1, w))
    return res[0], res[1], res[2], res[3], res[4][0], res[5][0]


def _log_sigmoid(x):
    return jnp.minimum(x, 0.0) - jnp.log(1.0 + jnp.exp(-jnp.abs(x)))


def _fgate_fwd(cf, bf, name):
    s, wl = cf.shape
    tb = _pick(s, (512, 256, 128, 64))

    def body(c_ref, b_ref, f_ref, carry):
        @pl.when(pl.program_id(0) == 0)
        def _():
            carry[...] = jnp.zeros_like(carry)

        ls = _log_sigmoid(c_ref[...] + b_ref[...])
        f_ref[...] = _tri_matmul(_tril(tb), ls, 3) + carry[...]
        carry[...] += jnp.sum(ls, axis=0, keepdims=True)

    return pl.pallas_call(
        body, grid=(s // tb,), in_specs=[pl.BlockSpec((tb, wl), lambda i: (i, 0)), pl.BlockSpec((1, wl), lambda i: (0, 0))],
        out_specs=pl.BlockSpec((tb, wl), lambda i: (i, 0)), out_shape=jax.ShapeDtypeStruct((s, wl), F32),
        scratch_shapes=[pltpu.VMEM((1, wl), F32)],
        compiler_params=_cparams(("arbitrary",)), name=name)(cf, bf)


def _fgate_bwd(dF, cf, bf, name):
    s, wl = cf.shape
    tb = _pick(s, (512, 256, 128, 64))
    nb = s // tb

    def body(d_ref, c_ref, b_ref, dc_ref, db_ref, carry):
        @pl.when(pl.program_id(0) == 0)
        def _():
            carry[...] = jnp.zeros_like(carry)
            db_ref[...] = jnp.zeros_like(db_ref)

        dv = d_ref[...]
        dls = _tri_matmul(_tril(tb, upper=True), dv, 3) + carry[...]
        carry[...] += jnp.sum(dv, axis=0, keepdims=True)
        dc = dls * (1.0 - _sigmoid(c_ref[...] + b_ref[...]))
        dc_ref[...] = dc.astype(BF16)
        db_ref[...] += jnp.sum(dc, axis=0, keepdims=True)

    blk = pl.BlockSpec((tb, wl), lambda i: (nb - 1 - i, 0))
    vec = pl.BlockSpec((1, wl), lambda i: (0, 0))
    dc, db = pl.pallas_call(
        body, grid=(nb,), in_specs=[blk, blk, vec], out_specs=[blk, vec],
        out_shape=[jax.ShapeDtypeStruct((s, wl), BF16), jax.ShapeDtypeStruct((1, wl), F32)],
        scratch_shapes=[pltpu.VMEM((1, wl), F32)],
        compiler_params=_cparams(("arbitrary",)), name=name)(dF, cf, bf)
    return dc, db[0]


FOX_HPB = 2


def _fox_scores_t(q, k, fq_row, fk_col, diag):
    sc = _nt(k, q) * (DH ** -0.5) + fq_row - fk_col
    if not diag:
        return sc
    r = lax.broadcasted_iota(jnp.int32, sc.shape, 0)
    c = lax.broadcasted_iota(jnp.int32, sc.shape, 1)
    return jnp.where(r <= c, sc, NEG)


def _fox_probs_t(q, k, fq_row, fk_col, lse_row, diag):
    return jnp.exp(_fox_scores_t(q, k, fq_row, fk_col, diag) - lse_row)


def _fox_delta_row(do, o):
    prod = do * o
    hi = prod.astype(BF16)
    lo = (prod - hi.astype(F32)).astype(BF16)
    ones = jnp.ones((SUBLANES, DH), BF16)
    dims = (((1,), (1,)), ((), ()))
    return (lax.dot_general(ones, hi, dims, preferred_element_type=F32)
            + lax.dot_general(ones, lo, dims, preferred_element_type=F32))[0:1, :]


def _fox_when_needed(q_blk, k_blk, step):
    @pl.when(k_blk < q_blk)
    def _():
        step(False)

    @pl.when(k_blk == q_blk)
    def _():
        step(True)


def _fox_fwd(u, f_col, f_row, name):
    s = u.shape[0]
    t = _pick(s, (512, 256, 128))
    nb = s // t
    hpb = FOX_HPB
    wb = hpb * DH
    ng = HEADS // hpb

    def body(q_ref, k_ref, v_ref, fq_ref, fk_ref, o_ref, lse_ref, m_sc, l_sc, acc):
        i, j = pl.program_id(1), pl.program_id(2)

        @pl.when(j == 0)
        def _():
            m_sc[...] = jnp.full_like(m_sc, NEG)
            l_sc[...] = jnp.zeros_like(l_sc)
            acc[...] = jnp.zeros_like(acc)

        def step(diag):
            for hh in range(hpb):
                sl = slice(hh * DH, (hh + 1) * DH)
                sc = _fox_scores_t(q_ref[:, sl], k_ref[:, sl], fq_ref[hh], fk_ref[hh], diag)
                m_new = jnp.maximum(m_sc[hh], jnp.max(sc, axis=0, keepdims=True))
                a = jnp.exp(m_sc[hh] - m_new)
                p = jnp.exp(sc - m_new)
                l_sc[hh] = a * l_sc[hh] + jnp.sum(p, axis=0, keepdims=True)
                acc[sl, :] = a * acc[sl, :] + _tn(v_ref[:, sl], p)
                m_sc[hh] = m_new

        _fox_when_needed(i, j, step)

        @pl.when(j == i)
        def _():
            for hh in range(hpb):
                sl = slice(hh * DH, (hh + 1) * DH)
                o_ref[:, sl] = jnp.transpose(acc[sl, :] / l_sc[hh])
                lse_ref[hh] = m_sc[hh] + jnp.log(l_sc[hh])

    qs = pl.BlockSpec((t, wb), lambda h, i, j: (i, h))
    ks = lambda base: pl.BlockSpec((t, wb), lambda h, i, j: (jnp.minimum(j, i), base + h))
    return pl.pallas_call(
        body, grid=(ng, nb, nb),
        in_specs=[qs, ks(ng), ks(2 * ng),
                  pl.BlockSpec((hpb, 1, t), lambda h, i, j: (h, 0, i)),
                  pl.BlockSpec((hpb, t, 1), lambda h, i, j: (h, jnp.minimum(j, i), 0))],
        out_specs=[pl.BlockSpec((t, wb), lambda h, i, j: (i, h)), pl.BlockSpec((hpb, 1, t), lambda h, i, j: (h, 0, i))],
        out_shape=[jax.ShapeDtypeStruct((s, HEADS * DH), F32), jax.ShapeDtypeStruct((HEADS, 1, s), F32)],
        scratch_shapes=[pltpu.VMEM((hpb, 1, t), F32), pltpu.VMEM((hpb, 1, t), F32), pltpu.VMEM((wb, t), F32)],
        compiler_params=_cparams(("parallel", "parallel", "arbitrary")), name=name)(u, u, u, f_row, f_col)


def _fox_delta(o, dout, name):
    s = o.shape[0]
    t = _pick(s, (512, 256, 128))
    hpb = FOX_HPB
    wb = hpb * DH

    def body(o_ref, do_ref, d_ref):
        for hh in range(hpb):
            sl = slice(hh * DH, (hh + 1) * DH)
            d_ref[hh] = _fox_delta_row(do_ref[:, sl], o_ref[:, sl])

    blk = pl.BlockSpec((t, wb), lambda h, i: (i, h))
    return pl.pallas_call(
        body, grid=(HEADS // hpb, s // t), in_specs=[blk, blk],
        out_specs=pl.BlockSpec((hpb, 1, t), lambda h, i: (h, 0, i)),
        out_shape=jax.ShapeDtypeStruct((HEADS, 1, s), F32),
        compiler_params=_cparams(("parallel", "parallel")), name=name)(o, dout)


def _fox_bwd_kv(u, delta, lse_row, dout, f_col, f_row, name):
    s = u.shape[0]
    t = _pick(s, (512, 256, 128))
    nb = s // t
    hpb = FOX_HPB
    wb = hpb * DH
    ng = HEADS // hpb

    def body(q_ref, k_ref, v_ref, dl_ref, do_ref, lse_ref, fq_ref, fk_ref, dk_ref, dv_ref, dfk_ref, dk_acc, dv_acc, df_acc):
        j, i = pl.program_id(1), pl.program_id(2)

        @pl.when(i == 0)
        def _():
            dk_acc[...] = jnp.zeros_like(dk_acc)
            dv_acc[...] = jnp.zeros_like(dv_acc)
            df_acc[...] = jnp.zeros_like(df_acc)

        def step(diag):
            for hh in range(hpb):
                sl = slice(hh * DH, (hh + 1) * DH)
                q = q_ref[:, sl]
                do = do_ref[:, sl]
                p = _fox_probs_t(q, k_ref[:, sl], fq_ref[hh], fk_ref[hh], lse_ref[hh], diag)
                dv_acc[:, sl] += _nn(p, do)
                ds = p * (_nt(v_ref[:, sl], do) - dl_ref[hh])
                dk_acc[:, sl] += _nn(ds, q) * (DH ** -0.5)
                df_acc[hh] -= jnp.sum(ds, axis=1, keepdims=True)

        _fox_when_needed(i, j, step)

        @pl.when(i == nb - 1)
        def _():
            dk_ref[...] = dk_acc[...].astype(BF16)
            dv_ref[...] = dv_acc[...].astype(BF16)
            dfk_ref[...] = df_acc[...]

    qi = lambda j, i: jnp.maximum(i, j)
    qs = lambda base: pl.BlockSpec((t, wb), lambda h, j, i: (qi(j, i), base + h))
    ks = lambda base: pl.BlockSpec((t, wb), lambda h, j, i: (j, base + h))
    qrow = pl.BlockSpec((hpb, 1, t), lambda h, j, i: (h, 0, qi(j, i)))
    return pl.pallas_call(
        body, grid=(ng, nb, nb),
        in_specs=[qs(0), ks(ng), ks(2 * ng), qrow, qs(0), qrow, qrow,
                  pl.BlockSpec((hpb, t, 1), lambda h, j, i: (h, j, 0))],
        out_specs=[pl.BlockSpec((t, wb), lambda h, j, i: (j, h)), pl.BlockSpec((t, wb), lambda h, j, i: (j, h)),
                   pl.BlockSpec((hpb, t, 1), lambda h, j, i: (h, j, 0))],
        out_shape=[jax.ShapeDtypeStruct((s, HEADS * DH), BF16)] * 2 + [jax.ShapeDtypeStruct((HEADS, s, 1), F32)],
        scratch_shapes=[pltpu.VMEM((t, wb), F32), pltpu.VMEM((t, wb), F32), pltpu.VMEM((hpb, t, 1), F32)],
        compiler_params=_cparams(("parallel", "parallel", "arbitrary")), name=name)(
            u, u, u, delta, dout, lse_row, f_row, f_col)


def _fox_bwd_q(u, delta, lse_row, dout, f_col, f_row, name):
    s = u.shape[0]
    t = _pick(s, (512, 256, 128))
    nb = s // t
    hpb = FOX_HPB
    wb = hpb * DH
    ng = HEADS // hpb

    def body(q_ref, k_ref, v_ref, dl_ref, do_ref, lse_ref, fq_ref, fk_ref, dq_ref, dfq_ref, dq_acc, df_acc):
        i, j = pl.program_id(1), pl.program_id(2)

        @pl.when(j == 0)
        def _():
            dq_acc[...] = jnp.zeros_like(dq_acc)
            df_acc[...] = jnp.zeros_like(df_acc)

        def step(diag):
            for hh in range(hpb):
                sl = slice(hh * DH, (hh + 1) * DH)
                do = do_ref[:, sl]
                k = k_ref[:, sl]
                p = _fox_probs_t(q_ref[:, sl], k, fq_ref[hh], fk_ref[hh], lse_ref[hh], diag)
                ds = p * (_nt(v_ref[:, sl], do) - dl_ref[hh])
                dq_acc[sl, :] += _tn(k, ds) * (DH ** -0.5)
                df_acc[hh] += jnp.sum(ds, axis=0, keepdims=True)

        _fox_when_needed(i, j, step)

        @pl.when(j == i)
        def _():
            for hh in range(hpb):
                sl = slice(hh * DH, (hh + 1) * DH)
                dq_ref[:, sl] = jnp.transpose(dq_acc[sl, :]).astype(BF16)
            dfq_ref[...] = df_acc[...]

    qs = lambda base: pl.BlockSpec((t, wb), lambda h, i, j: (i, base + h))
    ks = lambda base: pl.BlockSpec((t, wb), lambda h, i, j: (jnp.minimum(j, i), base + h))
    row = pl.BlockSpec((hpb, 1, t), lambda h, i, j: (h, 0, i))
    return pl.pallas_call(
        body, grid=(ng, nb, nb),
        in_specs=[qs(0), ks(ng), ks(2 * ng), row, qs(0), row, row,
                  pl.BlockSpec((hpb, t, 1), lambda h, i, j: (h, jnp.minimum(j, i), 0))],
        out_specs=[pl.BlockSpec((t, wb), lambda h, i, j: (i, h)), row],
        out_shape=[jax.ShapeDtypeStruct((s, HEADS * DH), BF16), jax.ShapeDtypeStruct((HEADS, 1, s), F32)],
        scratch_shapes=[pltpu.VMEM((wb, t), F32), pltpu.VMEM((hpb, 1, t), F32)],
        compiler_params=_cparams(("parallel", "parallel", "arbitrary")), name=name)(
            u, u, u, delta, dout, lse_row, f_row, f_col)


def _ca_bias(rel_bias):
    nh = rel_bias.shape[0]
    n_clip = CA_BAND - REL_CLIP
    gv = jnp.concatenate([jnp.broadcast_to(rel_bias[:, REL_TABLE - 1:], (nh, n_clip)),
                          rel_bias[:, REL_TABLE - 2::-1]], axis=1)
    b = jnp.stack([gv[:, CHUNK - 1 - qi:CHUNK - 1 - qi + CA_BAND] for qi in range(CHUNK)], axis=1)
    neg = lambda n: jnp.full((nh, CHUNK, n), NEG, F32)
    blocks = [jnp.concatenate([neg(a * CHUNK), b, neg(CA_WIN - CA_BAND - a * CHUNK)], axis=2)
              for a in range(CA_TILE // CHUNK)]
    return jnp.concatenate(blocks, axis=1)


def _ca_bias_grad(db):
    nh = db.shape[0]
    n_clip = CA_BAND - REL_CLIP
    d64 = sum(db[:, a * CHUNK:(a + 1) * CHUNK, a * CHUNK:a * CHUNK + CA_BAND] for a in range(CA_TILE // CHUNK))
    wide = jnp.pad(d64[:, ::-1, :], ((0, 0), (0, 0), (0, CHUNK))).reshape(nh, CHUNK * (CA_BAND + CHUNK))
    n_gv = CA_BAND + CHUNK - 1
    dgv = jnp.sum(wide[:, :CHUNK * n_gv].reshape(nh, CHUNK, n_gv), axis=1)
    return jnp.concatenate([dgv[:, n_clip:][:, ::-1], jnp.sum(dgv[:, :n_clip], axis=1, keepdims=True)], axis=1)


def _ca_specs():
    cur = lambda base: pl.BlockSpec((CA_BLK, DH), lambda h, i: (i, base + h))
    prev = lambda base: pl.BlockSpec((CA_BLK, DH), lambda h, i: (jnp.maximum(i - 1, 0), base + h))
    bias = pl.BlockSpec((1, CA_TILE, CA_WIN), lambda h, i: (h, 0, 0))
    return cur, prev, bias


def _ca_fill(kcat, vcat, kp_ref, kc_ref, vp_ref, vc_ref):
    kcat[0:CA_BLK, :] = kp_ref[...]
    kcat[CA_BLK:2 * CA_BLK, :] = kc_ref[...]
    vcat[0:CA_BLK, :] = vp_ref[...]
    vcat[CA_BLK:2 * CA_BLK, :] = vc_ref[...]


def _ca_probs(i, tl, q, kw, bias):
    sc = _nt(q, kw) * (DH ** -0.5) + bias
    col = lax.broadcasted_iota(jnp.int32, sc.shape, 1)
    sc = jnp.where((i - 1) * CA_BLK + tl * CA_TILE + col >= 0, sc, NEG)
    p = jnp.exp(sc - jnp.max(sc, axis=-1, keepdims=True))
    return p / jnp.sum(p, axis=-1, keepdims=True)


def _ca_fwd(u, bias, name):
    s = u.shape[0]
    cur, prev, bsp = _ca_specs()

    def body(q_ref, kp_ref, kc_ref, vp_ref, vc_ref, b_ref, o_ref, kcat, vcat):
        i = pl.program_id(1)
        _ca_fill(kcat, vcat, kp_ref, kc_ref, vp_ref, vc_ref)
        for tl in range(CA_BLK // CA_TILE):
            rows = slice(tl * CA_TILE, (tl + 1) * CA_TILE)
            win = slice(tl * CA_TILE, tl * CA_TILE + CA_WIN)
            p = _ca_probs(i, tl, q_ref[rows, :], kcat[win, :], b_ref[0])
            o_ref[rows, :] = _nn(p, vcat[win, :])

    return pl.pallas_call(
        body, grid=(HEADS, s // CA_BLK),
        in_specs=[cur(3 * HEADS), prev(4 * HEADS), cur(4 * HEADS), prev(5 * HEADS), cur(5 * HEADS), bsp],
        out_specs=pl.BlockSpec((CA_BLK, DH), lambda h, i: (i, h)),
        out_shape=jax.ShapeDtypeStruct((s, HEADS * DH), F32),
        scratch_shapes=[pltpu.VMEM((2 * CA_BLK, DH), F32)] * 2,
        compiler_params=_cparams(("parallel", "parallel")), name=name)(u, u, u, u, u, bias)


def _ca_bwd(u, bias, dout, name):
    s = u.shape[0]
    cur, prev, bsp = _ca_specs()
    rows_cat = 2 * CA_BLK

    def body(q_ref, kp_ref, kc_ref, vp_ref, vc_ref, b_ref, do_ref,
             dq_ref, dka_ref, dkb_ref, dva_ref, dvb_ref, db_ref, kcat, vcat, dkcat, dvcat):
        i = pl.program_id(1)

        @pl.when(i == 0)
        def _():
            db_ref[...] = jnp.zeros_like(db_ref)

        _ca_fill(kcat, vcat, kp_ref, kc_ref, vp_ref, vc_ref)
        dkcat[...] = jnp.zeros_like(dkcat)
        dvcat[...] = jnp.zeros_like(dvcat)
        for tl in range(CA_BLK // CA_TILE):
            rows = slice(tl * CA_TILE, (tl + 1) * CA_TILE)
            win = slice(tl * CA_TILE, tl * CA_TILE + CA_WIN)
            q = q_ref[rows, :]
            kw = kcat[win, :]
            vw = vcat[win, :]
            do = do_ref[rows, :]
            p = _ca_probs(i, tl, q, kw, b_ref[0])
            dp = _nt(do, vw)
            ds = p * (dp - jnp.sum(p * dp, axis=-1, keepdims=True))
            dq_ref[rows, :] = (_nn(ds, kw) * (DH ** -0.5)).astype(BF16)
            dkcat[win, :] += _tn(ds, q) * (DH ** -0.5)
            dvcat[win, :] += _tn(p, do)
            db_ref[0] += ds
        dkb_ref[...] = dkcat[0:CA_BLK, :]
        dka_ref[...] = dkcat[CA_BLK:2 * CA_BLK, :]
        dvb_ref[...] = dvcat[0:CA_BLK, :]
        dva_ref[...] = dvcat[CA_BLK:2 * CA_BLK, :]

    osp = pl.BlockSpec((CA_BLK, DH), lambda h, i: (i, h))
    shp = jax.ShapeDtypeStruct((s, HEADS * DH), F32)
    return pl.pallas_call(
        body, grid=(HEADS, s // CA_BLK),
        in_specs=[cur(3 * HEADS), prev(4 * HEADS), cur(4 * HEADS), prev(5 * HEADS), cur(5 * HEADS), bsp,
                  pl.BlockSpec((CA_BLK, DH), lambda h, i: (i, HEADS + h))],
        out_specs=[osp, osp, osp, osp, osp, bsp],
        out_shape=[jax.ShapeDtypeStruct((s, HEADS * DH), BF16), shp, shp, shp, shp,
                   jax.ShapeDtypeStruct((HEADS, CA_TILE, CA_WIN), F32)],
        scratch_shapes=[pltpu.VMEM((rows_cat, DH), F32)] * 4,
        compiler_params=_cparams(("parallel", "arbitrary")), name=name)(u, u, u, u, u, bias, dout)


def _ca_merge(da, db):
    shifted = jnp.concatenate([db[CA_BLK:], jnp.zeros((CA_BLK, db.shape[1]), F32)], axis=0)
    return (da + shifted).astype(BF16)


def _place():
    x, y, c = lax.axis_index("x"), lax.axis_index("y"), lax.axis_index("c")
    return x, y, c, [(1 - x, y), (x, 1 - y), (1 - x, 1 - y)]


_ANY = pl.BlockSpec(memory_space=pl.ANY)


def _all_gather_chips(w, name):
    def body(w_ref, o_ref, send_sems, recv_sems, loc_sems):
        _exchange(["ag"], [w_ref], [o_ref], send_sems, recv_sems, loc_sems, start=True)
        _exchange(["ag"], [w_ref], [o_ref], send_sems, recv_sems, loc_sems, start=False)

    return pl.pallas_call(
        body, in_specs=[_ANY], out_specs=_ANY, out_shape=jax.ShapeDtypeStruct((4,) + w.shape, w.dtype),
        scratch_shapes=[pltpu.SemaphoreType.DMA((3,)), pltpu.SemaphoreType.DMA((3,)), pltpu.SemaphoreType.DMA((1,))],
        name=name)(w)


def _exchange(kinds, srcs, dsts, send_sems, recv_sems, loc_sems, start):
    x, y, c, peers = _place()
    me = 2 * x + y
    for n, kind in enumerate(kinds):
        src, dst = srcs[n], dsts[n]
        mine = src if kind == "ag" else src.at[me]
        loc = pltpu.make_async_copy(mine, dst.at[me], loc_sems.at[n])
        copies = []
        for k, (px, py) in enumerate(peers):
            out_src = src if kind == "ag" else src.at[2 * px + py]
            send = pltpu.make_async_remote_copy(src_ref=out_src, dst_ref=dst.at[me], send_sem=send_sems.at[3 * n + k],
                                                recv_sem=recv_sems.at[3 * n + k], device_id=(px, py, c),
                                                device_id_type=MESH)
            recv = pltpu.make_async_remote_copy(src_ref=mine, dst_ref=dst.at[2 * px + py],
                                                send_sem=send_sems.at[3 * n + k], recv_sem=recv_sems.at[3 * n + k],
                                                device_id=(px, py, c), device_id_type=MESH)
            copies.append((send, recv))
        if start:
            loc.start()
            for send, _ in copies:
                send.start()
        else:
            for _, recv in copies:
                recv.wait_recv()
            for send, _ in copies:
                send.wait_send()
            loc.wait()


def _core_swap(a, name):
    def body(a_ref, o_ref, send_sem, recv_sem):
        x, y, c, _ = _place()
        cp = pltpu.make_async_remote_copy(src_ref=a_ref, dst_ref=o_ref, send_sem=send_sem, recv_sem=recv_sem,
                                          device_id=(x, y, 1 - c), device_id_type=MESH)
        cp.start()
        cp.wait()

    return pl.pallas_call(
        body, in_specs=[_ANY], out_specs=_ANY, out_shape=jax.ShapeDtypeStruct(a.shape, a.dtype),
        scratch_shapes=[pltpu.SemaphoreType.DMA(()), pltpu.SemaphoreType.DMA(())], name=name)(a)


def _all_reduce_small(v, name):
    r, wl = v.shape

    def body(v_ref, o_ref, buf, send_sems, recv_sems):
        x, y, c, _ = _place()
        me = 4 * x + 2 * y + c
        buf[me] = v_ref[...]
        flips = [(fx, fy, fc) for fx in (0, 1) for fy in (0, 1) for fc in (0, 1) if (fx, fy, fc) != (0, 0, 0)]
        peer = lambda f: (x ^ f[0], y ^ f[1], c ^ f[2])
        sends = []
        for k, f in enumerate(flips):
            cp = pltpu.make_async_remote_copy(src_ref=v_ref, dst_ref=buf.at[me], send_sem=send_sems.at[k],
                                              recv_sem=recv_sems.at[k], device_id=peer(f), device_id_type=MESH)
            cp.start()
            sends.append(cp)
        for k, f in enumerate(flips):
            px, py, pc = peer(f)
            pltpu.make_async_remote_copy(src_ref=v_ref, dst_ref=buf.at[4 * px + 2 * py + pc], send_sem=send_sems.at[k],
                                         recv_sem=recv_sems.at[k], device_id=peer(f), device_id_type=MESH).wait_recv()
        for cp in sends:
            cp.wait_send()
        acc = buf[0]
        for d in range(1, 8):
            acc = acc + buf[d]
        o_ref[...] = acc

    vm = pl.BlockSpec(memory_space=pltpu.VMEM)
    return pl.pallas_call(
        body, in_specs=[vm], out_specs=vm, out_shape=jax.ShapeDtypeStruct((r, wl), F32),
        scratch_shapes=[pltpu.VMEM((8, r, wl), F32), pltpu.SemaphoreType.DMA((7,)), pltpu.SemaphoreType.DMA((7,))],
        name=name)(v)


def _sum_slots(g, name):
    _, r, cdim = g.shape
    br = _pick(r, (256, 128, 64, 32, 8))

    def body(g_ref, o_ref):
        o_ref[...] = ((g_ref[0].astype(F32) + g_ref[1].astype(F32)) + g_ref[2].astype(F32)) + g_ref[3].astype(F32)

    return pl.pallas_call(
        body, grid=(r // br,), in_specs=[pl.BlockSpec((4, br, cdim), lambda i: (0, i, 0))],
        out_specs=pl.BlockSpec((br, cdim), lambda i: (i, 0)), out_shape=jax.ShapeDtypeStruct((r, cdim), F32),
        compiler_params=_cparams(("parallel",)), name=name)(g)


def _adamw(w, grads, m, v, name):
    nl, r, cdim = w.shape
    br = _pick(r, (128, 64, 32, 8))
    c1 = 1.0 / (1.0 - ADAM_B1 ** ADAM_STEP)
    c2 = 1.0 / (1.0 - ADAM_B2 ** ADAM_STEP)
    counts = [len(t) for t in grads]
    flat = [a for t in grads for a in t]

    def body(*refs):
        w_ref = refs[0]
        g_refs = refs[1:1 + len(flat)]
        m_ref, v_ref, g_out, d_out, m_out, v_out = refs[1 + len(flat):]
        layer = pl.program_id(0)
        g, pos = None, 0
        for l, n in enumerate(counts):
            gl = g_refs[pos][...]
            for a in g_refs[pos + 1:pos + n]:
                gl = gl + a[...]
            pos += n
            g = gl if g is None else jnp.where(layer == l, gl, g)
        mn = ADAM_B1 * m_ref[...] + (1.0 - ADAM_B1) * g
        vn = ADAM_B2 * v_ref[...] + (1.0 - ADAM_B2) * (g * g)
        g_out[...] = g
        m_out[...] = mn
        v_out[...] = vn
        d_out[...] = -ADAM_LR * ((mn * c1) / (jnp.sqrt(vn * c2) + ADAM_EPS) + ADAM_WD * w_ref[...])

    blk = pl.BlockSpec((None, br, cdim), lambda l, i: (l, i, 0))
    gblk = pl.BlockSpec((br, cdim), lambda l, i: (i, 0))
    return pl.pallas_call(
        body, grid=(nl, r // br), in_specs=[blk] + [gblk] * len(flat) + [blk, blk], out_specs=[blk] * 4,
        out_shape=[jax.ShapeDtypeStruct((nl, r, cdim), F32)] * 4,
        compiler_params=_cparams(("parallel", "parallel")), name=name)(w, *flat, m, v)


def _unshard_cols(g):
    return jnp.transpose(g, (1, 0, 2)).reshape(g.shape[1], 4 * g.shape[2])


def _unshard_rows(g):
    return g.reshape(4 * g.shape[1], g.shape[2])


def _shard_cols(g):
    k, n = g.shape
    return jnp.transpose(g.reshape(k, 4, n // 4), (1, 0, 2))


def _shard_rows(g):
    k, n = g.shape
    return g.reshape(4, k // 4, n)


def _lower_bound(logits):
    return jnp.cumsum(jax.nn.softmax(logits.astype(F32), axis=0), axis=0)[0]


def _pack(parts):
    flat = jnp.concatenate([p.reshape(-1) for p in parts])
    n = flat.shape[0]
    rows = -(-n // 1024) * 8
    return jnp.pad(flat, (0, rows * 128 - n)).reshape(rows, 128)


def _unpack(packed, shapes):
    flat = packed.reshape(-1)
    out, off = [], 0
    for shp in shapes:
        n = 1
        for d in shp:
            n *= d
        out.append(flat[off:off + n].reshape(shp))
        off += n
    return out


def kernel(x, ev_w_in, ev_conv_w, ev_conv_b, ev_conv_ln_g, ev_conv_ln_b, hgrn_lb_logits, ev_gnorm_g, ev_w_out, od_w_in, fox_b_f, rel_bias, od_w_out, ln_mix_g, ln_mix_b, mlp_w1, mlp_w2, ln_mlp_g, ln_mlp_b, loss_target, m_ev_w_in, m_ev_conv_w, m_ev_conv_b, m_ev_conv_ln_g, m_ev_conv_ln_b, m_hgrn_lb_logits, m_ev_gnorm_g, m_ev_w_out, m_od_w_in, m_fox_b_f, m_rel_bias, m_od_w_out, m_ln_mix_g, m_ln_mix_b, m_mlp_w1, m_mlp_w2, m_ln_mlp_g, m_ln_mlp_b, v_ev_w_in, v_ev_conv_w, v_ev_conv_b, v_ev_conv_ln_g, v_ev_conv_ln_b, v_hgrn_lb_logits, v_ev_gnorm_g, v_ev_w_out, v_od_w_in, v_fox_b_f, v_rel_bias, v_od_w_out, v_ln_mix_g, v_ln_mix_b, v_mlp_w1, v_mlp_w2, v_ln_mlp_g, v_ln_mlp_b):
    w_sharded = dict(ev_w_in=ev_w_in, ev_w_out=ev_w_out, od_w_in=od_w_in, od_w_out=od_w_out, mlp_w1=mlp_w1, mlp_w2=mlp_w2)
    m_sharded = dict(ev_w_in=m_ev_w_in, ev_w_out=m_ev_w_out, od_w_in=m_od_w_in, od_w_out=m_od_w_out, mlp_w1=m_mlp_w1, mlp_w2=m_mlp_w2)
    v_sharded = dict(ev_w_in=v_ev_w_in, ev_w_out=v_ev_w_out, od_w_in=v_od_w_in, od_w_out=v_od_w_out, mlp_w1=v_mlp_w1, mlp_w2=v_mlp_w2)
    small_names = ["ev_conv_w", "ev_conv_b", "ev_conv_ln_g", "ev_conv_ln_b", "hgrn_lb_logits", "ev_gnorm_g", "fox_b_f",
                   "rel_bias", "ln_mix_g", "ln_mix_b", "ln_mlp_g", "ln_mlp_b"]
    w_small = dict(ev_conv_w=ev_conv_w, ev_conv_b=ev_conv_b, ev_conv_ln_g=ev_conv_ln_g, ev_conv_ln_b=ev_conv_ln_b,
                   hgrn_lb_logits=hgrn_lb_logits, ev_gnorm_g=ev_gnorm_g, fox_b_f=fox_b_f, rel_bias=rel_bias,
                   ln_mix_g=ln_mix_g, ln_mix_b=ln_mix_b, ln_mlp_g=ln_mlp_g, ln_mlp_b=ln_mlp_b)
    m_small = dict(ev_conv_w=m_ev_conv_w, ev_conv_b=m_ev_conv_b, ev_conv_ln_g=m_ev_conv_ln_g, ev_conv_ln_b=m_ev_conv_ln_b,
                   hgrn_lb_logits=m_hgrn_lb_logits, ev_gnorm_g=m_ev_gnorm_g, fox_b_f=m_fox_b_f, rel_bias=m_rel_bias,
                   ln_mix_g=m_ln_mix_g, ln_mix_b=m_ln_mix_b, ln_mlp_g=m_ln_mlp_g, ln_mlp_b=m_ln_mlp_b)
    v_small = dict(ev_conv_w=v_ev_conv_w, ev_conv_b=v_ev_conv_b, ev_conv_ln_g=v_ev_conv_ln_g, ev_conv_ln_b=v_ev_conv_ln_b,
                   hgrn_lb_logits=v_hgrn_lb_logits, ev_gnorm_g=v_ev_gnorm_g, fox_b_f=v_fox_b_f, rel_bias=v_rel_bias,
                   ln_mix_g=v_ln_mix_g, ln_mix_b=v_ln_mix_b, ln_mlp_g=v_ln_mlp_g, ln_mlp_b=v_ln_mlp_b)

    chip = 2 * lax.axis_index("x") + lax.axis_index("y")
    hw = HEADS * DH

    w_ev_in = _all_gather_chips(ev_w_in[0].astype(BF16), "ag_ev_w_in")
    shards = dict(ev_w_out=ev_w_out[0].astype(BF16), od_w_in=od_w_in[0].astype(BF16), od_w_out=od_w_out[0].astype(BF16),
                  mlp_w1=[mlp_w1[l].astype(BF16) for l in range(DEPTH)],
                  mlp_w2=[mlp_w2[l].astype(BF16) for l in range(DEPTH)])
    tables = _all_gather_chips(jnp.concatenate(
        [ev_conv_w[0].reshape(1, -1), jnp.pad(rel_bias[0].reshape(1, -1), ((0, 0), (0, (-rel_bias[0].size) % 128)))],
        axis=1), "ag_tables")
    ncw = ev_conv_w[0].size
    cshard = ev_conv_w.shape[2]
    conv_w = jnp.transpose(tables[:, 0, :ncw].reshape(4, CONV_WIDTH, cshard), (1, 0, 2)).reshape(CONV_WIDTH, 4 * cshard)
    rshard = rel_bias.shape[2]
    rel_full = jnp.transpose(tables[:, 0, ncw:ncw + HEADS * rshard].reshape(4, HEADS, rshard), (1, 0, 2)).reshape(HEADS, 4 * rshard)

    loss_part, grad_x, g, small_partial = _local_step(
        x[0], loss_target[0], w_ev_in, shards, conv_w, rel_full, ev_conv_b, ev_conv_ln_g,
        ev_conv_ln_b, hgrn_lb_logits, ev_gnorm_g, fox_b_f, ln_mix_g, ln_mix_b, ln_mlp_g, ln_mlp_b)
    loss = lax.psum(loss_part, ("x", "y", "c"))

    full_shapes = [tuple(small_partial[n].shape) for n in small_names]
    reduced = _unpack(_all_reduce_small(_pack([small_partial[n] for n in small_names]), "ar_small"), full_shapes)
    g_small = {}
    for n, val in zip(small_names, reduced):
        if n == "ev_conv_w":
            val = lax.dynamic_slice_in_dim(val, chip * cshard, cshard, axis=1)
        elif n == "rel_bias":
            val = lax.dynamic_slice_in_dim(val, chip * rshard, rshard, axis=1)
        g_small[n] = val.reshape(w_small[n].shape)
    shard_shapes = [tuple(w_small[n].shape) for n in small_names]
    packed = [_pack([d[n] for n in small_names]) for d in (w_small, g_small, m_small, v_small)]
    small_out = _adamw(packed[0][None], [(packed[1],)], packed[2][None], packed[3][None], "adamw_small")
    out_small = {k: dict(zip(small_names, _unpack(val[0], shard_shapes))) for k, val in zip("gdmv", small_out)}

    out_big = {"g": {}, "d": {}, "m": {}, "v": {}}

    for name in ("ev_w_in", "ev_w_out", "od_w_in", "od_w_out", "mlp_w1", "mlp_w2"):
        pairs = []
        for l, slots in enumerate(g[name]):
            mine = _sum_slots(slots, "rs_%s_%d_sum" % (name, l))
            pairs.append((mine, _core_swap(mine, "rs_%s_%d_swap" % (name, l))))
        outs = _adamw(w_sharded[name], pairs, m_sharded[name], v_sharded[name], "adamw_%s" % name)
        for key, val in zip("gdmv", outs):
            out_big[key][name] = val

    order = ["ev_w_in", "ev_conv_w", "ev_conv_b", "ev_conv_ln_g", "ev_conv_ln_b", "hgrn_lb_logits", "ev_gnorm_g", "ev_w_out",
             "od_w_in", "fox_b_f", "rel_bias", "od_w_out", "ln_mix_g", "ln_mix_b", "mlp_w1", "mlp_w2", "ln_mlp_g", "ln_mlp_b"]

    def pick(key, n):
        return out_big[key][n] if n in out_big[key] else out_small[key][n]

    outs = [loss, grad_x[None]]
    for key in ("g", "d", "m", "v"):
        outs.extend(pick(key, n) for n in order)
    return tuple(outs)


def _local_step(xin, tgt, w_ev_in, shards, conv_w, rel_full, ev_conv_b, ev_conv_ln_g,
                ev_conv_ln_b, hgrn_lb_logits, ev_gnorm_g, fox_b_f, ln_mix_g, ln_mix_b, ln_mlp_g, ln_mlp_b):
    hw = HEADS * DH
    bf_pad = jnp.pad(fox_b_f[0], (0, 128 - HEADS)).reshape(1, 128)
    lb0 = _lower_bound(hgrn_lb_logits)
    ca_bias = _ca_bias(rel_full)
    ag = lambda w: ("ag", w)
    a2a = lambda g4: ("a2a", g4)

    xin_b = xin.astype(BF16)
    u0, (w_ev_out4,) = _matmul(xin_b, w_ev_in, b_sharded=True, comm=(ag(shards["ev_w_out"]),), name="l0_in")
    w_ev_out = _unshard_rows(w_ev_out4)
    a_out = _conv_fwd(u0, conv_w, ev_conv_b[0], ev_conv_ln_g[0], ev_conv_ln_b[0], "l0_conv")
    o_raw, b_out, states, (w1_0,) = _hgrn_fwd(u0, lb0, ev_gnorm_g[0], (ag(shards["mlp_w1"][0]),), "l0_hgrn")
    cat0 = jnp.concatenate([a_out, b_out], axis=1).astype(BF16)
    mix0 = _matmul(cat0, w_ev_out, name="l0_out")
    r0a, x0a, x0a_b = _ln_fwd(xin, mix0, ln_mix_g[0], ln_mix_b[0], "l0_ln_mix")
    (z0, h0), (w2_0,) = _matmul(x0a_b, w1_0, b_sharded=True, out_dtype=BF16, epi="relu2",
                                comm=(ag(shards["mlp_w2"][0]),), name="l0_mlp1")
    w2 = [_unshard_rows(w2_0), None]
    f0, (w_od_in4, w_od_out4) = _matmul(h0, w2[0], comm=(ag(shards["od_w_in"]), ag(shards["od_w_out"])), name="l0_mlp2")
    w_od_in = _unshard_cols(w_od_in4)
    w_od_out = _unshard_rows(w_od_out4)
    w_od_main = jnp.concatenate([w_od_in[:, :3 * hw], w_od_in[:, 3 * hw + HEADS:]], axis=1)
    w_od_f = jnp.pad(w_od_in[:, 3 * hw:3 * hw + HEADS], ((0, 0), (0, 128 - HEADS)))
    r0b, x1, x1_b = _ln_fwd(x0a, f0, ln_mlp_g[0], ln_mlp_b[0], "l0_ln_mlp")
    u1, (w1_1,) = _matmul(x1_b, w_od_main, comm=(ag(shards["mlp_w1"][1]),), name="l1_in")
    w1 = [w1_0, w1_1]
    cf = _matmul(x1_b, w_od_f, name="l1_in_f")
    fcum = _fgate_fwd(cf, bf_pad, "l1_fgate")
    f_col = jnp.transpose(fcum[:, :HEADS])[:, :, None]
    f_row = jnp.transpose(fcum[:, :HEADS])[:, None, :]
    c_out, lse_row = _fox_fwd(u1, f_col, f_row, "l1_fox")
    d_out = _ca_fwd(u1, ca_bias, "l1_ca")
    cat1 = jnp.concatenate([c_out, d_out], axis=1).astype(BF16)
    mix1 = _matmul(cat1, w_od_out, name="l1_out")
    r1a, x1a, x1a_b = _ln_fwd(x1, mix1, ln_mix_g[1], ln_mix_b[1], "l1_ln_mix")
    (z1, h1), (w2_1,) = _matmul(x1a_b, w1[1], b_sharded=True, out_dtype=BF16, epi="relu2",
                                comm=(ag(shards["mlp_w2"][1]),), name="l1_mlp1")
    w2[1] = _unshard_rows(w2_1)
    f1 = _matmul(h1, w2[1], name="l1_mlp2")
    r1b, x2, _ = _ln_fwd(x1a, f1, ln_mlp_g[1], ln_mlp_b[1], "l1_ln_mlp")
    dy, loss_part = _loss_head(x2, tgt, "loss")

    g = {}
    dr, drb, dg_, db_ = _ln_bwd(dy, r1b, ln_mlp_g[1], "l1_ln_mlp_bwd")
    g_ln_mlp = [None, (dg_, db_)]
    gw = _shard_rows(_matmul(h1, drb, ta=True, out_dtype=BF16, name="l1_dw2"))
    dz, (gw2_1,) = _matmul(drb, w2[1], tb=True, out_dtype=BF16, epi="drelu2", extra=z1, comm=(a2a(gw),), name="l1_dz")
    gw = _matmul(x1a_b, dz, ta=True, out_dtype=BF16, out_sharded=True, name="l1_dw1")
    dx, (gw1_1,) = _matmul(dz, w1[1], tb=True, b_sharded=True, epi="add", extra=dr, comm=(a2a(gw),), name="l1_dx_mlp")
    dr, drb, dg_, db_ = _ln_bwd(dx, r1a, ln_mix_g[1], "l1_ln_mix_bwd")
    g_ln_mix = [None, (dg_, db_)]
    gw = _shard_rows(_matmul(cat1, drb, ta=True, out_dtype=BF16, name="l1_dwout"))
    dcat, g["od_w_out"] = _matmul(drb, w_od_out, tb=True, comm=(a2a(gw),), name="l1_dcat")
    delta = _fox_delta(c_out, dcat, "l1_fox_delta")
    dk_c, dv_c, dfk = _fox_bwd_kv(u1, delta, lse_row, dcat, f_col, f_row, "l1_fox_bwd_kv")
    dq_c, dfq = _fox_bwd_q(u1, delta, lse_row, dcat, f_col, f_row, "l1_fox_bwd_q")
    d_f = jnp.pad(jnp.transpose(dfk[:, :, 0] + dfq[:, 0, :]), ((0, 0), (0, 128 - HEADS)))
    dcf, dbf = _fgate_bwd(d_f, cf, bf_pad, "l1_fgate_bwd")
    dq_d, dka, dkb, dva, dvb, dbias = _ca_bwd(u1, ca_bias, dcat, "l1_ca_bwd")
    du1 = jnp.concatenate([dq_c, dk_c, dv_c, dq_d, _ca_merge(dka, dkb), _ca_merge(dva, dvb)], axis=1)
    g_main = _matmul(x1_b, du1, ta=True, out_dtype=BF16, name="l1_dwin")
    g_f = _matmul(x1_b, dcf, ta=True, out_dtype=BF16, name="l1_dwin_f")
    gw = _shard_cols(jnp.concatenate([g_main[:, :3 * hw], g_f[:, :HEADS], g_main[:, 3 * hw:]], axis=1))
    dx_f = _matmul(dcf, w_od_f, tb=True, epi="add", extra=dr, name="l1_dx_f")
    dx, g["od_w_in"] = _matmul(du1, w_od_main, tb=True, epi="add", extra=dx_f, scale=1.0, comm=(a2a(gw),),
                               name="l1_dx_in")
    dr, drb, dg_, db_ = _ln_bwd(dx, r0b, ln_mlp_g[0], "l0_ln_mlp_bwd")
    g_ln_mlp[0] = (dg_, db_)
    gw = _shard_rows(_matmul(h0, drb, ta=True, out_dtype=BF16, name="l0_dw2"))
    dz, (gw2_0,) = _matmul(drb, w2[0], tb=True, out_dtype=BF16, epi="drelu2", extra=z0, comm=(a2a(gw),), name="l0_dz")
    gw = _matmul(x0a_b, dz, ta=True, out_dtype=BF16, out_sharded=True, name="l0_dw1")
    dx, (gw1_0,) = _matmul(dz, w1[0], tb=True, b_sharded=True, epi="add", extra=dr, comm=(a2a(gw),), name="l0_dx_mlp")
    dr, drb, dg_, db_ = _ln_bwd(dx, r0a, ln_mix_g[0], "l0_ln_mix_bwd")
    g_ln_mix[0] = (dg_, db_)
    gw = _shard_rows(_matmul(cat0, drb, ta=True, out_dtype=BF16, name="l0_dwout"))
    dcat, g["ev_w_out"] = _matmul(drb, w_ev_out, tb=True, comm=(a2a(gw),), name="l0_dcat")
    dc, g_conv_w, g_conv_b, g_conv_lg, g_conv_lb = _conv_bwd_params(
        u0, dcat, conv_w, ev_conv_b[0], ev_conv_ln_g[0], ev_conv_ln_b[0], "l0_conv_bwd_p")
    da, dgate = _conv_bwd_input(u0, dc, conv_w, "l0_conv_bwd_i")
    dhq, dhf, dhi, dhg, g_lb0, g_gnorm = _hgrn_bwd(u0, o_raw, states, dcat, lb0, ev_gnorm_g[0], "l0_hgrn_bwd")
    du0 = jnp.concatenate([da, dgate, dhq, dhf, dhi, dhg], axis=1)
    gw = _matmul(xin_b, du0, ta=True, out_dtype=BF16, out_sharded=True, name="l0_dwin")
    grad_x, g["ev_w_in"] = _matmul(du0, w_ev_in, tb=True, b_sharded=True, epi="add", extra=dr, comm=(a2a(gw),),
                                   name="l0_dx_in")
    g["mlp_w1"] = [gw1_0, gw1_1]
    g["mlp_w2"] = [gw2_0, gw2_1]

    g_lb_logits = jax.vjp(_lower_bound, hgrn_lb_logits)[1](g_lb0)[0]
    g_rel = _ca_bias_grad(dbias)
    small_partial = dict(
        ev_conv_w=g_conv_w, ev_conv_b=g_conv_b, ev_conv_ln_g=g_conv_lg, ev_conv_ln_b=g_conv_lb,
        hgrn_lb_logits=g_lb_logits, ev_gnorm_g=g_gnorm, fox_b_f=dbf[:HEADS], rel_bias=g_rel,
        ln_mix_g=jnp.stack([g_ln_mix[0][0], g_ln_mix[1][0]]), ln_mix_b=jnp.stack([g_ln_mix[0][1], g_ln_mix[1][1]]),
        ln_mlp_g=jnp.stack([g_ln_mlp[0][0], g_ln_mlp[1][0]]), ln_mlp_b=jnp.stack([g_ln_mlp[0][1], g_ln_mlp[1][1]]))
    return loss_part, grad_x, g, small_partial
```

```python
import functools

import jax
import jax.numpy as jnp
from jax import lax
from jax.experimental import pallas as pl
from jax.experimental.pallas import tpu as pltpu

F32 = jnp.float32
BF16 = jnp.bfloat16
MESH = pl.DeviceIdType.MESH

DEPTH = 2
ALPHA = (2 * DEPTH) ** 0.25
LN_EPS = 1e-5
HEADS = 8
DH = 128
CHUNK = 64
CONV_WIDTH = 31
HALO = 32
CA_LEFT = 8
CA_BLK = CA_LEFT * CHUNK
CA_BAND = (CA_LEFT + 1) * CHUNK
CA_TILE = 4 * CHUNK
CA_WIN = CA_TILE + CA_LEFT * CHUNK
REL_CLIP = 256
REL_TABLE = (CHUNK - 1) + REL_CLIP + 1
NEG = -1e30

ADAM_LR = 0.001
ADAM_B1 = 0.9
ADAM_B2 = 0.999
ADAM_EPS = 1e-08
ADAM_WD = 0.01
ADAM_STEP = 10

VMEM_LIMIT = 48 * 1024 * 1024


def _cparams(sem):
    return pltpu.CompilerParams(dimension_semantics=sem, vmem_limit_bytes=VMEM_LIMIT)


def _pick(n, cands):
    for c in cands:
        if n % c == 0:
            return c
    return n


def _sigmoid(x):
    return 1.0 / (1.0 + jnp.exp(-x))


def _dot(a, b, dims):
    return lax.dot_general(a.astype(BF16), b.astype(BF16), (dims, ((), ())), preferred_element_type=F32)


def _nn(a, b):
    return _dot(a, b, ((1,), (0,)))


def _nt(a, b):
    return _dot(a, b, ((1,), (1,)))


def _tn(a, b):
    return _dot(a, b, ((0,), (0,)))


def _dot3(a, b, dims):
    a_hi = a.astype(BF16)
    b_hi = b.astype(BF16)
    a_lo = (a - a_hi.astype(F32)).astype(BF16)
    b_lo = (b - b_hi.astype(F32)).astype(BF16)
    dn = (dims, ((), ()))
    return (lax.dot_general(a_hi, b_hi, dn, preferred_element_type=F32)
            + (lax.dot_general(a_hi, b_lo, dn, preferred_element_type=F32)
               + lax.dot_general(a_lo, b_hi, dn, preferred_element_type=F32)))


def _nn3(a, b):
    return _dot3(a, b, ((1,), (0,)))


def _nt3(a, b):
    return _dot3(a, b, ((1,), (1,)))


def _tn3(a, b):
    return _dot3(a, b, ((0,), (0,)))


def _split3(x):
    hi = x.astype(BF16)
    r1 = x - hi.astype(F32)
    mid = r1.astype(BF16)
    lo = (r1 - mid.astype(F32)).astype(BF16)
    return hi, mid, lo


def _tri_matmul(tri, x, terms):
    parts = _split3(x)[:terms]
    acc = None
    for p in parts:
        t = lax.dot_general(tri, p, (((1,), (0,)), ((), ())), preferred_element_type=F32)
        acc = t if acc is None else acc + t
    return acc


def _tril(n, upper=False):
    r = lax.broadcasted_iota(jnp.int32, (n, n), 0)
    c = lax.broadcasted_iota(jnp.int32, (n, n), 1)
    m = (c >= r) if upper else (c <= r)
    return jnp.where(m, 1.0, 0.0).astype(BF16)


def _matmul(a, b, *, ta=False, tb=False, out_dtype=F32, epi=None, extra=None, scale=ALPHA, b_sharded=False,
            out_sharded=False, comm=(), name):
    m = a.shape[1] if ta else a.shape[0]
    kd = a.shape[0] if ta else a.shape[1]
    if b_sharded:
        shard = b.shape[2]
        n = b.shape[1] if tb else 4 * shard
        assert (b.shape[1] if not tb else 4 * shard) == kd
    else:
        n = b.shape[0] if tb else b.shape[1]
    bm = _pick(m, (1024, 512, 256, 128))
    bn = _pick(shard if (b_sharded and not tb) else (n // 4 if out_sharded else n), (1024, 768, 512, 256, 128))
    bk = _pick(shard if (b_sharded and tb) else kd, (2048, 1536, 1024, 768, 512, 256, 128))
    ni, nj, nk = m // bm, n // bn, kd // bk
    n_out = 2 if epi == "relu2" else 1
    n_comm = len(comm)
    kinds = [c[0] for c in comm]

    def body(*refs):
        a_ref, b_ref = refs[0], refs[1]
        pos = 2
        e_ref = refs[pos] if extra is not None else None
        pos += extra is not None
        c_in = refs[pos:pos + n_comm]
        pos += n_comm
        outs = refs[pos:pos + n_out]
        pos += n_out
        c_out = refs[pos:pos + n_comm]
        pos += n_comm
        acc_ref = refs[pos]
        sems = refs[pos + 1:]
        i, j, k = pl.program_id(0), pl.program_id(1), pl.program_id(2)

        if n_comm:
            @pl.when(jnp.logical_and(jnp.logical_and(i == 0, j == 0), k == 0))
            def _():
                _exchange(kinds, c_in, c_out, *sems, start=True)

        dims = ((0 if ta else 1,), (1 if tb else 0,))
        part = _dot(a_ref[...], b_ref[...], dims)
        if nk > 1:
            @pl.when(k == 0)
            def _():
                acc_ref[...] = part

            @pl.when(jnp.logical_and(k > 0, k < nk - 1))
            def _():
                acc_ref[...] += part

        @pl.when(k == nk - 1)
        def _():
            r = part + acc_ref[...] if nk > 1 else part
            if epi == "relu2":
                outs[0][...] = r.astype(out_dtype)
                outs[1][...] = jnp.square(jnp.maximum(r, 0.0)).astype(out_dtype)
            elif epi == "drelu2":
                outs[0][...] = (r * (2.0 * jnp.maximum(e_ref[...].astype(F32), 0.0))).astype(out_dtype)
            elif epi == "add":
                outs[0][...] = (r + scale * e_ref[...].astype(F32)).astype(out_dtype)
            else:
                outs[0][...] = r.astype(out_dtype)

        if n_comm:
            @pl.when(jnp.logical_and(jnp.logical_and(i == ni - 1, j == nj - 1), k == nk - 1))
            def _():
                _exchange(kinds, c_in, c_out, *sems, start=False)

    a_spec = pl.BlockSpec((bk, bm), lambda i, j, k: (k, i)) if ta else pl.BlockSpec((bm, bk), lambda i, j, k: (i, k))
    if b_sharded and tb:
        per = shard // bk
        b_spec = pl.BlockSpec((None, bn, bk), lambda i, j, k: (k // per, j, k % per))
    elif b_sharded:
        per = shard // bn
        b_spec = pl.BlockSpec((None, bk, bn), lambda i, j, k: (j // per, k, j % per))
    elif tb:
        b_spec = pl.BlockSpec((bn, bk), lambda i, j, k: (j, k))
    else:
        b_spec = pl.BlockSpec((bk, bn), lambda i, j, k: (k, j))
    e_spec = pl.BlockSpec((bm, bn), lambda i, j, k: (i, j))
    if out_sharded:
        per_o = (n // 4) // bn
        o_spec = pl.BlockSpec((None, bm, bn), lambda i, j, k: (j // per_o, i, j % per_o))
        o_shape = jax.ShapeDtypeStruct((4, m, n // 4), out_dtype)
    else:
        o_spec = e_spec
        o_shape = jax.ShapeDtypeStruct((m, n), out_dtype)
    in_specs = [a_spec, b_spec] + ([e_spec] if extra is not None else []) + [_ANY] * n_comm
    args = (a, b) + ((extra,) if extra is not None else ()) + tuple(c[1] for c in comm)
    c_shapes = [jax.ShapeDtypeStruct((4,) + c[1].shape if c[0] == "ag" else c[1].shape, c[1].dtype) for c in comm]
    scratch = [pltpu.VMEM((bm, bn) if nk > 1 else (8, 128), F32)]
    if n_comm:
        scratch += [pltpu.SemaphoreType.DMA((3 * n_comm,)), pltpu.SemaphoreType.DMA((3 * n_comm,)),
                    pltpu.SemaphoreType.DMA((n_comm,))]
    sem = ("arbitrary",) * 3 if n_comm else ("parallel", "parallel", "arbitrary")
    res = pl.pallas_call(
        body, grid=(ni, nj, nk), in_specs=in_specs,
        out_specs=[o_spec] * n_out + [_ANY] * n_comm, out_shape=[o_shape] * n_out + c_shapes,
        scratch_shapes=scratch, compiler_params=_cparams(sem), name=name)(*args)
    main = tuple(res[:n_out]) if n_out == 2 else res[0]
    return (main, list(res[n_out:])) if n_comm else main


def _ln_fwd(x, mix, g, b, name):
    s, d = x.shape
    br = _pick(s, (256, 128, 64, 8))

    def body(x_ref, m_ref, g_ref, b_ref, r_ref, y_ref, yb_ref):
        r = ALPHA * x_ref[...] + m_ref[...]
        mu = jnp.mean(r, axis=-1, keepdims=True)
        dlt = r - mu
        var = jnp.mean(dlt * dlt, axis=-1, keepdims=True)
        y = dlt * lax.rsqrt(var + LN_EPS) * g_ref[...] + b_ref[...]
        r_ref[...] = r
        y_ref[...] = y
        yb_ref[...] = y.astype(BF16)

    row = pl.BlockSpec((br, d), lambda i: (i, 0))
    vec = pl.BlockSpec((1, d), lambda i: (0, 0))
    return pl.pallas_call(
        body, grid=(s // br,), in_specs=[row, row, vec, vec], out_specs=[row, row, row],
        out_shape=[jax.ShapeDtypeStruct((s, d), F32), jax.ShapeDtypeStruct((s, d), F32),
                   jax.ShapeDtypeStruct((s, d), BF16)],
        compiler_params=_cparams(("parallel",)), name=name)(x, mix, g.reshape(1, d), b.reshape(1, d))


def _ln_bwd(dy, r, g, name):
    s, d = r.shape
    br = _pick(s, (256, 128, 64, 8))

    def body(dy_ref, r_ref, g_ref, dr_ref, drb_ref, dg_ref, db_ref):
        @pl.when(pl.program_id(0) == 0)
        def _():
            dg_ref[...] = jnp.zeros_like(dg_ref)
            db_ref[...] = jnp.zeros_like(db_ref)

        rv = r_ref[...]
        dyv = dy_ref[...]
        mu = jnp.mean(rv, axis=-1, keepdims=True)
        dlt = rv - mu
        var = jnp.mean(dlt * dlt, axis=-1, keepdims=True)
        rstd = lax.rsqrt(var + LN_EPS)
        xhat = dlt * rstd
        dxh = dyv * g_ref[...]
        m1 = jnp.mean(dxh, axis=-1, keepdims=True)
        m2 = jnp.mean(dxh * xhat, axis=-1, keepdims=True)
        dr = rstd * (dxh - m1 - xhat * m2)
        dr_ref[...] = dr
        drb_ref[...] = dr.astype(BF16)
        dg_ref[...] += jnp.sum(dyv * xhat, axis=0, keepdims=True)
        db_ref[...] += jnp.sum(dyv, axis=0, keepdims=True)

    row = pl.BlockSpec((br, d), lambda i: (i, 0))
    vec = pl.BlockSpec((1, d), lambda i: (0, 0))
    dr, drb, dg, db = pl.pallas_call(
        body, grid=(s // br,), in_specs=[row, row, vec], out_specs=[row, row, vec, vec],
        out_shape=[jax.ShapeDtypeStruct((s, d), F32), jax.ShapeDtypeStruct((s, d), BF16),
                   jax.ShapeDtypeStruct((1, d), F32), jax.ShapeDtypeStruct((1, d), F32)],
        compiler_params=_cparams(("arbitrary",)), name=name)(dy, r, g.reshape(1, d))
    return dr, drb, dg[0], db[0]


def _loss_head(y, tgt, name):
    s, d = y.shape
    br = _pick(s, (256, 128, 64, 8))

    def body(y_ref, t_ref, dy_ref, l_ref):
        @pl.when(pl.program_id(0) == 0)
        def _():
            l_ref[...] = jnp.zeros_like(l_ref)

        e = y_ref[...] - t_ref[...]
        dy_ref[...] = e * (1.0 / d)
        rows = jnp.sum(e * e, axis=-1, keepdims=True) * (0.5 / d)
        l_ref[...] += jnp.sum(rows, axis=0, keepdims=True)

    row = pl.BlockSpec((br, d), lambda i: (i, 0))
    dy, l = pl.pallas_call(
        body, grid=(s // br,), in_specs=[row, row],
        out_specs=[row, pl.BlockSpec((1, 1), lambda i: (0, 0))],
        out_shape=[jax.ShapeDtypeStruct((s, d), F32), jax.ShapeDtypeStruct((1, 1), F32)],
        compiler_params=_cparams(("arbitrary",)), name=name)(y, tgt)
    return dy, l[0, 0]


SUBLANES = 8


def _shifted_rows(ext_ref, sh_ref, tt):
    n = tt + HALO - SUBLANES
    for r in range(1, SUBLANES):
        sh_ref[r - 1] = ext_ref[pl.ds(r, n), :]

    def tap(o):
        r = o % SUBLANES
        if r == 0:
            return ext_ref[pl.ds(o, tt), :]
        return sh_ref[r - 1, pl.ds(o - r, tt), :]

    return tap


def _conv_scratch(tt, cc):
    return [pltpu.VMEM((tt + HALO, cc), F32), pltpu.VMEM((SUBLANES - 1, tt + HALO - SUBLANES, cc), F32)]


def _conv_recompute(i, a_ref, gt_ref, ah_ref, gh_ref, w_ref, cb_ref, hext_ref, sh_ref, tt):
    h = a_ref[...] * _sigmoid(gt_ref[...])
    hh = ah_ref[...] * _sigmoid(gh_ref[...])
    hh = jnp.where(i > 0, hh, 0.0)
    hext_ref[0:HALO, :] = hh
    hext_ref[HALO:HALO + tt, :] = h
    tap = _shifted_rows(hext_ref, sh_ref, tt)
    acc = jnp.zeros_like(h) + cb_ref[...]
    off = HALO - (CONV_WIDTH - 1)
    for j in range(CONV_WIDTH):
        acc = acc + w_ref[j:j + 1, :] * tap(off + j)
    mu = jnp.mean(acc, axis=-1, keepdims=True)
    dlt = acc - mu
    var = jnp.mean(dlt * dlt, axis=-1, keepdims=True)
    rstd = lax.rsqrt(var + LN_EPS)
    return dlt * rstd, rstd, tap


def _conv_specs(tt, cc, s):
    nh = tt // HALO
    cur = lambda cb: pl.BlockSpec((tt, cc), lambda i: (i, cb))
    prev = lambda cb: pl.BlockSpec((HALO, cc), lambda i: (jnp.maximum(i * nh - 1, 0), cb))
    vec = pl.BlockSpec((1, cc), lambda i: (0, 0))
    wsp = pl.BlockSpec((HALO, cc), lambda i: (0, 0))
    return cur, prev, vec, wsp


def _pad_conv_w(w):
    return jnp.concatenate([w, jnp.zeros((HALO - CONV_WIDTH, w.shape[1]), F32)], axis=0)


def _conv_fwd(u, w, cb, lg, lb, name):
    s = u.shape[0]
    cc = w.shape[1]
    tt = _pick(s, (256, 128, 64))
    cur, prev, vec, wsp = _conv_specs(tt, cc, s)

    def body(a_ref, gt_ref, ah_ref, gh_ref, w_ref, cb_ref, lg_ref, lb_ref, o_ref, hext_ref, sh_ref):
        xhat, _, _ = _conv_recompute(pl.program_id(0), a_ref, gt_ref, ah_ref, gh_ref, w_ref, cb_ref, hext_ref,
                                     sh_ref, tt)
        nrm = xhat * lg_ref[...] + lb_ref[...]
        o_ref[...] = nrm * _sigmoid(nrm)

    return pl.pallas_call(
        body, grid=(s // tt,), in_specs=[cur(0), cur(1), prev(0), prev(1), wsp, vec, vec, vec],
        out_specs=pl.BlockSpec((tt, cc), lambda i: (i, 0)), out_shape=jax.ShapeDtypeStruct((s, cc), F32),
        scratch_shapes=_conv_scratch(tt, cc),
        compiler_params=_cparams(("parallel",)), name=name)(
            u, u, u, u, _pad_conv_w(w), cb.reshape(1, cc), lg.reshape(1, cc), lb.reshape(1, cc))


def _conv_bwd_params(u, dout, w, cb, lg, lb, name):
    s = u.shape[0]
    cc = w.shape[1]
    tt = _pick(s, (256, 128, 64))
    cur, prev, vec, wsp = _conv_specs(tt, cc, s)

    def body(a_ref, gt_ref, ah_ref, gh_ref, w_ref, cb_ref, lg_ref, lb_ref, do_ref,
             dc_ref, dw_ref, dcb_ref, dlg_ref, dlb_ref, hext_ref, sh_ref, dw_acc):
        i = pl.program_id(0)

        @pl.when(i == 0)
        def _():
            dw_acc[...] = jnp.zeros_like(dw_acc)
            dcb_ref[...] = jnp.zeros_like(dcb_ref)
            dlg_ref[...] = jnp.zeros_like(dlg_ref)
            dlb_ref[...] = jnp.zeros_like(dlb_ref)

        xhat, rstd, tap = _conv_recompute(i, a_ref, gt_ref, ah_ref, gh_ref, w_ref, cb_ref, hext_ref, sh_ref, tt)
        nrm = xhat * lg_ref[...] + lb_ref[...]
        sg = _sigmoid(nrm)
        dn = do_ref[...] * (sg * (1.0 + nrm * (1.0 - sg)))
        dxh = dn * lg_ref[...]
        m1 = jnp.mean(dxh, axis=-1, keepdims=True)
        m2 = jnp.mean(dxh * xhat, axis=-1, keepdims=True)
        dc = rstd * (dxh - m1 - xhat * m2)
        dc_ref[...] = dc
        dlg_ref[...] += jnp.sum(dn * xhat, axis=0, keepdims=True)
        dlb_ref[...] += jnp.sum(dn, axis=0, keepdims=True)
        dcb_ref[...] += jnp.sum(dc, axis=0, keepdims=True)
        off = HALO - (CONV_WIDTH - 1)
        for j in range(CONV_WIDTH):
            dw_acc[j] += jnp.sum((dc * tap(off + j)).reshape(tt // SUBLANES, SUBLANES, cc), axis=0)

        @pl.when(i == nblk - 1)
        def _():
            dw_ref[...] = jnp.zeros_like(dw_ref)
            for j in range(CONV_WIDTH):
                dw_ref[j:j + 1, :] = jnp.sum(dw_acc[j], axis=0, keepdims=True)

    nblk = s // tt
    dcol = pl.BlockSpec((tt, cc), lambda i: (i, 0))
    dc, dw, dcb, dlg, dlb = pl.pallas_call(
        body, grid=(nblk,), in_specs=[cur(0), cur(1), prev(0), prev(1), wsp, vec, vec, vec, dcol],
        out_specs=[dcol, wsp, vec, vec, vec],
        out_shape=[jax.ShapeDtypeStruct((s, cc), F32), jax.ShapeDtypeStruct((HALO, cc), F32)]
        + [jax.ShapeDtypeStruct((1, cc), F32)] * 3,
        scratch_shapes=_conv_scratch(tt, cc) + [pltpu.VMEM((HALO, SUBLANES, cc), F32)],
        compiler_params=_cparams(("arbitrary",)), name=name)(
            u, u, u, u, _pad_conv_w(w), cb.reshape(1, cc), lg.reshape(1, cc), lb.reshape(1, cc), dout)
    return dc, dw[:CONV_WIDTH], dcb[0], dlg[0], dlb[0]


def _conv_bwd_input(u, dc, w, name):
    s = u.shape[0]
    cc = w.shape[1]
    tt = _pick(s, (256, 128, 64))
    nh = tt // HALO
    nlast = s // HALO - 1
    cur = lambda cb: pl.BlockSpec((tt, cc), lambda i: (i, cb))
    nxt = pl.BlockSpec((HALO, cc), lambda i: (jnp.minimum((i + 1) * nh, nlast), 0))
    wsp = pl.BlockSpec((HALO, cc), lambda i: (0, 0))
    nblk = s // tt

    def body(a_ref, gt_ref, dc_ref, dn_ref, w_ref, da_ref, dg_ref, ext_ref, sh_ref):
        i = pl.program_id(0)
        ext_ref[0:tt, :] = dc_ref[...]
        ext_ref[tt:tt + HALO, :] = jnp.where(i < nblk - 1, dn_ref[...], 0.0)
        tap = _shifted_rows(ext_ref, sh_ref, tt)
        dh = jnp.zeros((tt, cc), F32)
        for j in range(CONV_WIDTH):
            dh = dh + w_ref[j:j + 1, :] * tap(CONV_WIDTH - 1 - j)
        a = a_ref[...]
        sg = _sigmoid(gt_ref[...])
        da_ref[...] = (dh * sg).astype(BF16)
        dg_ref[...] = (dh * a * sg * (1.0 - sg)).astype(BF16)

    ocol = pl.BlockSpec((tt, cc), lambda i: (i, 0))
    return pl.pallas_call(
        body, grid=(nblk,), in_specs=[cur(0), cur(1), ocol, nxt, wsp], out_specs=[ocol, ocol],
        out_shape=[jax.ShapeDtypeStruct((s, cc), BF16)] * 2,
        scratch_shapes=_conv_scratch(tt, cc),
        compiler_params=_cparams(("parallel",)), name=name)(u, u, dc, dc, _pad_conv_w(w))


def _hgrn_gates(hq, hf, lb):
    sg = _sigmoid(hf)
    f = lb + (1.0 - lb) * sg
    lf = jnp.log(f)
    big_l = _tri_matmul(_tril(CHUNK), lf, 3)
    l_end = jnp.sum(lf, axis=0, keepdims=True)
    l_mid = jnp.sum(lf[0:CHUNK // 2, :], axis=0, keepdims=True)
    sq = _sigmoid(hq)
    q = hq * sq
    return sg, f, 1.0 - f, big_l, l_end, l_mid, sq, q


def _causal_mask(n):
    r = lax.broadcasted_iota(jnp.int32, (n, n), 0)
    c = lax.broadcasted_iota(jnp.int32, (n, n), 1)
    return c <= r


def _hgrn_fwd(u, lb, gg, comm, name):
    s = u.shape[0]
    w = HEADS * DH
    nch = s // CHUNK
    n_comm = len(comm)
    kinds = [c[0] for c in comm]

    def body(*refs):
        q_ref, f_ref, i_ref, g_ref, lb_ref, gg_ref = refs[:6]
        c_in = refs[6:6 + n_comm]
        o_ref, out_ref, st_ref = refs[6 + n_comm:9 + n_comm]
        c_out = refs[9 + n_comm:9 + 2 * n_comm]
        state = refs[9 + 2 * n_comm]
        sems = refs[10 + 2 * n_comm:]

        @pl.when(pl.program_id(0) == 0)
        def _():
            state[...] = jnp.zeros_like(state)
            _exchange(kinds, c_in, c_out, *sems, start=True)

        mask = _causal_mask(CHUNK)
        for hd in range(HEADS):
            sl = slice(hd * DH, (hd + 1) * DH)
            _, _, kk, big_l, l_end, l_mid, _, q = _hgrn_gates(q_ref[:, sl], f_ref[:, sl], lb_ref[:, sl])
            v = i_ref[:, sl]
            qs = q * jnp.exp(big_l - l_mid)
            ks = kk * jnp.exp(l_mid - big_l)
            att = jnp.where(mask, _nt3(qs, ks), 0.0)
            st0 = state[hd]
            st_ref[0, hd] = st0
            o = _nn3(att, v) + _nt3(q * jnp.exp(big_l), st0)
            state[hd] = st0 * jnp.exp(l_end) + _tn3(v, kk * jnp.exp(l_end - big_l))
            o_ref[:, sl] = o
            on = o * lax.rsqrt(jnp.mean(o * o, axis=-1, keepdims=True) + LN_EPS)
            gv = g_ref[:, sl]
            out_ref[:, sl] = on * gg_ref[:, sl] * (gv * _sigmoid(gv))

        @pl.when(pl.program_id(0) == nch - 1)
        def _():
            _exchange(kinds, c_in, c_out, *sems, start=False)

    col = lambda cb: pl.BlockSpec((CHUNK, w), lambda i: (i, cb))
    vec = pl.BlockSpec((1, w), lambda i: (0, 0))
    ocol = pl.BlockSpec((CHUNK, w), lambda i: (i, 0))
    c_shapes = [jax.ShapeDtypeStruct((4,) + c[1].shape if c[0] == "ag" else c[1].shape, c[1].dtype) for c in comm]
    res = pl.pallas_call(
        body, grid=(nch,), in_specs=[col(2), col(3), col(4), col(5), vec, vec] + [_ANY] * n_comm,
        out_specs=[ocol, ocol, pl.BlockSpec((1, HEADS, DH, DH), lambda i: (i, 0, 0, 0))] + [_ANY] * n_comm,
        out_shape=[jax.ShapeDtypeStruct((s, w), F32), jax.ShapeDtypeStruct((s, w), F32),
                   jax.ShapeDtypeStruct((nch, HEADS, DH, DH), F32)] + c_shapes,
        scratch_shapes=[pltpu.VMEM((HEADS, DH, DH), F32), pltpu.SemaphoreType.DMA((3 * n_comm,)),
                        pltpu.SemaphoreType.DMA((3 * n_comm,)), pltpu.SemaphoreType.DMA((n_comm,))],
        compiler_params=_cparams(("arbitrary",)), name=name)(
            u, u, u, u, lb.reshape(1, w), gg.reshape(1, w), *[c[1] for c in comm])
    return res[0], res[1], res[2], list(res[3:])


def _hgrn_bwd(u, o_raw, states, dout, lb, gg, name):
    s = u.shape[0]
    w = HEADS * DH
    nch = s // CHUNK

    def body(q_ref, f_ref, i_ref, g_ref, o_ref, st_ref, do_ref, lb_ref, gg_ref,
             dq_ref, df_ref, di_ref, dg_ref, dlb_ref, dgg_ref, dstate):
        @pl.when(pl.program_id(0) == 0)
        def _():
            dstate[...] = jnp.zeros_like(dstate)
            dlb_ref[...] = jnp.zeros_like(dlb_ref)
            dgg_ref[...] = jnp.zeros_like(dgg_ref)

        mask = _causal_mask(CHUNK)
        last_row = lax.broadcasted_iota(jnp.int32, (CHUNK, DH), 0) == CHUNK - 1
        tri_up = _tril(CHUNK, upper=True)
        for hd in range(HEADS):
            sl = slice(hd * DH, (hd + 1) * DH)
            hq = q_ref[:, sl]
            lbv = lb_ref[:, sl]
            sg, f, kk, big_l, l_end, l_mid, sq, q = _hgrn_gates(hq, f_ref[:, sl], lbv)
            v = i_ref[:, sl]
            e_l = jnp.exp(big_l)
            e_qm = jnp.exp(big_l - l_mid)
            e_km = jnp.exp(l_mid - big_l)
            e_ke = jnp.exp(l_end - big_l)
            e_end = jnp.exp(l_end)
            qs = q * e_qm
            ks = kk * e_km
            qe = q * e_l
            ke = kk * e_ke
            att = jnp.where(mask, _nt3(qs, ks), 0.0)
            st0 = st_ref[0, hd]
            dst1 = dstate[hd]
            o = o_ref[:, sl]
            rinv = lax.rsqrt(jnp.mean(o * o, axis=-1, keepdims=True) + LN_EPS)
            on = o * rinv
            gv = g_ref[:, sl]
            sgg = _sigmoid(gv)
            gsil = gv * sgg
            ggv = gg_ref[:, sl]
            dov = do_ref[:, sl]
            don = dov * ggv * gsil
            dg_ref[:, sl] = (dov * on * ggv * (sgg * (1.0 + gv * (1.0 - sgg)))).astype(BF16)
            dgg_ref[:, sl] += jnp.sum(dov * on * gsil, axis=0, keepdims=True)
            do = rinv * (don - on * jnp.mean(don * on, axis=-1, keepdims=True))
            datt = jnp.where(mask, _nt3(do, v), 0.0)
            dv = _tn3(att, do) + _nt3(ke, dst1)
            dqs = _nn3(datt, ks)
            dks = _tn3(datt, qs)
            dqe = _nn3(do, st0)
            dke = _nn3(v, dst1)
            dq = dqs * e_qm + dqe * e_l
            dk = dks * e_km + dke * e_ke
            dke_ke = dke * ke
            dl = dqs * qs - dks * ks + dqe * qe - dke_ke
            dl_end = jnp.sum(dke_ke, axis=0, keepdims=True) + jnp.sum(dst1 * st0, axis=0, keepdims=True) * e_end
            dl = dl + jnp.where(last_row, dl_end, 0.0)
            dlf = _tri_matmul(tri_up, dl, 2)
            dfv = dlf / f - dk
            df_ref[:, sl] = (dfv * (1.0 - lbv) * sg * (1.0 - sg)).astype(BF16)
            dlb_ref[:, sl] += jnp.sum(dfv * (1.0 - sg), axis=0, keepdims=True)
            dq_ref[:, sl] = (dq * (sq * (1.0 + hq * (1.0 - sq)))).astype(BF16)
            di_ref[:, sl] = dv.astype(BF16)
            dstate[hd] = dst1 * e_end + _tn3(do, qe)

    rev = lambda i: nch - 1 - i
    col = lambda cb: pl.BlockSpec((CHUNK, w), lambda i: (rev(i), cb))
    vec = pl.BlockSpec((1, w), lambda i: (0, 0))
    ocol = pl.BlockSpec((CHUNK, w), lambda i: (rev(i), 0))
    res = pl.pallas_call(
        body, grid=(nch,),
        in_specs=[col(2), col(3), col(4), col(5), ocol,
                  pl.BlockSpec((1, HEADS, DH, DH), lambda i: (rev(i), 0, 0, 0)), col(1), vec, vec],
        out_specs=[ocol, ocol, ocol, ocol, vec, vec],
        out_shape=[jax.ShapeDtypeStruct((s, w), BF16)] * 4 + [jax.ShapeDtypeStruct((1, w), F32)] * 2,
        scratch_shapes=[pltpu.VMEM((HEADS, DH, DH), F32)],
        compiler_params=_cparams(("arbitrary",)), name=name)(
            u, u, u, u, o_raw, states, dout, lb.reshape(1, w), gg.reshape(1, w))
    return res[0], res[1], res[2], res[3], res[4][0], res[5][0]


def _log_sigmoid(x):
    return jnp.minimum(x, 0.0) - jnp.log(1.0 + jnp.exp(-jnp.abs(x)))


def _fgate_fwd(cf, bf, name):
    s, wl = cf.shape
    tb = _pick(s, (512, 256, 128, 64))

    def body(c_ref, b_ref, f_ref, carry):
        @pl.when(pl.program_id(0) == 0)
        def _():
            carry[...] = jnp.zeros_like(carry)

        ls = _log_sigmoid(c_ref[...] + b_ref[...])
        f_ref[...] = _tri_matmul(_tril(tb), ls, 3) + carry[...]
        carry[...] += jnp.sum(ls, axis=0, keepdims=True)

    return pl.pallas_call(
        body, grid=(s // tb,), in_specs=[pl.BlockSpec((tb, wl), lambda i: (i, 0)), pl.BlockSpec((1, wl), lambda i: (0, 0))],
        out_specs=pl.BlockSpec((tb, wl), lambda i: (i, 0)), out_shape=jax.ShapeDtypeStruct((s, wl), F32),
        scratch_shapes=[pltpu.VMEM((1, wl), F32)],
        compiler_params=_cparams(("arbitrary",)), name=name)(cf, bf)


def _fgate_bwd(dF, cf, bf, name):
    s, wl = cf.shape
    tb = _pick(s, (512, 256, 128, 64))
    nb = s // tb

    def body(d_ref, c_ref, b_ref, dc_ref, db_ref, carry):
        @pl.when(pl.program_id(0) == 0)
        def _():
            carry[...] = jnp.zeros_like(carry)
            db_ref[...] = jnp.zeros_like(db_ref)

        dv = d_ref[...]
        dls = _tri_matmul(_tril(tb, upper=True), dv, 3) + carry[...]
        carry[...] += jnp.sum(dv, axis=0, keepdims=True)
        dc = dls * (1.0 - _sigmoid(c_ref[...] + b_ref[...]))
        dc_ref[...] = dc.astype(BF16)
        db_ref[...] += jnp.sum(dc, axis=0, keepdims=True)

    blk = pl.BlockSpec((tb, wl), lambda i: (nb - 1 - i, 0))
    vec = pl.BlockSpec((1, wl), lambda i: (0, 0))
    dc, db = pl.pallas_call(
        body, grid=(nb,), in_specs=[blk, blk, vec], out_specs=[blk, vec],
        out_shape=[jax.ShapeDtypeStruct((s, wl), BF16), jax.ShapeDtypeStruct((1, wl), F32)],
        scratch_shapes=[pltpu.VMEM((1, wl), F32)],
        compiler_params=_cparams(("arbitrary",)), name=name)(dF, cf, bf)
    return dc, db[0]


FOX_HPB = 2
FOX_TILES = (1024, 512, 256, 128)


def _fox_scores_t(q, k, fq_row, fk_col, diag):
    sc = _nt(k, q) * (DH ** -0.5) + fq_row - fk_col
    if not diag:
        return sc
    r = lax.broadcasted_iota(jnp.int32, sc.shape, 0)
    c = lax.broadcasted_iota(jnp.int32, sc.shape, 1)
    return jnp.where(r <= c, sc, NEG)


def _fox_probs_t(q, k, fq_row, fk_col, lse_row, diag):
    return jnp.exp(_fox_scores_t(q, k, fq_row, fk_col, diag) - lse_row)


def _fox_delta_row(do, o):
    prod = do * o
    hi = prod.astype(BF16)
    lo = (prod - hi.astype(F32)).astype(BF16)
    ones = jnp.ones((SUBLANES, DH), BF16)
    dims = (((1,), (1,)), ((), ()))
    return (lax.dot_general(ones, hi, dims, preferred_element_type=F32)
            + lax.dot_general(ones, lo, dims, preferred_element_type=F32))[0:1, :]


def _fox_when_needed(q_blk, k_blk, step):
    @pl.when(k_blk < q_blk)
    def _():
        step(False)

    @pl.when(k_blk == q_blk)
    def _():
        step(True)


def _fox_fwd(u, f_col, f_row, name):
    s = u.shape[0]
    t = _pick(s, FOX_TILES)
    nb = s // t
    hpb = FOX_HPB
    wb = hpb * DH
    ng = HEADS // hpb

    def body(q_ref, k_ref, v_ref, fq_ref, fk_ref, o_ref, lse_ref, m_sc, l_sc, acc):
        i, j = pl.program_id(1), pl.program_id(2)

        @pl.when(j == 0)
        def _():
            m_sc[...] = jnp.full_like(m_sc, NEG)
            l_sc[...] = jnp.zeros_like(l_sc)
            acc[...] = jnp.zeros_like(acc)

        def step(diag):
            for hh in range(hpb):
                sl = slice(hh * DH, (hh + 1) * DH)
                sc = _fox_scores_t(q_ref[:, sl], k_ref[:, sl], fq_ref[hh], fk_ref[hh], diag)
                m_new = jnp.maximum(m_sc[hh], jnp.max(sc, axis=0, keepdims=True))
                a = jnp.exp(m_sc[hh] - m_new)
                p = jnp.exp(sc - m_new)
                l_sc[hh] = a * l_sc[hh] + jnp.sum(p, axis=0, keepdims=True)
                acc[sl, :] = a * acc[sl, :] + _tn(v_ref[:, sl], p)
                m_sc[hh] = m_new

        _fox_when_needed(i, j, step)

        @pl.when(j == i)
        def _():
            for hh in range(hpb):
                sl = slice(hh * DH, (hh + 1) * DH)
                o_ref[:, sl] = jnp.transpose(acc[sl, :] / l_sc[hh])
                lse_ref[hh] = m_sc[hh] + jnp.log(l_sc[hh])

    qs = pl.BlockSpec((t, wb), lambda h, i, j: (i, h))
    ks = lambda base: pl.BlockSpec((t, wb), lambda h, i, j: (jnp.minimum(j, i), base + h))
    return pl.pallas_call(
        body, grid=(ng, nb, nb),
        in_specs=[qs, ks(ng), ks(2 * ng),
                  pl.BlockSpec((hpb, 1, t), lambda h, i, j: (h, 0, i)),
                  pl.BlockSpec((hpb, t, 1), lambda h, i, j: (h, jnp.minimum(j, i), 0))],
        out_specs=[pl.BlockSpec((t, wb), lambda h, i, j: (i, h)), pl.BlockSpec((hpb, 1, t), lambda h, i, j: (h, 0, i))],
        out_shape=[jax.ShapeDtypeStruct((s, HEADS * DH), F32), jax.ShapeDtypeStruct((HEADS, 1, s), F32)],
        scratch_shapes=[pltpu.VMEM((hpb, 1, t), F32), pltpu.VMEM((hpb, 1, t), F32), pltpu.VMEM((wb, t), F32)],
        compiler_params=_cparams(("parallel", "parallel", "arbitrary")), name=name)(u, u, u, f_row, f_col)


def _fox_delta(o, dout, name):
    s = o.shape[0]
    t = _pick(s, FOX_TILES)
    hpb = FOX_HPB
    wb = hpb * DH

    def body(o_ref, do_ref, d_ref):
        for hh in range(hpb):
            sl = slice(hh * DH, (hh + 1) * DH)
            d_ref[hh] = _fox_delta_row(do_ref[:, sl], o_ref[:, sl])

    blk = pl.BlockSpec((t, wb), lambda h, i: (i, h))
    return pl.pallas_call(
        body, grid=(HEADS // hpb, s // t), in_specs=[blk, blk],
        out_specs=pl.BlockSpec((hpb, 1, t), lambda h, i: (h, 0, i)),
        out_shape=jax.ShapeDtypeStruct((HEADS, 1, s), F32),
        compiler_params=_cparams(("parallel", "parallel")), name=name)(o, dout)


def _fox_bwd_kv(u, delta, lse_row, dout, f_col, f_row, name):
    s = u.shape[0]
    t = _pick(s, FOX_TILES)
    nb = s // t
    hpb = FOX_HPB
    wb = hpb * DH
    ng = HEADS // hpb

    def body(q_ref, k_ref, v_ref, dl_ref, do_ref, lse_ref, fq_ref, fk_ref, dk_ref, dv_ref, dfk_ref, dk_acc, dv_acc, df_acc):
        j, i = pl.program_id(1), pl.program_id(2)

        @pl.when(i == 0)
        def _():
            dk_acc[...] = jnp.zeros_like(dk_acc)
            dv_acc[...] = jnp.zeros_like(dv_acc)
            df_acc[...] = jnp.zeros_like(df_acc)

        def step(diag):
            for hh in range(hpb):
                sl = slice(hh * DH, (hh + 1) * DH)
                q = q_ref[:, sl]
                do = do_ref[:, sl]
                p = _fox_probs_t(q, k_ref[:, sl], fq_ref[hh], fk_ref[hh], lse_ref[hh], diag)
                dv_acc[:, sl] += _nn(p, do)
                ds = p * (_nt(v_ref[:, sl], do) - dl_ref[hh])
                dk_acc[:, sl] += _nn(ds, q) * (DH ** -0.5)
                df_acc[hh] -= jnp.sum(ds, axis=1, keepdims=True)

        _fox_when_needed(i, j, step)

        @pl.when(i == nb - 1)
        def _():
            dk_ref[...] = dk_acc[...].astype(BF16)
            dv_ref[...] = dv_acc[...].astype(BF16)
            dfk_ref[...] = df_acc[...]

    qi = lambda j, i: jnp.maximum(i, j)
    qs = lambda base: pl.BlockSpec((t, wb), lambda h, j, i: (qi(j, i), base + h))
    ks = lambda base: pl.BlockSpec((t, wb), lambda h, j, i: (j, base + h))
    qrow = pl.BlockSpec((hpb, 1, t), lambda h, j, i: (h, 0, qi(j, i)))
    return pl.pallas_call(
        body, grid=(ng, nb, nb),
        in_specs=[qs(0), ks(ng), ks(2 * ng), qrow, qs(0), qrow, qrow,
                  pl.BlockSpec((hpb, t, 1), lambda h, j, i: (h, j, 0))],
        out_specs=[pl.BlockSpec((t, wb), lambda h, j, i: (j, h)), pl.BlockSpec((t, wb), lambda h, j, i: (j, h)),
                   pl.BlockSpec((hpb, t, 1), lambda h, j, i: (h, j, 0))],
        out_shape=[jax.ShapeDtypeStruct((s, HEADS * DH), BF16)] * 2 + [jax.ShapeDtypeStruct((HEADS, s, 1), F32)],
        scratch_shapes=[pltpu.VMEM((t, wb), F32), pltpu.VMEM((t, wb), F32), pltpu.VMEM((hpb, t, 1), F32)],
        compiler_params=_cparams(("parallel", "parallel", "arbitrary")), name=name)(
            u, u, u, delta, dout, lse_row, f_row, f_col)


def _fox_bwd_q(u, delta, lse_row, dout, f_col, f_row, name):
    s = u.shape[0]
    t = _pick(s, FOX_TILES)
    nb = s // t
    hpb = FOX_HPB
    wb = hpb * DH
    ng = HEADS // hpb

    def body(q_ref, k_ref, v_ref, dl_ref, do_ref, lse_ref, fq_ref, fk_ref, dq_ref, dfq_ref, dq_acc, df_acc):
        i, j = pl.program_id(1), pl.program_id(2)

        @pl.when(j == 0)
        def _():
            dq_acc[...] = jnp.zeros_like(dq_acc)
            df_acc[...] = jnp.zeros_like(df_acc)

        def step(diag):
            for hh in range(hpb):
                sl = slice(hh * DH, (hh + 1) * DH)
                do = do_ref[:, sl]
                k = k_ref[:, sl]
                p = _fox_probs_t(q_ref[:, sl], k, fq_ref[hh], fk_ref[hh], lse_ref[hh], diag)
                ds = p * (_nt(v_ref[:, sl], do) - dl_ref[hh])
                dq_acc[sl, :] += _tn(k, ds) * (DH ** -0.5)
                df_acc[hh] += jnp.sum(ds, axis=0, keepdims=True)

        _fox_when_needed(i, j, step)

        @pl.when(j == i)
        def _():
            for hh in range(hpb):
                sl = slice(hh * DH, (hh + 1) * DH)
                dq_ref[:, sl] = jnp.transpose(dq_acc[sl, :]).astype(BF16)
            dfq_ref[...] = df_acc[...]

    qs = lambda base: pl.BlockSpec((t, wb), lambda h, i, j: (i, base + h))
    ks = lambda base: pl.BlockSpec((t, wb), lambda h, i, j: (jnp.minimum(j, i), base + h))
    row = pl.BlockSpec((hpb, 1, t), lambda h, i, j: (h, 0, i))
    return pl.pallas_call(
        body, grid=(ng, nb, nb),
        in_specs=[qs(0), ks(ng), ks(2 * ng), row, qs(0), row, row,
                  pl.BlockSpec((hpb, t, 1), lambda h, i, j: (h, jnp.minimum(j, i), 0))],
        out_specs=[pl.BlockSpec((t, wb), lambda h, i, j: (i, h)), row],
        out_shape=[jax.ShapeDtypeStruct((s, HEADS * DH), BF16), jax.ShapeDtypeStruct((HEADS, 1, s), F32)],
        scratch_shapes=[pltpu.VMEM((wb, t), F32), pltpu.VMEM((hpb, 1, t), F32)],
        compiler_params=_cparams(("parallel", "parallel", "arbitrary")), name=name)(
            u, u, u, delta, dout, lse_row, f_row, f_col)


def _ca_bias(rel_bias):
    nh = rel_bias.shape[0]
    n_clip = CA_BAND - REL_CLIP
    gv = jnp.concatenate([jnp.broadcast_to(rel_bias[:, REL_TABLE - 1:], (nh, n_clip)),
                          rel_bias[:, REL_TABLE - 2::-1]], axis=1)
    b = jnp.stack([gv[:, CHUNK - 1 - qi:CHUNK - 1 - qi + CA_BAND] for qi in range(CHUNK)], axis=1)
    neg = lambda n: jnp.full((nh, CHUNK, n), NEG, F32)
    blocks = [jnp.concatenate([neg(a * CHUNK), b, neg(CA_WIN - CA_BAND - a * CHUNK)], axis=2)
              for a in range(CA_TILE // CHUNK)]
    return jnp.concatenate(blocks, axis=1)


def _ca_bias_grad(db):
    nh = db.shape[0]
    n_clip = CA_BAND - REL_CLIP
    d64 = sum(db[:, a * CHUNK:(a + 1) * CHUNK, a * CHUNK:a * CHUNK + CA_BAND] for a in range(CA_TILE // CHUNK))
    wide = jnp.pad(d64[:, ::-1, :], ((0, 0), (0, 0), (0, CHUNK))).reshape(nh, CHUNK * (CA_BAND + CHUNK))
    n_gv = CA_BAND + CHUNK - 1
    dgv = jnp.sum(wide[:, :CHUNK * n_gv].reshape(nh, CHUNK, n_gv), axis=1)
    return jnp.concatenate([dgv[:, n_clip:][:, ::-1], jnp.sum(dgv[:, :n_clip], axis=1, keepdims=True)], axis=1)


def _ca_specs():
    cur = lambda base: pl.BlockSpec((CA_BLK, DH), lambda h, i: (i, base + h))
    prev = lambda base: pl.BlockSpec((CA_BLK, DH), lambda h, i: (jnp.maximum(i - 1, 0), base + h))
    bias = pl.BlockSpec((1, CA_TILE, CA_WIN), lambda h, i: (h, 0, 0))
    return cur, prev, bias


def _ca_fill(kcat, vcat, kp_ref, kc_ref, vp_ref, vc_ref):
    kcat[0:CA_BLK, :] = kp_ref[...]
    kcat[CA_BLK:2 * CA_BLK, :] = kc_ref[...]
    vcat[0:CA_BLK, :] = vp_ref[...]
    vcat[CA_BLK:2 * CA_BLK, :] = vc_ref[...]


def _ca_probs(i, tl, q, kw, bias):
    sc = _nt(q, kw) * (DH ** -0.5) + bias
    col = lax.broadcasted_iota(jnp.int32, sc.shape, 1)
    sc = jnp.where((i - 1) * CA_BLK + tl * CA_TILE + col >= 0, sc, NEG)
    p = jnp.exp(sc - jnp.max(sc, axis=-1, keepdims=True))
    return p / jnp.sum(p, axis=-1, keepdims=True)


def _ca_fwd(u, bias, name):
    s = u.shape[0]
    cur, prev, bsp = _ca_specs()

    def body(q_ref, kp_ref, kc_ref, vp_ref, vc_ref, b_ref, o_ref, kcat, vcat):
        i = pl.program_id(1)
        _ca_fill(kcat, vcat, kp_ref, kc_ref, vp_ref, vc_ref)
        for tl in range(CA_BLK // CA_TILE):
            rows = slice(tl * CA_TILE, (tl + 1) * CA_TILE)
            win = slice(tl * CA_TILE, tl * CA_TILE + CA_WIN)
            p = _ca_probs(i, tl, q_ref[rows, :], kcat[win, :], b_ref[0])
            o_ref[rows, :] = _nn(p, vcat[win, :])

    return pl.pallas_call(
        body, grid=(HEADS, s // CA_BLK),
        in_specs=[cur(3 * HEADS), prev(4 * HEADS), cur(4 * HEADS), prev(5 * HEADS), cur(5 * HEADS), bsp],
        out_specs=pl.BlockSpec((CA_BLK, DH), lambda h, i: (i, h)),
        out_shape=jax.ShapeDtypeStruct((s, HEADS * DH), F32),
        scratch_shapes=[pltpu.VMEM((2 * CA_BLK, DH), F32)] * 2,
        compiler_params=_cparams(("parallel", "parallel")), name=name)(u, u, u, u, u, bias)


def _ca_bwd(u, bias, dout, name):
    s = u.shape[0]
    cur, prev, bsp = _ca_specs()
    rows_cat = 2 * CA_BLK

    def body(q_ref, kp_ref, kc_ref, vp_ref, vc_ref, b_ref, do_ref,
             dq_ref, dka_ref, dkb_ref, dva_ref, dvb_ref, db_ref, kcat, vcat, dkcat, dvcat):
        i = pl.program_id(1)

        @pl.when(i == 0)
        def _():
            db_ref[...] = jnp.zeros_like(db_ref)

        _ca_fill(kcat, vcat, kp_ref, kc_ref, vp_ref, vc_ref)
        dkcat[...] = jnp.zeros_like(dkcat)
        dvcat[...] = jnp.zeros_like(dvcat)
        for tl in range(CA_BLK // CA_TILE):
            rows = slice(tl * CA_TILE, (tl + 1) * CA_TILE)
            win = slice(tl * CA_TILE, tl * CA_TILE + CA_WIN)
            q = q_ref[rows, :]
            kw = kcat[win, :]
            vw = vcat[win, :]
            do = do_ref[rows, :]
            p = _ca_probs(i, tl, q, kw, b_ref[0])
            dp = _nt(do, vw)
            ds = p * (dp - jnp.sum(p * dp, axis=-1, keepdims=True))
            dq_ref[rows, :] = (_nn(ds, kw) * (DH ** -0.5)).astype(BF16)
            dkcat[win, :] += _tn(ds, q) * (DH ** -0.5)
            dvcat[win, :] += _tn(p, do)
            db_ref[0] += ds
        dkb_ref[...] = dkcat[0:CA_BLK, :]
        dka_ref[...] = dkcat[CA_BLK:2 * CA_BLK, :]
        dvb_ref[...] = dvcat[0:CA_BLK, :]
        dva_ref[...] = dvcat[CA_BLK:2 * CA_BLK, :]

    osp = pl.BlockSpec((CA_BLK, DH), lambda h, i: (i, h))
    shp = jax.ShapeDtypeStruct((s, HEADS * DH), F32)
    return pl.pallas_call(
        body, grid=(HEADS, s // CA_BLK),
        in_specs=[cur(3 * HEADS), prev(4 * HEADS), cur(4 * HEADS), prev(5 * HEADS), cur(5 * HEADS), bsp,
                  pl.BlockSpec((CA_BLK, DH), lambda h, i: (i, HEADS + h))],
        out_specs=[osp, osp, osp, osp, osp, bsp],
        out_shape=[jax.ShapeDtypeStruct((s, HEADS * DH), BF16), shp, shp, shp, shp,
                   jax.ShapeDtypeStruct((HEADS, CA_TILE, CA_WIN), F32)],
        scratch_shapes=[pltpu.VMEM((rows_cat, DH), F32)] * 4,
        compiler_params=_cparams(("parallel", "arbitrary")), name=name)(u, u, u, u, u, bias, dout)


def _ca_merge(da, db):
    shifted = jnp.concatenate([db[CA_BLK:], jnp.zeros((CA_BLK, db.shape[1]), F32)], axis=0)
    return (da + shifted).astype(BF16)


def _place():
    x, y, c = lax.axis_index("x"), lax.axis_index("y"), lax.axis_index("c")
    return x, y, c, [(1 - x, y), (x, 1 - y), (1 - x, 1 - y)]


_ANY = pl.BlockSpec(memory_space=pl.ANY)


def _all_gather_chips(w, name):
    def body(w_ref, o_ref, send_sems, recv_sems, loc_sems):
        _exchange(["ag"], [w_ref], [o_ref], send_sems, recv_sems, loc_sems, start=True)
        _exchange(["ag"], [w_ref], [o_ref], send_sems, recv_sems, loc_sems, start=False)

    return pl.pallas_call(
        body, in_specs=[_ANY], out_specs=_ANY, out_shape=jax.ShapeDtypeStruct((4,) + w.shape, w.dtype),
        scratch_shapes=[pltpu.SemaphoreType.DMA((3,)), pltpu.SemaphoreType.DMA((3,)), pltpu.SemaphoreType.DMA((1,))],
        name=name)(w)


def _all_gather_two_level(w, name):
    half = w.shape[0] // 2

    def body(w_ref, o_ref, send_sems, recv_sems, loc_sem):
        x, y, c, peers = _place()
        me = 2 * x + y
        sibling = (x, y, 1 - c)
        mine = pl.ds(pl.multiple_of(c * half, SUBLANES), half)
        other = pl.ds(pl.multiple_of((1 - c) * half, SUBLANES), half)

        def copy(k, src, dst, to):
            return pltpu.make_async_remote_copy(src_ref=src, dst_ref=dst, send_sem=send_sems.at[k],
                                                recv_sem=recv_sems.at[k], device_id=to, device_id_type=MESH)

        loc = pltpu.make_async_copy(w_ref, o_ref.at[me], loc_sem)
        loc.start()
        sends = [copy(k, w_ref.at[mine], o_ref.at[me, mine], (px, py, c)) for k, (px, py) in enumerate(peers)]
        for cp in sends:
            cp.start()
        for k, (px, py) in enumerate(peers):
            landed = o_ref.at[2 * px + py, mine]
            copy(k, w_ref.at[mine], landed, (px, py, c)).wait_recv()
            fwd = copy(3 + k, landed, landed, sibling)
            fwd.start()
            sends.append(fwd)
        for k, (px, py) in enumerate(peers):
            copy(3 + k, w_ref.at[other], o_ref.at[2 * px + py, other], sibling).wait_recv()
        for cp in sends:
            cp.wait_send()
        loc.wait()

    return pl.pallas_call(
        body, in_specs=[_ANY], out_specs=_ANY, out_shape=jax.ShapeDtypeStruct((4,) + w.shape, w.dtype),
        scratch_shapes=[pltpu.SemaphoreType.DMA((6,)), pltpu.SemaphoreType.DMA((6,)), pltpu.SemaphoreType.DMA(())],
        name=name)(w)


def _exchange(kinds, srcs, dsts, send_sems, recv_sems, loc_sems, start):
    x, y, c, peers = _place()
    me = 2 * x + y
    for n, kind in enumerate(kinds):
        src, dst = srcs[n], dsts[n]
        mine = src if kind == "ag" else src.at[me]
        loc = pltpu.make_async_copy(mine, dst.at[me], loc_sems.at[n])
        copies = []
        for k, (px, py) in enumerate(peers):
            out_src = src if kind == "ag" else src.at[2 * px + py]
            send = pltpu.make_async_remote_copy(src_ref=out_src, dst_ref=dst.at[me], send_sem=send_sems.at[3 * n + k],
                                                recv_sem=recv_sems.at[3 * n + k], device_id=(px, py, c),
                                                device_id_type=MESH)
            recv = pltpu.make_async_remote_copy(src_ref=mine, dst_ref=dst.at[2 * px + py],
                                                send_sem=send_sems.at[3 * n + k], recv_sem=recv_sems.at[3 * n + k],
                                                device_id=(px, py, c), device_id_type=MESH)
            copies.append((send, recv))
        if start:
            loc.start()
            for send, _ in copies:
                send.start()
        else:
            for _, recv in copies:
                recv.wait_recv()
            for send, _ in copies:
                send.wait_send()
            loc.wait()


def _core_swap(a, name):
    def body(a_ref, o_ref, send_sem, recv_sem):
        x, y, c, _ = _place()
        cp = pltpu.make_async_remote_copy(src_ref=a_ref, dst_ref=o_ref, send_sem=send_sem, recv_sem=recv_sem,
                                          device_id=(x, y, 1 - c), device_id_type=MESH)
        cp.start()
        cp.wait()

    return pl.pallas_call(
        body, in_specs=[_ANY], out_specs=_ANY, out_shape=jax.ShapeDtypeStruct(a.shape, a.dtype),
        scratch_shapes=[pltpu.SemaphoreType.DMA(()), pltpu.SemaphoreType.DMA(())], name=name)(a)


def _all_reduce_small(v, name):
    r, wl = v.shape

    def body(v_ref, o_ref, buf, send_sems, recv_sems):
        x, y, c, _ = _place()
        me = 4 * x + 2 * y + c
        buf[me] = v_ref[...]
        flips = [(fx, fy, fc) for fx in (0, 1) for fy in (0, 1) for fc in (0, 1) if (fx, fy, fc) != (0, 0, 0)]
        peer = lambda f: (x ^ f[0], y ^ f[1], c ^ f[2])
        sends = []
        for k, f in enumerate(flips):
            cp = pltpu.make_async_remote_copy(src_ref=v_ref, dst_ref=buf.at[me], send_sem=send_sems.at[k],
                                              recv_sem=recv_sems.at[k], device_id=peer(f), device_id_type=MESH)
            cp.start()
            sends.append(cp)
        for k, f in enumerate(flips):
            px, py, pc = peer(f)
            pltpu.make_async_remote_copy(src_ref=v_ref, dst_ref=buf.at[4 * px + 2 * py + pc], send_sem=send_sems.at[k],
                                         recv_sem=recv_sems.at[k], device_id=peer(f), device_id_type=MESH).wait_recv()
        for cp in sends:
            cp.wait_send()
        acc = buf[0]
        for d in range(1, 8):
            acc = acc + buf[d]
        o_ref[...] = acc

    vm = pl.BlockSpec(memory_space=pltpu.VMEM)
    return pl.pallas_call(
        body, in_specs=[vm], out_specs=vm, out_shape=jax.ShapeDtypeStruct((r, wl), F32),
        scratch_shapes=[pltpu.VMEM((8, r, wl), F32), pltpu.SemaphoreType.DMA((7,)), pltpu.SemaphoreType.DMA((7,))],
        name=name)(v)


def _sum_slots(g, name):
    _, r, cdim = g.shape
    br = _pick(r, (256, 128, 64, 32, 8))

    def body(g_ref, o_ref):
        o_ref[...] = ((g_ref[0].astype(F32) + g_ref[1].astype(F32)) + g_ref[2].astype(F32)) + g_ref[3].astype(F32)

    return pl.pallas_call(
        body, grid=(r // br,), in_specs=[pl.BlockSpec((4, br, cdim), lambda i: (0, i, 0))],
        out_specs=pl.BlockSpec((br, cdim), lambda i: (i, 0)), out_shape=jax.ShapeDtypeStruct((r, cdim), F32),
        compiler_params=_cparams(("parallel",)), name=name)(g)


def _adamw(w, grads, m, v, name):
    nl, r, cdim = w.shape
    br = _pick(r, (128, 64, 32, 8))
    c1 = 1.0 / (1.0 - ADAM_B1 ** ADAM_STEP)
    c2 = 1.0 / (1.0 - ADAM_B2 ** ADAM_STEP)
    counts = [len(t) for t in grads]
    flat = [a for t in grads for a in t]

    def body(*refs):
        w_ref = refs[0]
        g_refs = refs[1:1 + len(flat)]
        m_ref, v_ref, g_out, d_out, m_out, v_out = refs[1 + len(flat):]
        layer = pl.program_id(0)
        g, pos = None, 0
        for l, n in enumerate(counts):
            gl = g_refs[pos][...]
            for a in g_refs[pos + 1:pos + n]:
                gl = gl + a[...]
            pos += n
            g = gl if g is None else jnp.where(layer == l, gl, g)
        mn = ADAM_B1 * m_ref[...] + (1.0 - ADAM_B1) * g
        vn = ADAM_B2 * v_ref[...] + (1.0 - ADAM_B2) * (g * g)
        g_out[...] = g
        m_out[...] = mn
        v_out[...] = vn
        d_out[...] = -ADAM_LR * ((mn * c1) / (jnp.sqrt(vn * c2) + ADAM_EPS) + ADAM_WD * w_ref[...])

    blk = pl.BlockSpec((None, br, cdim), lambda l, i: (l, i, 0))
    gblk = pl.BlockSpec((br, cdim), lambda l, i: (i, 0))
    return pl.pallas_call(
        body, grid=(nl, r // br), in_specs=[blk] + [gblk] * len(flat) + [blk, blk], out_specs=[blk] * 4,
        out_shape=[jax.ShapeDtypeStruct((nl, r, cdim), F32)] * 4,
        compiler_params=_cparams(("parallel", "parallel")), name=name)(w, *flat, m, v)


def _unshard_cols(g):
    return jnp.transpose(g, (1, 0, 2)).reshape(g.shape[1], 4 * g.shape[2])


def _unshard_rows(g):
    return g.reshape(4 * g.shape[1], g.shape[2])


def _shard_cols(g):
    k, n = g.shape
    return jnp.transpose(g.reshape(k, 4, n // 4), (1, 0, 2))


def _shard_rows(g):
    k, n = g.shape
    return g.reshape(4, k // 4, n)


def _lower_bound(logits):
    return jnp.cumsum(jax.nn.softmax(logits.astype(F32), axis=0), axis=0)[0]


def _pack(parts):
    flat = jnp.concatenate([p.reshape(-1) for p in parts])
    n = flat.shape[0]
    rows = -(-n // 1024) * 8
    return jnp.pad(flat, (0, rows * 128 - n)).reshape(rows, 128)


def _unpack(packed, shapes):
    flat = packed.reshape(-1)
    out, off = [], 0
    for shp in shapes:
        n = 1
        for d in shp:
            n *= d
        out.append(flat[off:off + n].reshape(shp))
        off += n
    return out


def kernel(x, ev_w_in, ev_conv_w, ev_conv_b, ev_conv_ln_g, ev_conv_ln_b, hgrn_lb_logits, ev_gnorm_g, ev_w_out, od_w_in, fox_b_f, rel_bias, od_w_out, ln_mix_g, ln_mix_b, mlp_w1, mlp_w2, ln_mlp_g, ln_mlp_b, loss_target, m_ev_w_in, m_ev_conv_w, m_ev_conv_b, m_ev_conv_ln_g, m_ev_conv_ln_b, m_hgrn_lb_logits, m_ev_gnorm_g, m_ev_w_out, m_od_w_in, m_fox_b_f, m_rel_bias, m_od_w_out, m_ln_mix_g, m_ln_mix_b, m_mlp_w1, m_mlp_w2, m_ln_mlp_g, m_ln_mlp_b, v_ev_w_in, v_ev_conv_w, v_ev_conv_b, v_ev_conv_ln_g, v_ev_conv_ln_b, v_hgrn_lb_logits, v_ev_gnorm_g, v_ev_w_out, v_od_w_in, v_fox_b_f, v_rel_bias, v_od_w_out, v_ln_mix_g, v_ln_mix_b, v_mlp_w1, v_mlp_w2, v_ln_mlp_g, v_ln_mlp_b):
    w_sharded = dict(ev_w_in=ev_w_in, ev_w_out=ev_w_out, od_w_in=od_w_in, od_w_out=od_w_out, mlp_w1=mlp_w1, mlp_w2=mlp_w2)
    m_sharded = dict(ev_w_in=m_ev_w_in, ev_w_out=m_ev_w_out, od_w_in=m_od_w_in, od_w_out=m_od_w_out, mlp_w1=m_mlp_w1, mlp_w2=m_mlp_w2)
    v_sharded = dict(ev_w_in=v_ev_w_in, ev_w_out=v_ev_w_out, od_w_in=v_od_w_in, od_w_out=v_od_w_out, mlp_w1=v_mlp_w1, mlp_w2=v_mlp_w2)
    small_names = ["ev_conv_w", "ev_conv_b", "ev_conv_ln_g", "ev_conv_ln_b", "hgrn_lb_logits", "ev_gnorm_g", "fox_b_f",
                   "rel_bias", "ln_mix_g", "ln_mix_b", "ln_mlp_g", "ln_mlp_b"]
    w_small = dict(ev_conv_w=ev_conv_w, ev_conv_b=ev_conv_b, ev_conv_ln_g=ev_conv_ln_g, ev_conv_ln_b=ev_conv_ln_b,
                   hgrn_lb_logits=hgrn_lb_logits, ev_gnorm_g=ev_gnorm_g, fox_b_f=fox_b_f, rel_bias=rel_bias,
                   ln_mix_g=ln_mix_g, ln_mix_b=ln_mix_b, ln_mlp_g=ln_mlp_g, ln_mlp_b=ln_mlp_b)
    m_small = dict(ev_conv_w=m_ev_conv_w, ev_conv_b=m_ev_conv_b, ev_conv_ln_g=m_ev_conv_ln_g, ev_conv_ln_b=m_ev_conv_ln_b,
                   hgrn_lb_logits=m_hgrn_lb_logits, ev_gnorm_g=m_ev_gnorm_g, fox_b_f=m_fox_b_f, rel_bias=m_rel_bias,
                   ln_mix_g=m_ln_mix_g, ln_mix_b=m_ln_mix_b, ln_mlp_g=m_ln_mlp_g, ln_mlp_b=m_ln_mlp_b)
    v_small = dict(ev_conv_w=v_ev_conv_w, ev_conv_b=v_ev_conv_b, ev_conv_ln_g=v_ev_conv_ln_g, ev_conv_ln_b=v_ev_conv_ln_b,
                   hgrn_lb_logits=v_hgrn_lb_logits, ev_gnorm_g=v_ev_gnorm_g, fox_b_f=v_fox_b_f, rel_bias=v_rel_bias,
                   ln_mix_g=v_ln_mix_g, ln_mix_b=v_ln_mix_b, ln_mlp_g=v_ln_mlp_g, ln_mlp_b=v_ln_mlp_b)

    chip = 2 * lax.axis_index("x") + lax.axis_index("y")
    hw = HEADS * DH

    w_ev_in = _all_gather_two_level(ev_w_in[0].astype(BF16), "ag_ev_w_in")
    shards = dict(ev_w_out=ev_w_out[0].astype(BF16), od_w_in=od_w_in[0].astype(BF16), od_w_out=od_w_out[0].astype(BF16),
                  mlp_w1=[mlp_w1[l].astype(BF16) for l in range(DEPTH)],
                  mlp_w2=[mlp_w2[l].astype(BF16) for l in range(DEPTH)])
    tables = _all_gather_chips(jnp.concatenate(
        [ev_conv_w[0].reshape(1, -1), jnp.pad(rel_bias[0].reshape(1, -1), ((0, 0), (0, (-rel_bias[0].size) % 128)))],
        axis=1), "ag_tables")
    ncw = ev_conv_w[0].size
    cshard = ev_conv_w.shape[2]
    conv_w = jnp.transpose(tables[:, 0, :ncw].reshape(4, CONV_WIDTH, cshard), (1, 0, 2)).reshape(CONV_WIDTH, 4 * cshard)
    rshard = rel_bias.shape[2]
    rel_full = jnp.transpose(tables[:, 0, ncw:ncw + HEADS * rshard].reshape(4, HEADS, rshard), (1, 0, 2)).reshape(HEADS, 4 * rshard)

    loss_part, grad_x, g, small_partial = _local_step(
        x[0], loss_target[0], w_ev_in, shards, conv_w, rel_full, ev_conv_b, ev_conv_ln_g,
        ev_conv_ln_b, hgrn_lb_logits, ev_gnorm_g, fox_b_f, ln_mix_g, ln_mix_b, ln_mlp_g, ln_mlp_b)
    loss = lax.psum(loss_part, ("x", "y", "c"))

    full_shapes = [tuple(small_partial[n].shape) for n in small_names]
    reduced = _unpack(_all_reduce_small(_pack([small_partial[n] for n in small_names]), "ar_small"), full_shapes)
    g_small = {}
    for n, val in zip(small_names, reduced):
        if n == "ev_conv_w":
            val = lax.dynamic_slice_in_dim(val, chip * cshard, cshard, axis=1)
        elif n == "rel_bias":
            val = lax.dynamic_slice_in_dim(val, chip * rshard, rshard, axis=1)
        g_small[n] = val.reshape(w_small[n].shape)
    shard_shapes = [tuple(w_small[n].shape) for n in small_names]
    packed = [_pack([d[n] for n in small_names]) for d in (w_small, g_small, m_small, v_small)]
    small_out = _adamw(packed[0][None], [(packed[1],)], packed[2][None], packed[3][None], "adamw_small")
    out_small = {k: dict(zip(small_names, _unpack(val[0], shard_shapes))) for k, val in zip("gdmv", small_out)}

    out_big = {"g": {}, "d": {}, "m": {}, "v": {}}

    for name in ("ev_w_in", "ev_w_out", "od_w_in", "od_w_out", "mlp_w1", "mlp_w2"):
        pairs = []
        for l, slots in enumerate(g[name]):
            mine = _sum_slots(slots, "rs_%s_%d_sum" % (name, l))
            pairs.append((mine, _core_swap(mine, "rs_%s_%d_swap" % (name, l))))
        outs = _adamw(w_sharded[name], pairs, m_sharded[name], v_sharded[name], "adamw_%s" % name)
        for key, val in zip("gdmv", outs):
            out_big[key][name] = val

    order = ["ev_w_in", "ev_conv_w", "ev_conv_b", "ev_conv_ln_g", "ev_conv_ln_b", "hgrn_lb_logits", "ev_gnorm_g", "ev_w_out",
             "od_w_in", "fox_b_f", "rel_bias", "od_w_out", "ln_mix_g", "ln_mix_b", "mlp_w1", "mlp_w2", "ln_mlp_g", "ln_mlp_b"]

    def pick(key, n):
        return out_big[key][n] if n in out_big[key] else out_small[key][n]

    outs = [loss, grad_x[None]]
    for key in ("g", "d", "m", "v"):
        outs.extend(pick(key, n) for n in order)
    return tuple(outs)


def _local_step(xin, tgt, w_ev_in, shards, conv_w, rel_full, ev_conv_b, ev_conv_ln_g,
                ev_conv_ln_b, hgrn_lb_logits, ev_gnorm_g, fox_b_f, ln_mix_g, ln_mix_b, ln_mlp_g, ln_mlp_b):
    hw = HEADS * DH
    bf_pad = jnp.pad(fox_b_f[0], (0, 128 - HEADS)).reshape(1, 128)
    lb0 = _lower_bound(hgrn_lb_logits)
    ca_bias = _ca_bias(rel_full)
    ag = lambda w: ("ag", w)
    a2a = lambda g4: ("a2a", g4)

    xin_b = xin.astype(BF16)
    u0, (w_ev_out4,) = _matmul(xin_b, w_ev_in, b_sharded=True, comm=(ag(shards["ev_w_out"]),), name="l0_in")
    w_ev_out = _unshard_rows(w_ev_out4)
    a_out = _conv_fwd(u0, conv_w, ev_conv_b[0], ev_conv_ln_g[0], ev_conv_ln_b[0], "l0_conv")
    o_raw, b_out, states, (w1_0,) = _hgrn_fwd(u0, lb0, ev_gnorm_g[0], (ag(shards["mlp_w1"][0]),), "l0_hgrn")
    cat0 = jnp.concatenate([a_out, b_out], axis=1).astype(BF16)
    mix0, (w_od_out4,) = _matmul(cat0, w_ev_out, comm=(ag(shards["od_w_out"]),), name="l0_out")
    r0a, x0a, x0a_b = _ln_fwd(xin, mix0, ln_mix_g[0], ln_mix_b[0], "l0_ln_mix")
    (z0, h0), (w2_0,) = _matmul(x0a_b, w1_0, b_sharded=True, out_dtype=BF16, epi="relu2",
                                comm=(ag(shards["mlp_w2"][0]),), name="l0_mlp1")
    w2 = [_unshard_rows(w2_0), None]
    f0, (w_od_in4,) = _matmul(h0, w2[0], comm=(ag(shards["od_w_in"]),), name="l0_mlp2")
    w_od_in = _unshard_cols(w_od_in4)
    w_od_out = _unshard_rows(w_od_out4)
    w_od_main = jnp.concatenate([w_od_in[:, :3 * hw], w_od_in[:, 3 * hw + HEADS:]], axis=1)
    w_od_f = jnp.pad(w_od_in[:, 3 * hw:3 * hw + HEADS], ((0, 0), (0, 128 - HEADS)))
    r0b, x1, x1_b = _ln_fwd(x0a, f0, ln_mlp_g[0], ln_mlp_b[0], "l0_ln_mlp")
    u1, (w1_1,) = _matmul(x1_b, w_od_main, comm=(ag(shards["mlp_w1"][1]),), name="l1_in")
    w1 = [w1_0, w1_1]
    cf = _matmul(x1_b, w_od_f, name="l1_in_f")
    fcum = _fgate_fwd(cf, bf_pad, "l1_fgate")
    f_col = jnp.transpose(fcum[:, :HEADS])[:, :, None]
    f_row = jnp.transpose(fcum[:, :HEADS])[:, None, :]
    c_out, lse_row = _fox_fwd(u1, f_col, f_row, "l1_fox")
    d_out = _ca_fwd(u1, ca_bias, "l1_ca")
    cat1 = jnp.concatenate([c_out, d_out], axis=1).astype(BF16)
    mix1 = _matmul(cat1, w_od_out, name="l1_out")
    r1a, x1a, x1a_b = _ln_fwd(x1, mix1, ln_mix_g[1], ln_mix_b[1], "l1_ln_mix")
    (z1, h1), (w2_1,) = _matmul(x1a_b, w1[1], b_sharded=True, out_dtype=BF16, epi="relu2",
                                comm=(ag(shards["mlp_w2"][1]),), name="l1_mlp1")
    w2[1] = _unshard_rows(w2_1)
    f1 = _matmul(h1, w2[1], name="l1_mlp2")
    r1b, x2, _ = _ln_fwd(x1a, f1, ln_mlp_g[1], ln_mlp_b[1], "l1_ln_mlp")
    dy, loss_part = _loss_head(x2, tgt, "loss")

    g = {}
    dr, drb, dg_, db_ = _ln_bwd(dy, r1b, ln_mlp_g[1], "l1_ln_mlp_bwd")
    g_ln_mlp = [None, (dg_, db_)]
    gw = _shard_rows(_matmul(h1, drb, ta=True, out_dtype=BF16, name="l1_dw2"))
    dz, (gw2_1,) = _matmul(drb, w2[1], tb=True, out_dtype=BF16, epi="drelu2", extra=z1, comm=(a2a(gw),), name="l1_dz")
    gw = _matmul(x1a_b, dz, ta=True, out_dtype=BF16, out_sharded=True, name="l1_dw1")
    dx, (gw1_1,) = _matmul(dz, w1[1], tb=True, b_sharded=True, epi="add", extra=dr, comm=(a2a(gw),), name="l1_dx_mlp")
    dr, drb, dg_, db_ = _ln_bwd(dx, r1a, ln_mix_g[1], "l1_ln_mix_bwd")
    g_ln_mix = [None, (dg_, db_)]
    gw = _shard_rows(_matmul(cat1, drb, ta=True, out_dtype=BF16, name="l1_dwout"))
    dcat, g["od_w_out"] = _matmul(drb, w_od_out, tb=True, comm=(a2a(gw),), name="l1_dcat")
    delta = _fox_delta(c_out, dcat, "l1_fox_delta")
    dk_c, dv_c, dfk = _fox_bwd_kv(u1, delta, lse_row, dcat, f_col, f_row, "l1_fox_bwd_kv")
    dq_c, dfq = _fox_bwd_q(u1, delta, lse_row, dcat, f_col, f_row, "l1_fox_bwd_q")
    d_f = jnp.pad(jnp.transpose(dfk[:, :, 0] + dfq[:, 0, :]), ((0, 0), (0, 128 - HEADS)))
    dcf, dbf = _fgate_bwd(d_f, cf, bf_pad, "l1_fgate_bwd")
    dq_d, dka, dkb, dva, dvb, dbias = _ca_bwd(u1, ca_bias, dcat, "l1_ca_bwd")
    du1 = jnp.concatenate([dq_c, dk_c, dv_c, dq_d, _ca_merge(dka, dkb), _ca_merge(dva, dvb)], axis=1)
    g_main = _matmul(x1_b, du1, ta=True, out_dtype=BF16, name="l1_dwin")
    g_f = _matmul(x1_b, dcf, ta=True, out_dtype=BF16, name="l1_dwin_f")
    gw = _shard_cols(jnp.concatenate([g_main[:, :3 * hw], g_f[:, :HEADS], g_main[:, 3 * hw:]], axis=1))
    dx_f = _matmul(dcf, w_od_f, tb=True, epi="add", extra=dr, name="l1_dx_f")
    dx, g["od_w_in"] = _matmul(du1, w_od_main, tb=True, epi="add", extra=dx_f, scale=1.0, comm=(a2a(gw),),
                               name="l1_dx_in")
    dr, drb, dg_, db_ = _ln_bwd(dx, r0b, ln_mlp_g[0], "l0_ln_mlp_bwd")
    g_ln_mlp[0] = (dg_, db_)
    gw = _shard_rows(_matmul(h0, drb, ta=True, out_dtype=BF16, name="l0_dw2"))
    dz, (gw2_0,) = _matmul(drb, w2[0], tb=True, out_dtype=BF16, epi="drelu2", extra=z0, comm=(a2a(gw),), name="l0_dz")
    gw = _matmul(x0a_b, dz, ta=True, out_dtype=BF16, out_sharded=True, name="l0_dw1")
    dx, (gw1_0,) = _matmul(dz, w1[0], tb=True, b_sharded=True, epi="add", extra=dr, comm=(a2a(gw),), name="l0_dx_mlp")
    dr, drb, dg_, db_ = _ln_bwd(dx, r0a, ln_mix_g[0], "l0_ln_mix_bwd")
    g_ln_mix[0] = (dg_, db_)
    gw = _shard_rows(_matmul(cat0, drb, ta=True, out_dtype=BF16, name="l0_dwout"))
    dcat, g["ev_w_out"] = _matmul(drb, w_ev_out, tb=True, comm=(a2a(gw),), name="l0_dcat")
    dc, g_conv_w, g_conv_b, g_conv_lg, g_conv_lb = _conv_bwd_params(
        u0, dcat, conv_w, ev_conv_b[0], ev_conv_ln_g[0], ev_conv_ln_b[0], "l0_conv_bwd_p")
    da, dgate = _conv_bwd_input(u0, dc, conv_w, "l0_conv_bwd_i")
    dhq, dhf, dhi, dhg, g_lb0, g_gnorm = _hgrn_bwd(u0, o_raw, states, dcat, lb0, ev_gnorm_g[0], "l0_hgrn_bwd")
    du0 = jnp.concatenate([da, dgate, dhq, dhf, dhi, dhg], axis=1)
    gw = _matmul(xin_b, du0, ta=True, out_dtype=BF16, out_sharded=True, name="l0_dwin")
    grad_x, g["ev_w_in"] = _matmul(du0, w_ev_in, tb=True, b_sharded=True, epi="add", extra=dr, comm=(a2a(gw),),
                                   name="l0_dx_in")
    g["mlp_w1"] = [gw1_0, gw1_1]
    g["mlp_w2"] = [gw2_0, gw2_1]

    g_lb_logits = jax.vjp(_lower_bound, hgrn_lb_logits)[1](g_lb0)[0]
    g_rel = _ca_bias_grad(dbias)
    small_partial = dict(
        ev_conv_w=g_conv_w, ev_conv_b=g_conv_b, ev_conv_ln_g=g_conv_lg, ev_conv_ln_b=g_conv_lb,
        hgrn_lb_logits=g_lb_logits, ev_gnorm_g=g_gnorm, fox_b_f=dbf[:HEADS], rel_bias=g_rel,
        ln_mix_g=jnp.stack([g_ln_mix[0][0], g_ln_mix[1][0]]), ln_mix_b=jnp.stack([g_ln_mix[0][1], g_ln_mix[1][1]]),
        ln_mlp_g=jnp.stack([g_ln_mlp[0][0], g_ln_mlp[1][0]]), ln_mlp_b=jnp.stack([g_ln_mlp[0][1], g_ln_mlp[1][1]]))
    return loss_part, grad_x, g, small_partial
```

```python
import functools

import jax
import jax.numpy as jnp
from jax import lax
from jax.experimental import pallas as pl
from jax.experimental.pallas import tpu as pltpu

F32 = jnp.float32
BF16 = jnp.bfloat16
MESH = pl.DeviceIdType.MESH

DEPTH = 2
ALPHA = (2 * DEPTH) ** 0.25
LN_EPS = 1e-5
HEADS = 8
DH = 128
CHUNK = 64
CONV_WIDTH = 31
HALO = 32
CA_LEFT = 8
CA_BLK = CA_LEFT * CHUNK
CA_BAND = (CA_LEFT + 1) * CHUNK
CA_TILE = 8 * CHUNK
CA_WIN = CA_TILE + CA_LEFT * CHUNK
REL_CLIP = 256
REL_TABLE = (CHUNK - 1) + REL_CLIP + 1
NEG = -1e30

ADAM_LR = 0.001
ADAM_B1 = 0.9
ADAM_B2 = 0.999
ADAM_EPS = 1e-08
ADAM_WD = 0.01
ADAM_STEP = 10

VMEM_LIMIT = 48 * 1024 * 1024


def _cparams(sem):
    return pltpu.CompilerParams(dimension_semantics=sem, vmem_limit_bytes=VMEM_LIMIT)


def _pick(n, cands):
    for c in cands:
        if n % c == 0:
            return c
    return n


def _sigmoid(x):
    return 1.0 / (1.0 + jnp.exp(-x))


def _dot(a, b, dims):
    return lax.dot_general(a.astype(BF16), b.astype(BF16), (dims, ((), ())), preferred_element_type=F32)


def _nn(a, b):
    return _dot(a, b, ((1,), (0,)))


def _nt(a, b):
    return _dot(a, b, ((1,), (1,)))


def _tn(a, b):
    return _dot(a, b, ((0,), (0,)))


def _dot3(a, b, dims):
    a_hi = a.astype(BF16)
    b_hi = b.astype(BF16)
    a_lo = (a - a_hi.astype(F32)).astype(BF16)
    b_lo = (b - b_hi.astype(F32)).astype(BF16)
    dn = (dims, ((), ()))
    return (lax.dot_general(a_hi, b_hi, dn, preferred_element_type=F32)
            + (lax.dot_general(a_hi, b_lo, dn, preferred_element_type=F32)
               + lax.dot_general(a_lo, b_hi, dn, preferred_element_type=F32)))


def _nn3(a, b):
    return _dot3(a, b, ((1,), (0,)))


def _nt3(a, b):
    return _dot3(a, b, ((1,), (1,)))


def _tn3(a, b):
    return _dot3(a, b, ((0,), (0,)))


def _split3(x):
    hi = x.astype(BF16)
    r1 = x - hi.astype(F32)
    mid = r1.astype(BF16)
    lo = (r1 - mid.astype(F32)).astype(BF16)
    return hi, mid, lo


def _tri_matmul(tri, x, terms):
    parts = _split3(x)[:terms]
    acc = None
    for p in parts:
        t = lax.dot_general(tri, p, (((1,), (0,)), ((), ())), preferred_element_type=F32)
        acc = t if acc is None else acc + t
    return acc


def _tril(n, upper=False):
    r = lax.broadcasted_iota(jnp.int32, (n, n), 0)
    c = lax.broadcasted_iota(jnp.int32, (n, n), 1)
    m = (c >= r) if upper else (c <= r)
    return jnp.where(m, 1.0, 0.0).astype(BF16)


def _matmul(a, b, *, ta=False, tb=False, out_dtype=F32, epi=None, extra=None, scale=ALPHA, b_sharded=False,
            out_sharded=False, comm=(), name):
    m = a.shape[1] if ta else a.shape[0]
    kd = a.shape[0] if ta else a.shape[1]
    if b_sharded:
        shard = b.shape[2]
        n = b.shape[1] if tb else 4 * shard
        assert (b.shape[1] if not tb else 4 * shard) == kd
    else:
        n = b.shape[0] if tb else b.shape[1]
    bm = _pick(m, (1024, 512, 256, 128))
    bn = _pick(shard if (b_sharded and not tb) else (n // 4 if out_sharded else n), (1024, 768, 512, 256, 128))
    bk = _pick(shard if (b_sharded and tb) else kd, (2048, 1536, 1024, 768, 512, 256, 128))
    ni, nj, nk = m // bm, n // bn, kd // bk
    n_out = 2 if epi == "relu2" else 1
    n_comm = len(comm)
    kinds = [c[0] for c in comm]

    def body(*refs):
        a_ref, b_ref = refs[0], refs[1]
        pos = 2
        e_ref = refs[pos] if extra is not None else None
        pos += extra is not None
        c_in = refs[pos:pos + n_comm]
        pos += n_comm
        outs = refs[pos:pos + n_out]
        pos += n_out
        c_out = refs[pos:pos + n_comm]
        pos += n_comm
        acc_ref = refs[pos]
        sems = refs[pos + 1:]
        i, j, k = pl.program_id(0), pl.program_id(1), pl.program_id(2)

        if n_comm:
            @pl.when(jnp.logical_and(jnp.logical_and(i == 0, j == 0), k == 0))
            def _():
                _exchange(kinds, c_in, c_out, *sems, start=True)

        dims = ((0 if ta else 1,), (1 if tb else 0,))
        part = _dot(a_ref[...], b_ref[...], dims)
        if nk > 1:
            @pl.when(k == 0)
            def _():
                acc_ref[...] = part

            @pl.when(jnp.logical_and(k > 0, k < nk - 1))
            def _():
                acc_ref[...] += part

        @pl.when(k == nk - 1)
        def _():
            r = part + acc_ref[...] if nk > 1 else part
            if epi == "relu2":
                outs[0][...] = r.astype(out_dtype)
                outs[1][...] = jnp.square(jnp.maximum(r, 0.0)).astype(out_dtype)
            elif epi == "drelu2":
                outs[0][...] = (r * (2.0 * jnp.maximum(e_ref[...].astype(F32), 0.0))).astype(out_dtype)
            elif epi == "add":
                outs[0][...] = (r + scale * e_ref[...].astype(F32)).astype(out_dtype)
            else:
                outs[0][...] = r.astype(out_dtype)

        if n_comm:
            @pl.when(jnp.logical_and(jnp.logical_and(i == ni - 1, j == nj - 1), k == nk - 1))
            def _():
                _exchange(kinds, c_in, c_out, *sems, start=False)

    a_spec = pl.BlockSpec((bk, bm), lambda i, j, k: (k, i)) if ta else pl.BlockSpec((bm, bk), lambda i, j, k: (i, k))
    if b_sharded and tb:
        per = shard // bk
        b_spec = pl.BlockSpec((None, bn, bk), lambda i, j, k: (k // per, j, k % per))
    elif b_sharded:
        per = shard // bn
        b_spec = pl.BlockSpec((None, bk, bn), lambda i, j, k: (j // per, k, j % per))
    elif tb:
        b_spec = pl.BlockSpec((bn, bk), lambda i, j, k: (j, k))
    else:
        b_spec = pl.BlockSpec((bk, bn), lambda i, j, k: (k, j))
    e_spec = pl.BlockSpec((bm, bn), lambda i, j, k: (i, j))
    if out_sharded:
        per_o = (n // 4) // bn
        o_spec = pl.BlockSpec((None, bm, bn), lambda i, j, k: (j // per_o, i, j % per_o))
        o_shape = jax.ShapeDtypeStruct((4, m, n // 4), out_dtype)
    else:
        o_spec = e_spec
        o_shape = jax.ShapeDtypeStruct((m, n), out_dtype)
    in_specs = [a_spec, b_spec] + ([e_spec] if extra is not None else []) + [_ANY] * n_comm
    args = (a, b) + ((extra,) if extra is not None else ()) + tuple(c[1] for c in comm)
    c_shapes = [jax.ShapeDtypeStruct((4,) + c[1].shape if c[0] == "ag" else c[1].shape, c[1].dtype) for c in comm]
    scratch = [pltpu.VMEM((bm, bn) if nk > 1 else (8, 128), F32)]
    if n_comm:
        scratch += [pltpu.SemaphoreType.DMA((3 * n_comm,)), pltpu.SemaphoreType.DMA((3 * n_comm,)),
                    pltpu.SemaphoreType.DMA((n_comm,))]
    sem = ("arbitrary",) * 3 if n_comm else ("parallel", "parallel", "arbitrary")
    res = pl.pallas_call(
        body, grid=(ni, nj, nk), in_specs=in_specs,
        out_specs=[o_spec] * n_out + [_ANY] * n_comm, out_shape=[o_shape] * n_out + c_shapes,
        scratch_shapes=scratch, compiler_params=_cparams(sem), name=name)(*args)
    main = tuple(res[:n_out]) if n_out == 2 else res[0]
    return (main, list(res[n_out:])) if n_comm else main


def _ln_fwd(x, mix, g, b, name):
    s, d = x.shape
    br = _pick(s, (256, 128, 64, 8))

    def body(x_ref, m_ref, g_ref, b_ref, r_ref, y_ref, yb_ref):
        r = ALPHA * x_ref[...] + m_ref[...]
        mu = jnp.mean(r, axis=-1, keepdims=True)
        dlt = r - mu
        var = jnp.mean(dlt * dlt, axis=-1, keepdims=True)
        y = dlt * lax.rsqrt(var + LN_EPS) * g_ref[...] + b_ref[...]
        r_ref[...] = r
        y_ref[...] = y
        yb_ref[...] = y.astype(BF16)

    row = pl.BlockSpec((br, d), lambda i: (i, 0))
    vec = pl.BlockSpec((1, d), lambda i: (0, 0))
    return pl.pallas_call(
        body, grid=(s // br,), in_specs=[row, row, vec, vec], out_specs=[row, row, row],
        out_shape=[jax.ShapeDtypeStruct((s, d), F32), jax.ShapeDtypeStruct((s, d), F32),
                   jax.ShapeDtypeStruct((s, d), BF16)],
        compiler_params=_cparams(("parallel",)), name=name)(x, mix, g.reshape(1, d), b.reshape(1, d))


def _ln_bwd(dy, r, g, name):
    s, d = r.shape
    br = _pick(s, (256, 128, 64, 8))

    def body(dy_ref, r_ref, g_ref, dr_ref, drb_ref, dg_ref, db_ref):
        @pl.when(pl.program_id(0) == 0)
        def _():
            dg_ref[...] = jnp.zeros_like(dg_ref)
            db_ref[...] = jnp.zeros_like(db_ref)

        rv = r_ref[...]
        dyv = dy_ref[...]
        mu = jnp.mean(rv, axis=-1, keepdims=True)
        dlt = rv - mu
        var = jnp.mean(dlt * dlt, axis=-1, keepdims=True)
        rstd = lax.rsqrt(var + LN_EPS)
        xhat = dlt * rstd
        dxh = dyv * g_ref[...]
        m1 = jnp.mean(dxh, axis=-1, keepdims=True)
        m2 = jnp.mean(dxh * xhat, axis=-1, keepdims=True)
        dr = rstd * (dxh - m1 - xhat * m2)
        dr_ref[...] = dr
        drb_ref[...] = dr.astype(BF16)
        dg_ref[...] += jnp.sum(dyv * xhat, axis=0, keepdims=True)
        db_ref[...] += jnp.sum(dyv, axis=0, keepdims=True)

    row = pl.BlockSpec((br, d), lambda i: (i, 0))
    vec = pl.BlockSpec((1, d), lambda i: (0, 0))
    dr, drb, dg, db = pl.pallas_call(
        body, grid=(s // br,), in_specs=[row, row, vec], out_specs=[row, row, vec, vec],
        out_shape=[jax.ShapeDtypeStruct((s, d), F32), jax.ShapeDtypeStruct((s, d), BF16),
                   jax.ShapeDtypeStruct((1, d), F32), jax.ShapeDtypeStruct((1, d), F32)],
        compiler_params=_cparams(("arbitrary",)), name=name)(dy, r, g.reshape(1, d))
    return dr, drb, dg[0], db[0]


def _loss_head(y, tgt, name):
    s, d = y.shape
    br = _pick(s, (256, 128, 64, 8))

    def body(y_ref, t_ref, dy_ref, l_ref):
        @pl.when(pl.program_id(0) == 0)
        def _():
            l_ref[...] = jnp.zeros_like(l_ref)

        e = y_ref[...] - t_ref[...]
        dy_ref[...] = e * (1.0 / d)
        rows = jnp.sum(e * e, axis=-1, keepdims=True) * (0.5 / d)
        l_ref[...] += jnp.sum(rows, axis=0, keepdims=True)

    row = pl.BlockSpec((br, d), lambda i: (i, 0))
    dy, l = pl.pallas_call(
        body, grid=(s // br,), in_specs=[row, row],
        out_specs=[row, pl.BlockSpec((1, 1), lambda i: (0, 0))],
        out_shape=[jax.ShapeDtypeStruct((s, d), F32), jax.ShapeDtypeStruct((1, 1), F32)],
        compiler_params=_cparams(("arbitrary",)), name=name)(y, tgt)
    return dy, l[0, 0]


SUBLANES = 8


def _shifted_rows(ext_ref, sh_ref, tt):
    n = tt + HALO - SUBLANES
    for r in range(1, SUBLANES):
        sh_ref[r - 1] = ext_ref[pl.ds(r, n), :]

    def tap(o):
        r = o % SUBLANES
        if r == 0:
            return ext_ref[pl.ds(o, tt), :]
        return sh_ref[r - 1, pl.ds(o - r, tt), :]

    return tap


def _conv_scratch(tt, cc):
    return [pltpu.VMEM((tt + HALO, cc), F32), pltpu.VMEM((SUBLANES - 1, tt + HALO - SUBLANES, cc), F32)]


def _conv_recompute(i, a_ref, gt_ref, ah_ref, gh_ref, w_ref, cb_ref, hext_ref, sh_ref, tt):
    h = a_ref[...] * _sigmoid(gt_ref[...])
    hh = ah_ref[...] * _sigmoid(gh_ref[...])
    hh = jnp.where(i > 0, hh, 0.0)
    hext_ref[0:HALO, :] = hh
    hext_ref[HALO:HALO + tt, :] = h
    tap = _shifted_rows(hext_ref, sh_ref, tt)
    acc = jnp.zeros_like(h) + cb_ref[...]
    off = HALO - (CONV_WIDTH - 1)
    for j in range(CONV_WIDTH):
        acc = acc + w_ref[j:j + 1, :] * tap(off + j)
    mu = jnp.mean(acc, axis=-1, keepdims=True)
    dlt = acc - mu
    var = jnp.mean(dlt * dlt, axis=-1, keepdims=True)
    rstd = lax.rsqrt(var + LN_EPS)
    return dlt * rstd, rstd, tap


def _conv_specs(tt, cc, s):
    nh = tt // HALO
    cur = lambda cb: pl.BlockSpec((tt, cc), lambda i: (i, cb))
    prev = lambda cb: pl.BlockSpec((HALO, cc), lambda i: (jnp.maximum(i * nh - 1, 0), cb))
    vec = pl.BlockSpec((1, cc), lambda i: (0, 0))
    wsp = pl.BlockSpec((HALO, cc), lambda i: (0, 0))
    return cur, prev, vec, wsp


def _pad_conv_w(w):
    return jnp.concatenate([w, jnp.zeros((HALO - CONV_WIDTH, w.shape[1]), F32)], axis=0)


def _conv_fwd(u, w, cb, lg, lb, name):
    s = u.shape[0]
    cc = w.shape[1]
    tt = _pick(s, (256, 128, 64))
    cur, prev, vec, wsp = _conv_specs(tt, cc, s)

    def body(a_ref, gt_ref, ah_ref, gh_ref, w_ref, cb_ref, lg_ref, lb_ref, o_ref, hext_ref, sh_ref):
        xhat, _, _ = _conv_recompute(pl.program_id(0), a_ref, gt_ref, ah_ref, gh_ref, w_ref, cb_ref, hext_ref,
                                     sh_ref, tt)
        nrm = xhat * lg_ref[...] + lb_ref[...]
        o_ref[...] = nrm * _sigmoid(nrm)

    return pl.pallas_call(
        body, grid=(s // tt,), in_specs=[cur(0), cur(1), prev(0), prev(1), wsp, vec, vec, vec],
        out_specs=pl.BlockSpec((tt, cc), lambda i: (i, 0)), out_shape=jax.ShapeDtypeStruct((s, cc), F32),
        scratch_shapes=_conv_scratch(tt, cc),
        compiler_params=_cparams(("parallel",)), name=name)(
            u, u, u, u, _pad_conv_w(w), cb.reshape(1, cc), lg.reshape(1, cc), lb.reshape(1, cc))


def _conv_bwd_params(u, dout, w, cb, lg, lb, name):
    s = u.shape[0]
    cc = w.shape[1]
    tt = _pick(s, (256, 128, 64))
    cur, prev, vec, wsp = _conv_specs(tt, cc, s)

    def body(a_ref, gt_ref, ah_ref, gh_ref, w_ref, cb_ref, lg_ref, lb_ref, do_ref,
             dc_ref, dw_ref, dcb_ref, dlg_ref, dlb_ref, hext_ref, sh_ref, dw_acc):
        i = pl.program_id(0)

        @pl.when(i == 0)
        def _():
            dw_acc[...] = jnp.zeros_like(dw_acc)
            dcb_ref[...] = jnp.zeros_like(dcb_ref)
            dlg_ref[...] = jnp.zeros_like(dlg_ref)
            dlb_ref[...] = jnp.zeros_like(dlb_ref)

        xhat, rstd, tap = _conv_recompute(i, a_ref, gt_ref, ah_ref, gh_ref, w_ref, cb_ref, hext_ref, sh_ref, tt)
        nrm = xhat * lg_ref[...] + lb_ref[...]
        sg = _sigmoid(nrm)
        dn = do_ref[...] * (sg * (1.0 + nrm * (1.0 - sg)))
        dxh = dn * lg_ref[...]
        m1 = jnp.mean(dxh, axis=-1, keepdims=True)
        m2 = jnp.mean(dxh * xhat, axis=-1, keepdims=True)
        dc = rstd * (dxh - m1 - xhat * m2)
        dc_ref[...] = dc
        dlg_ref[...] += jnp.sum(dn * xhat, axis=0, keepdims=True)
        dlb_ref[...] += jnp.sum(dn, axis=0, keepdims=True)
        dcb_ref[...] += jnp.sum(dc, axis=0, keepdims=True)
        off = HALO - (CONV_WIDTH - 1)
        for j in range(CONV_WIDTH):
            dw_acc[j] += jnp.sum((dc * tap(off + j)).reshape(tt // SUBLANES, SUBLANES, cc), axis=0)

        @pl.when(i == nblk - 1)
        def _():
            dw_ref[...] = jnp.zeros_like(dw_ref)
            for j in range(CONV_WIDTH):
                dw_ref[j:j + 1, :] = jnp.sum(dw_acc[j], axis=0, keepdims=True)

    nblk = s // tt
    dcol = pl.BlockSpec((tt, cc), lambda i: (i, 0))
    dc, dw, dcb, dlg, dlb = pl.pallas_call(
        body, grid=(nblk,), in_specs=[cur(0), cur(1), prev(0), prev(1), wsp, vec, vec, vec, dcol],
        out_specs=[dcol, wsp, vec, vec, vec],
        out_shape=[jax.ShapeDtypeStruct((s, cc), F32), jax.ShapeDtypeStruct((HALO, cc), F32)]
        + [jax.ShapeDtypeStruct((1, cc), F32)] * 3,
        scratch_shapes=_conv_scratch(tt, cc) + [pltpu.VMEM((HALO, SUBLANES, cc), F32)],
        compiler_params=_cparams(("arbitrary",)), name=name)(
            u, u, u, u, _pad_conv_w(w), cb.reshape(1, cc), lg.reshape(1, cc), lb.reshape(1, cc), dout)
    return dc, dw[:CONV_WIDTH], dcb[0], dlg[0], dlb[0]


def _conv_bwd_input(u, dc, w, name):
    s = u.shape[0]
    cc = w.shape[1]
    tt = _pick(s, (256, 128, 64))
    nh = tt // HALO
    nlast = s // HALO - 1
    cur = lambda cb: pl.BlockSpec((tt, cc), lambda i: (i, cb))
    nxt = pl.BlockSpec((HALO, cc), lambda i: (jnp.minimum((i + 1) * nh, nlast), 0))
    wsp = pl.BlockSpec((HALO, cc), lambda i: (0, 0))
    nblk = s // tt

    def body(a_ref, gt_ref, dc_ref, dn_ref, w_ref, da_ref, dg_ref, ext_ref, sh_ref):
        i = pl.program_id(0)
        ext_ref[0:tt, :] = dc_ref[...]
        ext_ref[tt:tt + HALO, :] = jnp.where(i < nblk - 1, dn_ref[...], 0.0)
        tap = _shifted_rows(ext_ref, sh_ref, tt)
        dh = jnp.zeros((tt, cc), F32)
        for j in range(CONV_WIDTH):
            dh = dh + w_ref[j:j + 1, :] * tap(CONV_WIDTH - 1 - j)
        a = a_ref[...]
        sg = _sigmoid(gt_ref[...])
        da_ref[...] = (dh * sg).astype(BF16)
        dg_ref[...] = (dh * a * sg * (1.0 - sg)).astype(BF16)

    ocol = pl.BlockSpec((tt, cc), lambda i: (i, 0))
    return pl.pallas_call(
        body, grid=(nblk,), in_specs=[cur(0), cur(1), ocol, nxt, wsp], out_specs=[ocol, ocol],
        out_shape=[jax.ShapeDtypeStruct((s, cc), BF16)] * 2,
        scratch_shapes=_conv_scratch(tt, cc),
        compiler_params=_cparams(("parallel",)), name=name)(u, u, dc, dc, _pad_conv_w(w))


def _hgrn_gates(hq, hf, lb):
    sg = _sigmoid(hf)
    f = lb + (1.0 - lb) * sg
    lf = jnp.log(f)
    big_l = _tri_matmul(_tril(CHUNK), lf, 3)
    l_end = jnp.sum(lf, axis=0, keepdims=True)
    l_mid = jnp.sum(lf[0:CHUNK // 2, :], axis=0, keepdims=True)
    sq = _sigmoid(hq)
    q = hq * sq
    return sg, f, 1.0 - f, big_l, l_end, l_mid, sq, q


def _causal_mask(n):
    r = lax.broadcasted_iota(jnp.int32, (n, n), 0)
    c = lax.broadcasted_iota(jnp.int32, (n, n), 1)
    return c <= r


def _hgrn_fwd(u, lb, gg, comm, name):
    s = u.shape[0]
    w = HEADS * DH
    nch = s // CHUNK
    n_comm = len(comm)
    kinds = [c[0] for c in comm]

    def body(*refs):
        q_ref, f_ref, i_ref, g_ref, lb_ref, gg_ref = refs[:6]
        c_in = refs[6:6 + n_comm]
        o_ref, out_ref, st_ref = refs[6 + n_comm:9 + n_comm]
        c_out = refs[9 + n_comm:9 + 2 * n_comm]
        state = refs[9 + 2 * n_comm]
        sems = refs[10 + 2 * n_comm:]

        @pl.when(pl.program_id(0) == 0)
        def _():
            state[...] = jnp.zeros_like(state)
            _exchange(kinds, c_in, c_out, *sems, start=True)

        mask = _causal_mask(CHUNK)
        for hd in range(HEADS):
            sl = slice(hd * DH, (hd + 1) * DH)
            _, _, kk, big_l, l_end, l_mid, _, q = _hgrn_gates(q_ref[:, sl], f_ref[:, sl], lb_ref[:, sl])
            v = i_ref[:, sl]
            qs = q * jnp.exp(big_l - l_mid)
            ks = kk * jnp.exp(l_mid - big_l)
            att = jnp.where(mask, _nt3(qs, ks), 0.0)
            st0 = state[hd]
            st_ref[0, hd] = st0
            o = _nn3(att, v) + _nt3(q * jnp.exp(big_l), st0)
            state[hd] = st0 * jnp.exp(l_end) + _tn3(v, kk * jnp.exp(l_end - big_l))
            o_ref[:, sl] = o
            on = o * lax.rsqrt(jnp.mean(o * o, axis=-1, keepdims=True) + LN_EPS)
            gv = g_ref[:, sl]
            out_ref[:, sl] = on * gg_ref[:, sl] * (gv * _sigmoid(gv))

        @pl.when(pl.program_id(0) == nch - 1)
        def _():
            _exchange(kinds, c_in, c_out, *sems, start=False)

    col = lambda cb: pl.BlockSpec((CHUNK, w), lambda i: (i, cb))
    vec = pl.BlockSpec((1, w), lambda i: (0, 0))
    ocol = pl.BlockSpec((CHUNK, w), lambda i: (i, 0))
    c_shapes = [jax.ShapeDtypeStruct((4,) + c[1].shape if c[0] == "ag" else c[1].shape, c[1].dtype) for c in comm]
    res = pl.pallas_call(
        body, grid=(nch,), in_specs=[col(2), col(3), col(4), col(5), vec, vec] + [_ANY] * n_comm,
        out_specs=[ocol, ocol, pl.BlockSpec((1, HEADS, DH, DH), lambda i: (i, 0, 0, 0))] + [_ANY] * n_comm,
        out_shape=[jax.ShapeDtypeStruct((s, w), F32), jax.ShapeDtypeStruct((s, w), F32),
                   jax.ShapeDtypeStruct((nch, HEADS, DH, DH), F32)] + c_shapes,
        scratch_shapes=[pltpu.VMEM((HEADS, DH, DH), F32), pltpu.SemaphoreType.DMA((3 * n_comm,)),
                        pltpu.SemaphoreType.DMA((3 * n_comm,)), pltpu.SemaphoreType.DMA((n_comm,))],
        compiler_params=_cparams(("arbitrary",)), name=name)(
            u, u, u, u, lb.reshape(1, w), gg.reshape(1, w), *[c[1] for c in comm])
    return res[0], res[1], res[2], list(res[3:])


def _hgrn_bwd(u, o_raw, states, dout, lb, gg, name):
    s = u.shape[0]
    w = HEADS * DH
    nch = s // CHUNK

    def body(q_ref, f_ref, i_ref, g_ref, o_ref, st_ref, do_ref, lb_ref, gg_ref,
             dq_ref, df_ref, di_ref, dg_ref, dlb_ref, dgg_ref, dstate):
        @pl.when(pl.program_id(0) == 0)
        def _():
            dstate[...] = jnp.zeros_like(dstate)
            dlb_ref[...] = jnp.zeros_like(dlb_ref)
            dgg_ref[...] = jnp.zeros_like(dgg_ref)

        mask = _causal_mask(CHUNK)
        last_row = lax.broadcasted_iota(jnp.int32, (CHUNK, DH), 0) == CHUNK - 1
        tri_up = _tril(CHUNK, upper=True)
        for hd in range(HEADS):
            sl = slice(hd * DH, (hd + 1) * DH)
            hq = q_ref[:, sl]
            lbv = lb_ref[:, sl]
            sg, f, kk, big_l, l_end, l_mid, sq, q = _hgrn_gates(hq, f_ref[:, sl], lbv)
            v = i_ref[:, sl]
            e_l = jnp.exp(big_l)
            e_qm = jnp.exp(big_l - l_mid)
            e_km = jnp.exp(l_mid - big_l)
            e_ke = jnp.exp(l_end - big_l)
            e_end = jnp.exp(l_end)
            qs = q * e_qm
            ks = kk * e_km
            qe = q * e_l
            ke = kk * e_ke
            att = jnp.where(mask, _nt3(qs, ks), 0.0)
            st0 = st_ref[0, hd]
            dst1 = dstate[hd]
            o = o_ref[:, sl]
            rinv = lax.rsqrt(jnp.mean(o * o, axis=-1, keepdims=True) + LN_EPS)
            on = o * rinv
            gv = g_ref[:, sl]
            sgg = _sigmoid(gv)
            gsil = gv * sgg
            ggv = gg_ref[:, sl]
            dov = do_ref[:, sl]
            don = dov * ggv * gsil
            dg_ref[:, sl] = (dov * on * ggv * (sgg * (1.0 + gv * (1.0 - sgg)))).astype(BF16)
            dgg_ref[:, sl] += jnp.sum(dov * on * gsil, axis=0, keepdims=True)
            do = rinv * (don - on * jnp.mean(don * on, axis=-1, keepdims=True))
            datt = jnp.where(mask, _nt3(do, v), 0.0)
            dv = _tn3(att, do) + _nt3(ke, dst1)
            dqs = _nn3(datt, ks)
            dks = _tn3(datt, qs)
            dqe = _nn3(do, st0)
            dke = _nn3(v, dst1)
            dq = dqs * e_qm + dqe * e_l
            dk = dks * e_km + dke * e_ke
            dke_ke = dke * ke
            dl = dqs * qs - dks * ks + dqe * qe - dke_ke
            dl_end = jnp.sum(dke_ke, axis=0, keepdims=True) + jnp.sum(dst1 * st0, axis=0, keepdims=True) * e_end
            dl = dl + jnp.where(last_row, dl_end, 0.0)
            dlf = _tri_matmul(tri_up, dl, 2)
            dfv = dlf / f - dk
            df_ref[:, sl] = (dfv * (1.0 - lbv) * sg * (1.0 - sg)).astype(BF16)
            dlb_ref[:, sl] += jnp.sum(dfv * (1.0 - sg), axis=0, keepdims=True)
            dq_ref[:, sl] = (dq * (sq * (1.0 + hq * (1.0 - sq)))).astype(BF16)
            di_ref[:, sl] = dv.astype(BF16)
            dstate[hd] = dst1 * e_end + _tn3(do, qe)

    rev = lambda i: nch - 1 - i
    col = lambda cb: pl.BlockSpec((CHUNK, w), lambda i: (rev(i), cb))
    vec = pl.BlockSpec((1, w), lambda i: (0, 0))
    ocol = pl.BlockSpec((CHUNK, w), lambda i: (rev(i), 0))
    res = pl.pallas_call(
        body, grid=(nch,),
        in_specs=[col(2), col(3), col(4), col(5), ocol,
                  pl.BlockSpec((1, HEADS, DH, DH), lambda i: (rev(i), 0, 0, 0)), col(1), vec, vec],
        out_specs=[ocol, ocol, ocol, ocol, vec, vec],
        out_shape=[jax.ShapeDtypeStruct((s, w), BF16)] * 4 + [jax.ShapeDtypeStruct((1, w), F32)] * 2,
        scratch_shapes=[pltpu.VMEM((HEADS, DH, DH), F32)],
        compiler_params=_cparams(("arbitrary",)), name=name)(
            u, u, u, u, o_raw, states, dout, lb.reshape(1, w), gg.reshape(1, w))
    return res[0], res[1], res[2], res[3], res[4][0], res[5][0]


def _log_sigmoid(x):
    return jnp.minimum(x, 0.0) - jnp.log(1.0 + jnp.exp(-jnp.abs(x)))


def _fgate_fwd(cf, bf, name):
    s, wl = cf.shape
    tb = _pick(s, (512, 256, 128, 64))

    def body(c_ref, b_ref, f_ref, carry):
        @pl.when(pl.program_id(0) == 0)
        def _():
            carry[...] = jnp.zeros_like(carry)

        ls = _log_sigmoid(c_ref[...] + b_ref[...])
        f_ref[...] = _tri_matmul(_tril(tb), ls, 3) + carry[...]
        carry[...] += jnp.sum(ls, axis=0, keepdims=True)

    return pl.pallas_call(
        body, grid=(s // tb,), in_specs=[pl.BlockSpec((tb, wl), lambda i: (i, 0)), pl.BlockSpec((1, wl), lambda i: (0, 0))],
        out_specs=pl.BlockSpec((tb, wl), lambda i: (i, 0)), out_shape=jax.ShapeDtypeStruct((s, wl), F32),
        scratch_shapes=[pltpu.VMEM((1, wl), F32)],
        compiler_params=_cparams(("arbitrary",)), name=name)(cf, bf)


def _fgate_bwd(dF, cf, bf, name):
    s, wl = cf.shape
    tb = _pick(s, (512, 256, 128, 64))
    nb = s // tb

    def body(d_ref, c_ref, b_ref, dc_ref, db_ref, carry):
        @pl.when(pl.program_id(0) == 0)
        def _():
            carry[...] = jnp.zeros_like(carry)
            db_ref[...] = jnp.zeros_like(db_ref)

        dv = d_ref[...]
        dls = _tri_matmul(_tril(tb, upper=True), dv, 3) + carry[...]
        carry[...] += jnp.sum(dv, axis=0, keepdims=True)
        dc = dls * (1.0 - _sigmoid(c_ref[...] + b_ref[...]))
        dc_ref[...] = dc.astype(BF16)
        db_ref[...] += jnp.sum(dc, axis=0, keepdims=True)

    blk = pl.BlockSpec((tb, wl), lambda i: (nb - 1 - i, 0))
    vec = pl.BlockSpec((1, wl), lambda i: (0, 0))
    dc, db = pl.pallas_call(
        body, grid=(nb,), in_specs=[blk, blk, vec], out_specs=[blk, vec],
        out_shape=[jax.ShapeDtypeStruct((s, wl), BF16), jax.ShapeDtypeStruct((1, wl), F32)],
        scratch_shapes=[pltpu.VMEM((1, wl), F32)],
        compiler_params=_cparams(("arbitrary",)), name=name)(dF, cf, bf)
    return dc, db[0]


FOX_HPB = 2
FOX_TILES = (1024, 512, 256, 128)


def _fox_scores_t(q, k, fq_row, fk_col, diag):
    sc = _nt(k, q) * (DH ** -0.5) + fq_row - fk_col
    if not diag:
        return sc
    r = lax.broadcasted_iota(jnp.int32, sc.shape, 0)
    c = lax.broadcasted_iota(jnp.int32, sc.shape, 1)
    return jnp.where(r <= c, sc, NEG)


def _fox_probs_t(q, k, fq_row, fk_col, lse_row, diag):
    return jnp.exp(_fox_scores_t(q, k, fq_row, fk_col, diag) - lse_row)


def _fox_delta_row(do, o):
    prod = do * o
    hi = prod.astype(BF16)
    lo = (prod - hi.astype(F32)).astype(BF16)
    ones = jnp.ones((SUBLANES, DH), BF16)
    dims = (((1,), (1,)), ((), ()))
    return (lax.dot_general(ones, hi, dims, preferred_element_type=F32)
            + lax.dot_general(ones, lo, dims, preferred_element_type=F32))[0:1, :]


def _fox_when_needed(q_blk, k_blk, step):
    @pl.when(k_blk < q_blk)
    def _():
        step(False)

    @pl.when(k_blk == q_blk)
    def _():
        step(True)


def _fox_fwd(u, f_col, f_row, name):
    s = u.shape[0]
    t = _pick(s, FOX_TILES)
    nb = s // t
    hpb = FOX_HPB
    wb = hpb * DH
    ng = HEADS // hpb

    def body(q_ref, k_ref, v_ref, fq_ref, fk_ref, o_ref, lse_ref, m_sc, l_sc, acc):
        i, j = pl.program_id(1), pl.program_id(2)

        @pl.when(j == 0)
        def _():
            m_sc[...] = jnp.full_like(m_sc, NEG)
            l_sc[...] = jnp.zeros_like(l_sc)
            acc[...] = jnp.zeros_like(acc)

        def step(diag):
            for hh in range(hpb):
                sl = slice(hh * DH, (hh + 1) * DH)
                sc = _fox_scores_t(q_ref[:, sl], k_ref[:, sl], fq_ref[hh], fk_ref[hh], diag)
                m_new = jnp.maximum(m_sc[hh], jnp.max(sc, axis=0, keepdims=True))
                a = jnp.exp(m_sc[hh] - m_new)
                p = jnp.exp(sc - m_new)
                l_sc[hh] = a * l_sc[hh] + jnp.sum(p, axis=0, keepdims=True)
                acc[sl, :] = a * acc[sl, :] + _tn(v_ref[:, sl], p)
                m_sc[hh] = m_new

        _fox_when_needed(i, j, step)

        @pl.when(j == i)
        def _():
            for hh in range(hpb):
                sl = slice(hh * DH, (hh + 1) * DH)
                o_ref[:, sl] = jnp.transpose(acc[sl, :] / l_sc[hh])
                lse_ref[hh] = m_sc[hh] + jnp.log(l_sc[hh])

    qs = pl.BlockSpec((t, wb), lambda h, i, j: (i, h))
    ks = lambda base: pl.BlockSpec((t, wb), lambda h, i, j: (jnp.minimum(j, i), base + h))
    return pl.pallas_call(
        body, grid=(ng, nb, nb),
        in_specs=[qs, ks(ng), ks(2 * ng),
                  pl.BlockSpec((hpb, 1, t), lambda h, i, j: (h, 0, i)),
                  pl.BlockSpec((hpb, t, 1), lambda h, i, j: (h, jnp.minimum(j, i), 0))],
        out_specs=[pl.BlockSpec((t, wb), lambda h, i, j: (i, h)), pl.BlockSpec((hpb, 1, t), lambda h, i, j: (h, 0, i))],
        out_shape=[jax.ShapeDtypeStruct((s, HEADS * DH), F32), jax.ShapeDtypeStruct((HEADS, 1, s), F32)],
        scratch_shapes=[pltpu.VMEM((hpb, 1, t), F32), pltpu.VMEM((hpb, 1, t), F32), pltpu.VMEM((wb, t), F32)],
        compiler_params=_cparams(("parallel", "parallel", "arbitrary")), name=name)(u, u, u, f_row, f_col)


def _fox_delta(o, dout, name):
    s = o.shape[0]
    t = _pick(s, FOX_TILES)
    hpb = FOX_HPB
    wb = hpb * DH

    def body(o_ref, do_ref, d_ref):
        for hh in range(hpb):
            sl = slice(hh * DH, (hh + 1) * DH)
            d_ref[hh] = _fox_delta_row(do_ref[:, sl], o_ref[:, sl])

    blk = pl.BlockSpec((t, wb), lambda h, i: (i, h))
    return pl.pallas_call(
        body, grid=(HEADS // hpb, s // t), in_specs=[blk, blk],
        out_specs=pl.BlockSpec((hpb, 1, t), lambda h, i: (h, 0, i)),
        out_shape=jax.ShapeDtypeStruct((HEADS, 1, s), F32),
        compiler_params=_cparams(("parallel", "parallel")), name=name)(o, dout)


def _fox_bwd_kv(u, delta, lse_row, dout, f_col, f_row, name):
    s = u.shape[0]
    t = _pick(s, FOX_TILES)
    nb = s // t
    hpb = FOX_HPB
    wb = hpb * DH
    ng = HEADS // hpb

    def body(q_ref, k_ref, v_ref, dl_ref, do_ref, lse_ref, fq_ref, fk_ref, dk_ref, dv_ref, dfk_ref, dk_acc, dv_acc, df_acc):
        j, i = pl.program_id(1), pl.program_id(2)

        @pl.when(i == 0)
        def _():
            dk_acc[...] = jnp.zeros_like(dk_acc)
            dv_acc[...] = jnp.zeros_like(dv_acc)
            df_acc[...] = jnp.zeros_like(df_acc)

        def step(diag):
            for hh in range(hpb):
                sl = slice(hh * DH, (hh + 1) * DH)
                q = q_ref[:, sl]
                do = do_ref[:, sl]
                p = _fox_probs_t(q, k_ref[:, sl], fq_ref[hh], fk_ref[hh], lse_ref[hh], diag)
                dv_acc[:, sl] += _nn(p, do)
                ds = p * (_nt(v_ref[:, sl], do) - dl_ref[hh])
                dk_acc[:, sl] += _nn(ds, q) * (DH ** -0.5)
                df_acc[hh] -= jnp.sum(ds, axis=1, keepdims=True)

        _fox_when_needed(i, j, step)

        @pl.when(i == nb - 1)
        def _():
            dk_ref[...] = dk_acc[...].astype(BF16)
            dv_ref[...] = dv_acc[...].astype(BF16)
            dfk_ref[...] = df_acc[...]

    qi = lambda j, i: jnp.maximum(i, j)
    qs = lambda base: pl.BlockSpec((t, wb), lambda h, j, i: (qi(j, i), base + h))
    ks = lambda base: pl.BlockSpec((t, wb), lambda h, j, i: (j, base + h))
    qrow = pl.BlockSpec((hpb, 1, t), lambda h, j, i: (h, 0, qi(j, i)))
    return pl.pallas_call(
        body, grid=(ng, nb, nb),
        in_specs=[qs(0), ks(ng), ks(2 * ng), qrow, qs(0), qrow, qrow,
                  pl.BlockSpec((hpb, t, 1), lambda h, j, i: (h, j, 0))],
        out_specs=[pl.BlockSpec((t, wb), lambda h, j, i: (j, h)), pl.BlockSpec((t, wb), lambda h, j, i: (j, h)),
                   pl.BlockSpec((hpb, t, 1), lambda h, j, i: (h, j, 0))],
        out_shape=[jax.ShapeDtypeStruct((s, HEADS * DH), BF16)] * 2 + [jax.ShapeDtypeStruct((HEADS, s, 1), F32)],
        scratch_shapes=[pltpu.VMEM((t, wb), F32), pltpu.VMEM((t, wb), F32), pltpu.VMEM((hpb, t, 1), F32)],
        compiler_params=_cparams(("parallel", "parallel", "arbitrary")), name=name)(
            u, u, u, delta, dout, lse_row, f_row, f_col)


def _fox_bwd_q(u, delta, lse_row, dout, f_col, f_row, name):
    s = u.shape[0]
    t = _pick(s, FOX_TILES)
    nb = s // t
    hpb = FOX_HPB
    wb = hpb * DH
    ng = HEADS // hpb

    def body(q_ref, k_ref, v_ref, dl_ref, do_ref, lse_ref, fq_ref, fk_ref, dq_ref, dfq_ref, dq_acc, df_acc):
        i, j = pl.program_id(1), pl.program_id(2)

        @pl.when(j == 0)
        def _():
            dq_acc[...] = jnp.zeros_like(dq_acc)
            df_acc[...] = jnp.zeros_like(df_acc)

        def step(diag):
            for hh in range(hpb):
                sl = slice(hh * DH, (hh + 1) * DH)
                do = do_ref[:, sl]
                k = k_ref[:, sl]
                p = _fox_probs_t(q_ref[:, sl], k, fq_ref[hh], fk_ref[hh], lse_ref[hh], diag)
                ds = p * (_nt(v_ref[:, sl], do) - dl_ref[hh])
                dq_acc[sl, :] += _tn(k, ds) * (DH ** -0.5)
                df_acc[hh] += jnp.sum(ds, axis=0, keepdims=True)

        _fox_when_needed(i, j, step)

        @pl.when(j == i)
        def _():
            for hh in range(hpb):
                sl = slice(hh * DH, (hh + 1) * DH)
                dq_ref[:, sl] = jnp.transpose(dq_acc[sl, :]).astype(BF16)
            dfq_ref[...] = df_acc[...]

    qs = lambda base: pl.BlockSpec((t, wb), lambda h, i, j: (i, base + h))
    ks = lambda base: pl.BlockSpec((t, wb), lambda h, i, j: (jnp.minimum(j, i), base + h))
    row = pl.BlockSpec((hpb, 1, t), lambda h, i, j: (h, 0, i))
    return pl.pallas_call(
        body, grid=(ng, nb, nb),
        in_specs=[qs(0), ks(ng), ks(2 * ng), row, qs(0), row, row,
                  pl.BlockSpec((hpb, t, 1), lambda h, i, j: (h, jnp.minimum(j, i), 0))],
        out_specs=[pl.BlockSpec((t, wb), lambda h, i, j: (i, h)), row],
        out_shape=[jax.ShapeDtypeStruct((s, HEADS * DH), BF16), jax.ShapeDtypeStruct((HEADS, 1, s), F32)],
        scratch_shapes=[pltpu.VMEM((wb, t), F32), pltpu.VMEM((hpb, 1, t), F32)],
        compiler_params=_cparams(("parallel", "parallel", "arbitrary")), name=name)(
            u, u, u, delta, dout, lse_row, f_row, f_col)


def _ca_bias(rel_bias):
    nh = rel_bias.shape[0]
    n_clip = CA_BAND - REL_CLIP
    gv = jnp.concatenate([jnp.broadcast_to(rel_bias[:, REL_TABLE - 1:], (nh, n_clip)),
                          rel_bias[:, REL_TABLE - 2::-1]], axis=1)
    b = jnp.stack([gv[:, CHUNK - 1 - qi:CHUNK - 1 - qi + CA_BAND] for qi in range(CHUNK)], axis=1)
    neg = lambda n: jnp.full((nh, CHUNK, n), NEG, F32)
    blocks = [jnp.concatenate([neg(a * CHUNK), b, neg(CA_WIN - CA_BAND - a * CHUNK)], axis=2)
              for a in range(CA_TILE // CHUNK)]
    return jnp.concatenate(blocks, axis=1)


def _ca_bias_grad(db):
    nh = db.shape[0]
    n_clip = CA_BAND - REL_CLIP
    d64 = sum(db[:, a * CHUNK:(a + 1) * CHUNK, a * CHUNK:a * CHUNK + CA_BAND] for a in range(CA_TILE // CHUNK))
    wide = jnp.pad(d64[:, ::-1, :], ((0, 0), (0, 0), (0, CHUNK))).reshape(nh, CHUNK * (CA_BAND + CHUNK))
    n_gv = CA_BAND + CHUNK - 1
    dgv = jnp.sum(wide[:, :CHUNK * n_gv].reshape(nh, CHUNK, n_gv), axis=1)
    return jnp.concatenate([dgv[:, n_clip:][:, ::-1], jnp.sum(dgv[:, :n_clip], axis=1, keepdims=True)], axis=1)


def _ca_specs():
    cur = lambda base: pl.BlockSpec((CA_BLK, DH), lambda h, i: (i, base + h))
    prev = lambda base: pl.BlockSpec((CA_BLK, DH), lambda h, i: (jnp.maximum(i - 1, 0), base + h))
    bias = pl.BlockSpec((1, CA_TILE, CA_WIN), lambda h, i: (h, 0, 0))
    return cur, prev, bias


def _ca_fill(kcat, vcat, kp_ref, kc_ref, vp_ref, vc_ref):
    kcat[0:CA_BLK, :] = kp_ref[...]
    kcat[CA_BLK:2 * CA_BLK, :] = kc_ref[...]
    vcat[0:CA_BLK, :] = vp_ref[...]
    vcat[CA_BLK:2 * CA_BLK, :] = vc_ref[...]


def _ca_probs(i, tl, q, kw, bias):
    sc = _nt(q, kw) * (DH ** -0.5) + bias
    col = lax.broadcasted_iota(jnp.int32, sc.shape, 1)
    sc = jnp.where((i - 1) * CA_BLK + tl * CA_TILE + col >= 0, sc, NEG)
    p = jnp.exp(sc - jnp.max(sc, axis=-1, keepdims=True))
    return p / jnp.sum(p, axis=-1, keepdims=True)


def _ca_fwd(u, bias, name):
    s = u.shape[0]
    cur, prev, bsp = _ca_specs()

    def body(q_ref, kp_ref, kc_ref, vp_ref, vc_ref, b_ref, o_ref, kcat, vcat):
        i = pl.program_id(1)
        _ca_fill(kcat, vcat, kp_ref, kc_ref, vp_ref, vc_ref)
        for tl in range(CA_BLK // CA_TILE):
            rows = slice(tl * CA_TILE, (tl + 1) * CA_TILE)
            win = slice(tl * CA_TILE, tl * CA_TILE + CA_WIN)
            p = _ca_probs(i, tl, q_ref[rows, :], kcat[win, :], b_ref[0])
            o_ref[rows, :] = _nn(p, vcat[win, :])

    return pl.pallas_call(
        body, grid=(HEADS, s // CA_BLK),
        in_specs=[cur(3 * HEADS), prev(4 * HEADS), cur(4 * HEADS), prev(5 * HEADS), cur(5 * HEADS), bsp],
        out_specs=pl.BlockSpec((CA_BLK, DH), lambda h, i: (i, h)),
        out_shape=jax.ShapeDtypeStruct((s, HEADS * DH), F32),
        scratch_shapes=[pltpu.VMEM((2 * CA_BLK, DH), F32)] * 2,
        compiler_params=_cparams(("parallel", "parallel")), name=name)(u, u, u, u, u, bias)


def _ca_bwd(u, bias, dout, name):
    s = u.shape[0]
    cur, prev, bsp = _ca_specs()
    rows_cat = 2 * CA_BLK

    def body(q_ref, kp_ref, kc_ref, vp_ref, vc_ref, b_ref, do_ref,
             dq_ref, dka_ref, dkb_ref, dva_ref, dvb_ref, db_ref, kcat, vcat, dkcat, dvcat):
        i = pl.program_id(1)

        @pl.when(i == 0)
        def _():
            db_ref[...] = jnp.zeros_like(db_ref)

        _ca_fill(kcat, vcat, kp_ref, kc_ref, vp_ref, vc_ref)
        dkcat[...] = jnp.zeros_like(dkcat)
        dvcat[...] = jnp.zeros_like(dvcat)
        for tl in range(CA_BLK // CA_TILE):
            rows = slice(tl * CA_TILE, (tl + 1) * CA_TILE)
            win = slice(tl * CA_TILE, tl * CA_TILE + CA_WIN)
            q = q_ref[rows, :]
            kw = kcat[win, :]
            vw = vcat[win, :]
            do = do_ref[rows, :]
            p = _ca_probs(i, tl, q, kw, b_ref[0])
            dp = _nt(do, vw)
            ds = p * (dp - jnp.sum(p * dp, axis=-1, keepdims=True))
            dq_ref[rows, :] = (_nn(ds, kw) * (DH ** -0.5)).astype(BF16)
            dkcat[win, :] += _tn(ds, q) * (DH ** -0.5)
            dvcat[win, :] += _tn(p, do)
            db_ref[0] += ds
        dkb_ref[...] = dkcat[0:CA_BLK, :]
        dka_ref[...] = dkcat[CA_BLK:2 * CA_BLK, :]
        dvb_ref[...] = dvcat[0:CA_BLK, :]
        dva_ref[...] = dvcat[CA_BLK:2 * CA_BLK, :]

    osp = pl.BlockSpec((CA_BLK, DH), lambda h, i: (i, h))
    shp = jax.ShapeDtypeStruct((s, HEADS * DH), F32)
    return pl.pallas_call(
        body, grid=(HEADS, s // CA_BLK),
        in_specs=[cur(3 * HEADS), prev(4 * HEADS), cur(4 * HEADS), prev(5 * HEADS), cur(5 * HEADS), bsp,
                  pl.BlockSpec((CA_BLK, DH), lambda h, i: (i, HEADS + h))],
        out_specs=[osp, osp, osp, osp, osp, bsp],
        out_shape=[jax.ShapeDtypeStruct((s, HEADS * DH), BF16), shp, shp, shp, shp,
                   jax.ShapeDtypeStruct((HEADS, CA_TILE, CA_WIN), F32)],
        scratch_shapes=[pltpu.VMEM((rows_cat, DH), F32)] * 4,
        compiler_params=_cparams(("parallel", "arbitrary")), name=name)(u, u, u, u, u, bias, dout)


def _ca_merge(da, db):
    shifted = jnp.concatenate([db[CA_BLK:], jnp.zeros((CA_BLK, db.shape[1]), F32)], axis=0)
    return (da + shifted).astype(BF16)


def _place():
    x, y, c = lax.axis_index("x"), lax.axis_index("y"), lax.axis_index("c")
    return x, y, c, [(1 - x, y), (x, 1 - y), (1 - x, 1 - y)]


_ANY = pl.BlockSpec(memory_space=pl.ANY)


def _all_gather_chips(w, name):
    def body(w_ref, o_ref, send_sems, recv_sems, loc_sems):
        _exchange(["ag"], [w_ref], [o_ref], send_sems, recv_sems, loc_sems, start=True)
        _exchange(["ag"], [w_ref], [o_ref], send_sems, recv_sems, loc_sems, start=False)

    return pl.pallas_call(
        body, in_specs=[_ANY], out_specs=_ANY, out_shape=jax.ShapeDtypeStruct((4,) + w.shape, w.dtype),
        scratch_shapes=[pltpu.SemaphoreType.DMA((3,)), pltpu.SemaphoreType.DMA((3,)), pltpu.SemaphoreType.DMA((1,))],
        name=name)(w)


def _all_gather_two_level(w, name):
    half = w.shape[0] // 2

    def body(w_ref, o_ref, send_sems, recv_sems, loc_sem):
        x, y, c, peers = _place()
        me = 2 * x + y
        sibling = (x, y, 1 - c)
        mine = pl.ds(pl.multiple_of(c * half, SUBLANES), half)
        other = pl.ds(pl.multiple_of((1 - c) * half, SUBLANES), half)

        def copy(k, src, dst, to):
            return pltpu.make_async_remote_copy(src_ref=src, dst_ref=dst, send_sem=send_sems.at[k],
                                                recv_sem=recv_sems.at[k], device_id=to, device_id_type=MESH)

        loc = pltpu.make_async_copy(w_ref, o_ref.at[me], loc_sem)
        loc.start()
        sends = [copy(k, w_ref.at[mine], o_ref.at[me, mine], (px, py, c)) for k, (px, py) in enumerate(peers)]
        for cp in sends:
            cp.start()
        for k, (px, py) in enumerate(peers):
            landed = o_ref.at[2 * px + py, mine]
            copy(k, w_ref.at[mine], landed, (px, py, c)).wait_recv()
            fwd = copy(3 + k, landed, landed, sibling)
            fwd.start()
            sends.append(fwd)
        for k, (px, py) in enumerate(peers):
            copy(3 + k, w_ref.at[other], o_ref.at[2 * px + py, other], sibling).wait_recv()
        for cp in sends:
            cp.wait_send()
        loc.wait()

    return pl.pallas_call(
        body, in_specs=[_ANY], out_specs=_ANY, out_shape=jax.ShapeDtypeStruct((4,) + w.shape, w.dtype),
        scratch_shapes=[pltpu.SemaphoreType.DMA((6,)), pltpu.SemaphoreType.DMA((6,)), pltpu.SemaphoreType.DMA(())],
        name=name)(w)


def _exchange(kinds, srcs, dsts, send_sems, recv_sems, loc_sems, start):
    x, y, c, peers = _place()
    me = 2 * x + y
    for n, kind in enumerate(kinds):
        src, dst = srcs[n], dsts[n]
        mine = src if kind == "ag" else src.at[me]
        loc = pltpu.make_async_copy(mine, dst.at[me], loc_sems.at[n])
        copies = []
        for k, (px, py) in enumerate(peers):
            out_src = src if kind == "ag" else src.at[2 * px + py]
            send = pltpu.make_async_remote_copy(src_ref=out_src, dst_ref=dst.at[me], send_sem=send_sems.at[3 * n + k],
                                                recv_sem=recv_sems.at[3 * n + k], device_id=(px, py, c),
                                                device_id_type=MESH)
            recv = pltpu.make_async_remote_copy(src_ref=mine, dst_ref=dst.at[2 * px + py],
                                                send_sem=send_sems.at[3 * n + k], recv_sem=recv_sems.at[3 * n + k],
                                                device_id=(px, py, c), device_id_type=MESH)
            copies.append((send, recv))
        if start:
            loc.start()
            for send, _ in copies:
                send.start()
        else:
            for _, recv in copies:
                recv.wait_recv()
            for send, _ in copies:
                send.wait_send()
            loc.wait()


def _core_swap(a, name):
    def body(a_ref, o_ref, send_sem, recv_sem):
        x, y, c, _ = _place()
        cp = pltpu.make_async_remote_copy(src_ref=a_ref, dst_ref=o_ref, send_sem=send_sem, recv_sem=recv_sem,
                                          device_id=(x, y, 1 - c), device_id_type=MESH)
        cp.start()
        cp.wait()

    return pl.pallas_call(
        body, in_specs=[_ANY], out_specs=_ANY, out_shape=jax.ShapeDtypeStruct(a.shape, a.dtype),
        scratch_shapes=[pltpu.SemaphoreType.DMA(()), pltpu.SemaphoreType.DMA(())], name=name)(a)


def _all_reduce_small(v, name):
    r, wl = v.shape

    def body(v_ref, o_ref, buf, send_sems, recv_sems):
        x, y, c, _ = _place()
        me = 4 * x + 2 * y + c
        buf[me] = v_ref[...]
        flips = [(fx, fy, fc) for fx in (0, 1) for fy in (0, 1) for fc in (0, 1) if (fx, fy, fc) != (0, 0, 0)]
        peer = lambda f: (x ^ f[0], y ^ f[1], c ^ f[2])
        sends = []
        for k, f in enumerate(flips):
            cp = pltpu.make_async_remote_copy(src_ref=v_ref, dst_ref=buf.at[me], send_sem=send_sems.at[k],
                                              recv_sem=recv_sems.at[k], device_id=peer(f), device_id_type=MESH)
            cp.start()
            sends.append(cp)
        for k, f in enumerate(flips):
            px, py, pc = peer(f)
            pltpu.make_async_remote_copy(src_ref=v_ref, dst_ref=buf.at[4 * px + 2 * py + pc], send_sem=send_sems.at[k],
                                         recv_sem=recv_sems.at[k], device_id=peer(f), device_id_type=MESH).wait_recv()
        for cp in sends:
            cp.wait_send()
        acc = buf[0]
        for d in range(1, 8):
            acc = acc + buf[d]
        o_ref[...] = acc

    vm = pl.BlockSpec(memory_space=pltpu.VMEM)
    return pl.pallas_call(
        body, in_specs=[vm], out_specs=vm, out_shape=jax.ShapeDtypeStruct((r, wl), F32),
        scratch_shapes=[pltpu.VMEM((8, r, wl), F32), pltpu.SemaphoreType.DMA((7,)), pltpu.SemaphoreType.DMA((7,))],
        name=name)(v)


def _sum_slots(g, name):
    _, r, cdim = g.shape
    br = _pick(r, (256, 128, 64, 32, 8))

    def body(g_ref, o_ref):
        o_ref[...] = ((g_ref[0].astype(F32) + g_ref[1].astype(F32)) + g_ref[2].astype(F32)) + g_ref[3].astype(F32)

    return pl.pallas_call(
        body, grid=(r // br,), in_specs=[pl.BlockSpec((4, br, cdim), lambda i: (0, i, 0))],
        out_specs=pl.BlockSpec((br, cdim), lambda i: (i, 0)), out_shape=jax.ShapeDtypeStruct((r, cdim), F32),
        compiler_params=_cparams(("parallel",)), name=name)(g)


def _adamw(w, grads, m, v, name):
    nl, r, cdim = w.shape
    br = _pick(r, (128, 64, 32, 8))
    c1 = 1.0 / (1.0 - ADAM_B1 ** ADAM_STEP)
    c2 = 1.0 / (1.0 - ADAM_B2 ** ADAM_STEP)
    counts = [len(t) for t in grads]
    flat = [a for t in grads for a in t]

    def body(*refs):
        w_ref = refs[0]
        g_refs = refs[1:1 + len(flat)]
        m_ref, v_ref, g_out, d_out, m_out, v_out = refs[1 + len(flat):]
        layer = pl.program_id(0)
        g, pos = None, 0
        for l, n in enumerate(counts):
            gl = g_refs[pos][...]
            for a in g_refs[pos + 1:pos + n]:
                gl = gl + a[...]
            pos += n
            g = gl if g is None else jnp.where(layer == l, gl, g)
        mn = ADAM_B1 * m_ref[...] + (1.0 - ADAM_B1) * g
        vn = ADAM_B2 * v_ref[...] + (1.0 - ADAM_B2) * (g * g)
        g_out[...] = g
        m_out[...] = mn
        v_out[...] = vn
        d_out[...] = -ADAM_LR * ((mn * c1) / (jnp.sqrt(vn * c2) + ADAM_EPS) + ADAM_WD * w_ref[...])

    blk = pl.BlockSpec((None, br, cdim), lambda l, i: (l, i, 0))
    gblk = pl.BlockSpec((br, cdim), lambda l, i: (i, 0))
    return pl.pallas_call(
        body, grid=(nl, r // br), in_specs=[blk] + [gblk] * len(flat) + [blk, blk], out_specs=[blk] * 4,
        out_shape=[jax.ShapeDtypeStruct((nl, r, cdim), F32)] * 4,
        compiler_params=_cparams(("parallel", "parallel")), name=name)(w, *flat, m, v)


def _unshard_cols(g):
    return jnp.transpose(g, (1, 0, 2)).reshape(g.shape[1], 4 * g.shape[2])


def _unshard_rows(g):
    return g.reshape(4 * g.shape[1], g.shape[2])


def _shard_cols(g):
    k, n = g.shape
    return jnp.transpose(g.reshape(k, 4, n // 4), (1, 0, 2))


def _shard_rows(g):
    k, n = g.shape
    return g.reshape(4, k // 4, n)


def _lower_bound(logits):
    return jnp.cumsum(jax.nn.softmax(logits.astype(F32), axis=0), axis=0)[0]


def _pack(parts):
    flat = jnp.concatenate([p.reshape(-1) for p in parts])
    n = flat.shape[0]
    rows = -(-n // 1024) * 8
    return jnp.pad(flat, (0, rows * 128 - n)).reshape(rows, 128)


def _unpack(packed, shapes):
    flat = packed.reshape(-1)
    out, off = [], 0
    for shp in shapes:
        n = 1
        for d in shp:
            n *= d
        out.append(flat[off:off + n].reshape(shp))
        off += n
    return out


def kernel(x, ev_w_in, ev_conv_w, ev_conv_b, ev_conv_ln_g, ev_conv_ln_b, hgrn_lb_logits, ev_gnorm_g, ev_w_out, od_w_in, fox_b_f, rel_bias, od_w_out, ln_mix_g, ln_mix_b, mlp_w1, mlp_w2, ln_mlp_g, ln_mlp_b, loss_target, m_ev_w_in, m_ev_conv_w, m_ev_conv_b, m_ev_conv_ln_g, m_ev_conv_ln_b, m_hgrn_lb_logits, m_ev_gnorm_g, m_ev_w_out, m_od_w_in, m_fox_b_f, m_rel_bias, m_od_w_out, m_ln_mix_g, m_ln_mix_b, m_mlp_w1, m_mlp_w2, m_ln_mlp_g, m_ln_mlp_b, v_ev_w_in, v_ev_conv_w, v_ev_conv_b, v_ev_conv_ln_g, v_ev_conv_ln_b, v_hgrn_lb_logits, v_ev_gnorm_g, v_ev_w_out, v_od_w_in, v_fox_b_f, v_rel_bias, v_od_w_out, v_ln_mix_g, v_ln_mix_b, v_mlp_w1, v_mlp_w2, v_ln_mlp_g, v_ln_mlp_b):
    w_sharded = dict(ev_w_in=ev_w_in, ev_w_out=ev_w_out, od_w_in=od_w_in, od_w_out=od_w_out, mlp_w1=mlp_w1, mlp_w2=mlp_w2)
    m_sharded = dict(ev_w_in=m_ev_w_in, ev_w_out=m_ev_w_out, od_w_in=m_od_w_in, od_w_out=m_od_w_out, mlp_w1=m_mlp_w1, mlp_w2=m_mlp_w2)
    v_sharded = dict(ev_w_in=v_ev_w_in, ev_w_out=v_ev_w_out, od_w_in=v_od_w_in, od_w_out=v_od_w_out, mlp_w1=v_mlp_w1, mlp_w2=v_mlp_w2)
    small_names = ["ev_conv_w", "ev_conv_b", "ev_conv_ln_g", "ev_conv_ln_b", "hgrn_lb_logits", "ev_gnorm_g", "fox_b_f",
                   "rel_bias", "ln_mix_g", "ln_mix_b", "ln_mlp_g", "ln_mlp_b"]
    w_small = dict(ev_conv_w=ev_conv_w, ev_conv_b=ev_conv_b, ev_conv_ln_g=ev_conv_ln_g, ev_conv_ln_b=ev_conv_ln_b,
                   hgrn_lb_logits=hgrn_lb_logits, ev_gnorm_g=ev_gnorm_g, fox_b_f=fox_b_f, rel_bias=rel_bias,
                   ln_mix_g=ln_mix_g, ln_mix_b=ln_mix_b, ln_mlp_g=ln_mlp_g, ln_mlp_b=ln_mlp_b)
    m_small = dict(ev_conv_w=m_ev_conv_w, ev_conv_b=m_ev_conv_b, ev_conv_ln_g=m_ev_conv_ln_g, ev_conv_ln_b=m_ev_conv_ln_b,
                   hgrn_lb_logits=m_hgrn_lb_logits, ev_gnorm_g=m_ev_gnorm_g, fox_b_f=m_fox_b_f, rel_bias=m_rel_bias,
                   ln_mix_g=m_ln_mix_g, ln_mix_b=m_ln_mix_b, ln_mlp_g=m_ln_mlp_g, ln_mlp_b=m_ln_mlp_b)
    v_small = dict(ev_conv_w=v_ev_conv_w, ev_conv_b=v_ev_conv_b, ev_conv_ln_g=v_ev_conv_ln_g, ev_conv_ln_b=v_ev_conv_ln_b,
                   hgrn_lb_logits=v_hgrn_lb_logits, ev_gnorm_g=v_ev_gnorm_g, fox_b_f=v_fox_b_f, rel_bias=v_rel_bias,
                   ln_mix_g=v_ln_mix_g, ln_mix_b=v_ln_mix_b, ln_mlp_g=v_ln_mlp_g, ln_mlp_b=v_ln_mlp_b)

    chip = 2 * lax.axis_index("x") + lax.axis_index("y")
    hw = HEADS * DH

    w_ev_in = _all_gather_two_level(ev_w_in[0].astype(BF16), "ag_ev_w_in")
    shards = dict(ev_w_out=ev_w_out[0].astype(BF16), od_w_in=od_w_in[0].astype(BF16), od_w_out=od_w_out[0].astype(BF16),
                  mlp_w1=[mlp_w1[l].astype(BF16) for l in range(DEPTH)],
                  mlp_w2=[mlp_w2[l].astype(BF16) for l in range(DEPTH)])
    tables = _all_gather_chips(jnp.concatenate(
        [ev_conv_w[0].reshape(1, -1), jnp.pad(rel_bias[0].reshape(1, -1), ((0, 0), (0, (-rel_bias[0].size) % 128)))],
        axis=1), "ag_tables")
    ncw = ev_conv_w[0].size
    cshard = ev_conv_w.shape[2]
    conv_w = jnp.transpose(tables[:, 0, :ncw].reshape(4, CONV_WIDTH, cshard), (1, 0, 2)).reshape(CONV_WIDTH, 4 * cshard)
    rshard = rel_bias.shape[2]
    rel_full = jnp.transpose(tables[:, 0, ncw:ncw + HEADS * rshard].reshape(4, HEADS, rshard), (1, 0, 2)).reshape(HEADS, 4 * rshard)

    loss_part, grad_x, g, small_partial = _local_step(
        x[0], loss_target[0], w_ev_in, shards, conv_w, rel_full, ev_conv_b, ev_conv_ln_g,
        ev_conv_ln_b, hgrn_lb_logits, ev_gnorm_g, fox_b_f, ln_mix_g, ln_mix_b, ln_mlp_g, ln_mlp_b)
    loss = lax.psum(loss_part, ("x", "y", "c"))

    full_shapes = [tuple(small_partial[n].shape) for n in small_names]
    reduced = _unpack(_all_reduce_small(_pack([small_partial[n] for n in small_names]), "ar_small"), full_shapes)
    g_small = {}
    for n, val in zip(small_names, reduced):
        if n == "ev_conv_w":
            val = lax.dynamic_slice_in_dim(val, chip * cshard, cshard, axis=1)
        elif n == "rel_bias":
            val = lax.dynamic_slice_in_dim(val, chip * rshard, rshard, axis=1)
        g_small[n] = val.reshape(w_small[n].shape)
    shard_shapes = [tuple(w_small[n].shape) for n in small_names]
    packed = [_pack([d[n] for n in small_names]) for d in (w_small, g_small, m_small, v_small)]
    small_out = _adamw(packed[0][None], [(packed[1],)], packed[2][None], packed[3][None], "adamw_small")
    out_small = {k: dict(zip(small_names, _unpack(val[0], shard_shapes))) for k, val in zip("gdmv", small_out)}

    out_big = {"g": {}, "d": {}, "m": {}, "v": {}}

    for name in ("ev_w_in", "ev_w_out", "od_w_in", "od_w_out", "mlp_w1", "mlp_w2"):
        pairs = []
        for l, slots in enumerate(g[name]):
            mine = _sum_slots(slots, "rs_%s_%d_sum" % (name, l))
            pairs.append((mine, _core_swap(mine, "rs_%s_%d_swap" % (name, l))))
        outs = _adamw(w_sharded[name], pairs, m_sharded[name], v_sharded[name], "adamw_%s" % name)
        for key, val in zip("gdmv", outs):
            out_big[key][name] = val

    order = ["ev_w_in", "ev_conv_w", "ev_conv_b", "ev_conv_ln_g", "ev_conv_ln_b", "hgrn_lb_logits", "ev_gnorm_g", "ev_w_out",
             "od_w_in", "fox_b_f", "rel_bias", "od_w_out", "ln_mix_g", "ln_mix_b", "mlp_w1", "mlp_w2", "ln_mlp_g", "ln_mlp_b"]

    def pick(key, n):
        return out_big[key][n] if n in out_big[key] else out_small[key][n]

    outs = [loss, grad_x[None]]
    for key in ("g", "d", "m", "v"):
        outs.extend(pick(key, n) for n in order)
    return tuple(outs)


def _local_step(xin, tgt, w_ev_in, shards, conv_w, rel_full, ev_conv_b, ev_conv_ln_g,
                ev_conv_ln_b, hgrn_lb_logits, ev_gnorm_g, fox_b_f, ln_mix_g, ln_mix_b, ln_mlp_g, ln_mlp_b):
    hw = HEADS * DH
    bf_pad = jnp.pad(fox_b_f[0], (0, 128 - HEADS)).reshape(1, 128)
    lb0 = _lower_bound(hgrn_lb_logits)
    ca_bias = _ca_bias(rel_full)
    ag = lambda w: ("ag", w)
    a2a = lambda g4: ("a2a", g4)

    xin_b = xin.astype(BF16)
    u0, (w_ev_out4,) = _matmul(xin_b, w_ev_in, b_sharded=True, comm=(ag(shards["ev_w_out"]),), name="l0_in")
    w_ev_out = _unshard_rows(w_ev_out4)
    a_out = _conv_fwd(u0, conv_w, ev_conv_b[0], ev_conv_ln_g[0], ev_conv_ln_b[0], "l0_conv")
    o_raw, b_out, states, (w1_0,) = _hgrn_fwd(u0, lb0, ev_gnorm_g[0], (ag(shards["mlp_w1"][0]),), "l0_hgrn")
    cat0 = jnp.concatenate([a_out, b_out], axis=1).astype(BF16)
    mix0, (w_od_out4,) = _matmul(cat0, w_ev_out, comm=(ag(shards["od_w_out"]),), name="l0_out")
    r0a, x0a, x0a_b = _ln_fwd(xin, mix0, ln_mix_g[0], ln_mix_b[0], "l0_ln_mix")
    (z0, h0), (w2_0,) = _matmul(x0a_b, w1_0, b_sharded=True, out_dtype=BF16, epi="relu2",
                                comm=(ag(shards["mlp_w2"][0]),), name="l0_mlp1")
    w2 = [_unshard_rows(w2_0), None]
    f0, (w_od_in4,) = _matmul(h0, w2[0], comm=(ag(shards["od_w_in"]),), name="l0_mlp2")
    w_od_in = _unshard_cols(w_od_in4)
    w_od_out = _unshard_rows(w_od_out4)
    w_od_main = jnp.concatenate([w_od_in[:, :3 * hw], w_od_in[:, 3 * hw + HEADS:]], axis=1)
    w_od_f = jnp.pad(w_od_in[:, 3 * hw:3 * hw + HEADS], ((0, 0), (0, 128 - HEADS)))
    r0b, x1, x1_b = _ln_fwd(x0a, f0, ln_mlp_g[0], ln_mlp_b[0], "l0_ln_mlp")
    u1, (w1_1,) = _matmul(x1_b, w_od_main, comm=(ag(shards["mlp_w1"][1]),), name="l1_in")
    w1 = [w1_0, w1_1]
    cf = _matmul(x1_b, w_od_f, name="l1_in_f")
    fcum = _fgate_fwd(cf, bf_pad, "l1_fgate")
    f_col = jnp.transpose(fcum[:, :HEADS])[:, :, None]
    f_row = jnp.transpose(fcum[:, :HEADS])[:, None, :]
    c_out, lse_row = _fox_fwd(u1, f_col, f_row, "l1_fox")
    d_out = _ca_fwd(u1, ca_bias, "l1_ca")
    cat1 = jnp.concatenate([c_out, d_out], axis=1).astype(BF16)
    mix1 = _matmul(cat1, w_od_out, name="l1_out")
    r1a, x1a, x1a_b = _ln_fwd(x1, mix1, ln_mix_g[1], ln_mix_b[1], "l1_ln_mix")
    (z1, h1), (w2_1,) = _matmul(x1a_b, w1[1], b_sharded=True, out_dtype=BF16, epi="relu2",
                                comm=(ag(shards["mlp_w2"][1]),), name="l1_mlp1")
    w2[1] = _unshard_rows(w2_1)
    f1 = _matmul(h1, w2[1], name="l1_mlp2")
    r1b, x2, _ = _ln_fwd(x1a, f1, ln_mlp_g[1], ln_mlp_b[1], "l1_ln_mlp")
    dy, loss_part = _loss_head(x2, tgt, "loss")

    g = {}
    dr, drb, dg_, db_ = _ln_bwd(dy, r1b, ln_mlp_g[1], "l1_ln_mlp_bwd")
    g_ln_mlp = [None, (dg_, db_)]
    gw = _shard_rows(_matmul(h1, drb, ta=True, out_dtype=BF16, name="l1_dw2"))
    dz, (gw2_1,) = _matmul(drb, w2[1], tb=True, out_dtype=BF16, epi="drelu2", extra=z1, comm=(a2a(gw),), name="l1_dz")
    gw = _matmul(x1a_b, dz, ta=True, out_dtype=BF16, out_sharded=True, name="l1_dw1")
    dx, (gw1_1,) = _matmul(dz, w1[1], tb=True, b_sharded=True, epi="add", extra=dr, comm=(a2a(gw),), name="l1_dx_mlp")
    dr, drb, dg_, db_ = _ln_bwd(dx, r1a, ln_mix_g[1], "l1_ln_mix_bwd")
    g_ln_mix = [None, (dg_, db_)]
    gw = _shard_rows(_matmul(cat1, drb, ta=True, out_dtype=BF16, name="l1_dwout"))
    dcat, g["od_w_out"] = _matmul(drb, w_od_out, tb=True, comm=(a2a(gw),), name="l1_dcat")
    delta = _fox_delta(c_out, dcat, "l1_fox_delta")
    dk_c, dv_c, dfk = _fox_bwd_kv(u1, delta, lse_row, dcat, f_col, f_row, "l1_fox_bwd_kv")
    dq_c, dfq = _fox_bwd_q(u1, delta, lse_row, dcat, f_col, f_row, "l1_fox_bwd_q")
    d_f = jnp.pad(jnp.transpose(dfk[:, :, 0] + dfq[:, 0, :]), ((0, 0), (0, 128 - HEADS)))
    dcf, dbf = _fgate_bwd(d_f, cf, bf_pad, "l1_fgate_bwd")
    dq_d, dka, dkb, dva, dvb, dbias = _ca_bwd(u1, ca_bias, dcat, "l1_ca_bwd")
    du1 = jnp.concatenate([dq_c, dk_c, dv_c, dq_d, _ca_merge(dka, dkb), _ca_merge(dva, dvb)], axis=1)
    g_main = _matmul(x1_b, du1, ta=True, out_dtype=BF16, name="l1_dwin")
    g_f = _matmul(x1_b, dcf, ta=True, out_dtype=BF16, name="l1_dwin_f")
    gw = _shard_cols(jnp.concatenate([g_main[:, :3 * hw], g_f[:, :HEADS], g_main[:, 3 * hw:]], axis=1))
    dx_f = _matmul(dcf, w_od_f, tb=True, epi="add", extra=dr, name="l1_dx_f")
    dx, g["od_w_in"] = _matmul(du1, w_od_main, tb=True, epi="add", extra=dx_f, scale=1.0, comm=(a2a(gw),),
                               name="l1_dx_in")
    dr, drb, dg_, db_ = _ln_bwd(dx, r0b, ln_mlp_g[0], "l0_ln_mlp_bwd")
    g_ln_mlp[0] = (dg_, db_)
    gw = _shard_rows(_matmul(h0, drb, ta=True, out_dtype=BF16, name="l0_dw2"))
    dz, (gw2_0,) = _matmul(drb, w2[0], tb=True, out_dtype=BF16, epi="drelu2", extra=z0, comm=(a2a(gw),), name="l0_dz")
    gw = _matmul(x0a_b, dz, ta=True, out_dtype=BF16, out_sharded=True, name="l0_dw1")
    dx, (gw1_0,) = _matmul(dz, w1[0], tb=True, b_sharded=True, epi="add", extra=dr, comm=(a2a(gw),), name="l0_dx_mlp")
    dr, drb, dg_, db_ = _ln_bwd(dx, r0a, ln_mix_g[0], "l0_ln_mix_bwd")
    g_ln_mix[0] = (dg_, db_)
    gw = _shard_rows(_matmul(cat0, drb, ta=True, out_dtype=BF16, name="l0_dwout"))
    dcat, g["ev_w_out"] = _matmul(drb, w_ev_out, tb=True, comm=(a2a(gw),), name="l0_dcat")
    dc, g_conv_w, g_conv_b, g_conv_lg, g_conv_lb = _conv_bwd_params(
        u0, dcat, conv_w, ev_conv_b[0], ev_conv_ln_g[0], ev_conv_ln_b[0], "l0_conv_bwd_p")
    da, dgate = _conv_bwd_input(u0, dc, conv_w, "l0_conv_bwd_i")
    dhq, dhf, dhi, dhg, g_lb0, g_gnorm = _hgrn_bwd(u0, o_raw, states, dcat, lb0, ev_gnorm_g[0], "l0_hgrn_bwd")
    du0 = jnp.concatenate([da, dgate, dhq, dhf, dhi, dhg], axis=1)
    gw = _matmul(xin_b, du0, ta=True, out_dtype=BF16, out_sharded=True, name="l0_dwin")
    grad_x, g["ev_w_in"] = _matmul(du0, w_ev_in, tb=True, b_sharded=True, epi="add", extra=dr, comm=(a2a(gw),),
                                   name="l0_dx_in")
    g["mlp_w1"] = [gw1_0, gw1_1]
    g["mlp_w2"] = [gw2_0, gw2_1]

    g_lb_logits = jax.vjp(_lower_bound, hgrn_lb_logits)[1](g_lb0)[0]
    g_rel = _ca_bias_grad(dbias)
    small_partial = dict(
        ev_conv_w=g_conv_w, ev_conv_b=g_conv_b, ev_conv_ln_g=g_conv_lg, ev_conv_ln_b=g_conv_lb,
        hgrn_lb_logits=g_lb_logits, ev_gnorm_g=g_gnorm, fox_b_f=dbf[:HEADS], rel_bias=g_rel,
        ln_mix_g=jnp.stack([g_ln_mix[0][0], g_ln_mix[1][0]]), ln_mix_b=jnp.stack([g_ln_mix[0][1], g_ln_mix[1][1]]),
        ln_mlp_g=jnp.stack([g_ln_mlp[0][0], g_ln_mlp[1][0]]), ln_mlp_b=jnp.stack([g_ln_mlp[0][1], g_ln_mlp[1][1]]))
    return loss_part, grad_x, g, small_partial
```
